```python
import math
import jax, jax.numpy as jnp
from jax import lax
import numpy as np

D_MODEL = 2048
BATCH = 4
SEQ = 4096
DEPTH = 1

NORM_EPS = 1e-6
RWKV_HEADS = 16
RWKV_HEAD_DIM = 64
RWKV_WIDTH = RWKV_HEADS * RWKV_HEAD_DIM
DECAY_LORA = max(32, int(round(1.8 * D_MODEL ** 0.5 / 32)) * 32)
ICLR_LORA = max(32, int(round(1.8 * D_MODEL ** 0.5 / 32)) * 32)
GATE_LORA = max(32, int(round(0.6 * D_MODEL ** 0.8 / 32)) * 32)
GN_EPS = 64e-5
RWKV_COLS = 3 * RWKV_WIDTH + DECAY_LORA + ICLR_LORA + GATE_LORA
RWKV_SPLITS = [RWKV_WIDTH, 2 * RWKV_WIDTH, 3 * RWKV_WIDTH,
               3 * RWKV_WIDTH + DECAY_LORA, 3 * RWKV_WIDTH + DECAY_LORA + ICLR_LORA]
ATT_Q_HEADS = 16
ATT_KV_HEADS = 4
ATT_GROUP = ATT_Q_HEADS // ATT_KV_HEADS
ATT_HEAD_DIM = 64
ATT_Q_WIDTH = ATT_Q_HEADS * ATT_HEAD_DIM
ATT_KV_WIDTH = ATT_KV_HEADS * ATT_HEAD_DIM
WINDOW = 128
ATT_BLOCK = 128
IN_COLS = RWKV_COLS + ATT_Q_WIDTH + 2 * ATT_KV_WIDTH + 2 * D_MODEL
IN_SPLITS = [RWKV_COLS, RWKV_COLS + ATT_Q_WIDTH, RWKV_COLS + ATT_Q_WIDTH + ATT_KV_WIDTH,
             RWKV_COLS + ATT_Q_WIDTH + 2 * ATT_KV_WIDTH,
             RWKV_COLS + ATT_Q_WIDTH + 2 * ATT_KV_WIDTH + D_MODEL]
N_GROUPS = 8
EXPERTS_PER_GROUP = 8
N_EXPERTS = N_GROUPS * EXPERTS_PER_GROUP
TOP_K_FINE = 2
EXPERT_FF = 1024
MOE_BLOCK = 128

kernel_name = "hybrid_rwkv7_swa_sink_hmoe"


def rms_norm(t, gain, eps=NORM_EPS):
    t32 = t.astype(jnp.float32)
    y = t32 * lax.rsqrt(jnp.mean(t32 * t32, axis=-1, keepdims=True) + eps)
    return (y * gain.astype(jnp.float32)).astype(t.dtype)


def rwkv7_time_mix(p, mu, w0, w2, a0, a2, g2, k_k, k_a, r_k, gn_w, gn_b):
    B, S, _ = p.shape
    H, N = RWKV_HEADS, RWKV_HEAD_DIM
    f32 = jnp.float32
    p_prev = jnp.pad(p[:, :-1], ((0, 0), (1, 0), (0, 0)))
    p = p + (p_prev - p) * mu
    r, k, v, xw, xa, xg = jnp.split(p, RWKV_SPLITS, axis=-1)
    w = -jax.nn.softplus(-(w0 + jnp.tanh(xw) @ w2).astype(f32)) - 0.5
    decay = jnp.exp(-jnp.exp(w))
    a = jax.nn.sigmoid((a0 + xa @ a2).astype(f32))
    g = jax.nn.sigmoid(xg) @ g2
    heads = lambda t: t.astype(f32).reshape(B, S, H, N)
    kk = heads(k * k_k)
    kk = kk / jnp.maximum(jnp.sqrt(jnp.sum(kk * kk, axis=-1, keepdims=True)), 1e-12)
    k = heads(k) * (1.0 + (heads(a) - 1.0) * k_a.astype(f32).reshape(H, N))
    r_h, v_h, a_h, w_h = heads(r), heads(v), heads(a), heads(decay)
    to_time = lambda t: jnp.moveaxis(t, 1, 0)

    def step(state, inp):
        rt, wt, kt, vt, an, bt = inp
        sa = jnp.einsum('bhij,bhj->bhi', state, an)
        state = state * wt[:, :, None, :] + sa[..., None] * bt[:, :, None, :] + vt[..., None] * kt[:, :, None, :]
        return state, jnp.einsum('bhij,bhj->bhi', state, rt)

    state0 = jnp.zeros((B, H, N, N), f32)
    xs = (to_time(r_h), to_time(w_h), to_time(k), to_time(v_h), to_time(-kk), to_time(kk * a_h))
    _, y = lax.scan(step, state0, xs)
    y = jnp.moveaxis(y, 0, 1)
    mean = jnp.mean(y, axis=-1, keepdims=True)
    var = jnp.mean(jnp.square(y - mean), axis=-1, keepdims=True)
    y = (y - mean) * lax.rsqrt(var + GN_EPS) * gn_w.astype(f32).reshape(H, N) + gn_b.astype(f32).reshape(H, N)
    bonus = jnp.sum(r_h * k * r_k.astype(f32), axis=-1, keepdims=True) * v_h
    out = (y + bonus).reshape(B, S, H * N).astype(p.dtype)
    return out * g


def sliding_window_sink_attention(q, k, v, q_gain, k_gain, sinks):
    B, S, _ = q.shape
    nb = S // ATT_BLOCK
    q = rms_norm(q.reshape(B, S, ATT_Q_HEADS, ATT_HEAD_DIM), q_gain)
    k = rms_norm(k.reshape(B, S, ATT_KV_HEADS, ATT_HEAD_DIM), k_gain)
    v = v.reshape(B, S, ATT_KV_HEADS, ATT_HEAD_DIM)
    qb = q.reshape(B, nb, ATT_BLOCK, ATT_KV_HEADS, ATT_GROUP, ATT_HEAD_DIM)

    def with_prev(t):
        tb = t.reshape(B, nb, ATT_BLOCK, ATT_KV_HEADS, ATT_HEAD_DIM)
        prev = jnp.pad(tb[:, :-1], ((0, 0), (1, 0), (0, 0), (0, 0), (0, 0)))
        return jnp.concatenate([prev, tb], axis=2)

    kw, vw = with_prev(k), with_prev(v)
    scale = ATT_HEAD_DIM ** -0.5
    s = jnp.einsum('bnqhgd,bnkhd->bnhgqk', qb, kw, preferred_element_type=jnp.float32) * scale
    i = jnp.arange(ATT_BLOCK)[:, None]
    j = jnp.arange(2 * ATT_BLOCK)[None, :]
    rel = ATT_BLOCK + i - j
    blk = jnp.arange(nb)[:, None, None]
    valid = (rel >= 0) & (rel < WINDOW) & ((blk > 0) | (j >= ATT_BLOCK))
    s = jnp.where(valid[None, :, None, None], s, -jnp.inf)
    sink = sinks.astype(jnp.float32).reshape(ATT_KV_HEADS, ATT_GROUP)[None, None, :, :, None, None]
    m = jnp.maximum(jnp.max(s, axis=-1, keepdims=True), sink)
    pr = jnp.exp(s - m)
    denom = jnp.sum(pr, axis=-1, keepdims=True) + jnp.exp(sink - m)
    o = jnp.einsum('bnhgqk,bnkhd->bnqhgd', (pr / denom).astype(v.dtype), vw)
    return o.reshape(B, S, ATT_Q_WIDTH)


def hierarchical_moe(h, wc, bc, wf, bf, wg, wu, wd):
    B, S, D = h.shape
    N = B * S
    M = N * TOP_K_FINE
    hf = h.reshape(N, D)
    coarse = jax.nn.softmax((hf @ wc).astype(jnp.float32) + bc.astype(jnp.float32), axis=-1)
    g_prob, g_idx = lax.top_k(coarse, 1)
    fine_logits = ((hf @ wf).astype(jnp.float32) + bf.astype(jnp.float32)).reshape(N, N_GROUPS, EXPERTS_PER_GROUP)
    fine_in_group = jnp.take_along_axis(fine_logits, g_idx[:, :, None], axis=1)[:, 0]
    f_prob, f_idx = lax.top_k(jax.nn.softmax(fine_in_group, axis=-1), TOP_K_FINE)
    weights = g_prob * f_prob / jnp.sum(f_prob, axis=-1, keepdims=True)
    expert_ids = (g_idx * EXPERTS_PER_GROUP + f_idx).reshape(M)
    order = jnp.argsort(expert_ids)
    sorted_ids = expert_ids[order]
    counts = jnp.bincount(expert_ids, length=N_EXPERTS)
    padded = (counts + MOE_BLOCK - 1) // MOE_BLOCK * MOE_BLOCK
    seg_start = jnp.cumsum(counts) - counts
    pad_end = jnp.cumsum(padded)
    pad_start = pad_end - padded
    dest = pad_start[sorted_ids] + jnp.arange(M) - seg_start[sorted_ids]
    token_of_row = order // TOP_K_FINE
    P = M + N_EXPERTS * MOE_BLOCK
    n_blocks = P // MOE_BLOCK
    buf = jnp.zeros((P, D), h.dtype).at[dest].set(hf[token_of_row])
    block_expert = jnp.minimum(jnp.searchsorted(pad_end, jnp.arange(n_blocks) * MOE_BLOCK, side='right'), N_EXPERTS - 1)

    def expert_block(args):
        xb, e = args
        return (jax.nn.silu(xb @ wg[e]) * (xb @ wu[e])) @ wd[e]

    out = lax.map(expert_block, (buf.reshape(n_blocks, MOE_BLOCK, D), block_expert)).reshape(P, D)
    row_w = weights.reshape(M)[order].astype(out.dtype)
    y = jax.ops.segment_sum(out[dest] * row_w[:, None], token_of_row, num_segments=N)
    return y.reshape(B, S, D)


def setup_inputs(seed: int = 0) -> dict:
    key = jax.random.key(seed)
    ks = jax.random.split(key, 32)
    L = DEPTH
    f32 = jnp.float32
    nrm = lambda k, shape, scale: jax.random.normal(k, shape, f32) * scale
    return {
        "x": nrm(ks[0], (BATCH, SEQ, D_MODEL), 1.0),
        "norm1_w": 1.0 + nrm(ks[1], (L, D_MODEL), 0.02),
        "w_in": nrm(ks[2], (L, D_MODEL, IN_COLS), D_MODEL ** -0.5),
        "rwkv_mu": jax.random.uniform(ks[3], (L, RWKV_COLS), f32),
        "rwkv_w0": jax.random.uniform(ks[4], (L, RWKV_WIDTH), f32, -6.0, 1.0),
        "rwkv_w2": nrm(ks[5], (L, DECAY_LORA, RWKV_WIDTH), 0.1 * DECAY_LORA ** -0.5),
        "rwkv_a0": nrm(ks[6], (L, RWKV_WIDTH), 0.5),
        "rwkv_a2": nrm(ks[7], (L, ICLR_LORA, RWKV_WIDTH), 0.1 * ICLR_LORA ** -0.5),
        "rwkv_g2": nrm(ks[8], (L, GATE_LORA, RWKV_WIDTH), GATE_LORA ** -0.5),
        "rwkv_k_k": 0.85 + nrm(ks[9], (L, RWKV_WIDTH), 0.05),
        "rwkv_k_a": 1.0 + nrm(ks[10], (L, RWKV_WIDTH), 0.05),
        "rwkv_r_k": nrm(ks[11], (L, RWKV_HEADS, RWKV_HEAD_DIM), 0.1),
        "rwkv_gn_w": 1.0 + nrm(ks[12], (L, RWKV_WIDTH), 0.02),
        "rwkv_gn_b": nrm(ks[13], (L, RWKV_WIDTH), 0.02),
        "q_norm_w": 1.0 + nrm(ks[14], (L, ATT_HEAD_DIM), 0.02),
        "k_norm_w": 1.0 + nrm(ks[15], (L, ATT_HEAD_DIM), 0.02),
        "attn_sinks": nrm(ks[16], (L, ATT_Q_HEADS), 1.0),
        "proj_rwkv": nrm(ks[17], (L, RWKV_WIDTH, D_MODEL), RWKV_WIDTH ** -0.5),
        "proj_attn": nrm(ks[18], (L, ATT_Q_WIDTH, D_MODEL), ATT_Q_WIDTH ** -0.5),
        "w_out": nrm(ks[19], (L, D_MODEL, D_MODEL), D_MODEL ** -0.5),
        "norm2_w": 1.0 + nrm(ks[20], (L, D_MODEL), 0.02),
        "router_coarse_w": nrm(ks[21], (L, D_MODEL, N_GROUPS), D_MODEL ** -0.5),
        "router_coarse_b": nrm(ks[22], (L, N_GROUPS), 0.01),
        "router_fine_w": nrm(ks[23], (L, D_MODEL, N_EXPERTS), D_MODEL ** -0.5),
        "router_fine_b": nrm(ks[24], (L, N_EXPERTS), 0.01),
        "expert_w_gate": nrm(ks[25], (L, N_EXPERTS, D_MODEL, EXPERT_FF), D_MODEL ** -0.5),
        "expert_w_up": nrm(ks[26], (L, N_EXPERTS, D_MODEL, EXPERT_FF), D_MODEL ** -0.5),
        "expert_w_down": nrm(ks[27], (L, N_EXPERTS, EXPERT_FF, D_MODEL), EXPERT_FF ** -0.5),
    }


def reference(x, norm1_w, w_in, rwkv_mu, rwkv_w0, rwkv_w2, rwkv_a0, rwkv_a2, rwkv_g2,
              rwkv_k_k, rwkv_k_a, rwkv_r_k, rwkv_gn_w, rwkv_gn_b, q_norm_w, k_norm_w,
              attn_sinks, proj_rwkv, proj_attn, w_out, norm2_w, router_coarse_w,
              router_coarse_b, router_fine_w, router_fine_b, expert_w_gate, expert_w_up,
              expert_w_down):
    for layer in range(DEPTH):
        h = rms_norm(x, norm1_w[layer])
        proj = h @ w_in[layer]
        p_rwkv, q, k_att, v_att, gate_a, gate_b = jnp.split(proj, IN_SPLITS, axis=-1)
        a_out = rwkv7_time_mix(p_rwkv, rwkv_mu[layer], rwkv_w0[layer], rwkv_w2[layer],
                               rwkv_a0[layer], rwkv_a2[layer], rwkv_g2[layer], rwkv_k_k[layer],
                               rwkv_k_a[layer], rwkv_r_k[layer], rwkv_gn_w[layer], rwkv_gn_b[layer])
        b_out = sliding_window_sink_attention(q, k_att, v_att, q_norm_w[layer], k_norm_w[layer],
                                              attn_sinks[layer])
        merged = (jax.nn.sigmoid(gate_a) * (a_out @ proj_rwkv[layer])
                  + jax.nn.sigmoid(gate_b) * (b_out @ proj_attn[layer]))
        x = x + merged @ w_out[layer]
        h2 = rms_norm(x, norm2_w[layer])
        x = x + hierarchical_moe(h2, router_coarse_w[layer], router_coarse_b[layer],
                                 router_fine_w[layer], router_fine_b[layer], expert_w_gate[layer],
                                 expert_w_up[layer], expert_w_down[layer])
    return x
```

```python
import functools

import jax
import jax.numpy as jnp
from jax import lax
from jax.experimental import pallas as pl
from jax.experimental.pallas import tpu as pltpu

F32 = jnp.float32
BF16 = jnp.bfloat16
I32 = jnp.int32

NORM_EPS = 1e-6
GN_EPS = 64e-5
HEAD_DIM = 64
LANES = 128
CHUNK = 64
WINDOW = 128
MOE_BLOCK = 128
TOP_K = 2
N_GROUPS = 8
GROUP_SIZE = 8
VMEM_LIMIT = 56 * 1024 * 1024

_NT = (((1,), (1,)), ((), ()))
_TN = (((0,), (0,)), ((), ()))


def _dot(a, b):
    return jnp.dot(a, b, preferred_element_type=F32)


def _split2(x):
    hi = x.astype(BF16)
    mid = (x - hi.astype(F32)).astype(BF16)
    return hi, mid


def _split3(x):
    hi = x.astype(BF16)
    r1 = x - hi.astype(F32)
    mid = r1.astype(BF16)
    lo = (r1 - mid.astype(F32)).astype(BF16)
    return hi, mid, lo


def _dot_exact_rhs(x, m):
    hi, mid, lo = _split3(x)
    return _dot(hi, m) + _dot(mid, m) + _dot(lo, m)


def _exact_lhs_dot(m, x):
    hi, mid, lo = _split3(x)
    return _dot(m, hi) + _dot(m, mid) + _dot(m, lo)


def _cparams(sem, vmem=VMEM_LIMIT):
    return pltpu.CompilerParams(dimension_semantics=sem, vmem_limit_bytes=vmem)


def _norm_proj_body(x_ref, g_ref, w_ref, o_ref, h_ref, *, sigmoid):
    @pl.when(pl.program_id(1) == 0)
    def _():
        x = x_ref[...]
        ms = jnp.mean(x * x, axis=-1, keepdims=True)
        h_ref[...] = (x * lax.rsqrt(ms + NORM_EPS) * g_ref[...]).astype(BF16)

    acc = _dot(h_ref[...], w_ref[...])
    if sigmoid:
        acc = jax.nn.sigmoid(acc)
    o_ref[...] = acc.astype(o_ref.dtype)


def norm_proj(x2d, gain, w, *, out_dtype, sigmoid, tm=1024, tn=512):
    n, d = x2d.shape
    c = w.shape[1]
    tm = min(tm, n)
    return pl.pallas_call(
        functools.partial(_norm_proj_body, sigmoid=sigmoid),
        grid=(n // tm, c // tn),
        in_specs=[pl.BlockSpec((tm, d), lambda i, j: (i, 0)),
                  pl.BlockSpec((1, d), lambda i, j: (0, 0)),
                  pl.BlockSpec((d, tn), lambda i, j: (0, j))],
        out_specs=pl.BlockSpec((tm, tn), lambda i, j: (i, j)),
        out_shape=jax.ShapeDtypeStruct((n, c), out_dtype),
        scratch_shapes=[pltpu.VMEM((tm, d), BF16)],
        compiler_params=_cparams(("parallel", "arbitrary")),
        name="norm_proj",
    )(x2d, gain, w)


def _head_sum(x, e, et):
    hi, mid = _split2(x)
    s = _dot(hi, e) + _dot(mid, e)
    return _dot_exact_rhs(s, et)


def _rwkv_prep_body(p_ref, pprev_ref, mu_ref, w0_ref, w2_ref, a0_ref, a2_ref, g2_ref,
                    kk_ref, ka_ref, rk_ref, e_ref, et_ref, tri_ref, sel_ref,
                    rt_ref, at_ref, bt_ref, kt_ref, v_ref, g_ref, bonus_ref, pc_ref,
                    *, seq_len, width):
    tm = p_ref.shape[0]
    w_ = width
    first = (pl.program_id(0) * tm) % seq_len == 0
    p = p_ref[...]
    prev_row = jnp.where(first, 0.0, pprev_ref[7:8, :])
    row = lax.broadcasted_iota(I32, (tm, 1), 0)
    shifted = jnp.where(row == 0, prev_row, pltpu.roll(p, 1, 0))
    m = p + (shifted - p) * mu_ref[...]
    r = m[:, 0:w_]
    k = m[:, w_:2 * w_]
    v = m[:, 2 * w_:3 * w_]
    xw = m[:, 3 * w_:3 * w_ + 128]
    xa = m[:, 3 * w_ + 128:3 * w_ + 256]
    xg = m[:, 3 * w_ + 256:]

    z = -(w0_ref[...] + _dot(jnp.tanh(xw).astype(BF16), w2_ref[...]))
    softplus = jnp.maximum(z, 0.0) + jnp.log1p(jnp.exp(-jnp.abs(z)))
    logw = -jnp.exp(-softplus - 0.5)
    a = jax.nn.sigmoid(a0_ref[...] + _dot(xa.astype(BF16), a2_ref[...]))
    g = _dot(jax.nn.sigmoid(xg).astype(BF16), g2_ref[...])

    e = e_ref[...]
    et = et_ref[...]
    kk = k * kk_ref[...]
    kk = kk / jnp.maximum(jnp.sqrt(_head_sum(kk * kk, e, et)), 1e-12)
    kmod = k * (1.0 + (a - 1.0) * ka_ref[...])
    bonus = _head_sum(r * kmod * rk_ref[...], e, et) * v

    cum = _exact_lhs_dot(tri_ref[...], logw)
    pc_ref[...] = jnp.exp(_exact_lhs_dot(sel_ref[...], logw))
    inv = jnp.exp(-cum)
    rt_ref[...] = (r * jnp.exp(cum)).astype(BF16)
    at_ref[...] = (-kk * jnp.exp(cum - logw)).astype(BF16)
    bt_ref[...] = (kk * a * inv).astype(BF16)
    kt_ref[...] = (kmod * inv).astype(BF16)
    v_ref[...] = v.astype(BF16)
    g_ref[...] = g.astype(BF16)
    bonus_ref[...] = bonus


def _head_indicator(width):
    heads = width // HEAD_DIM
    c = jnp.arange(width)[:, None] // HEAD_DIM
    h = jnp.arange(LANES)[None, :]
    e = (c == h).astype(BF16)
    assert heads <= LANES
    return e, e.T


def rwkv_prep(p, mu, w0, w2, a0, a2, g2, k_k, k_a, r_k, *, seq_len, tm=512):
    n, cols = p.shape
    width = w0.shape[1]
    nchunk = tm // CHUNK
    e, et = _head_indicator(width)
    t = jnp.arange(tm)
    same = (t[:, None] // CHUNK) == (t[None, :] // CHUNK)
    tri = (same & (t[:, None] >= t[None, :])).astype(BF16)
    sel = ((t[None, :] // CHUNK) == jnp.arange(nchunk)[:, None]).astype(BF16)
    const = lambda shape: pl.BlockSpec(shape, lambda i: (0, 0))
    stream = lambda dt: jax.ShapeDtypeStruct((n, width), dt)
    outs = pl.pallas_call(
        functools.partial(_rwkv_prep_body, seq_len=seq_len, width=width),
        grid=(n // tm,),
        in_specs=[pl.BlockSpec((tm, cols), lambda i: (i, 0)),
                  pl.BlockSpec((8, cols), lambda i: (jnp.maximum(i * (tm // 8) - 1, 0), 0)),
                  const((1, cols)), const((1, width)), const(w2.shape), const((1, width)),
                  const(a2.shape), const(g2.shape), const((1, width)), const((1, width)),
                  const((1, width)), const(e.shape), const(et.shape), const(tri.shape),
                  const(sel.shape)],
        out_specs=[pl.BlockSpec((tm, width), lambda i: (i, 0))] * 7
                  + [pl.BlockSpec((nchunk, width), lambda i: (i, 0))],
        out_shape=[stream(BF16)] * 6 + [stream(F32),
                   jax.ShapeDtypeStruct((n // CHUNK, width), F32)],
        compiler_params=_cparams(("parallel",)),
        name="rwkv_prep",
    )(p, p, mu, w0, w2, a0, a2, g2, k_k, k_a, r_k, e, et, tri, sel)
    return outs


def _pair_chunk(rt, at, bt, kt, vv, pc, s_prev, masks):
    lo, strict, incl, eye = masks
    zero = jnp.zeros_like(rt)

    def stack(x):
        return jnp.concatenate([jnp.where(lo, x, zero), jnp.where(lo, zero, x)], axis=0)

    ar = jnp.concatenate([stack(at), stack(rt)], axis=0)
    bk = jnp.concatenate([stack(bt), stack(kt)], axis=0)
    vb = stack(vv)
    c2 = 2 * rt.shape[0]
    gram = lax.dot_general(ar, bk, _NT, preferred_element_type=F32)
    a_ab = jnp.where(strict, gram[:c2, :c2], 0.0)
    a_ak = jnp.where(strict, gram[:c2, c2:], 0.0)
    a_rb = jnp.where(incl, gram[c2:, :c2], 0.0)
    a_rk = jnp.where(incl, gram[c2:, c2:], 0.0)

    ars = lax.dot_general(ar, s_prev.astype(BF16), _NT, preferred_element_type=F32)
    rhs = ars[:c2] + _dot(a_ak.astype(BF16), vb)

    t_inv = eye + a_ab
    nk = a_ab
    steps = (rt.shape[0] - 1).bit_length() - 1
    for _ in range(steps):
        nkb = nk.astype(BF16)
        nk = _dot(nkb, nkb)
        t_inv = t_inv + _dot(t_inv.astype(BF16), nk.astype(BF16))
    u = _dot(t_inv.astype(BF16), rhs.astype(BF16))

    uv = jnp.concatenate([u.astype(BF16), vb], axis=0)
    ybd = ars[c2:] + _dot(jnp.concatenate([a_rb, a_rk], axis=1).astype(BF16), uv)
    y = ybd[:rt.shape[0]] + ybd[rt.shape[0]:]

    bk_end = (bk.astype(F32) * pc).astype(BF16)
    s_new = s_prev * pc + lax.dot_general(uv, bk_end, _TN, preferred_element_type=F32)
    return y, s_new


def _rwkv_scan_body(rt_ref, at_ref, bt_ref, kt_ref, v_ref, pc_ref, g_ref, bonus_ref,
                    gnw_ref, gnb_ref, e_ref, et_ref, o_ref, s_ref, y_ref):
    tr, width = rt_ref.shape
    npairs = width // LANES

    @pl.when(pl.program_id(1) == 0)
    def _():
        s_ref[...] = jnp.zeros_like(s_ref)

    lane = lax.broadcasted_iota(I32, (CHUNK, LANES), 1)
    lo = lane < HEAD_DIM
    ri = lax.broadcasted_iota(I32, (2 * CHUNK, 2 * CHUNK), 0)
    ci = lax.broadcasted_iota(I32, (2 * CHUNK, 2 * CHUNK), 1)
    same = (ri // CHUNK) == (ci // CHUNK)
    masks = (lo, same & (ri > ci), same & (ri >= ci), (ri == ci).astype(F32))

    def chunk_step(c, carry):
        rows = pl.ds(pl.multiple_of(c * CHUNK, CHUNK), CHUNK)
        this_chunk = lax.broadcasted_iota(I32, (pc_ref.shape[0], 1), 0) == c
        for pr in range(npairs):
            cols = slice(pr * LANES, (pr + 1) * LANES)
            pc = jnp.sum(jnp.where(this_chunk, pc_ref[:, cols], 0.0), axis=0, keepdims=True)
            y, s_new = _pair_chunk(rt_ref[rows, cols], at_ref[rows, cols], bt_ref[rows, cols],
                                   kt_ref[rows, cols], v_ref[rows, cols], pc, s_ref[pr], masks)
            s_ref[pr] = s_new
            y_ref[rows, cols] = y
        return carry

    lax.fori_loop(0, tr // CHUNK, chunk_step, 0)

    e = e_ref[...]
    et = et_ref[...]
    y = y_ref[...]
    mean = _head_sum(y, e, et) * (1.0 / HEAD_DIM)
    d = y - mean
    var = _head_sum(d * d, e, et) * (1.0 / HEAD_DIM)
    out = d * lax.rsqrt(var + GN_EPS) * gnw_ref[...] + gnb_ref[...]
    out = (out + bonus_ref[...]) * g_ref[...].astype(F32)
    o_ref[...] = out.astype(o_ref.dtype)


def rwkv_scan(rt, at, bt, kt, vv, pc, g, bonus, gn_w, gn_b, *, batch, tr=512):
    n, width = rt.shape
    seq = n // batch
    nst = seq // tr
    e, et = _head_indicator(width)
    blk = lambda rows: pl.BlockSpec((rows, width), lambda b, s: (b * nst + s, 0))
    const = lambda shape: pl.BlockSpec(shape, lambda b, s: (0, 0))
    return pl.pallas_call(
        _rwkv_scan_body,
        grid=(batch, nst),
        in_specs=[blk(tr)] * 5 + [blk(tr // CHUNK), blk(tr), blk(tr),
                  const((1, width)), const((1, width)), const(e.shape), const(et.shape)],
        out_specs=blk(tr),
        out_shape=jax.ShapeDtypeStruct((n, width), BF16),
        scratch_shapes=[pltpu.VMEM((width // LANES, LANES, LANES), F32),
                        pltpu.VMEM((tr, width), F32)],
        compiler_params=_cparams(("parallel", "arbitrary")),
        name="rwkv_scan",
    )(rt, at, bt, kt, vv, pc, g, bonus, gn_w, gn_b, e, et)


def _head_rmsnorm(x, gain, bd):
    hi, mid = _split2(x * x)
    ms = (_dot(hi, bd) + _dot(mid, bd)) * (1.0 / HEAD_DIM)
    return x * lax.rsqrt(ms + NORM_EPS) * gain


def _swa_body(sink_ref, q_ref, kp_ref, kc_ref, vp_ref, vc_ref, qg_ref, kg_ref, bd_ref, o_ref,
              *, group):
    blk = q_ref.shape[0]
    n = pl.program_id(1)
    bd = bd_ref[...]
    scale = HEAD_DIM ** -0.5
    lane = lax.broadcasted_iota(I32, (1, LANES), 1)
    lo = lane < HEAD_DIM
    ri = lax.broadcasted_iota(I32, (2 * blk, 2 * blk), 0)
    cj = lax.broadcasted_iota(I32, (2 * blk, 2 * blk), 1)
    rel = blk + (ri % blk) - cj
    jmin = jnp.where(n > 0, 0, blk)
    valid = (rel >= 0) & (rel < WINDOW) & (cj >= jmin)
    top_rows = lax.broadcasted_iota(I32, (2 * blk, 1), 0) < blk

    kcat = jnp.concatenate([kp_ref[...], kc_ref[...]], axis=0).astype(F32)
    vcat = jnp.concatenate([vp_ref[...], vc_ref[...]], axis=0).astype(F32)
    kv_heads = kcat.shape[1] // HEAD_DIM
    for jt in range(kv_heads // 2):
        cols = slice(jt * LANES, (jt + 1) * LANES)
        kn = _head_rmsnorm(kcat[:, cols], kg_ref[...], bd)
        kn_r = pltpu.roll(kn, HEAD_DIM, 1)
        vt = vcat[:, cols]
        vt_r = pltpu.roll(vt, HEAD_DIM, 1)
        for hh in range(2):
            hk = 2 * jt + hh
            if hh == 0:
                k2 = jnp.where(lo, kn, kn_r)
                v2 = jnp.where(lo, vt, vt_r)
            else:
                k2 = jnp.where(lo, kn_r, kn)
                v2 = jnp.where(lo, vt_r, vt)
            k2 = k2.astype(BF16)
            v2 = v2.astype(BF16)
            for pp in range(group // 2):
                qt = hk * (group // 2) + pp
                qcols = slice(qt * LANES, (qt + 1) * LANES)
                qn = _head_rmsnorm(q_ref[:, qcols].astype(F32), qg_ref[...], bd)
                qst = jnp.concatenate([jnp.where(lo, qn, 0.0), jnp.where(lo, 0.0, qn)],
                                      axis=0).astype(BF16)
                s = lax.dot_general(qst, k2, _NT, preferred_element_type=F32) * scale
                s = jnp.where(valid, s, -jnp.inf)
                sink = jnp.where(top_rows, sink_ref[2 * qt], sink_ref[2 * qt + 1])
                mx = jnp.maximum(jnp.max(s, axis=-1, keepdims=True), sink)
                pr = jnp.exp(s - mx)
                den = jnp.sum(pr, axis=-1, keepdims=True) + jnp.exp(sink - mx)
                o = _dot((pr / den).astype(BF16), v2)
                o_ref[:, qcols] = jnp.where(lo, o[:blk], o[blk:]).astype(o_ref.dtype)


def swa_attention(qkv, q_gain, k_gain, sinks, *, batch, q_width, kv_width):
    n = qkv.shape[0]
    blk = WINDOW
    nb = n // batch // blk
    group = (q_width // HEAD_DIM) // (kv_width // HEAD_DIM)
    kcol = q_width // kv_width
    t = jnp.arange(LANES)
    bd = ((t[:, None] // HEAD_DIM) == (t[None, :] // HEAD_DIM)).astype(BF16)
    qg = jnp.tile(q_gain.reshape(1, HEAD_DIM), (1, 2))
    kg = jnp.tile(k_gain.reshape(1, HEAD_DIM), (1, 2))
    cur = lambda col: (lambda b, i, *_: (b * nb + i, col))
    prev = lambda col: (lambda b, i, *_: (b * nb + jnp.maximum(i - 1, 0), col))
    const = lambda shape: pl.BlockSpec(shape, lambda b, i, *_: (0, 0))
    grid_spec = pltpu.PrefetchScalarGridSpec(
        num_scalar_prefetch=1,
        grid=(batch, nb),
        in_specs=[pl.BlockSpec((blk, q_width), cur(0)),
                  pl.BlockSpec((blk, kv_width), prev(kcol)),
                  pl.BlockSpec((blk, kv_width), cur(kcol)),
                  pl.BlockSpec((blk, kv_width), prev(kcol + 1)),
                  pl.BlockSpec((blk, kv_width), cur(kcol + 1)),
                  const((1, LANES)), const((1, LANES)), const((LANES, LANES))],
        out_specs=pl.BlockSpec((blk, q_width), cur(0)),
    )
    return pl.pallas_call(
        functools.partial(_swa_body, group=group),
        grid_spec=grid_spec,
        out_shape=jax.ShapeDtypeStruct((n, q_width), BF16),
        compiler_params=_cparams(("parallel", "arbitrary")),
        name="swa_attn",
    )(sinks, qkv, qkv, qkv, qkv, qkv, qg, kg, bd)


def _merge_route_body(a_ref, b_ref, ga_ref, gb_ref, x_ref, pr_ref, pa_ref, wo_ref, n2_ref,
                      rh_ref, rm_ref, rl_ref, rb_ref, x1_ref, h2_ref, route_ref):
    merged = (ga_ref[...].astype(F32) * _dot(a_ref[...], pr_ref[...])
              + gb_ref[...].astype(F32) * _dot(b_ref[...], pa_ref[...]))
    x1 = x_ref[...] + _dot(merged.astype(BF16), wo_ref[...])
    x1_ref[...] = x1
    ms = jnp.mean(x1 * x1, axis=-1, keepdims=True)
    h2 = x1 * lax.rsqrt(ms + NORM_EPS) * n2_ref[...]
    h2_ref[...] = h2

    hi, mid, lo = _split3(h2)
    rh, rm, rl = rh_ref[...], rm_ref[...], rl_ref[...]
    logits = (_dot(hi, rh) + (_dot(hi, rm) + _dot(mid, rh))
              + (_dot(hi, rl) + _dot(mid, rm) + _dot(lo, rh))) + rb_ref[...]

    lane = lax.broadcasted_iota(I32, logits.shape, 1)
    big = jnp.int32(1 << 20)
    neg = -jnp.inf
    is_coarse = lane < N_GROUPS
    cl = jnp.where(is_coarse, logits, neg)
    ce = jnp.exp(cl - jnp.max(cl, axis=-1, keepdims=True))
    cp = ce / jnp.sum(ce, axis=-1, keepdims=True)
    g_prob = jnp.max(cp, axis=-1, keepdims=True)
    g_idx = jnp.min(jnp.where(is_coarse & (cp == g_prob), lane, big), axis=-1, keepdims=True)

    fine_lane = lane - N_GROUPS
    in_group = ((lane >= N_GROUPS) & (lane < N_GROUPS + N_GROUPS * GROUP_SIZE)
                & ((fine_lane // GROUP_SIZE) == g_idx))
    fl = jnp.where(in_group, logits, neg)
    fe = jnp.exp(fl - jnp.max(fl, axis=-1, keepdims=True))
    fp = fe / jnp.sum(fe, axis=-1, keepdims=True)
    p1 = jnp.max(jnp.where(in_group, fp, -1.0), axis=-1, keepdims=True)
    i1 = jnp.min(jnp.where(in_group & (fp == p1), lane, big), axis=-1, keepdims=True)
    rest = in_group & (lane != i1)
    p2 = jnp.max(jnp.where(rest, fp, -1.0), axis=-1, keepdims=True)
    i2 = jnp.min(jnp.where(rest & (fp == p2), lane, big), axis=-1, keepdims=True)
    den = p1 + p2
    w1 = g_prob * p1 / den
    w2 = g_prob * p2 / den
    e1 = (i1 - N_GROUPS).astype(F32)
    e2 = (i2 - N_GROUPS).astype(F32)
    route_ref[...] = jnp.where(lane == 0, w1, jnp.where(lane == 1, w2,
                     jnp.where(lane == 2, e1, jnp.where(lane == 3, e2, 0.0))))


def merge_route(a_out, b_out, gates, x2d, proj_r, proj_a, w_out, norm2_w, router_w, router_b,
                *, tm=256):
    n, d = x2d.shape
    wa = a_out.shape[1]
    rh, rm, rl = _split3(router_w)
    const = lambda arr: pl.BlockSpec(arr.shape, lambda i: (0, 0), pipeline_mode=pl.Buffered(1))
    row = lambda w: pl.BlockSpec((tm, w), lambda i: (i, 0))
    return pl.pallas_call(
        _merge_route_body,
        grid=(n // tm,),
        in_specs=[row(wa), row(wa), pl.BlockSpec((tm, d), lambda i: (i, 0)),
                  pl.BlockSpec((tm, d), lambda i: (i, 1)), row(d),
                  const(proj_r), const(proj_a), const(w_out), const(norm2_w),
                  const(rh), const(rm), const(rl), const(router_b)],
        out_specs=[row(d), row(d), row(LANES)],
        out_shape=[jax.ShapeDtypeStruct((n, d), F32), jax.ShapeDtypeStruct((n, d), F32),
                   jax.ShapeDtypeStruct((n, LANES), F32)],
        compiler_params=_cparams(("parallel",)),
        name="merge_route",
    )(a_out, b_out, gates, gates, x2d, proj_r, proj_a, w_out, norm2_w, rh, rm, rl, router_b)


def _moe_rank_body(ids_ref, ut_ref, rank_ref, counts_ref, carry_ref):
    @pl.when(pl.program_id(0) == 0)
    def _():
        carry_ref[...] = jnp.zeros_like(carry_ref)

    ids = ids_ref[...]
    n_exp = carry_ref.shape[0]
    sub = lax.broadcasted_iota(I32, (n_exp, ids.shape[1]), 0)
    onehot = (sub == ids).astype(F32)
    before = _dot(onehot.astype(BF16), ut_ref[...])
    carry = carry_ref[...]
    rank_ref[...] = jnp.sum(onehot * (before + carry), axis=0, keepdims=True).astype(I32)
    carry = carry + jnp.sum(onehot, axis=1, keepdims=True)
    carry_ref[...] = carry
    counts_ref[...] = jnp.broadcast_to(carry, counts_ref.shape)


def moe_rank(ids_row, n_experts, *, tb=512):
    m = ids_row.shape[1]
    t = jnp.arange(tb)
    ut = (t[:, None] < t[None, :]).astype(BF16)
    return pl.pallas_call(
        _moe_rank_body,
        grid=(m // tb,),
        in_specs=[pl.BlockSpec((1, tb), lambda i: (0, i)),
                  pl.BlockSpec((tb, tb), lambda i: (0, 0))],
        out_specs=[pl.BlockSpec((1, tb), lambda i: (0, i)),
                   pl.BlockSpec((n_experts, LANES), lambda i: (0, 0))],
        out_shape=[jax.ShapeDtypeStruct((1, m), I32),
                   jax.ShapeDtypeStruct((n_experts, LANES), F32)],
        scratch_shapes=[pltpu.VMEM((n_experts, 1), F32)],
        compiler_params=_cparams(("arbitrary",)),
        name="moe_rank",
    )(ids_row, ut)


def _moe_dest_body(ids_ref, rank_ref, start_ref, dest_ref):
    ids = ids_ref[...]
    sub = lax.broadcasted_iota(I32, (start_ref.shape[0], ids.shape[1]), 0)
    start = jnp.sum(jnp.where(sub == ids, start_ref[...], 0.0), axis=0, keepdims=True)
    dest_ref[...] = start.astype(I32) + rank_ref[...]


def moe_dest(ids_row, rank_row, pad_start_col, *, tb=512):
    m = ids_row.shape[1]
    row = pl.BlockSpec((1, tb), lambda i: (0, i))
    return pl.pallas_call(
        _moe_dest_body,
        grid=(m // tb,),
        in_specs=[row, row, pl.BlockSpec(pad_start_col.shape, lambda i: (0, 0))],
        out_specs=row,
        out_shape=jax.ShapeDtypeStruct((1, m), I32),
        compiler_params=_cparams(("parallel",)),
        name="moe_dest",
    )(ids_row, rank_row, pad_start_col)


def _moe_dispatch_body(dest_ref, h_ref, buf_in_ref, buf_ref, sem):
    del buf_in_ref
    tm = h_ref.shape[0]
    base = pl.program_id(0) * tm * TOP_K

    def row_copy(j):
        slot = dest_ref[base + j]
        return pltpu.make_async_copy(h_ref.at[pl.ds(j // TOP_K, 1)], buf_ref.at[pl.ds(slot, 1)], sem)

    def start(j, c):
        row_copy(j).start()
        return c

    def wait(j, c):
        row_copy(j).wait()
        return c

    lax.fori_loop(0, tm * TOP_K, start, 0)
    lax.fori_loop(0, tm * TOP_K, wait, 0)


def moe_dispatch(dest, h2, n_slots, *, tm=256):
    n, d = h2.shape
    buf0 = jnp.zeros((n_slots, d), h2.dtype)
    grid_spec = pltpu.PrefetchScalarGridSpec(
        num_scalar_prefetch=1,
        grid=(n // tm,),
        in_specs=[pl.BlockSpec((tm, d), lambda i, *_: (i, 0)),
                  pl.BlockSpec(memory_space=pl.ANY)],
        out_specs=pl.BlockSpec(memory_space=pl.ANY),
        scratch_shapes=[pltpu.SemaphoreType.DMA(())],
    )
    return pl.pallas_call(
        _moe_dispatch_body,
        grid_spec=grid_spec,
        out_shape=jax.ShapeDtypeStruct((n_slots, d), h2.dtype),
        input_output_aliases={2: 0},
        compiler_params=_cparams(("arbitrary",)),
        name="moe_dispatch",
    )(dest, h2, buf0)


def _moe_expert_body(be_ref, nused_ref, x_ref, wg_ref, wu_ref, wd_ref, o_ref):
    del be_ref

    @pl.when(pl.program_id(0) < nused_ref[0])
    def _():
        x = x_ref[...].astype(BF16)
        hg = _dot(x, wg_ref[0])
        hu = _dot(x, wu_ref[0])
        act = (jax.nn.silu(hg) * hu).astype(BF16)
        o_ref[...] = _dot(act, wd_ref[0])

    @pl.when(pl.program_id(0) >= nused_ref[0])
    def _():
        o_ref[...] = jnp.zeros_like(o_ref)


def moe_experts(block_expert, n_used, buf, wg, wu, wd):
    p, d = buf.shape
    ff = wg.shape[2]
    nblk = p // MOE_BLOCK
    grid_spec = pltpu.PrefetchScalarGridSpec(
        num_scalar_prefetch=2,
        grid=(nblk,),
        in_specs=[pl.BlockSpec((MOE_BLOCK, d), lambda i, be, nu: (i, 0)),
                  pl.BlockSpec((1, d, ff), lambda i, be, nu: (be[i], 0, 0)),
                  pl.BlockSpec((1, d, ff), lambda i, be, nu: (be[i], 0, 0)),
                  pl.BlockSpec((1, ff, d), lambda i, be, nu: (be[i], 0, 0))],
        out_specs=pl.BlockSpec((MOE_BLOCK, d), lambda i, be, nu: (i, 0)),
    )
    return pl.pallas_call(
        _moe_expert_body,
        grid_spec=grid_spec,
        out_shape=jax.ShapeDtypeStruct((p, d), F32),
        compiler_params=_cparams(("arbitrary",)),
        name="moe_experts",
    )(block_expert, n_used, buf, wg, wu, wd)


def _moe_combine_body(dest_ref, x1_ref, route_ref, out_ref, y_ref, rows_ref, sem):
    tm = x1_ref.shape[0]
    base = pl.program_id(0) * tm * TOP_K

    def row_copy(j):
        slot = dest_ref[base + j]
        return pltpu.make_async_copy(out_ref.at[pl.ds(slot, 1)],
                                     rows_ref.at[j % TOP_K, pl.ds(j // TOP_K, 1)], sem)

    def start(j, c):
        row_copy(j).start()
        return c

    def wait(j, c):
        row_copy(j).wait()
        return c

    lax.fori_loop(0, tm * TOP_K, start, 0)
    lax.fori_loop(0, tm * TOP_K, wait, 0)
    route = route_ref[...]
    y_ref[...] = x1_ref[...] + (rows_ref[0] * route[:, 0:1] + rows_ref[1] * route[:, 1:2])


def moe_combine(dest, x1, route, expert_out, *, tm=256):
    n, d = x1.shape
    grid_spec = pltpu.PrefetchScalarGridSpec(
        num_scalar_prefetch=1,
        grid=(n // tm,),
        in_specs=[pl.BlockSpec((tm, d), lambda i, *_: (i, 0)),
                  pl.BlockSpec((tm, LANES), lambda i, *_: (i, 0)),
                  pl.BlockSpec(memory_space=pl.ANY)],
        out_specs=pl.BlockSpec((tm, d), lambda i, *_: (i, 0)),
        scratch_shapes=[pltpu.VMEM((TOP_K, tm, d), F32), pltpu.SemaphoreType.DMA(())],
    )
    return pl.pallas_call(
        _moe_combine_body,
        grid_spec=grid_spec,
        out_shape=jax.ShapeDtypeStruct((n, d), F32),
        compiler_params=_cparams(("arbitrary",)),
        name="moe_combine",
    )(dest, x1, route, expert_out)


def hierarchical_moe(x1, h2, route, wg, wu, wd):
    n, d = h2.shape
    n_experts = wg.shape[0]
    m = n * TOP_K
    ids_row = route[:, 2:4].astype(I32).reshape(1, m)
    rank_row, counts = moe_rank(ids_row, n_experts)
    counts = counts[:, 0].astype(I32)
    padded = (counts + MOE_BLOCK - 1) // MOE_BLOCK * MOE_BLOCK
    pad_end = jnp.cumsum(padded)
    pad_start = pad_end - padded
    dest = moe_dest(ids_row, rank_row, pad_start.astype(F32).reshape(n_experts, 1)).reshape(m)
    n_slots = m + n_experts * MOE_BLOCK
    nblk = n_slots // MOE_BLOCK
    n_used = (pad_end[-1] // MOE_BLOCK).astype(I32).reshape(1)
    block_expert = jnp.minimum(
        jnp.searchsorted(pad_end, jnp.arange(nblk, dtype=I32) * MOE_BLOCK, side="right"),
        n_experts - 1).astype(I32)
    last_expert = block_expert[jnp.maximum(n_used[0] - 1, 0)]
    block_expert = jnp.where(jnp.arange(nblk) < n_used[0], block_expert, last_expert)
    buf = moe_dispatch(dest, h2, n_slots)
    expert_out = moe_experts(block_expert, n_used, buf, wg, wu, wd)
    return moe_combine(dest, x1, route, expert_out)


def _pad_cols(w, to):
    return jnp.pad(w, ((0, 0), (0, to - w.shape[1])))


def _pad_rows(w, to):
    return jnp.pad(w, ((0, to - w.shape[0]), (0, 0)))


def _layer(x2d, batch, norm1_w, w_in, mu, w0, w2, a0, a2, g2, k_k, k_a, r_k, gn_w, gn_b,
           q_norm_w, k_norm_w, sinks, proj_rwkv, proj_attn, w_out, norm2_w, wc, bc, wf, bf,
           wg, wu, wd):
    n, d = x2d.shape
    seq = n // batch
    width = w0.shape[0]
    dl, il, gl = w2.shape[0], a2.shape[0], g2.shape[0]
    q_width = proj_attn.shape[0]
    rwkv_cols = 3 * width + dl + il + gl
    kv_width = (w_in.shape[1] - rwkv_cols - q_width - 2 * d) // 2
    row = lambda v: v.reshape(1, -1).astype(F32)

    c0 = 3 * width
    w_rwkv = jnp.concatenate([w_in[:, :c0], _pad_cols(w_in[:, c0:c0 + dl], LANES),
                              _pad_cols(w_in[:, c0 + dl:c0 + dl + il], LANES),
                              w_in[:, c0 + dl + il:rwkv_cols]], axis=1).astype(BF16)
    mu_p = jnp.concatenate([mu[:c0], jnp.pad(mu[c0:c0 + dl], (0, LANES - dl)),
                            jnp.pad(mu[c0 + dl:c0 + dl + il], (0, LANES - il)),
                            mu[c0 + dl + il:]]).reshape(1, -1)
    q0 = rwkv_cols
    w_qkv = w_in[:, q0:q0 + q_width + 2 * kv_width].astype(BF16)
    w_gates = w_in[:, q0 + q_width + 2 * kv_width:].astype(BF16)

    g1 = row(norm1_w)
    p_rwkv = norm_proj(x2d, g1, w_rwkv, out_dtype=F32, sigmoid=False)
    qkv = norm_proj(x2d, g1, w_qkv, out_dtype=BF16, sigmoid=False)
    gates = norm_proj(x2d, g1, w_gates, out_dtype=BF16, sigmoid=True)

    rt, at, bt, kt, vv, g, bonus, pc = rwkv_prep(
        p_rwkv, mu_p, row(w0), _pad_rows(w2, LANES).astype(BF16), row(a0),
        _pad_rows(a2, LANES).astype(BF16), g2.astype(BF16), row(k_k), row(k_a), row(r_k),
        seq_len=seq)
    a_out = rwkv_scan(rt, at, bt, kt, vv, pc, g, bonus, row(gn_w), row(gn_b), batch=batch)

    b_out = swa_attention(qkv, q_norm_w, k_norm_w, sinks.astype(F32), batch=batch,
                          q_width=q_width, kv_width=kv_width)

    n_groups, n_experts = wc.shape[1], wf.shape[1]
    router_w = _pad_cols(jnp.concatenate([wc, wf], axis=1), LANES)
    router_b = _pad_cols(jnp.concatenate([bc, bf]).reshape(1, -1), LANES)
    assert n_groups == N_GROUPS and n_experts == N_GROUPS * GROUP_SIZE
    x1, h2, route = merge_route(a_out, b_out, gates, x2d, proj_rwkv.astype(BF16),
                                proj_attn.astype(BF16), w_out.astype(BF16), row(norm2_w),
                                router_w, router_b)
    return hierarchical_moe(x1, h2, route, wg.astype(BF16), wu.astype(BF16), wd.astype(BF16))


def kernel(x, norm1_w, w_in, rwkv_mu, rwkv_w0, rwkv_w2, rwkv_a0, rwkv_a2, rwkv_g2, rwkv_k_k,
           rwkv_k_a, rwkv_r_k, rwkv_gn_w, rwkv_gn_b, q_norm_w, k_norm_w, attn_sinks, proj_rwkv,
           proj_attn, w_out, norm2_w, router_coarse_w, router_coarse_b, router_fine_w,
           router_fine_b, expert_w_gate, expert_w_up, expert_w_down):
    batch, seq, d = x.shape
    x2d = x.reshape(batch * seq, d)
    for layer in range(norm1_w.shape[0]):
        x2d = _layer(x2d, batch, norm1_w[layer], w_in[layer], rwkv_mu[layer], rwkv_w0[layer],
                     rwkv_w2[layer], rwkv_a0[layer], rwkv_a2[layer], rwkv_g2[layer],
                     rwkv_k_k[layer], rwkv_k_a[layer], rwkv_r_k[layer].reshape(-1),
                     rwkv_gn_w[layer], rwkv_gn_b[layer], q_norm_w[layer], k_norm_w[layer],
                     attn_sinks[layer], proj_rwkv[layer], proj_attn[layer], w_out[layer],
                     norm2_w[layer], router_coarse_w[layer], router_coarse_b[layer],
                     router_fine_w[layer], router_fine_b[layer], expert_w_gate[layer],
                     expert_w_up[layer], expert_w_down[layer])
    return x2d.reshape(batch, seq, d)
```

```python
import functools

import jax
import jax.numpy as jnp
from jax import lax
from jax.experimental import pallas as pl
from jax.experimental.pallas import tpu as pltpu

F32 = jnp.float32
BF16 = jnp.bfloat16
I32 = jnp.int32

NORM_EPS = 1e-6
GN_EPS = 64e-5
HEAD_DIM = 64
LANES = 128
CHUNK = 64
WINDOW = 128
MOE_BLOCK = 128
TOP_K = 2
N_GROUPS = 8
GROUP_SIZE = 8
VMEM_LIMIT = 56 * 1024 * 1024

_NT = (((1,), (1,)), ((), ()))
_TN = (((0,), (0,)), ((), ()))


def _dot(a, b):
    return jnp.dot(a, b, preferred_element_type=F32)


def _split2(x):
    hi = x.astype(BF16)
    mid = (x - hi.astype(F32)).astype(BF16)
    return hi, mid


def _split3(x):
    hi = x.astype(BF16)
    r1 = x - hi.astype(F32)
    mid = r1.astype(BF16)
    lo = (r1 - mid.astype(F32)).astype(BF16)
    return hi, mid, lo


def _dot_exact_rhs(x, m):
    hi, mid, lo = _split3(x)
    return _dot(hi, m) + _dot(mid, m) + _dot(lo, m)


def _exact_lhs_dot(m, x):
    hi, mid, lo = _split3(x)
    return _dot(m, hi) + _dot(m, mid) + _dot(m, lo)


def _cparams(sem, vmem=VMEM_LIMIT):
    return pltpu.CompilerParams(dimension_semantics=sem, vmem_limit_bytes=vmem)


def _norm_proj_body(x_ref, g_ref, w_ref, o_ref, h_ref, *, sigmoid):
    @pl.when(pl.program_id(1) == 0)
    def _():
        x = x_ref[...]
        ms = jnp.mean(x * x, axis=-1, keepdims=True)
        h_ref[...] = (x * lax.rsqrt(ms + NORM_EPS) * g_ref[...]).astype(BF16)

    acc = _dot(h_ref[...], w_ref[...])
    if sigmoid:
        acc = jax.nn.sigmoid(acc)
    o_ref[...] = acc.astype(o_ref.dtype)


def norm_proj(x2d, gain, w, *, out_dtype, sigmoid, tm=1024, tn=512):
    n, d = x2d.shape
    c = w.shape[1]
    tm = min(tm, n)
    return pl.pallas_call(
        functools.partial(_norm_proj_body, sigmoid=sigmoid),
        grid=(n // tm, c // tn),
        in_specs=[pl.BlockSpec((tm, d), lambda i, j: (i, 0)),
                  pl.BlockSpec((1, d), lambda i, j: (0, 0)),
                  pl.BlockSpec((d, tn), lambda i, j: (0, j))],
        out_specs=pl.BlockSpec((tm, tn), lambda i, j: (i, j)),
        out_shape=jax.ShapeDtypeStruct((n, c), out_dtype),
        scratch_shapes=[pltpu.VMEM((tm, d), BF16)],
        compiler_params=_cparams(("parallel", "arbitrary")),
        name="norm_proj",
    )(x2d, gain, w)


def _head_sum(x, e, et):
    s = _dot(x.astype(BF16), e)
    hi, mid = _split2(s)
    return _dot(hi, et) + _dot(mid, et)


def _rwkv_prep_body(p_ref, pprev_ref, mu_ref, w0_ref, w2_ref, a0_ref, a2_ref, g2_ref,
                    kk_ref, ka_ref, rk_ref, e_ref, et_ref, tri_ref, sel_ref,
                    rt_ref, at_ref, bt_ref, kt_ref, v_ref, g_ref, bonus_ref, pc_ref,
                    *, seq_len, width):
    tm = p_ref.shape[0]
    w_ = width
    first = (pl.program_id(0) * tm) % seq_len == 0
    p = p_ref[...]
    prev_row = jnp.where(first, 0.0, pprev_ref[7:8, :])
    row = lax.broadcasted_iota(I32, (tm, 1), 0)
    shifted = jnp.where(row == 0, prev_row, pltpu.roll(p, 1, 0))
    m = p + (shifted - p) * mu_ref[...]
    r = m[:, 0:w_]
    k = m[:, w_:2 * w_]
    v = m[:, 2 * w_:3 * w_]
    xw = m[:, 3 * w_:3 * w_ + 128]
    xa = m[:, 3 * w_ + 128:3 * w_ + 256]
    xg = m[:, 3 * w_ + 256:]

    z = -(w0_ref[...] + _dot(jnp.tanh(xw).astype(BF16), w2_ref[...]))
    softplus = jnp.maximum(z, 0.0) + jnp.log1p(jnp.exp(-jnp.abs(z)))
    logw = -jnp.exp(-softplus - 0.5)
    a = jax.nn.sigmoid(a0_ref[...] + _dot(xa.astype(BF16), a2_ref[...]))
    g = _dot(jax.nn.sigmoid(xg).astype(BF16), g2_ref[...])

    e = e_ref[...]
    et = et_ref[...]
    kk = k * kk_ref[...]
    kk = kk / jnp.maximum(jnp.sqrt(_head_sum(kk * kk, e, et)), 1e-12)
    kmod = k * (1.0 + (a - 1.0) * ka_ref[...])
    bonus = _head_sum(r * kmod * rk_ref[...], e, et) * v

    cum = _exact_lhs_dot(tri_ref[...], logw)
    pc_ref[...] = jnp.exp(_exact_lhs_dot(sel_ref[...], logw))
    inv = jnp.exp(-cum)
    rt_ref[...] = (r * jnp.exp(cum)).astype(BF16)
    at_ref[...] = (-kk * jnp.exp(cum - logw)).astype(BF16)
    bt_ref[...] = (kk * a * inv).astype(BF16)
    kt_ref[...] = (kmod * inv).astype(BF16)
    v_ref[...] = v.astype(BF16)
    g_ref[...] = g.astype(BF16)
    bonus_ref[...] = bonus


def _head_indicator(width):
    heads = width // HEAD_DIM
    c = jnp.arange(width)[:, None] // HEAD_DIM
    h = jnp.arange(LANES)[None, :]
    e = (c == h).astype(BF16)
    assert heads <= LANES
    return e, e.T


def rwkv_prep(p, mu, w0, w2, a0, a2, g2, k_k, k_a, r_k, *, seq_len, tm=512):
    n, cols = p.shape
    width = w0.shape[1]
    nchunk = tm // CHUNK
    e, et = _head_indicator(width)
    t = jnp.arange(tm)
    same = (t[:, None] // CHUNK) == (t[None, :] // CHUNK)
    tri = (same & (t[:, None] >= t[None, :])).astype(BF16)
    sel = ((t[None, :] // CHUNK) == jnp.arange(nchunk)[:, None]).astype(BF16)
    const = lambda shape: pl.BlockSpec(shape, lambda i: (0, 0))
    stream = lambda dt: jax.ShapeDtypeStruct((n, width), dt)
    outs = pl.pallas_call(
        functools.partial(_rwkv_prep_body, seq_len=seq_len, width=width),
        grid=(n // tm,),
        in_specs=[pl.BlockSpec((tm, cols), lambda i: (i, 0)),
                  pl.BlockSpec((8, cols), lambda i: (jnp.maximum(i * (tm // 8) - 1, 0), 0)),
                  const((1, cols)), const((1, width)), const(w2.shape), const((1, width)),
                  const(a2.shape), const(g2.shape), const((1, width)), const((1, width)),
                  const((1, width)), const(e.shape), const(et.shape), const(tri.shape),
                  const(sel.shape)],
        out_specs=[pl.BlockSpec((tm, width), lambda i: (i, 0))] * 7
                  + [pl.BlockSpec((nchunk, width), lambda i: (i, 0))],
        out_shape=[stream(BF16)] * 6 + [stream(F32),
                   jax.ShapeDtypeStruct((n // CHUNK, width), F32)],
        compiler_params=_cparams(("parallel",)),
        name="rwkv_prep",
    )(p, p, mu, w0, w2, a0, a2, g2, k_k, k_a, r_k, e, et, tri, sel)
    return outs


def _chunk_pairs(rts, ats, bts, kts, vvs, pcs, s_prevs, masks):
    lo, strict, incl, eye = masks
    c = rts[0].shape[0]
    c2 = 2 * c
    zero = jnp.zeros_like(rts[0])
    each = lambda f, *ls: [f(*a) for a in zip(*ls)]

    def stack(x):
        return jnp.concatenate([jnp.where(lo, x, zero), jnp.where(lo, zero, x)], axis=0)

    ar = each(lambda a, r: jnp.concatenate([stack(a), stack(r)], axis=0), ats, rts)
    bk = each(lambda b, k: jnp.concatenate([stack(b), stack(k)], axis=0), bts, kts)
    vb = each(stack, vvs)
    gram = each(lambda x, y: lax.dot_general(x, y, _NT, preferred_element_type=F32), ar, bk)
    ars = each(lambda x, s: lax.dot_general(x, s.astype(BF16), _NT, preferred_element_type=F32),
               ar, s_prevs)
    a_ab = each(lambda g: jnp.where(strict, g[:c2, :c2], 0.0), gram)
    rhs = each(lambda g, v, x: x[:c2] + _dot(jnp.where(strict, g[:c2, c2:], 0.0).astype(BF16), v),
               gram, vb, ars)

    t_inv = each(lambda n: eye + n, a_ab)
    nk = a_ab
    for _ in range((c - 1).bit_length() - 1):
        nkb = each(lambda n: n.astype(BF16), nk)
        nk = each(lambda n: _dot(n, n), nkb)
        t_inv = each(lambda t, n: t + _dot(t.astype(BF16), n.astype(BF16)), t_inv, nk)
    u = each(lambda t, r: _dot(t.astype(BF16), r.astype(BF16)), t_inv, rhs)

    uv = each(lambda uu, v: jnp.concatenate([uu.astype(BF16), v], axis=0), u, vb)
    a_r = each(lambda g: jnp.where(incl, g[c2:, :], 0.0).astype(BF16), gram)
    ybd = each(lambda x, a, w: x[c2:] + _dot(a, w), ars, a_r, uv)
    ys = each(lambda yb: yb[:c] + yb[c:], ybd)

    bk_end = each(lambda x, pc: (x.astype(F32) * pc).astype(BF16), bk, pcs)
    s_new = each(lambda s, pc, w, x: s * pc + lax.dot_general(w, x, _TN, preferred_element_type=F32),
                 s_prevs, pcs, uv, bk_end)
    return ys, s_new


def _rwkv_scan_body(rt_ref, at_ref, bt_ref, kt_ref, v_ref, pc_ref, g_ref, bonus_ref,
                    gnw_ref, gnb_ref, e_ref, et_ref, o_ref, s_ref, y_ref):
    tr, width = rt_ref.shape
    npairs = width // LANES

    @pl.when(pl.program_id(1) == 0)
    def _():
        s_ref[...] = jnp.zeros_like(s_ref)

    lane = lax.broadcasted_iota(I32, (CHUNK, LANES), 1)
    lo = lane < HEAD_DIM
    ri = lax.broadcasted_iota(I32, (2 * CHUNK, 2 * CHUNK), 0)
    ci = lax.broadcasted_iota(I32, (2 * CHUNK, 2 * CHUNK), 1)
    same = (ri // CHUNK) == (ci // CHUNK)
    ri2 = lax.broadcasted_iota(I32, (2 * CHUNK, 4 * CHUNK), 0)
    ci2 = lax.broadcasted_iota(I32, (2 * CHUNK, 4 * CHUNK), 1) % (2 * CHUNK)
    incl = ((ri2 // CHUNK) == (ci2 // CHUNK)) & (ri2 >= ci2)
    masks = (lo, same & (ri > ci), incl, (ri == ci).astype(F32))

    def chunk_step(c, carry):
        rows = pl.ds(pl.multiple_of(c * CHUNK, CHUNK), CHUNK)
        this_chunk = lax.broadcasted_iota(I32, (pc_ref.shape[0], 1), 0) == c
        cols = [slice(pr * LANES, (pr + 1) * LANES) for pr in range(npairs)]
        pcs = [jnp.sum(jnp.where(this_chunk, pc_ref[:, cl], 0.0), axis=0, keepdims=True)
               for cl in cols]
        load = lambda ref: [ref[rows, cl] for cl in cols]
        ys, s_new = _chunk_pairs(load(rt_ref), load(at_ref), load(bt_ref), load(kt_ref),
                                 load(v_ref), pcs, [s_ref[pr] for pr in range(npairs)], masks)
        for pr in range(npairs):
            s_ref[pr] = s_new[pr]
            y_ref[rows, cols[pr]] = ys[pr]
        return carry

    lax.fori_loop(0, tr // CHUNK, chunk_step, 0)

    e = e_ref[...]
    et = et_ref[...]
    y = y_ref[...]
    mean = _head_sum(y, e, et) * (1.0 / HEAD_DIM)
    d = y - mean
    var = _head_sum(d * d, e, et) * (1.0 / HEAD_DIM)
    out = d * lax.rsqrt(var + GN_EPS) * gnw_ref[...] + gnb_ref[...]
    out = (out + bonus_ref[...]) * g_ref[...].astype(F32)
    o_ref[...] = out.astype(o_ref.dtype)


def rwkv_scan(rt, at, bt, kt, vv, pc, g, bonus, gn_w, gn_b, *, batch, tr=512):
    n, width = rt.shape
    seq = n // batch
    nst = seq // tr
    e, et = _head_indicator(width)
    blk = lambda rows: pl.BlockSpec((rows, width), lambda b, s: (b * nst + s, 0))
    const = lambda shape: pl.BlockSpec(shape, lambda b, s: (0, 0))
    return pl.pallas_call(
        _rwkv_scan_body,
        grid=(batch, nst),
        in_specs=[blk(tr)] * 5 + [blk(tr // CHUNK), blk(tr), blk(tr),
                  const((1, width)), const((1, width)), const(e.shape), const(et.shape)],
        out_specs=blk(tr),
        out_shape=jax.ShapeDtypeStruct((n, width), BF16),
        scratch_shapes=[pltpu.VMEM((width // LANES, LANES, LANES), F32),
                        pltpu.VMEM((tr, width), F32)],
        compiler_params=_cparams(("parallel", "arbitrary")),
        name="rwkv_scan",
    )(rt, at, bt, kt, vv, pc, g, bonus, gn_w, gn_b, e, et)


def _head_rmsnorm(x, gain, bd):
    hi, mid = _split2(x * x)
    ms = (_dot(hi, bd) + _dot(mid, bd)) * (1.0 / HEAD_DIM)
    return x * lax.rsqrt(ms + NORM_EPS) * gain


def _swa_body(sink_ref, q_ref, kp_ref, kc_ref, vp_ref, vc_ref, qg_ref, kg_ref, bd_ref, o_ref,
              *, group):
    blk = q_ref.shape[0]
    n = pl.program_id(1)
    bd = bd_ref[...]
    scale = HEAD_DIM ** -0.5
    lane = lax.broadcasted_iota(I32, (1, LANES), 1)
    lo = lane < HEAD_DIM
    ri = lax.broadcasted_iota(I32, (2 * blk, 2 * blk), 0)
    cj = lax.broadcasted_iota(I32, (2 * blk, 2 * blk), 1)
    rel = blk + (ri % blk) - cj
    jmin = jnp.where(n > 0, 0, blk)
    valid = (rel >= 0) & (rel < WINDOW) & (cj >= jmin)
    top_rows = lax.broadcasted_iota(I32, (2 * blk, 1), 0) < blk

    kcat = jnp.concatenate([kp_ref[...], kc_ref[...]], axis=0).astype(F32)
    vcat = jnp.concatenate([vp_ref[...], vc_ref[...]], axis=0).astype(F32)
    kv_heads = kcat.shape[1] // HEAD_DIM
    q_tiles = q_ref.shape[1] // LANES
    tiles_per_kv = group // 2
    each = lambda f, *ls: [f(*a) for a in zip(*ls)]

    kv_cols = [slice(jt * LANES, (jt + 1) * LANES) for jt in range(kv_heads // 2)]
    kn = [_head_rmsnorm(kcat[:, c], kg_ref[...], bd) for c in kv_cols]
    kn_r = [pltpu.roll(x, HEAD_DIM, 1) for x in kn]
    vt = [vcat[:, c] for c in kv_cols]
    vt_r = [pltpu.roll(x, HEAD_DIM, 1) for x in vt]
    own = lambda hk: lo if hk % 2 == 0 else jnp.logical_not(lo)
    k2 = [jnp.where(own(hk), kn[hk // 2], kn_r[hk // 2]).astype(BF16) for hk in range(kv_heads)]
    v2 = [jnp.where(own(hk), vt[hk // 2], vt_r[hk // 2]).astype(BF16) for hk in range(kv_heads)]

    q_cols = [slice(t * LANES, (t + 1) * LANES) for t in range(q_tiles)]
    qn = [_head_rmsnorm(q_ref[:, c].astype(F32), qg_ref[...] * scale, bd) for c in q_cols]
    qst = [jnp.concatenate([jnp.where(lo, x, 0.0), jnp.where(lo, 0.0, x)], axis=0).astype(BF16)
           for x in qn]
    s = [jnp.where(valid, lax.dot_general(x, k2[t // tiles_per_kv], _NT,
                                          preferred_element_type=F32), -jnp.inf)
         for t, x in enumerate(qst)]
    sink = [jnp.where(top_rows, sink_ref[2 * t], sink_ref[2 * t + 1]) for t in range(q_tiles)]
    mx = each(lambda x, sk: jnp.maximum(jnp.max(x, axis=-1, keepdims=True), sk), s, sink)
    pr = each(lambda x, m: jnp.exp(x - m), s, mx)
    inv = each(lambda p, sk, m: 1.0 / (jnp.sum(p, axis=-1, keepdims=True) + jnp.exp(sk - m)),
               pr, sink, mx)
    o = [_dot((p * r).astype(BF16), v2[t // tiles_per_kv])
         for t, (p, r) in enumerate(zip(pr, inv))]
    for t in range(q_tiles):
        o_ref[:, q_cols[t]] = jnp.where(lo, o[t][:blk], o[t][blk:]).astype(o_ref.dtype)


def swa_attention(qkv, q_gain, k_gain, sinks, *, batch, q_width, kv_width):
    n = qkv.shape[0]
    blk = WINDOW
    nb = n // batch // blk
    group = (q_width // HEAD_DIM) // (kv_width // HEAD_DIM)
    kcol = q_width // kv_width
    t = jnp.arange(LANES)
    bd = ((t[:, None] // HEAD_DIM) == (t[None, :] // HEAD_DIM)).astype(BF16)
    qg = jnp.tile(q_gain.reshape(1, HEAD_DIM), (1, 2))
    kg = jnp.tile(k_gain.reshape(1, HEAD_DIM), (1, 2))
    cur = lambda col: (lambda b, i, *_: (b * nb + i, col))
    prev = lambda col: (lambda b, i, *_: (b * nb + jnp.maximum(i - 1, 0), col))
    const = lambda shape: pl.BlockSpec(shape, lambda b, i, *_: (0, 0))
    grid_spec = pltpu.PrefetchScalarGridSpec(
        num_scalar_prefetch=1,
        grid=(batch, nb),
        in_specs=[pl.BlockSpec((blk, q_width), cur(0)),
                  pl.BlockSpec((blk, kv_width), prev(kcol)),
                  pl.BlockSpec((blk, kv_width), cur(kcol)),
                  pl.BlockSpec((blk, kv_width), prev(kcol + 1)),
                  pl.BlockSpec((blk, kv_width), cur(kcol + 1)),
                  const((1, LANES)), const((1, LANES)), const((LANES, LANES))],
        out_specs=pl.BlockSpec((blk, q_width), cur(0)),
    )
    return pl.pallas_call(
        functools.partial(_swa_body, group=group),
        grid_spec=grid_spec,
        out_shape=jax.ShapeDtypeStruct((n, q_width), BF16),
        compiler_params=_cparams(("parallel", "arbitrary")),
        name="swa_attn",
    )(sinks, qkv, qkv, qkv, qkv, qkv, qg, kg, bd)


def _merge_route_body(a_ref, b_ref, ga_ref, gb_ref, x_ref, pr_ref, pa_ref, wo_ref, n2_ref,
                      rh_ref, rm_ref, rl_ref, rb_ref, x1_ref, h2_ref, route_ref):
    merged = (ga_ref[...].astype(F32) * _dot(a_ref[...], pr_ref[...])
              + gb_ref[...].astype(F32) * _dot(b_ref[...], pa_ref[...]))
    x1 = x_ref[...] + _dot(merged.astype(BF16), wo_ref[...])
    x1_ref[...] = x1
    ms = jnp.mean(x1 * x1, axis=-1, keepdims=True)
    h2 = x1 * lax.rsqrt(ms + NORM_EPS) * n2_ref[...]
    h2_ref[...] = h2

    hi, mid, lo = _split3(h2)
    rh, rm, rl = rh_ref[...], rm_ref[...], rl_ref[...]
    logits = (_dot(hi, rh) + (_dot(hi, rm) + _dot(mid, rh))
              + (_dot(hi, rl) + _dot(mid, rm) + _dot(lo, rh))) + rb_ref[...]

    lane = lax.broadcasted_iota(I32, logits.shape, 1)
    big = jnp.int32(1 << 20)
    neg = -jnp.inf
    is_coarse = lane < N_GROUPS
    cl = jnp.where(is_coarse, logits, neg)
    ce = jnp.exp(cl - jnp.max(cl, axis=-1, keepdims=True))
    cp = ce / jnp.sum(ce, axis=-1, keepdims=True)
    g_prob = jnp.max(cp, axis=-1, keepdims=True)
    g_idx = jnp.min(jnp.where(is_coarse & (cp == g_prob), lane, big), axis=-1, keepdims=True)

    fine_lane = lane - N_GROUPS
    in_group = ((lane >= N_GROUPS) & (lane < N_GROUPS + N_GROUPS * GROUP_SIZE)
                & ((fine_lane // GROUP_SIZE) == g_idx))
    fl = jnp.where(in_group, logits, neg)
    fe = jnp.exp(fl - jnp.max(fl, axis=-1, keepdims=True))
    fp = fe / jnp.sum(fe, axis=-1, keepdims=True)
    p1 = jnp.max(jnp.where(in_group, fp, -1.0), axis=-1, keepdims=True)
    i1 = jnp.min(jnp.where(in_group & (fp == p1), lane, big), axis=-1, keepdims=True)
    rest = in_group & (lane != i1)
    p2 = jnp.max(jnp.where(rest, fp, -1.0), axis=-1, keepdims=True)
    i2 = jnp.min(jnp.where(rest & (fp == p2), lane, big), axis=-1, keepdims=True)
    den = p1 + p2
    w1 = g_prob * p1 / den
    w2 = g_prob * p2 / den
    e1 = (i1 - N_GROUPS).astype(F32)
    e2 = (i2 - N_GROUPS).astype(F32)
    route_ref[...] = jnp.where(lane == 0, w1, jnp.where(lane == 1, w2,
                     jnp.where(lane == 2, e1, jnp.where(lane == 3, e2, 0.0))))


def merge_route(a_out, b_out, gates, x2d, proj_r, proj_a, w_out, norm2_w, router_w, router_b,
                *, tm=256):
    n, d = x2d.shape
    wa = a_out.shape[1]
    rh, rm, rl = _split3(router_w)
    const = lambda arr: pl.BlockSpec(arr.shape, lambda i: (0, 0), pipeline_mode=pl.Buffered(1))
    row = lambda w: pl.BlockSpec((tm, w), lambda i: (i, 0))
    return pl.pallas_call(
        _merge_route_body,
        grid=(n // tm,),
        in_specs=[row(wa), row(wa), pl.BlockSpec((tm, d), lambda i: (i, 0)),
                  pl.BlockSpec((tm, d), lambda i: (i, 1)), row(d),
                  const(proj_r), const(proj_a), const(w_out), const(norm2_w),
                  const(rh), const(rm), const(rl), const(router_b)],
        out_specs=[row(d), row(d), row(LANES)],
        out_shape=[jax.ShapeDtypeStruct((n, d), F32), jax.ShapeDtypeStruct((n, d), F32),
                   jax.ShapeDtypeStruct((n, LANES), F32)],
        compiler_params=_cparams(("parallel",)),
        name="merge_route",
    )(a_out, b_out, gates, gates, x2d, proj_r, proj_a, w_out, norm2_w, rh, rm, rl, router_b)


def _moe_rank_body(ids_ref, ut_ref, rank_ref, counts_ref, carry_ref):
    @pl.when(pl.program_id(0) == 0)
    def _():
        carry_ref[...] = jnp.zeros_like(carry_ref)

    ids = ids_ref[...]
    n_exp = carry_ref.shape[0]
    sub = lax.broadcasted_iota(I32, (n_exp, ids.shape[1]), 0)
    onehot = (sub == ids).astype(F32)
    before = _dot(onehot.astype(BF16), ut_ref[...])
    carry = carry_ref[...]
    rank_ref[...] = jnp.sum(onehot * (before + carry), axis=0, keepdims=True).astype(I32)
    carry = carry + jnp.sum(onehot, axis=1, keepdims=True)
    carry_ref[...] = carry
    counts_ref[...] = jnp.broadcast_to(carry, counts_ref.shape)


def moe_rank(ids_row, n_experts, *, tb=512):
    m = ids_row.shape[1]
    t = jnp.arange(tb)
    ut = (t[:, None] < t[None, :]).astype(BF16)
    return pl.pallas_call(
        _moe_rank_body,
        grid=(m // tb,),
        in_specs=[pl.BlockSpec((1, tb), lambda i: (0, i)),
                  pl.BlockSpec((tb, tb), lambda i: (0, 0))],
        out_specs=[pl.BlockSpec((1, tb), lambda i: (0, i)),
                   pl.BlockSpec((n_experts, LANES), lambda i: (0, 0))],
        out_shape=[jax.ShapeDtypeStruct((1, m), I32),
                   jax.ShapeDtypeStruct((n_experts, LANES), F32)],
        scratch_shapes=[pltpu.VMEM((n_experts, 1), F32)],
        compiler_params=_cparams(("arbitrary",)),
        name="moe_rank",
    )(ids_row, ut)


def _moe_dest_body(ids_ref, rank_ref, start_ref, dest_ref):
    ids = ids_ref[...]
    sub = lax.broadcasted_iota(I32, (start_ref.shape[0], ids.shape[1]), 0)
    start = jnp.sum(jnp.where(sub == ids, start_ref[...], 0.0), axis=0, keepdims=True)
    dest_ref[...] = start.astype(I32) + rank_ref[...]


def moe_dest(ids_row, rank_row, pad_start_col, *, tb=512):
    m = ids_row.shape[1]
    row = pl.BlockSpec((1, tb), lambda i: (0, i))
    return pl.pallas_call(
        _moe_dest_body,
        grid=(m // tb,),
        in_specs=[row, row, pl.BlockSpec(pad_start_col.shape, lambda i: (0, 0))],
        out_specs=row,
        out_shape=jax.ShapeDtypeStruct((1, m), I32),
        compiler_params=_cparams(("parallel",)),
        name="moe_dest",
    )(ids_row, rank_row, pad_start_col)


def _moe_dispatch_body(dest_ref, h_ref, buf_in_ref, buf_ref, sem):
    del buf_in_ref
    tm = h_ref.shape[0]
    base = pl.program_id(0) * tm * TOP_K

    def row_copy(r, k):
        slot = dest_ref[base + r * TOP_K + k]
        return pltpu.make_async_copy(h_ref.at[pl.ds(r, 1)], buf_ref.at[pl.ds(slot, 1)], sem)

    def start(r, c):
        for k in range(TOP_K):
            row_copy(r, k).start()
        return c

    def wait(r, c):
        for k in range(TOP_K):
            row_copy(r, k).wait()
        return c

    lax.fori_loop(0, tm, start, 0, unroll=4)
    lax.fori_loop(0, tm, wait, 0, unroll=4)


def moe_dispatch(dest, h2, n_slots, *, tm=256):
    n, d = h2.shape
    buf0 = jnp.zeros((n_slots, d), h2.dtype)
    grid_spec = pltpu.PrefetchScalarGridSpec(
        num_scalar_prefetch=1,
        grid=(n // tm,),
        in_specs=[pl.BlockSpec((tm, d), lambda i, *_: (i, 0)),
                  pl.BlockSpec(memory_space=pl.ANY)],
        out_specs=pl.BlockSpec(memory_space=pl.ANY),
        scratch_shapes=[pltpu.SemaphoreType.DMA(())],
    )
    return pl.pallas_call(
        _moe_dispatch_body,
        grid_spec=grid_spec,
        out_shape=jax.ShapeDtypeStruct((n_slots, d), h2.dtype),
        input_output_aliases={2: 0},
        compiler_params=_cparams(("arbitrary",)),
        name="moe_dispatch",
    )(dest, h2, buf0)


def _moe_expert_body(be_ref, nused_ref, x_ref, wg_ref, wu_ref, wd_ref, o_ref):
    del be_ref

    @pl.when(pl.program_id(0) < nused_ref[0])
    def _():
        x = x_ref[...].astype(BF16)
        hg = _dot(x, wg_ref[0])
        hu = _dot(x, wu_ref[0])
        act = (jax.nn.silu(hg) * hu).astype(BF16)
        o_ref[...] = _dot(act, wd_ref[0])

    @pl.when(pl.program_id(0) >= nused_ref[0])
    def _():
        o_ref[...] = jnp.zeros_like(o_ref)


def moe_experts(block_expert, n_used, buf, wg, wu, wd):
    p, d = buf.shape
    ff = wg.shape[2]
    nblk = p // MOE_BLOCK
    grid_spec = pltpu.PrefetchScalarGridSpec(
        num_scalar_prefetch=2,
        grid=(nblk,),
        in_specs=[pl.BlockSpec((MOE_BLOCK, d), lambda i, be, nu: (i, 0)),
                  pl.BlockSpec((1, d, ff), lambda i, be, nu: (be[i], 0, 0)),
                  pl.BlockSpec((1, d, ff), lambda i, be, nu: (be[i], 0, 0)),
                  pl.BlockSpec((1, ff, d), lambda i, be, nu: (be[i], 0, 0))],
        out_specs=pl.BlockSpec((MOE_BLOCK, d), lambda i, be, nu: (i, 0)),
    )
    return pl.pallas_call(
        _moe_expert_body,
        grid_spec=grid_spec,
        out_shape=jax.ShapeDtypeStruct((p, d), F32),
        compiler_params=_cparams(("arbitrary",)),
        name="moe_experts",
    )(block_expert, n_used, buf, wg, wu, wd)


def _moe_combine_body(dest_ref, x1_ref, route_ref, out_ref, y_ref, rows_ref, sem):
    tm = x1_ref.shape[0]
    base = pl.program_id(0) * tm * TOP_K

    def row_copy(r, k):
        slot = dest_ref[base + r * TOP_K + k]
        return pltpu.make_async_copy(out_ref.at[pl.ds(slot, 1)], rows_ref.at[k, pl.ds(r, 1)], sem)

    def start(r, c):
        for k in range(TOP_K):
            row_copy(r, k).start()
        return c

    def wait(r, c):
        for k in range(TOP_K):
            row_copy(r, k).wait()
        return c

    lax.fori_loop(0, tm, start, 0, unroll=4)
    lax.fori_loop(0, tm, wait, 0, unroll=4)
    route = route_ref[...]
    y_ref[...] = x1_ref[...] + (rows_ref[0] * route[:, 0:1] + rows_ref[1] * route[:, 1:2])


def moe_combine(dest, x1, route, expert_out, *, tm=256):
    n, d = x1.shape
    grid_spec = pltpu.PrefetchScalarGridSpec(
        num_scalar_prefetch=1,
        grid=(n // tm,),
        in_specs=[pl.BlockSpec((tm, d), lambda i, *_: (i, 0)),
                  pl.BlockSpec((tm, LANES), lambda i, *_: (i, 0)),
                  pl.BlockSpec(memory_space=pl.ANY)],
        out_specs=pl.BlockSpec((tm, d), lambda i, *_: (i, 0)),
        scratch_shapes=[pltpu.VMEM((TOP_K, tm, d), F32), pltpu.SemaphoreType.DMA(())],
    )
    return pl.pallas_call(
        _moe_combine_body,
        grid_spec=grid_spec,
        out_shape=jax.ShapeDtypeStruct((n, d), F32),
        compiler_params=_cparams(("arbitrary",)),
        name="moe_combine",
    )(dest, x1, route, expert_out)


def hierarchical_moe(x1, h2, route, wg, wu, wd):
    n, d = h2.shape
    n_experts = wg.shape[0]
    m = n * TOP_K
    ids_row = route[:, 2:4].astype(I32).reshape(1, m)
    rank_row, counts = moe_rank(ids_row, n_experts)
    counts = counts[:, 0].astype(I32)
    padded = (counts + MOE_BLOCK - 1) // MOE_BLOCK * MOE_BLOCK
    pad_end = jnp.cumsum(padded)
    pad_start = pad_end - padded
    dest = moe_dest(ids_row, rank_row, pad_start.astype(F32).reshape(n_experts, 1)).reshape(m)
    n_slots = m + n_experts * MOE_BLOCK
    nblk = n_slots // MOE_BLOCK
    n_used = (pad_end[-1] // MOE_BLOCK).astype(I32).reshape(1)
    block_start = jnp.arange(nblk, dtype=I32) * MOE_BLOCK
    block_expert = jnp.minimum(jnp.sum(pad_end[None, :] <= block_start[:, None], axis=1),
                               n_experts - 1).astype(I32)
    last_expert = block_expert[jnp.maximum(n_used[0] - 1, 0)]
    block_expert = jnp.where(jnp.arange(nblk) < n_used[0], block_expert, last_expert)
    buf = moe_dispatch(dest, h2, n_slots)
    expert_out = moe_experts(block_expert, n_used, buf, wg, wu, wd)
    return moe_combine(dest, x1, route, expert_out)


def _pad_cols(w, to):
    return jnp.pad(w, ((0, 0), (0, to - w.shape[1])))


def _pad_rows(w, to):
    return jnp.pad(w, ((0, to - w.shape[0]), (0, 0)))


def _layer(x2d, batch, norm1_w, w_in, mu, w0, w2, a0, a2, g2, k_k, k_a, r_k, gn_w, gn_b,
           q_norm_w, k_norm_w, sinks, proj_rwkv, proj_attn, w_out, norm2_w, wc, bc, wf, bf,
           wg, wu, wd):
    n, d = x2d.shape
    seq = n // batch
    width = w0.shape[0]
    dl, il, gl = w2.shape[0], a2.shape[0], g2.shape[0]
    q_width = proj_attn.shape[0]
    rwkv_cols = 3 * width + dl + il + gl
    kv_width = (w_in.shape[1] - rwkv_cols - q_width - 2 * d) // 2
    row = lambda v: v.reshape(1, -1).astype(F32)

    c0 = 3 * width
    w_rwkv = jnp.concatenate([w_in[:, :c0], _pad_cols(w_in[:, c0:c0 + dl], LANES),
                              _pad_cols(w_in[:, c0 + dl:c0 + dl + il], LANES),
                              w_in[:, c0 + dl + il:rwkv_cols]], axis=1).astype(BF16)
    mu_p = jnp.concatenate([mu[:c0], jnp.pad(mu[c0:c0 + dl], (0, LANES - dl)),
                            jnp.pad(mu[c0 + dl:c0 + dl + il], (0, LANES - il)),
                            mu[c0 + dl + il:]]).reshape(1, -1)
    q0 = rwkv_cols
    w_qkv = w_in[:, q0:q0 + q_width + 2 * kv_width].astype(BF16)
    w_gates = w_in[:, q0 + q_width + 2 * kv_width:].astype(BF16)

    g1 = row(norm1_w)
    p_rwkv = norm_proj(x2d, g1, w_rwkv, out_dtype=F32, sigmoid=False)
    qkv = norm_proj(x2d, g1, w_qkv, out_dtype=BF16, sigmoid=False)
    gates = norm_proj(x2d, g1, w_gates, out_dtype=BF16, sigmoid=True)

    rt, at, bt, kt, vv, g, bonus, pc = rwkv_prep(
        p_rwkv, mu_p, row(w0), _pad_rows(w2, LANES).astype(BF16), row(a0),
        _pad_rows(a2, LANES).astype(BF16), g2.astype(BF16), row(k_k), row(k_a), row(r_k),
        seq_len=seq)
    a_out = rwkv_scan(rt, at, bt, kt, vv, pc, g, bonus, row(gn_w), row(gn_b), batch=batch)

    b_out = swa_attention(qkv, q_norm_w, k_norm_w, sinks.astype(F32), batch=batch,
                          q_width=q_width, kv_width=kv_width)

    n_groups, n_experts = wc.shape[1], wf.shape[1]
    router_w = _pad_cols(jnp.concatenate([wc, wf], axis=1), LANES)
    router_b = _pad_cols(jnp.concatenate([bc, bf]).reshape(1, -1), LANES)
    assert n_groups == N_GROUPS and n_experts == N_GROUPS * GROUP_SIZE
    x1, h2, route = merge_route(a_out, b_out, gates, x2d, proj_rwkv.astype(BF16),
                                proj_attn.astype(BF16), w_out.astype(BF16), row(norm2_w),
                                router_w, router_b)
    return hierarchical_moe(x1, h2, route, wg.astype(BF16), wu.astype(BF16), wd.astype(BF16))


def kernel(x, norm1_w, w_in, rwkv_mu, rwkv_w0, rwkv_w2, rwkv_a0, rwkv_a2, rwkv_g2, rwkv_k_k,
           rwkv_k_a, rwkv_r_k, rwkv_gn_w, rwkv_gn_b, q_norm_w, k_norm_w, attn_sinks, proj_rwkv,
           proj_attn, w_out, norm2_w, router_coarse_w, router_coarse_b, router_fine_w,
           router_fine_b, expert_w_gate, expert_w_up, expert_w_down):
    batch, seq, d = x.shape
    x2d = x.reshape(batch * seq, d)
    for layer in range(norm1_w.shape[0]):
        x2d = _layer(x2d, batch, norm1_w[layer], w_in[layer], rwkv_mu[layer], rwkv_w0[layer],
                     rwkv_w2[layer], rwkv_a0[layer], rwkv_a2[layer], rwkv_g2[layer],
                     rwkv_k_k[layer], rwkv_k_a[layer], rwkv_r_k[layer].reshape(-1),
                     rwkv_gn_w[layer], rwkv_gn_b[layer], q_norm_w[layer], k_norm_w[layer],
                     attn_sinks[layer], proj_rwkv[layer], proj_attn[layer], w_out[layer],
                     norm2_w[layer], router_coarse_w[layer], router_coarse_b[layer],
                     router_fine_w[layer], router_fine_b[layer], expert_w_gate[layer],
                     expert_w_up[layer], expert_w_down[layer])
    return x2d.reshape(batch, seq, d)
```

```python
import functools

import jax
import jax.numpy as jnp
from jax import lax
from jax.experimental import pallas as pl
from jax.experimental.pallas import tpu as pltpu

F32 = jnp.float32
BF16 = jnp.bfloat16
I32 = jnp.int32

NORM_EPS = 1e-6
GN_EPS = 64e-5
HEAD_DIM = 64
LANES = 128
CHUNK = 64
WINDOW = 128
MOE_BLOCK = 128
SUPER_ROWS = 1024
FF_TILE = 512
TOP_K = 2
N_GROUPS = 8
GROUP_SIZE = 8
VMEM_LIMIT = 56 * 1024 * 1024

_NT = (((1,), (1,)), ((), ()))
_TN = (((0,), (0,)), ((), ()))


def _dot(a, b):
    return jnp.dot(a, b, preferred_element_type=F32)


def _split2(x):
    hi = x.astype(BF16)
    mid = (x - hi.astype(F32)).astype(BF16)
    return hi, mid


def _split3(x):
    hi = x.astype(BF16)
    r1 = x - hi.astype(F32)
    mid = r1.astype(BF16)
    lo = (r1 - mid.astype(F32)).astype(BF16)
    return hi, mid, lo


def _dot_exact_rhs(x, m):
    hi, mid, lo = _split3(x)
    return _dot(hi, m) + _dot(mid, m) + _dot(lo, m)


def _exact_lhs_dot(m, x):
    hi, mid, lo = _split3(x)
    return _dot(m, hi) + _dot(m, mid) + _dot(m, lo)


def _pack_bf16_halves(x):
    w = x.shape[1] // 2
    lo = lax.bitcast_convert_type(x[:, :w].astype(BF16).astype(F32), jnp.uint32)
    hi = lax.bitcast_convert_type(x[:, w:].astype(BF16).astype(F32), jnp.uint32)
    return (lo >> 16) | (hi & jnp.uint32(0xFFFF0000))


def _unpack_bf16_halves(xp):
    lo = lax.bitcast_convert_type(xp << 16, F32).astype(BF16)
    hi = lax.bitcast_convert_type(xp & jnp.uint32(0xFFFF0000), F32).astype(BF16)
    return lo, hi


def _cparams(sem, vmem=VMEM_LIMIT):
    return pltpu.CompilerParams(dimension_semantics=sem, vmem_limit_bytes=vmem)


def _norm_proj_body(x_ref, g_ref, w_ref, o_ref, h_ref, *, sigmoid):
    @pl.when(pl.program_id(1) == 0)
    def _():
        x = x_ref[...]
        ms = jnp.mean(x * x, axis=-1, keepdims=True)
        h_ref[...] = (x * lax.rsqrt(ms + NORM_EPS) * g_ref[...]).astype(BF16)

    acc = _dot(h_ref[...], w_ref[...])
    if sigmoid:
        acc = jax.nn.sigmoid(acc)
    o_ref[...] = acc.astype(o_ref.dtype)


def norm_proj(x2d, gain, w, *, out_dtype, sigmoid, tm=1024, tn=512):
    n, d = x2d.shape
    c = w.shape[1]
    tm = min(tm, n)
    return pl.pallas_call(
        functools.partial(_norm_proj_body, sigmoid=sigmoid),
        grid=(n // tm, c // tn),
        in_specs=[pl.BlockSpec((tm, d), lambda i, j: (i, 0)),
                  pl.BlockSpec((1, d), lambda i, j: (0, 0)),
                  pl.BlockSpec((d, tn), lambda i, j: (0, j))],
        out_specs=pl.BlockSpec((tm, tn), lambda i, j: (i, j)),
        out_shape=jax.ShapeDtypeStruct((n, c), out_dtype),
        scratch_shapes=[pltpu.VMEM((tm, d), BF16)],
        compiler_params=_cparams(("parallel", "arbitrary")),
        name="norm_proj",
    )(x2d, gain, w)


def _head_sum(x, e, et):
    s = _dot(x.astype(BF16), e)
    hi, mid = _split2(s)
    return _dot(hi, et) + _dot(mid, et)


def _rwkv_prep_body(p_ref, pprev_ref, mu_ref, w0_ref, w2_ref, a0_ref, a2_ref, g2_ref,
                    kk_ref, ka_ref, rk_ref, e_ref, et_ref, tri_ref, sel_ref,
                    rt_ref, at_ref, bt_ref, kt_ref, v_ref, g_ref, bonus_ref, pc_ref,
                    *, seq_len, width):
    tm = p_ref.shape[0]
    w_ = width
    first = (pl.program_id(0) * tm) % seq_len == 0
    p = p_ref[...]
    prev_row = jnp.where(first, 0.0, pprev_ref[7:8, :])
    row = lax.broadcasted_iota(I32, (tm, 1), 0)
    shifted = jnp.where(row == 0, prev_row, pltpu.roll(p, 1, 0))
    m = p + (shifted - p) * mu_ref[...]
    r = m[:, 0:w_]
    k = m[:, w_:2 * w_]
    v = m[:, 2 * w_:3 * w_]
    xw = m[:, 3 * w_:3 * w_ + 128]
    xa = m[:, 3 * w_ + 128:3 * w_ + 256]
    xg = m[:, 3 * w_ + 256:]

    z = -(w0_ref[...] + _dot(jnp.tanh(xw).astype(BF16), w2_ref[...]))
    softplus = jnp.maximum(z, 0.0) + jnp.log1p(jnp.exp(-jnp.abs(z)))
    logw = -jnp.exp(-softplus - 0.5)
    a = jax.nn.sigmoid(a0_ref[...] + _dot(xa.astype(BF16), a2_ref[...]))
    g = _dot(jax.nn.sigmoid(xg).astype(BF16), g2_ref[...])

    e = e_ref[...]
    et = et_ref[...]
    kk = k * kk_ref[...]
    kk = kk / jnp.maximum(jnp.sqrt(_head_sum(kk * kk, e, et)), 1e-12)
    kmod = k * (1.0 + (a - 1.0) * ka_ref[...])
    bonus = _head_sum(r * kmod * rk_ref[...], e, et) * v

    cum = _exact_lhs_dot(tri_ref[...], logw)
    pc_ref[...] = jnp.exp(_exact_lhs_dot(sel_ref[...], logw))
    inv = jnp.exp(-cum)
    rt_ref[...] = (r * jnp.exp(cum)).astype(BF16)
    at_ref[...] = (-kk * jnp.exp(cum - logw)).astype(BF16)
    bt_ref[...] = (kk * a * inv).astype(BF16)
    kt_ref[...] = (kmod * inv).astype(BF16)
    v_ref[...] = v.astype(BF16)
    g_ref[...] = g.astype(BF16)
    bonus_ref[...] = bonus


def _head_indicator(width):
    heads = width // HEAD_DIM
    c = jnp.arange(width)[:, None] // HEAD_DIM
    h = jnp.arange(LANES)[None, :]
    e = (c == h).astype(BF16)
    assert heads <= LANES
    return e, e.T


def rwkv_prep(p, mu, w0, w2, a0, a2, g2, k_k, k_a, r_k, *, seq_len, tm=512):
    n, cols = p.shape
    width = w0.shape[1]
    nchunk = tm // CHUNK
    e, et = _head_indicator(width)
    t = jnp.arange(tm)
    same = (t[:, None] // CHUNK) == (t[None, :] // CHUNK)
    tri = (same & (t[:, None] >= t[None, :])).astype(BF16)
    sel = ((t[None, :] // CHUNK) == jnp.arange(nchunk)[:, None]).astype(BF16)
    const = lambda shape: pl.BlockSpec(shape, lambda i: (0, 0))
    stream = lambda dt: jax.ShapeDtypeStruct((n, width), dt)
    outs = pl.pallas_call(
        functools.partial(_rwkv_prep_body, seq_len=seq_len, width=width),
        grid=(n // tm,),
        in_specs=[pl.BlockSpec((tm, cols), lambda i: (i, 0)),
                  pl.BlockSpec((8, cols), lambda i: (jnp.maximum(i * (tm // 8) - 1, 0), 0)),
                  const((1, cols)), const((1, width)), const(w2.shape), const((1, width)),
                  const(a2.shape), const(g2.shape), const((1, width)), const((1, width)),
                  const((1, width)), const(e.shape), const(et.shape), const(tri.shape),
                  const(sel.shape)],
        out_specs=[pl.BlockSpec((tm, width), lambda i: (i, 0))] * 7
                  + [pl.BlockSpec((nchunk, width), lambda i: (i, 0))],
        out_shape=[stream(BF16)] * 6 + [stream(F32),
                   jax.ShapeDtypeStruct((n // CHUNK, width), F32)],
        compiler_params=_cparams(("parallel",)),
        name="rwkv_prep",
    )(p, p, mu, w0, w2, a0, a2, g2, k_k, k_a, r_k, e, et, tri, sel)
    return outs


def _chunk_pairs(rts, ats, bts, kts, vvs, pcs, s_prevs, masks):
    lo, strict, incl, eye = masks
    c = rts[0].shape[0]
    c2 = 2 * c
    zero = jnp.zeros_like(rts[0])
    each = lambda f, *ls: [f(*a) for a in zip(*ls)]

    def stack(x):
        return jnp.concatenate([jnp.where(lo, x, zero), jnp.where(lo, zero, x)], axis=0)

    ar = each(lambda a, r: jnp.concatenate([stack(a), stack(r)], axis=0), ats, rts)
    bk = each(lambda b, k: jnp.concatenate([stack(b), stack(k)], axis=0), bts, kts)
    vb = each(stack, vvs)
    gram = each(lambda x, y: lax.dot_general(x, y, _NT, preferred_element_type=F32), ar, bk)
    ars = each(lambda x, s: lax.dot_general(x, s.astype(BF16), _NT, preferred_element_type=F32),
               ar, s_prevs)
    a_ab = each(lambda g: jnp.where(strict, g[:c2, :c2], 0.0), gram)
    rhs = each(lambda g, v, x: x[:c2] + _dot(jnp.where(strict, g[:c2, c2:], 0.0).astype(BF16), v),
               gram, vb, ars)

    t_inv = each(lambda n: eye + n, a_ab)
    nk = a_ab
    for _ in range((c - 1).bit_length() - 1):
        nkb = each(lambda n: n.astype(BF16), nk)
        nk = each(lambda n: _dot(n, n), nkb)
        t_inv = each(lambda t, n: t + _dot(t.astype(BF16), n.astype(BF16)), t_inv, nk)
    u = each(lambda t, r: _dot(t.astype(BF16), r.astype(BF16)), t_inv, rhs)

    uv = each(lambda uu, v: jnp.concatenate([uu.astype(BF16), v], axis=0), u, vb)
    a_r = each(lambda g: jnp.where(incl, g[c2:, :], 0.0).astype(BF16), gram)
    ybd = each(lambda x, a, w: x[c2:] + _dot(a, w), ars, a_r, uv)
    ys = each(lambda yb: yb[:c] + yb[c:], ybd)

    bk_end = each(lambda x, pc: (x.astype(F32) * pc).astype(BF16), bk, pcs)
    s_new = each(lambda s, pc, w, x: s * pc + lax.dot_general(w, x, _TN, preferred_element_type=F32),
                 s_prevs, pcs, uv, bk_end)
    return ys, s_new


def _rwkv_scan_body(rt_ref, at_ref, bt_ref, kt_ref, v_ref, pc_ref, g_ref, bonus_ref,
                    gnw_ref, gnb_ref, e_ref, et_ref, o_ref, s_ref, y_ref):
    tr, width = rt_ref.shape
    npairs = width // LANES

    @pl.when(pl.program_id(1) == 0)
    def _():
        s_ref[...] = jnp.zeros_like(s_ref)

    lane = lax.broadcasted_iota(I32, (CHUNK, LANES), 1)
    lo = lane < HEAD_DIM
    ri = lax.broadcasted_iota(I32, (2 * CHUNK, 2 * CHUNK), 0)
    ci = lax.broadcasted_iota(I32, (2 * CHUNK, 2 * CHUNK), 1)
    same = (ri // CHUNK) == (ci // CHUNK)
    ri2 = lax.broadcasted_iota(I32, (2 * CHUNK, 4 * CHUNK), 0)
    ci2 = lax.broadcasted_iota(I32, (2 * CHUNK, 4 * CHUNK), 1) % (2 * CHUNK)
    incl = ((ri2 // CHUNK) == (ci2 // CHUNK)) & (ri2 >= ci2)
    masks = (lo, same & (ri > ci), incl, (ri == ci).astype(F32))

    def chunk_step(c, carry):
        rows = pl.ds(pl.multiple_of(c * CHUNK, CHUNK), CHUNK)
        this_chunk = lax.broadcasted_iota(I32, (pc_ref.shape[0], 1), 0) == c
        cols = [slice(pr * LANES, (pr + 1) * LANES) for pr in range(npairs)]
        pcs = [jnp.sum(jnp.where(this_chunk, pc_ref[:, cl], 0.0), axis=0, keepdims=True)
               for cl in cols]
        load = lambda ref: [ref[rows, cl] for cl in cols]
        ys, s_new = _chunk_pairs(load(rt_ref), load(at_ref), load(bt_ref), load(kt_ref),
                                 load(v_ref), pcs, [s_ref[pr] for pr in range(npairs)], masks)
        for pr in range(npairs):
            s_ref[pr] = s_new[pr]
            y_ref[rows, cols[pr]] = ys[pr]
        return carry

    lax.fori_loop(0, tr // CHUNK, chunk_step, 0)

    e = e_ref[...]
    et = et_ref[...]
    y = y_ref[...]
    mean = _head_sum(y, e, et) * (1.0 / HEAD_DIM)
    d = y - mean
    var = _head_sum(d * d, e, et) * (1.0 / HEAD_DIM)
    out = d * lax.rsqrt(var + GN_EPS) * gnw_ref[...] + gnb_ref[...]
    out = (out + bonus_ref[...]) * g_ref[...].astype(F32)
    o_ref[...] = out.astype(o_ref.dtype)


def rwkv_scan(rt, at, bt, kt, vv, pc, g, bonus, gn_w, gn_b, *, batch, tr=512):
    n, width = rt.shape
    seq = n // batch
    nst = seq // tr
    e, et = _head_indicator(width)
    blk = lambda rows: pl.BlockSpec((rows, width), lambda b, s: (b * nst + s, 0))
    const = lambda shape: pl.BlockSpec(shape, lambda b, s: (0, 0))
    return pl.pallas_call(
        _rwkv_scan_body,
        grid=(batch, nst),
        in_specs=[blk(tr)] * 5 + [blk(tr // CHUNK), blk(tr), blk(tr),
                  const((1, width)), const((1, width)), const(e.shape), const(et.shape)],
        out_specs=blk(tr),
        out_shape=jax.ShapeDtypeStruct((n, width), BF16),
        scratch_shapes=[pltpu.VMEM((width // LANES, LANES, LANES), F32),
                        pltpu.VMEM((tr, width), F32)],
        compiler_params=_cparams(("parallel", "arbitrary")),
        name="rwkv_scan",
    )(rt, at, bt, kt, vv, pc, g, bonus, gn_w, gn_b, e, et)


def _head_rmsnorm(x, gain, bd):
    hi, mid = _split2(x * x)
    ms = (_dot(hi, bd) + _dot(mid, bd)) * (1.0 / HEAD_DIM)
    return x * lax.rsqrt(ms + NORM_EPS) * gain


def _swa_body(sink_ref, q_ref, kp_ref, kc_ref, vp_ref, vc_ref, qg_ref, kg_ref, bd_ref, o_ref,
              *, group):
    blk = q_ref.shape[0]
    n = pl.program_id(1)
    bd = bd_ref[...]
    scale = HEAD_DIM ** -0.5
    lane = lax.broadcasted_iota(I32, (1, LANES), 1)
    lo = lane < HEAD_DIM
    ri = lax.broadcasted_iota(I32, (2 * blk, 2 * blk), 0)
    cj = lax.broadcasted_iota(I32, (2 * blk, 2 * blk), 1)
    rel = blk + (ri % blk) - cj
    jmin = jnp.where(n > 0, 0, blk)
    valid = (rel >= 0) & (rel < WINDOW) & (cj >= jmin)
    top_rows = lax.broadcasted_iota(I32, (2 * blk, 1), 0) < blk

    kcat = jnp.concatenate([kp_ref[...], kc_ref[...]], axis=0).astype(F32)
    vcat = jnp.concatenate([vp_ref[...], vc_ref[...]], axis=0).astype(F32)
    kv_heads = kcat.shape[1] // HEAD_DIM
    q_tiles = q_ref.shape[1] // LANES
    tiles_per_kv = group // 2
    each = lambda f, *ls: [f(*a) for a in zip(*ls)]

    kv_cols = [slice(jt * LANES, (jt + 1) * LANES) for jt in range(kv_heads // 2)]
    kn = [_head_rmsnorm(kcat[:, c], kg_ref[...], bd) for c in kv_cols]
    kn_r = [pltpu.roll(x, HEAD_DIM, 1) for x in kn]
    vt = [vcat[:, c] for c in kv_cols]
    vt_r = [pltpu.roll(x, HEAD_DIM, 1) for x in vt]
    own = lambda hk: lo if hk % 2 == 0 else jnp.logical_not(lo)
    k2 = [jnp.where(own(hk), kn[hk // 2], kn_r[hk // 2]).astype(BF16) for hk in range(kv_heads)]
    v2 = [jnp.where(own(hk), vt[hk // 2], vt_r[hk // 2]).astype(BF16) for hk in range(kv_heads)]

    q_cols = [slice(t * LANES, (t + 1) * LANES) for t in range(q_tiles)]
    qn = [_head_rmsnorm(q_ref[:, c].astype(F32), qg_ref[...] * scale, bd) for c in q_cols]
    qst = [jnp.concatenate([jnp.where(lo, x, 0.0), jnp.where(lo, 0.0, x)], axis=0).astype(BF16)
           for x in qn]
    s = [jnp.where(valid, lax.dot_general(x, k2[t // tiles_per_kv], _NT,
                                          preferred_element_type=F32), -jnp.inf)
         for t, x in enumerate(qst)]
    sink = [jnp.where(top_rows, sink_ref[2 * t], sink_ref[2 * t + 1]) for t in range(q_tiles)]
    mx = each(lambda x, sk: jnp.maximum(jnp.max(x, axis=-1, keepdims=True), sk), s, sink)
    pr = each(lambda x, m: jnp.exp(x - m), s, mx)
    inv = each(lambda p, sk, m: 1.0 / (jnp.sum(p, axis=-1, keepdims=True) + jnp.exp(sk - m)),
               pr, sink, mx)
    o = [_dot((p * r).astype(BF16), v2[t // tiles_per_kv])
         for t, (p, r) in enumerate(zip(pr, inv))]
    for t in range(q_tiles):
        o_ref[:, q_cols[t]] = jnp.where(lo, o[t][:blk], o[t][blk:]).astype(o_ref.dtype)


def swa_attention(qkv, q_gain, k_gain, sinks, *, batch, q_width, kv_width):
    n = qkv.shape[0]
    blk = WINDOW
    nb = n // batch // blk
    group = (q_width // HEAD_DIM) // (kv_width // HEAD_DIM)
    kcol = q_width // kv_width
    t = jnp.arange(LANES)
    bd = ((t[:, None] // HEAD_DIM) == (t[None, :] // HEAD_DIM)).astype(BF16)
    qg = jnp.tile(q_gain.reshape(1, HEAD_DIM), (1, 2))
    kg = jnp.tile(k_gain.reshape(1, HEAD_DIM), (1, 2))
    cur = lambda col: (lambda b, i, *_: (b * nb + i, col))
    prev = lambda col: (lambda b, i, *_: (b * nb + jnp.maximum(i - 1, 0), col))
    const = lambda shape: pl.BlockSpec(shape, lambda b, i, *_: (0, 0))
    grid_spec = pltpu.PrefetchScalarGridSpec(
        num_scalar_prefetch=1,
        grid=(batch, nb),
        in_specs=[pl.BlockSpec((blk, q_width), cur(0)),
                  pl.BlockSpec((blk, kv_width), prev(kcol)),
                  pl.BlockSpec((blk, kv_width), cur(kcol)),
                  pl.BlockSpec((blk, kv_width), prev(kcol + 1)),
                  pl.BlockSpec((blk, kv_width), cur(kcol + 1)),
                  const((1, LANES)), const((1, LANES)), const((LANES, LANES))],
        out_specs=pl.BlockSpec((blk, q_width), cur(0)),
    )
    return pl.pallas_call(
        functools.partial(_swa_body, group=group),
        grid_spec=grid_spec,
        out_shape=jax.ShapeDtypeStruct((n, q_width), BF16),
        compiler_params=_cparams(("parallel", "arbitrary")),
        name="swa_attn",
    )(sinks, qkv, qkv, qkv, qkv, qkv, qg, kg, bd)


def _merge_route_body(a_ref, b_ref, ga_ref, gb_ref, x_ref, pr_ref, pa_ref, wo_ref, n2_ref,
                      rh_ref, rm_ref, rl_ref, rb_ref, x1_ref, h2_ref, route_ref):
    merged = (ga_ref[...].astype(F32) * _dot(a_ref[...], pr_ref[...])
              + gb_ref[...].astype(F32) * _dot(b_ref[...], pa_ref[...]))
    x1 = x_ref[...] + _dot(merged.astype(BF16), wo_ref[...])
    x1_ref[...] = x1
    ms = jnp.mean(x1 * x1, axis=-1, keepdims=True)
    h2 = x1 * lax.rsqrt(ms + NORM_EPS) * n2_ref[...]
    h2_ref[...] = _pack_bf16_halves(h2)

    hi, mid, lo = _split3(h2)
    rh, rm, rl = rh_ref[...], rm_ref[...], rl_ref[...]
    logits = (_dot(hi, rh) + (_dot(hi, rm) + _dot(mid, rh))
              + (_dot(hi, rl) + _dot(mid, rm) + _dot(lo, rh))) + rb_ref[...]

    lane = lax.broadcasted_iota(I32, logits.shape, 1)
    big = jnp.int32(1 << 20)
    neg = -jnp.inf
    is_coarse = lane < N_GROUPS
    cl = jnp.where(is_coarse, logits, neg)
    ce = jnp.exp(cl - jnp.max(cl, axis=-1, keepdims=True))
    cp = ce / jnp.sum(ce, axis=-1, keepdims=True)
    g_prob = jnp.max(cp, axis=-1, keepdims=True)
    g_idx = jnp.min(jnp.where(is_coarse & (cp == g_prob), lane, big), axis=-1, keepdims=True)

    fine_lane = lane - N_GROUPS
    in_group = ((lane >= N_GROUPS) & (lane < N_GROUPS + N_GROUPS * GROUP_SIZE)
                & ((fine_lane // GROUP_SIZE) == g_idx))
    fl = jnp.where(in_group, logits, neg)
    fe = jnp.exp(fl - jnp.max(fl, axis=-1, keepdims=True))
    fp = fe / jnp.sum(fe, axis=-1, keepdims=True)
    p1 = jnp.max(jnp.where(in_group, fp, -1.0), axis=-1, keepdims=True)
    i1 = jnp.min(jnp.where(in_group & (fp == p1), lane, big), axis=-1, keepdims=True)
    rest = in_group & (lane != i1)
    p2 = jnp.max(jnp.where(rest, fp, -1.0), axis=-1, keepdims=True)
    i2 = jnp.min(jnp.where(rest & (fp == p2), lane, big), axis=-1, keepdims=True)
    den = p1 + p2
    w1 = g_prob * p1 / den
    w2 = g_prob * p2 / den
    e1 = (i1 - N_GROUPS).astype(F32)
    e2 = (i2 - N_GROUPS).astype(F32)
    route_ref[...] = jnp.where(lane == 0, w1, jnp.where(lane == 1, w2,
                     jnp.where(lane == 2, e1, jnp.where(lane == 3, e2, 0.0))))


def merge_route(a_out, b_out, gates, x2d, proj_r, proj_a, w_out, norm2_w, router_w, router_b,
                *, tm=256):
    n, d = x2d.shape
    wa = a_out.shape[1]
    rh, rm, rl = _split3(router_w)
    const = lambda arr: pl.BlockSpec(arr.shape, lambda i: (0, 0), pipeline_mode=pl.Buffered(1))
    row = lambda w: pl.BlockSpec((tm, w), lambda i: (i, 0))
    return pl.pallas_call(
        _merge_route_body,
        grid=(n // tm,),
        in_specs=[row(wa), row(wa), pl.BlockSpec((tm, d), lambda i: (i, 0)),
                  pl.BlockSpec((tm, d), lambda i: (i, 1)), row(d),
                  const(proj_r), const(proj_a), const(w_out), const(norm2_w),
                  const(rh), const(rm), const(rl), const(router_b)],
        out_specs=[row(d), row(d // 2), row(LANES)],
        out_shape=[jax.ShapeDtypeStruct((n, d), F32), jax.ShapeDtypeStruct((n, d // 2), jnp.uint32),
                   jax.ShapeDtypeStruct((n, LANES), F32)],
        compiler_params=_cparams(("parallel",)),
        name="merge_route",
    )(a_out, b_out, gates, gates, x2d, proj_r, proj_a, w_out, norm2_w, rh, rm, rl, router_b)


def _moe_rank_body(ids_ref, ut_ref, rank_ref, counts_ref, carry_ref):
    @pl.when(pl.program_id(0) == 0)
    def _():
        carry_ref[...] = jnp.zeros_like(carry_ref)

    ids = ids_ref[...]
    n_exp = carry_ref.shape[0]
    sub = lax.broadcasted_iota(I32, (n_exp, ids.shape[1]), 0)
    onehot = (sub == ids).astype(F32)
    before = _dot(onehot.astype(BF16), ut_ref[...])
    carry = carry_ref[...]
    rank_ref[...] = jnp.sum(onehot * (before + carry), axis=0, keepdims=True).astype(I32)
    carry = carry + jnp.sum(onehot, axis=1, keepdims=True)
    carry_ref[...] = carry
    counts_ref[...] = jnp.broadcast_to(carry, counts_ref.shape)


def moe_rank(ids_row, n_experts, *, tb=512):
    m = ids_row.shape[1]
    t = jnp.arange(tb)
    ut = (t[:, None] < t[None, :]).astype(BF16)
    return pl.pallas_call(
        _moe_rank_body,
        grid=(m // tb,),
        in_specs=[pl.BlockSpec((1, tb), lambda i: (0, i)),
                  pl.BlockSpec((tb, tb), lambda i: (0, 0))],
        out_specs=[pl.BlockSpec((1, tb), lambda i: (0, i)),
                   pl.BlockSpec((n_experts, LANES), lambda i: (0, 0))],
        out_shape=[jax.ShapeDtypeStruct((1, m), I32),
                   jax.ShapeDtypeStruct((n_experts, LANES), F32)],
        scratch_shapes=[pltpu.VMEM((n_experts, 1), F32)],
        compiler_params=_cparams(("arbitrary",)),
        name="moe_rank",
    )(ids_row, ut)


def _moe_dest_body(ids_ref, rank_ref, start_ref, dest_ref):
    ids = ids_ref[...]
    sub = lax.broadcasted_iota(I32, (start_ref.shape[0], ids.shape[1]), 0)
    start = jnp.sum(jnp.where(sub == ids, start_ref[...], 0.0), axis=0, keepdims=True)
    dest_ref[...] = start.astype(I32) + rank_ref[...]


def moe_dest(ids_row, rank_row, pad_start_col, *, tb=512):
    m = ids_row.shape[1]
    row = pl.BlockSpec((1, tb), lambda i: (0, i))
    return pl.pallas_call(
        _moe_dest_body,
        grid=(m // tb,),
        in_specs=[row, row, pl.BlockSpec(pad_start_col.shape, lambda i: (0, 0))],
        out_specs=row,
        out_shape=jax.ShapeDtypeStruct((1, m), I32),
        compiler_params=_cparams(("parallel",)),
        name="moe_dest",
    )(ids_row, rank_row, pad_start_col)


def _moe_dispatch_body(dest_ref, h_ref, buf_in_ref, buf_ref, sem):
    del buf_in_ref
    tm = h_ref.shape[0]
    base = pl.program_id(0) * tm * TOP_K

    def row_copy(r, k):
        slot = dest_ref[base + r * TOP_K + k]
        return pltpu.make_async_copy(h_ref.at[pl.ds(r, 1)], buf_ref.at[pl.ds(slot, 1)], sem)

    def start(r, c):
        for k in range(TOP_K):
            row_copy(r, k).start()
        return c

    def wait(r, c):
        for k in range(TOP_K):
            row_copy(r, k).wait()
        return c

    lax.fori_loop(0, tm, start, 0, unroll=4)
    lax.fori_loop(0, tm, wait, 0, unroll=4)


def moe_dispatch(dest, h2, n_slots, *, tm=256):
    n, d = h2.shape
    buf0 = jnp.zeros((n_slots, d), h2.dtype)
    grid_spec = pltpu.PrefetchScalarGridSpec(
        num_scalar_prefetch=1,
        grid=(n // tm,),
        in_specs=[pl.BlockSpec((tm, d), lambda i, *_: (i, 0)),
                  pl.BlockSpec(memory_space=pl.ANY)],
        out_specs=pl.BlockSpec(memory_space=pl.ANY),
        scratch_shapes=[pltpu.SemaphoreType.DMA(())],
    )
    return pl.pallas_call(
        _moe_dispatch_body,
        grid_spec=grid_spec,
        out_shape=jax.ShapeDtypeStruct((n_slots, d), h2.dtype),
        input_output_aliases={2: 0},
        compiler_params=_cparams(("arbitrary",)),
        name="moe_dispatch",
    )(dest, h2, buf0)


def _moe_expert_body(sbe_ref, row0_ref, nb_ref, tot_ref, wg_ref, wu_ref, wd_ref, buf_ref, out_ref,
                     wgb, wub, wdb, xp, acc, sem_in, sem_out):
    del sbe_ref
    s = pl.program_id(0)
    f = pl.program_id(1)
    last_f = pl.num_programs(1) - 1
    nb = nb_ref[s]
    row0 = row0_ref[s]
    half = xp.shape[1]

    def blk(first_row, r):
        return pl.ds(pl.multiple_of(first_row + r * MOE_BLOCK, MOE_BLOCK), MOE_BLOCK)

    def rows_in(r):
        return pltpu.make_async_copy(buf_ref.at[blk(row0, r)], xp.at[blk(0, r)], sem_in)

    def rows_out(r):
        return pltpu.make_async_copy(acc.at[blk(0, r)], out_ref.at[blk(row0, r)], sem_out)

    def for_blocks(n, fn):
        def body(r, carry):
            fn(r)
            return carry
        lax.fori_loop(0, n, body, 0)

    def block(r, first):
        rows = pl.ds(pl.multiple_of(r * MOE_BLOCK, MOE_BLOCK), MOE_BLOCK)
        xlo, xhi = _unpack_bf16_halves(xp[rows, :])
        hg = _dot(xlo, wgb[:half, :]) + _dot(xhi, wgb[half:, :])
        hu = _dot(xlo, wub[:half, :]) + _dot(xhi, wub[half:, :])
        y = _dot((jax.nn.silu(hg) * hu).astype(BF16), wdb[...])
        if first:
            acc[rows, :] = y
        else:
            acc[rows, :] += y

    @pl.when(nb > 0)
    def _():
        @pl.when(f == 0)
        def _():
            for_blocks(nb, lambda r: rows_in(r).start())

        wgb[...] = wg_ref[0].astype(BF16)
        wub[...] = wu_ref[0].astype(BF16)
        wdb[...] = wd_ref[0].astype(BF16)

        @pl.when(f == 0)
        def _():
            for_blocks(nb, lambda r: rows_in(r).wait())
            for_blocks(nb, lambda r: block(r, True))

        @pl.when(f > 0)
        def _():
            for_blocks(nb, lambda r: block(r, False))

        @pl.when(f == last_f)
        def _():
            for_blocks(nb, lambda r: rows_out(r).start())
            for_blocks(nb, lambda r: rows_out(r).wait())

    @pl.when((s == pl.num_programs(0) - 1) & (f == last_f))
    def _():
        acc[pl.ds(0, MOE_BLOCK), :] = jnp.zeros((MOE_BLOCK, acc.shape[1]), acc.dtype)
        tail0 = tot_ref[0]

        def zero_out(r):
            return pltpu.make_async_copy(acc.at[pl.ds(0, MOE_BLOCK)], out_ref.at[blk(tail0, r)],
                                         sem_out)

        ntail = (out_ref.shape[0] - tail0) // MOE_BLOCK
        for_blocks(ntail, lambda r: zero_out(r).start())
        for_blocks(ntail, lambda r: zero_out(r).wait())


def moe_experts(sb_expert, sb_row0, sb_blocks, total_rows, buf, wg, wu, wd):
    p, half = buf.shape
    d = 2 * half
    ff = wg.shape[2]
    nf = ff // FF_TILE
    f_eff = lambda s, f, nb: jnp.where(nb[s] > 0, f, nf - 1)
    grid_spec = pltpu.PrefetchScalarGridSpec(
        num_scalar_prefetch=4,
        grid=(sb_expert.shape[0], nf),
        in_specs=[pl.BlockSpec((1, d, FF_TILE), lambda s, f, e, r0, nb, t: (e[s], 0, f_eff(s, f, nb))),
                  pl.BlockSpec((1, d, FF_TILE), lambda s, f, e, r0, nb, t: (e[s], 0, f_eff(s, f, nb))),
                  pl.BlockSpec((1, FF_TILE, d), lambda s, f, e, r0, nb, t: (e[s], f_eff(s, f, nb), 0)),
                  pl.BlockSpec(memory_space=pl.ANY)],
        out_specs=pl.BlockSpec(memory_space=pl.ANY),
        scratch_shapes=[pltpu.VMEM((d, FF_TILE), BF16), pltpu.VMEM((d, FF_TILE), BF16),
                        pltpu.VMEM((FF_TILE, d), BF16), pltpu.VMEM((SUPER_ROWS, half), jnp.uint32),
                        pltpu.VMEM((SUPER_ROWS, d), F32),
                        pltpu.SemaphoreType.DMA(()), pltpu.SemaphoreType.DMA(())],
    )
    return pl.pallas_call(
        _moe_expert_body,
        grid_spec=grid_spec,
        out_shape=jax.ShapeDtypeStruct((p, d), F32),
        compiler_params=_cparams(("arbitrary", "arbitrary")),
        name="moe_experts",
    )(sb_expert, sb_row0, sb_blocks, total_rows, wg, wu, wd, buf)


def _moe_combine_body(dest_ref, x1_ref, route_ref, out_ref, y_ref, rows_ref, sem):
    tm = x1_ref.shape[0]
    base = pl.program_id(0) * tm * TOP_K

    def row_copy(r, k):
        slot = dest_ref[base + r * TOP_K + k]
        return pltpu.make_async_copy(out_ref.at[pl.ds(slot, 1)], rows_ref.at[k, pl.ds(r, 1)], sem)

    def start(r, c):
        for k in range(TOP_K):
            row_copy(r, k).start()
        return c

    def wait(r, c):
        for k in range(TOP_K):
            row_copy(r, k).wait()
        return c

    lax.fori_loop(0, tm, start, 0, unroll=4)
    lax.fori_loop(0, tm, wait, 0, unroll=4)
    route = route_ref[...]
    y_ref[...] = x1_ref[...] + (rows_ref[0] * route[:, 0:1] + rows_ref[1] * route[:, 1:2])


def moe_combine(dest, x1, route, expert_out, *, tm=256):
    n, d = x1.shape
    grid_spec = pltpu.PrefetchScalarGridSpec(
        num_scalar_prefetch=1,
        grid=(n // tm,),
        in_specs=[pl.BlockSpec((tm, d), lambda i, *_: (i, 0)),
                  pl.BlockSpec((tm, LANES), lambda i, *_: (i, 0)),
                  pl.BlockSpec(memory_space=pl.ANY)],
        out_specs=pl.BlockSpec((tm, d), lambda i, *_: (i, 0)),
        scratch_shapes=[pltpu.VMEM((TOP_K, tm, d), F32), pltpu.SemaphoreType.DMA(())],
    )
    return pl.pallas_call(
        _moe_combine_body,
        grid_spec=grid_spec,
        out_shape=jax.ShapeDtypeStruct((n, d), F32),
        compiler_params=_cparams(("arbitrary",)),
        name="moe_combine",
    )(dest, x1, route, expert_out)


def hierarchical_moe(x1, h2, route, wg, wu, wd):
    n = h2.shape[0]
    n_experts = wg.shape[0]
    m = n * TOP_K
    ids_row = route[:, 2:4].astype(I32).reshape(1, m)
    rank_row, counts = moe_rank(ids_row, n_experts)
    counts = counts[:, 0].astype(I32)
    padded = (counts + MOE_BLOCK - 1) // MOE_BLOCK * MOE_BLOCK
    pad_end = jnp.cumsum(padded)
    pad_start = pad_end - padded
    dest = moe_dest(ids_row, rank_row, pad_start.astype(F32).reshape(n_experts, 1)).reshape(m)
    n_slots = m + n_experts * MOE_BLOCK
    n_sb = (padded + SUPER_ROWS - 1) // SUPER_ROWS
    sb_end = jnp.cumsum(n_sb)
    sb_start = sb_end - n_sb
    s_idx = jnp.arange(n_experts + m // SUPER_ROWS, dtype=I32)
    used = s_idx < sb_end[-1]
    e_of = jnp.minimum(jnp.sum(sb_end[None, :] <= s_idx[:, None], axis=1), n_experts - 1)
    piece = s_idx - sb_start[e_of]
    sb_row0 = jnp.where(used, pad_start[e_of] + piece * SUPER_ROWS, 0).astype(I32)
    sb_rows = jnp.clip(padded[e_of] - piece * SUPER_ROWS, 0, SUPER_ROWS)
    sb_blocks = jnp.where(used, sb_rows // MOE_BLOCK, 0).astype(I32)
    sb_expert = jnp.where(used, e_of, e_of[jnp.maximum(sb_end[-1] - 1, 0)]).astype(I32)
    buf = moe_dispatch(dest, h2, n_slots)
    expert_out = moe_experts(sb_expert, sb_row0, sb_blocks, pad_end[-1:].astype(I32), buf,
                             wg, wu, wd)
    return moe_combine(dest, x1, route, expert_out)


def _pad_cols(w, to):
    return jnp.pad(w, ((0, 0), (0, to - w.shape[1])))


def _pad_rows(w, to):
    return jnp.pad(w, ((0, to - w.shape[0]), (0, 0)))


def _layer(x2d, batch, norm1_w, w_in, mu, w0, w2, a0, a2, g2, k_k, k_a, r_k, gn_w, gn_b,
           q_norm_w, k_norm_w, sinks, proj_rwkv, proj_attn, w_out, norm2_w, wc, bc, wf, bf,
           wg, wu, wd):
    n, d = x2d.shape
    seq = n // batch
    width = w0.shape[0]
    dl, il, gl = w2.shape[0], a2.shape[0], g2.shape[0]
    q_width = proj_attn.shape[0]
    rwkv_cols = 3 * width + dl + il + gl
    kv_width = (w_in.shape[1] - rwkv_cols - q_width - 2 * d) // 2
    row = lambda v: v.reshape(1, -1).astype(F32)

    c0 = 3 * width
    w_rwkv = jnp.concatenate([w_in[:, :c0], _pad_cols(w_in[:, c0:c0 + dl], LANES),
                              _pad_cols(w_in[:, c0 + dl:c0 + dl + il], LANES),
                              w_in[:, c0 + dl + il:rwkv_cols]], axis=1).astype(BF16)
    mu_p = jnp.concatenate([mu[:c0], jnp.pad(mu[c0:c0 + dl], (0, LANES - dl)),
                            jnp.pad(mu[c0 + dl:c0 + dl + il], (0, LANES - il)),
                            mu[c0 + dl + il:]]).reshape(1, -1)
    q0 = rwkv_cols
    w_qkv = w_in[:, q0:q0 + q_width + 2 * kv_width].astype(BF16)
    w_gates = w_in[:, q0 + q_width + 2 * kv_width:].astype(BF16)

    g1 = row(norm1_w)
    p_rwkv = norm_proj(x2d, g1, w_rwkv, out_dtype=F32, sigmoid=False)
    qkv = norm_proj(x2d, g1, w_qkv, out_dtype=BF16, sigmoid=False)
    gates = norm_proj(x2d, g1, w_gates, out_dtype=BF16, sigmoid=True)

    rt, at, bt, kt, vv, g, bonus, pc = rwkv_prep(
        p_rwkv, mu_p, row(w0), _pad_rows(w2, LANES).astype(BF16), row(a0),
        _pad_rows(a2, LANES).astype(BF16), g2.astype(BF16), row(k_k), row(k_a), row(r_k),
        seq_len=seq)
    a_out = rwkv_scan(rt, at, bt, kt, vv, pc, g, bonus, row(gn_w), row(gn_b), batch=batch)

    b_out = swa_attention(qkv, q_norm_w, k_norm_w, sinks.astype(F32), batch=batch,
                          q_width=q_width, kv_width=kv_width)

    n_groups, n_experts = wc.shape[1], wf.shape[1]
    router_w = _pad_cols(jnp.concatenate([wc, wf], axis=1), LANES)
    router_b = _pad_cols(jnp.concatenate([bc, bf]).reshape(1, -1), LANES)
    assert n_groups == N_GROUPS and n_experts == N_GROUPS * GROUP_SIZE
    x1, h2, route = merge_route(a_out, b_out, gates, x2d, proj_rwkv.astype(BF16),
                                proj_attn.astype(BF16), w_out.astype(BF16), row(norm2_w),
                                router_w, router_b)
    return hierarchical_moe(x1, h2, route, wg, wu, wd)


def kernel(x, norm1_w, w_in, rwkv_mu, rwkv_w0, rwkv_w2, rwkv_a0, rwkv_a2, rwkv_g2, rwkv_k_k,
           rwkv_k_a, rwkv_r_k, rwkv_gn_w, rwkv_gn_b, q_norm_w, k_norm_w, attn_sinks, proj_rwkv,
           proj_attn, w_out, norm2_w, router_coarse_w, router_coarse_b, router_fine_w,
           router_fine_b, expert_w_gate, expert_w_up, expert_w_down):
    batch, seq, d = x.shape
    x2d = x.reshape(batch * seq, d)
    for layer in range(norm1_w.shape[0]):
        x2d = _layer(x2d, batch, norm1_w[layer], w_in[layer], rwkv_mu[layer], rwkv_w0[layer],
                     rwkv_w2[layer], rwkv_a0[layer], rwkv_a2[layer], rwkv_g2[layer],
                     rwkv_k_k[layer], rwkv_k_a[layer], rwkv_r_k[layer].reshape(-1),
                     rwkv_gn_w[layer], rwkv_gn_b[layer], q_norm_w[layer], k_norm_w[layer],
                     attn_sinks[layer], proj_rwkv[layer], proj_attn[layer], w_out[layer],
                     norm2_w[layer], router_coarse_w[layer], router_coarse_b[layer],
                     router_fine_w[layer], router_fine_b[layer], expert_w_gate[layer],
                     expert_w_up[layer], expert_w_down[layer])
    return x2d.reshape(batch, seq, d)
```

```python
import functools

import jax
import jax.numpy as jnp
from jax import lax
from jax.experimental import pallas as pl
from jax.experimental.pallas import tpu as pltpu

F32 = jnp.float32
BF16 = jnp.bfloat16
I32 = jnp.int32

NORM_EPS = 1e-6
GN_EPS = 64e-5
HEAD_DIM = 64
LANES = 128
CHUNK = 64
WINDOW = 128
MOE_BLOCK = 128
SUPER_ROWS = 1024
FF_TILE = 512
TOP_K = 2
N_GROUPS = 8
GROUP_SIZE = 8
VMEM_LIMIT = 56 * 1024 * 1024

_NT = (((1,), (1,)), ((), ()))
_TN = (((0,), (0,)), ((), ()))


def _dot(a, b):
    return jnp.dot(a, b, preferred_element_type=F32)


def _split2(x):
    hi = x.astype(BF16)
    mid = (x - hi.astype(F32)).astype(BF16)
    return hi, mid


def _split3(x):
    hi = x.astype(BF16)
    r1 = x - hi.astype(F32)
    mid = r1.astype(BF16)
    lo = (r1 - mid.astype(F32)).astype(BF16)
    return hi, mid, lo


def _dot_exact_rhs(x, m):
    hi, mid, lo = _split3(x)
    return _dot(hi, m) + _dot(mid, m) + _dot(lo, m)


def _exact_lhs_dot(m, x):
    hi, mid, lo = _split3(x)
    return _dot(m, hi) + _dot(m, mid) + _dot(m, lo)


def _pack_bf16_halves(x):
    w = x.shape[1] // 2
    lo = lax.bitcast_convert_type(x[:, :w].astype(BF16).astype(F32), jnp.uint32)
    hi = lax.bitcast_convert_type(x[:, w:].astype(BF16).astype(F32), jnp.uint32)
    return (lo >> 16) | (hi & jnp.uint32(0xFFFF0000))


def _unpack_bf16_halves(xp):
    lo = lax.bitcast_convert_type(xp << 16, F32).astype(BF16)
    hi = lax.bitcast_convert_type(xp & jnp.uint32(0xFFFF0000), F32).astype(BF16)
    return lo, hi


def _cparams(sem, vmem=VMEM_LIMIT):
    return pltpu.CompilerParams(dimension_semantics=sem, vmem_limit_bytes=vmem)


def _norm_proj_body(x_ref, g_ref, w_ref, o_ref, h_ref, *, sigmoid):
    @pl.when(pl.program_id(1) == 0)
    def _():
        x = x_ref[...]
        ms = jnp.mean(x * x, axis=-1, keepdims=True)
        h_ref[...] = (x * lax.rsqrt(ms + NORM_EPS) * g_ref[...]).astype(BF16)

    acc = _dot(h_ref[...], w_ref[...])
    if sigmoid:
        acc = jax.nn.sigmoid(acc)
    o_ref[...] = acc.astype(o_ref.dtype)


def norm_proj(x2d, gain, w, *, out_dtype, sigmoid, tm=1024, tn=512):
    n, d = x2d.shape
    c = w.shape[1]
    tm = min(tm, n)
    return pl.pallas_call(
        functools.partial(_norm_proj_body, sigmoid=sigmoid),
        grid=(n // tm, c // tn),
        in_specs=[pl.BlockSpec((tm, d), lambda i, j: (i, 0)),
                  pl.BlockSpec((1, d), lambda i, j: (0, 0)),
                  pl.BlockSpec((d, tn), lambda i, j: (0, j))],
        out_specs=pl.BlockSpec((tm, tn), lambda i, j: (i, j)),
        out_shape=jax.ShapeDtypeStruct((n, c), out_dtype),
        scratch_shapes=[pltpu.VMEM((tm, d), BF16)],
        compiler_params=_cparams(("parallel", "arbitrary")),
        name="norm_proj",
    )(x2d, gain, w)


def _head_sum(x, e, et):
    s = _dot(x.astype(BF16), e)
    hi, mid = _split2(s)
    return _dot(hi, et) + _dot(mid, et)


def _rwkv_prep_body(p_ref, pprev_ref, mu_ref, w0_ref, w2_ref, a0_ref, a2_ref, g2_ref,
                    kk_ref, ka_ref, rk_ref, e_ref, et_ref, tri_ref, sel_ref,
                    rt_ref, at_ref, bt_ref, kt_ref, v_ref, g_ref, bonus_ref, pc_ref,
                    *, seq_len, width):
    tm = p_ref.shape[0]
    w_ = width
    first = (pl.program_id(0) * tm) % seq_len == 0
    p = p_ref[...]
    prev_row = jnp.where(first, 0.0, pprev_ref[7:8, :])
    row = lax.broadcasted_iota(I32, (tm, 1), 0)
    shifted = jnp.where(row == 0, prev_row, pltpu.roll(p, 1, 0))
    m = p + (shifted - p) * mu_ref[...]
    r = m[:, 0:w_]
    k = m[:, w_:2 * w_]
    v = m[:, 2 * w_:3 * w_]
    xw = m[:, 3 * w_:3 * w_ + 128]
    xa = m[:, 3 * w_ + 128:3 * w_ + 256]
    xg = m[:, 3 * w_ + 256:]

    z = -(w0_ref[...] + _dot(jnp.tanh(xw).astype(BF16), w2_ref[...]))
    softplus = jnp.maximum(z, 0.0) + jnp.log1p(jnp.exp(-jnp.abs(z)))
    logw = -jnp.exp(-softplus - 0.5)
    a = jax.nn.sigmoid(a0_ref[...] + _dot(xa.astype(BF16), a2_ref[...]))
    g = _dot(jax.nn.sigmoid(xg).astype(BF16), g2_ref[...])

    e = e_ref[...]
    et = et_ref[...]
    kk = k * kk_ref[...]
    kk = kk / jnp.maximum(jnp.sqrt(_head_sum(kk * kk, e, et)), 1e-12)
    kmod = k * (1.0 + (a - 1.0) * ka_ref[...])
    bonus = _head_sum(r * kmod * rk_ref[...], e, et) * v

    cum = _exact_lhs_dot(tri_ref[...], logw)
    pc_ref[...] = jnp.exp(_exact_lhs_dot(sel_ref[...], logw))
    inv = jnp.exp(-cum)
    rt_ref[...] = (r * jnp.exp(cum)).astype(BF16)
    at_ref[...] = (-kk * jnp.exp(cum - logw)).astype(BF16)
    bt_ref[...] = (kk * a * inv).astype(BF16)
    kt_ref[...] = (kmod * inv).astype(BF16)
    v_ref[...] = v.astype(BF16)
    g_ref[...] = g.astype(BF16)
    bonus_ref[...] = bonus


def _head_indicator(width):
    heads = width // HEAD_DIM
    c = jnp.arange(width)[:, None] // HEAD_DIM
    h = jnp.arange(LANES)[None, :]
    e = (c == h).astype(BF16)
    assert heads <= LANES
    return e, e.T


def rwkv_prep(p, mu, w0, w2, a0, a2, g2, k_k, k_a, r_k, *, seq_len, tm=512):
    n, cols = p.shape
    width = w0.shape[1]
    nchunk = tm // CHUNK
    e, et = _head_indicator(width)
    t = jnp.arange(tm)
    same = (t[:, None] // CHUNK) == (t[None, :] // CHUNK)
    tri = (same & (t[:, None] >= t[None, :])).astype(BF16)
    sel = ((t[None, :] // CHUNK) == jnp.arange(nchunk)[:, None]).astype(BF16)
    const = lambda shape: pl.BlockSpec(shape, lambda i: (0, 0))
    stream = lambda dt: jax.ShapeDtypeStruct((n, width), dt)
    outs = pl.pallas_call(
        functools.partial(_rwkv_prep_body, seq_len=seq_len, width=width),
        grid=(n // tm,),
        in_specs=[pl.BlockSpec((tm, cols), lambda i: (i, 0)),
                  pl.BlockSpec((8, cols), lambda i: (jnp.maximum(i * (tm // 8) - 1, 0), 0)),
                  const((1, cols)), const((1, width)), const(w2.shape), const((1, width)),
                  const(a2.shape), const(g2.shape), const((1, width)), const((1, width)),
                  const((1, width)), const(e.shape), const(et.shape), const(tri.shape),
                  const(sel.shape)],
        out_specs=[pl.BlockSpec((tm, width), lambda i: (i, 0))] * 7
                  + [pl.BlockSpec((nchunk, width), lambda i: (i, 0))],
        out_shape=[stream(BF16)] * 6 + [stream(F32),
                   jax.ShapeDtypeStruct((n // CHUNK, width), F32)],
        compiler_params=_cparams(("parallel",)),
        name="rwkv_prep",
    )(p, p, mu, w0, w2, a0, a2, g2, k_k, k_a, r_k, e, et, tri, sel)
    return outs


def _chunk_pairs(rts, ats, bts, kts, vvs, pcs, s_prevs, masks):
    lo, strict, incl, eye = masks
    c = rts[0].shape[0]
    c2 = 2 * c
    zero = jnp.zeros_like(rts[0])
    each = lambda f, *ls: [f(*a) for a in zip(*ls)]

    def stack(x):
        return jnp.concatenate([jnp.where(lo, x, zero), jnp.where(lo, zero, x)], axis=0)

    ar = each(lambda a, r: jnp.concatenate([stack(a), stack(r)], axis=0), ats, rts)
    bk = each(lambda b, k: jnp.concatenate([stack(b), stack(k)], axis=0), bts, kts)
    vb = each(stack, vvs)
    gram = each(lambda x, y: lax.dot_general(x, y, _NT, preferred_element_type=F32), ar, bk)
    ars = each(lambda x, s: lax.dot_general(x, s.astype(BF16), _NT, preferred_element_type=F32),
               ar, s_prevs)
    a_ab = each(lambda g: jnp.where(strict, g[:c2, :c2], 0.0), gram)
    rhs = each(lambda g, v, x: x[:c2] + _dot(jnp.where(strict, g[:c2, c2:], 0.0).astype(BF16), v),
               gram, vb, ars)

    t_inv = each(lambda n: eye + n, a_ab)
    nk = a_ab
    for _ in range((c - 1).bit_length() - 1):
        nkb = each(lambda n: n.astype(BF16), nk)
        nk = each(lambda n: _dot(n, n), nkb)
        t_inv = each(lambda t, n: t + _dot(t.astype(BF16), n.astype(BF16)), t_inv, nk)
    u = each(lambda t, r: _dot(t.astype(BF16), r.astype(BF16)), t_inv, rhs)

    uv = each(lambda uu, v: jnp.concatenate([uu.astype(BF16), v], axis=0), u, vb)
    a_r = each(lambda g: jnp.where(incl, g[c2:, :], 0.0).astype(BF16), gram)
    ybd = each(lambda x, a, w: x[c2:] + _dot(a, w), ars, a_r, uv)
    ys = each(lambda yb: yb[:c] + yb[c:], ybd)

    bk_end = each(lambda x, pc: (x.astype(F32) * pc).astype(BF16), bk, pcs)
    s_new = each(lambda s, pc, w, x: s * pc + lax.dot_general(w, x, _TN, preferred_element_type=F32),
                 s_prevs, pcs, uv, bk_end)
    return ys, s_new


def _rwkv_scan_body(rt_ref, at_ref, bt_ref, kt_ref, v_ref, pc_ref, g_ref, bonus_ref,
                    gnw_ref, gnb_ref, e_ref, et_ref, o_ref, s_ref, y_ref):
    tr, width = rt_ref.shape
    npairs = width // LANES

    @pl.when(pl.program_id(1) == 0)
    def _():
        s_ref[...] = jnp.zeros_like(s_ref)

    lane = lax.broadcasted_iota(I32, (CHUNK, LANES), 1)
    lo = lane < HEAD_DIM
    ri = lax.broadcasted_iota(I32, (2 * CHUNK, 2 * CHUNK), 0)
    ci = lax.broadcasted_iota(I32, (2 * CHUNK, 2 * CHUNK), 1)
    same = (ri // CHUNK) == (ci // CHUNK)
    ri2 = lax.broadcasted_iota(I32, (2 * CHUNK, 4 * CHUNK), 0)
    ci2 = lax.broadcasted_iota(I32, (2 * CHUNK, 4 * CHUNK), 1) % (2 * CHUNK)
    incl = ((ri2 // CHUNK) == (ci2 // CHUNK)) & (ri2 >= ci2)
    masks = (lo, same & (ri > ci), incl, (ri == ci).astype(F32))

    def chunk_step(c, carry):
        rows = pl.ds(pl.multiple_of(c * CHUNK, CHUNK), CHUNK)
        this_chunk = lax.broadcasted_iota(I32, (pc_ref.shape[0], 1), 0) == c
        cols = [slice(pr * LANES, (pr + 1) * LANES) for pr in range(npairs)]
        pcs = [jnp.sum(jnp.where(this_chunk, pc_ref[:, cl], 0.0), axis=0, keepdims=True)
               for cl in cols]
        load = lambda ref: [ref[rows, cl] for cl in cols]
        ys, s_new = _chunk_pairs(load(rt_ref), load(at_ref), load(bt_ref), load(kt_ref),
                                 load(v_ref), pcs, [s_ref[pr] for pr in range(npairs)], masks)
        for pr in range(npairs):
            s_ref[pr] = s_new[pr]
            y_ref[rows, cols[pr]] = ys[pr]
        return carry

    lax.fori_loop(0, tr // CHUNK, chunk_step, 0)

    e = e_ref[...]
    et = et_ref[...]
    y = y_ref[...]
    mean = _head_sum(y, e, et) * (1.0 / HEAD_DIM)
    d = y - mean
    var = _head_sum(d * d, e, et) * (1.0 / HEAD_DIM)
    out = d * lax.rsqrt(var + GN_EPS) * gnw_ref[...] + gnb_ref[...]
    out = (out + bonus_ref[...]) * g_ref[...].astype(F32)
    o_ref[...] = out.astype(o_ref.dtype)


def rwkv_scan(rt, at, bt, kt, vv, pc, g, bonus, gn_w, gn_b, *, batch, tr=512):
    n, width = rt.shape
    seq = n // batch
    nst = seq // tr
    e, et = _head_indicator(width)
    blk = lambda rows: pl.BlockSpec((rows, width), lambda b, s: (b * nst + s, 0))
    const = lambda shape: pl.BlockSpec(shape, lambda b, s: (0, 0))
    return pl.pallas_call(
        _rwkv_scan_body,
        grid=(batch, nst),
        in_specs=[blk(tr)] * 5 + [blk(tr // CHUNK), blk(tr), blk(tr),
                  const((1, width)), const((1, width)), const(e.shape), const(et.shape)],
        out_specs=blk(tr),
        out_shape=jax.ShapeDtypeStruct((n, width), BF16),
        scratch_shapes=[pltpu.VMEM((width // LANES, LANES, LANES), F32),
                        pltpu.VMEM((tr, width), F32)],
        compiler_params=_cparams(("parallel", "arbitrary")),
        name="rwkv_scan",
    )(rt, at, bt, kt, vv, pc, g, bonus, gn_w, gn_b, e, et)


def _head_rmsnorm(x, gain, bd):
    hi, mid = _split2(x * x)
    ms = (_dot(hi, bd) + _dot(mid, bd)) * (1.0 / HEAD_DIM)
    return x * lax.rsqrt(ms + NORM_EPS) * gain


def _swa_body(sink_ref, q_ref, kp_ref, kc_ref, vp_ref, vc_ref, qg_ref, kg_ref, bd_ref, o_ref,
              *, group):
    blk = q_ref.shape[0]
    n = pl.program_id(1)
    bd = bd_ref[...]
    scale = HEAD_DIM ** -0.5
    lane = lax.broadcasted_iota(I32, (1, LANES), 1)
    lo = lane < HEAD_DIM
    ri = lax.broadcasted_iota(I32, (2 * blk, 2 * blk), 0)
    cj = lax.broadcasted_iota(I32, (2 * blk, 2 * blk), 1)
    rel = blk + (ri % blk) - cj
    jmin = jnp.where(n > 0, 0, blk)
    valid = (rel >= 0) & (rel < WINDOW) & (cj >= jmin)
    top_rows = lax.broadcasted_iota(I32, (2 * blk, 1), 0) < blk

    kcat = jnp.concatenate([kp_ref[...], kc_ref[...]], axis=0).astype(F32)
    vcat = jnp.concatenate([vp_ref[...], vc_ref[...]], axis=0).astype(F32)
    kv_heads = kcat.shape[1] // HEAD_DIM
    q_tiles = q_ref.shape[1] // LANES
    tiles_per_kv = group // 2
    each = lambda f, *ls: [f(*a) for a in zip(*ls)]

    kv_cols = [slice(jt * LANES, (jt + 1) * LANES) for jt in range(kv_heads // 2)]
    kn = [_head_rmsnorm(kcat[:, c], kg_ref[...], bd) for c in kv_cols]
    kn_r = [pltpu.roll(x, HEAD_DIM, 1) for x in kn]
    vt = [vcat[:, c] for c in kv_cols]
    vt_r = [pltpu.roll(x, HEAD_DIM, 1) for x in vt]
    own = lambda hk: lo if hk % 2 == 0 else jnp.logical_not(lo)
    k2 = [jnp.where(own(hk), kn[hk // 2], kn_r[hk // 2]).astype(BF16) for hk in range(kv_heads)]
    v2 = [jnp.where(own(hk), vt[hk // 2], vt_r[hk // 2]).astype(BF16) for hk in range(kv_heads)]

    q_cols = [slice(t * LANES, (t + 1) * LANES) for t in range(q_tiles)]
    qn = [_head_rmsnorm(q_ref[:, c].astype(F32), qg_ref[...] * scale, bd) for c in q_cols]
    qst = [jnp.concatenate([jnp.where(lo, x, 0.0), jnp.where(lo, 0.0, x)], axis=0).astype(BF16)
           for x in qn]
    s = [jnp.where(valid, lax.dot_general(x, k2[t // tiles_per_kv], _NT,
                                          preferred_element_type=F32), -jnp.inf)
         for t, x in enumerate(qst)]
    sink = [jnp.where(top_rows, sink_ref[2 * t], sink_ref[2 * t + 1]) for t in range(q_tiles)]
    mx = each(lambda x, sk: jnp.maximum(jnp.max(x, axis=-1, keepdims=True), sk), s, sink)
    pr = each(lambda x, m: jnp.exp(x - m), s, mx)
    inv = each(lambda p, sk, m: 1.0 / (jnp.sum(p, axis=-1, keepdims=True) + jnp.exp(sk - m)),
               pr, sink, mx)
    o = [_dot((p * r).astype(BF16), v2[t // tiles_per_kv])
         for t, (p, r) in enumerate(zip(pr, inv))]
    for t in range(q_tiles):
        o_ref[:, q_cols[t]] = jnp.where(lo, o[t][:blk], o[t][blk:]).astype(o_ref.dtype)


def swa_attention(qkv, q_gain, k_gain, sinks, *, batch, q_width, kv_width):
    n = qkv.shape[0]
    blk = WINDOW
    nb = n // batch // blk
    group = (q_width // HEAD_DIM) // (kv_width // HEAD_DIM)
    kcol = q_width // kv_width
    t = jnp.arange(LANES)
    bd = ((t[:, None] // HEAD_DIM) == (t[None, :] // HEAD_DIM)).astype(BF16)
    qg = jnp.tile(q_gain.reshape(1, HEAD_DIM), (1, 2))
    kg = jnp.tile(k_gain.reshape(1, HEAD_DIM), (1, 2))
    cur = lambda col: (lambda b, i, *_: (b * nb + i, col))
    prev = lambda col: (lambda b, i, *_: (b * nb + jnp.maximum(i - 1, 0), col))
    const = lambda shape: pl.BlockSpec(shape, lambda b, i, *_: (0, 0))
    grid_spec = pltpu.PrefetchScalarGridSpec(
        num_scalar_prefetch=1,
        grid=(batch, nb),
        in_specs=[pl.BlockSpec((blk, q_width), cur(0)),
                  pl.BlockSpec((blk, kv_width), prev(kcol)),
                  pl.BlockSpec((blk, kv_width), cur(kcol)),
                  pl.BlockSpec((blk, kv_width), prev(kcol + 1)),
                  pl.BlockSpec((blk, kv_width), cur(kcol + 1)),
                  const((1, LANES)), const((1, LANES)), const((LANES, LANES))],
        out_specs=pl.BlockSpec((blk, q_width), cur(0)),
    )
    return pl.pallas_call(
        functools.partial(_swa_body, group=group),
        grid_spec=grid_spec,
        out_shape=jax.ShapeDtypeStruct((n, q_width), BF16),
        compiler_params=_cparams(("parallel", "arbitrary")),
        name="swa_attn",
    )(sinks, qkv, qkv, qkv, qkv, qkv, qg, kg, bd)


def _merge_route_body(a_ref, b_ref, ga_ref, gb_ref, x_ref, pr_ref, pa_ref, wo_ref, n2_ref,
                      rh_ref, rm_ref, rl_ref, rb_ref, x1_ref, h2_ref, route_ref):
    merged = (ga_ref[...].astype(F32) * _dot(a_ref[...], pr_ref[...])
              + gb_ref[...].astype(F32) * _dot(b_ref[...], pa_ref[...]))
    x1 = x_ref[...] + _dot(merged.astype(BF16), wo_ref[...])
    x1_ref[...] = x1
    ms = jnp.mean(x1 * x1, axis=-1, keepdims=True)
    h2 = x1 * lax.rsqrt(ms + NORM_EPS) * n2_ref[...]
    h2_ref[...] = _pack_bf16_halves(h2)

    hi, mid, lo = _split3(h2)
    rh, rm, rl = rh_ref[...], rm_ref[...], rl_ref[...]
    logits = (_dot(hi, rh) + (_dot(hi, rm) + _dot(mid, rh))
              + (_dot(hi, rl) + _dot(mid, rm) + _dot(lo, rh))) + rb_ref[...]

    lane = lax.broadcasted_iota(I32, logits.shape, 1)
    big = jnp.int32(1 << 20)
    neg = -jnp.inf
    is_coarse = lane < N_GROUPS
    cl = jnp.where(is_coarse, logits, neg)
    ce = jnp.exp(cl - jnp.max(cl, axis=-1, keepdims=True))
    cp = ce / jnp.sum(ce, axis=-1, keepdims=True)
    g_prob = jnp.max(cp, axis=-1, keepdims=True)
    g_idx = jnp.min(jnp.where(is_coarse & (cp == g_prob), lane, big), axis=-1, keepdims=True)

    fine_lane = lane - N_GROUPS
    in_group = ((lane >= N_GROUPS) & (lane < N_GROUPS + N_GROUPS * GROUP_SIZE)
                & ((fine_lane // GROUP_SIZE) == g_idx))
    fl = jnp.where(in_group, logits, neg)
    fe = jnp.exp(fl - jnp.max(fl, axis=-1, keepdims=True))
    fp = fe / jnp.sum(fe, axis=-1, keepdims=True)
    p1 = jnp.max(jnp.where(in_group, fp, -1.0), axis=-1, keepdims=True)
    i1 = jnp.min(jnp.where(in_group & (fp == p1), lane, big), axis=-1, keepdims=True)
    rest = in_group & (lane != i1)
    p2 = jnp.max(jnp.where(rest, fp, -1.0), axis=-1, keepdims=True)
    i2 = jnp.min(jnp.where(rest & (fp == p2), lane, big), axis=-1, keepdims=True)
    den = p1 + p2
    w1 = g_prob * p1 / den
    w2 = g_prob * p2 / den
    e1 = (i1 - N_GROUPS).astype(F32)
    e2 = (i2 - N_GROUPS).astype(F32)
    route_ref[...] = jnp.where(lane == 0, w1, jnp.where(lane == 1, w2,
                     jnp.where(lane == 2, e1, jnp.where(lane == 3, e2, 0.0))))


def merge_route(a_out, b_out, gates, x2d, proj_r, proj_a, w_out, norm2_w, router_w, router_b,
                *, tm=256):
    n, d = x2d.shape
    wa = a_out.shape[1]
    rh, rm, rl = _split3(router_w)
    const = lambda arr: pl.BlockSpec(arr.shape, lambda i: (0, 0), pipeline_mode=pl.Buffered(1))
    row = lambda w: pl.BlockSpec((tm, w), lambda i: (i, 0))
    return pl.pallas_call(
        _merge_route_body,
        grid=(n // tm,),
        in_specs=[row(wa), row(wa), pl.BlockSpec((tm, d), lambda i: (i, 0)),
                  pl.BlockSpec((tm, d), lambda i: (i, 1)), row(d),
                  const(proj_r), const(proj_a), const(w_out), const(norm2_w),
                  const(rh), const(rm), const(rl), const(router_b)],
        out_specs=[row(d), row(d // 2), row(LANES)],
        out_shape=[jax.ShapeDtypeStruct((n, d), F32), jax.ShapeDtypeStruct((n, d // 2), jnp.uint32),
                   jax.ShapeDtypeStruct((n, LANES), F32)],
        compiler_params=_cparams(("parallel",)),
        name="merge_route",
    )(a_out, b_out, gates, gates, x2d, proj_r, proj_a, w_out, norm2_w, rh, rm, rl, router_b)


def _moe_rank_body(ids_ref, ut_ref, rank_ref, counts_ref, carry_ref):
    @pl.when(pl.program_id(0) == 0)
    def _():
        carry_ref[...] = jnp.zeros_like(carry_ref)

    ids = ids_ref[...]
    n_exp = carry_ref.shape[0]
    sub = lax.broadcasted_iota(I32, (n_exp, ids.shape[1]), 0)
    onehot = (sub == ids).astype(F32)
    before = _dot(onehot.astype(BF16), ut_ref[...])
    carry = carry_ref[...]
    rank_ref[...] = jnp.sum(onehot * (before + carry), axis=0, keepdims=True).astype(I32)
    carry = carry + jnp.sum(onehot, axis=1, keepdims=True)
    carry_ref[...] = carry
    counts_ref[...] = jnp.broadcast_to(carry, counts_ref.shape)


def moe_rank(ids_row, n_experts, *, tb=512):
    m = ids_row.shape[1]
    t = jnp.arange(tb)
    ut = (t[:, None] < t[None, :]).astype(BF16)
    return pl.pallas_call(
        _moe_rank_body,
        grid=(m // tb,),
        in_specs=[pl.BlockSpec((1, tb), lambda i: (0, i)),
                  pl.BlockSpec((tb, tb), lambda i: (0, 0))],
        out_specs=[pl.BlockSpec((1, tb), lambda i: (0, i)),
                   pl.BlockSpec((n_experts, LANES), lambda i: (0, 0))],
        out_shape=[jax.ShapeDtypeStruct((1, m), I32),
                   jax.ShapeDtypeStruct((n_experts, LANES), F32)],
        scratch_shapes=[pltpu.VMEM((n_experts, 1), F32)],
        compiler_params=_cparams(("arbitrary",)),
        name="moe_rank",
    )(ids_row, ut)


def _moe_dest_body(ids_ref, rank_ref, start_ref, dest_ref):
    ids = ids_ref[...]
    sub = lax.broadcasted_iota(I32, (start_ref.shape[0], ids.shape[1]), 0)
    start = jnp.sum(jnp.where(sub == ids, start_ref[...], 0.0), axis=0, keepdims=True)
    dest_ref[...] = start.astype(I32) + rank_ref[...]


def moe_dest(ids_row, rank_row, pad_start_col, *, tb=512):
    m = ids_row.shape[1]
    row = pl.BlockSpec((1, tb), lambda i: (0, i))
    return pl.pallas_call(
        _moe_dest_body,
        grid=(m // tb,),
        in_specs=[row, row, pl.BlockSpec(pad_start_col.shape, lambda i: (0, 0))],
        out_specs=row,
        out_shape=jax.ShapeDtypeStruct((1, m), I32),
        compiler_params=_cparams(("parallel",)),
        name="moe_dest",
    )(ids_row, rank_row, pad_start_col)


def _moe_dispatch_body(dest_ref, h_ref, buf_in_ref, buf_ref, sem):
    del buf_in_ref
    tm = h_ref.shape[0]
    base = pl.program_id(0) * tm * TOP_K

    def row_copy(r, k):
        slot = dest_ref[base + r * TOP_K + k]
        return pltpu.make_async_copy(h_ref.at[pl.ds(r, 1)], buf_ref.at[pl.ds(slot, 1)], sem)

    def start(r, c):
        for k in range(TOP_K):
            row_copy(r, k).start()
        return c

    def wait(r, c):
        for k in range(TOP_K):
            row_copy(r, k).wait()
        return c

    lax.fori_loop(0, tm, start, 0, unroll=4)
    lax.fori_loop(0, tm, wait, 0, unroll=4)


def moe_dispatch(dest, h2, n_slots, *, tm=256):
    n, d = h2.shape
    buf0 = jnp.zeros((n_slots, d), h2.dtype)
    grid_spec = pltpu.PrefetchScalarGridSpec(
        num_scalar_prefetch=1,
        grid=(n // tm,),
        in_specs=[pl.BlockSpec((tm, d), lambda i, *_: (i, 0)),
                  pl.BlockSpec(memory_space=pl.ANY)],
        out_specs=pl.BlockSpec(memory_space=pl.ANY),
        scratch_shapes=[pltpu.SemaphoreType.DMA(())],
    )
    return pl.pallas_call(
        _moe_dispatch_body,
        grid_spec=grid_spec,
        out_shape=jax.ShapeDtypeStruct((n_slots, d), h2.dtype),
        input_output_aliases={2: 0},
        compiler_params=_cparams(("arbitrary",)),
        name="moe_dispatch",
    )(dest, h2, buf0)


def _moe_expert_body(sbe_ref, row0_ref, nb_ref, tot_ref, wg_ref, wu_ref, wd_ref, buf_ref, out_ref,
                     wgb, wub, wdb, xp, acc, pending, sem_in, sem_out):
    del sbe_ref
    s = pl.program_id(0)
    f = pl.program_id(1)
    n_sb = pl.num_programs(0)
    last_f = pl.num_programs(1) - 1
    nb = nb_ref[s]
    half = xp.shape[2]

    def blk(first_row, r):
        return pl.ds(pl.multiple_of(first_row + r * MOE_BLOCK, MOE_BLOCK), MOE_BLOCK)

    def rows_in(sb, r):
        return pltpu.make_async_copy(buf_ref.at[blk(row0_ref[sb], r)], xp.at[sb % 2, blk(0, r)],
                                     sem_in)

    def rows_out(r):
        return pltpu.make_async_copy(acc.at[blk(0, r)], out_ref.at[blk(row0_ref[s], r)], sem_out)

    def for_blocks(n, fn):
        def body(r, carry):
            fn(r)
            return carry
        lax.fori_loop(0, n, body, 0)

    def drain_out():
        for_blocks(pending[0], lambda r: rows_out(0).wait())
        pending[0] = 0

    def block(r, first):
        rows = pl.ds(pl.multiple_of(r * MOE_BLOCK, MOE_BLOCK), MOE_BLOCK)
        xlo, xhi = _unpack_bf16_halves(xp[s % 2, rows, :])
        hg = _dot(xlo, wgb[:half, :]) + _dot(xhi, wgb[half:, :])
        hu = _dot(xlo, wub[:half, :]) + _dot(xhi, wub[half:, :])
        y = _dot((jax.nn.silu(hg) * hu).astype(BF16), wdb[...])
        if first:
            acc[rows, :] = y
        else:
            acc[rows, :] += y

    @pl.when((s == 0) & (f == 0))
    def _():
        pending[0] = 0
        for_blocks(nb, lambda r: rows_in(0, r).start())

    @pl.when(nb > 0)
    def _():
        wgb[...] = wg_ref[0].astype(BF16)
        wub[...] = wu_ref[0].astype(BF16)
        wdb[...] = wd_ref[0].astype(BF16)

        @pl.when(f == 0)
        def _():
            drain_out()
            for_blocks(nb, lambda r: rows_in(s, r).wait())
            for_blocks(nb, lambda r: block(r, True))

        @pl.when(f > 0)
        def _():
            for_blocks(nb, lambda r: block(r, False))

        @pl.when(f == last_f)
        def _():
            for_blocks(nb, lambda r: rows_out(r).start())
            pending[0] = nb
            nxt = jnp.minimum(s + 1, n_sb - 1)
            nb_next = jnp.where(s + 1 < n_sb, nb_ref[nxt], 0)
            for_blocks(nb_next, lambda r: rows_in(nxt, r).start())

    @pl.when((s == n_sb - 1) & (f == last_f))
    def _():
        drain_out()
        acc[pl.ds(0, MOE_BLOCK), :] = jnp.zeros((MOE_BLOCK, acc.shape[1]), acc.dtype)
        tail0 = tot_ref[0]

        def zero_out(r):
            return pltpu.make_async_copy(acc.at[pl.ds(0, MOE_BLOCK)], out_ref.at[blk(tail0, r)],
                                         sem_out)

        ntail = (out_ref.shape[0] - tail0) // MOE_BLOCK
        for_blocks(ntail, lambda r: zero_out(r).start())
        for_blocks(ntail, lambda r: zero_out(r).wait())


def moe_experts(sb_expert, sb_row0, sb_blocks, total_rows, buf, wg, wu, wd):
    p, half = buf.shape
    d = 2 * half
    ff = wg.shape[2]
    nf = ff // FF_TILE
    f_eff = lambda s, f, nb: jnp.where(nb[s] > 0, f, nf - 1)
    grid_spec = pltpu.PrefetchScalarGridSpec(
        num_scalar_prefetch=4,
        grid=(sb_expert.shape[0], nf),
        in_specs=[pl.BlockSpec((1, d, FF_TILE), lambda s, f, e, r0, nb, t: (e[s], 0, f_eff(s, f, nb))),
                  pl.BlockSpec((1, d, FF_TILE), lambda s, f, e, r0, nb, t: (e[s], 0, f_eff(s, f, nb))),
                  pl.BlockSpec((1, FF_TILE, d), lambda s, f, e, r0, nb, t: (e[s], f_eff(s, f, nb), 0)),
                  pl.BlockSpec(memory_space=pl.ANY)],
        out_specs=pl.BlockSpec(memory_space=pl.ANY),
        scratch_shapes=[pltpu.VMEM((d, FF_TILE), BF16), pltpu.VMEM((d, FF_TILE), BF16),
                        pltpu.VMEM((FF_TILE, d), BF16), pltpu.VMEM((2, SUPER_ROWS, half), jnp.uint32),
                        pltpu.VMEM((SUPER_ROWS, d), F32), pltpu.SMEM((1,), I32),
                        pltpu.SemaphoreType.DMA(()), pltpu.SemaphoreType.DMA(())],
    )
    return pl.pallas_call(
        _moe_expert_body,
        grid_spec=grid_spec,
        out_shape=jax.ShapeDtypeStruct((p, d), F32),
        compiler_params=_cparams(("arbitrary", "arbitrary")),
        name="moe_experts",
    )(sb_expert, sb_row0, sb_blocks, total_rows, wg, wu, wd, buf)


def _moe_combine_body(dest_ref, x1_ref, route_ref, out_ref, y_ref, rows_ref, sem):
    tm = x1_ref.shape[0]
    base = pl.program_id(0) * tm * TOP_K

    def row_copy(r, k):
        slot = dest_ref[base + r * TOP_K + k]
        return pltpu.make_async_copy(out_ref.at[pl.ds(slot, 1)], rows_ref.at[k, pl.ds(r, 1)], sem)

    def start(r, c):
        for k in range(TOP_K):
            row_copy(r, k).start()
        return c

    def wait(r, c):
        for k in range(TOP_K):
            row_copy(r, k).wait()
        return c

    lax.fori_loop(0, tm, start, 0, unroll=4)
    lax.fori_loop(0, tm, wait, 0, unroll=4)
    route = route_ref[...]
    y_ref[...] = x1_ref[...] + (rows_ref[0] * route[:, 0:1] + rows_ref[1] * route[:, 1:2])


def moe_combine(dest, x1, route, expert_out, *, tm=256):
    n, d = x1.shape
    grid_spec = pltpu.PrefetchScalarGridSpec(
        num_scalar_prefetch=1,
        grid=(n // tm,),
        in_specs=[pl.BlockSpec((tm, d), lambda i, *_: (i, 0)),
                  pl.BlockSpec((tm, LANES), lambda i, *_: (i, 0)),
                  pl.BlockSpec(memory_space=pl.ANY)],
        out_specs=pl.BlockSpec((tm, d), lambda i, *_: (i, 0)),
        scratch_shapes=[pltpu.VMEM((TOP_K, tm, d), F32), pltpu.SemaphoreType.DMA(())],
    )
    return pl.pallas_call(
        _moe_combine_body,
        grid_spec=grid_spec,
        out_shape=jax.ShapeDtypeStruct((n, d), F32),
        compiler_params=_cparams(("arbitrary",)),
        name="moe_combine",
    )(dest, x1, route, expert_out)


def hierarchical_moe(x1, h2, route, wg, wu, wd):
    n = h2.shape[0]
    n_experts = wg.shape[0]
    m = n * TOP_K
    ids_row = route[:, 2:4].astype(I32).reshape(1, m)
    rank_row, counts = moe_rank(ids_row, n_experts)
    counts = counts[:, 0].astype(I32)
    padded = (counts + MOE_BLOCK - 1) // MOE_BLOCK * MOE_BLOCK
    pad_end = jnp.cumsum(padded)
    pad_start = pad_end - padded
    dest = moe_dest(ids_row, rank_row, pad_start.astype(F32).reshape(n_experts, 1)).reshape(m)
    n_slots = m + n_experts * MOE_BLOCK
    n_sb = (padded + SUPER_ROWS - 1) // SUPER_ROWS
    sb_end = jnp.cumsum(n_sb)
    sb_start = sb_end - n_sb
    s_idx = jnp.arange(n_experts + m // SUPER_ROWS, dtype=I32)
    used = s_idx < sb_end[-1]
    e_of = jnp.minimum(jnp.sum(sb_end[None, :] <= s_idx[:, None], axis=1), n_experts - 1)
    piece = s_idx - sb_start[e_of]
    sb_row0 = jnp.where(used, pad_start[e_of] + piece * SUPER_ROWS, 0).astype(I32)
    sb_rows = jnp.clip(padded[e_of] - piece * SUPER_ROWS, 0, SUPER_ROWS)
    sb_blocks = jnp.where(used, sb_rows // MOE_BLOCK, 0).astype(I32)
    sb_expert = jnp.where(used, e_of, e_of[jnp.maximum(sb_end[-1] - 1, 0)]).astype(I32)
    buf = moe_dispatch(dest, h2, n_slots)
    expert_out = moe_experts(sb_expert, sb_row0, sb_blocks, pad_end[-1:].astype(I32), buf,
                             wg, wu, wd)
    return moe_combine(dest, x1, route, expert_out)


def _pad_cols(w, to):
    return jnp.pad(w, ((0, 0), (0, to - w.shape[1])))


def _pad_rows(w, to):
    return jnp.pad(w, ((0, to - w.shape[0]), (0, 0)))


def _layer(x2d, batch, norm1_w, w_in, mu, w0, w2, a0, a2, g2, k_k, k_a, r_k, gn_w, gn_b,
           q_norm_w, k_norm_w, sinks, proj_rwkv, proj_attn, w_out, norm2_w, wc, bc, wf, bf,
           wg, wu, wd):
    n, d = x2d.shape
    seq = n // batch
    width = w0.shape[0]
    dl, il, gl = w2.shape[0], a2.shape[0], g2.shape[0]
    q_width = proj_attn.shape[0]
    rwkv_cols = 3 * width + dl + il + gl
    kv_width = (w_in.shape[1] - rwkv_cols - q_width - 2 * d) // 2
    row = lambda v: v.reshape(1, -1).astype(F32)

    c0 = 3 * width
    w_rwkv = jnp.concatenate([w_in[:, :c0], _pad_cols(w_in[:, c0:c0 + dl], LANES),
                              _pad_cols(w_in[:, c0 + dl:c0 + dl + il], LANES),
                              w_in[:, c0 + dl + il:rwkv_cols]], axis=1).astype(BF16)
    mu_p = jnp.concatenate([mu[:c0], jnp.pad(mu[c0:c0 + dl], (0, LANES - dl)),
                            jnp.pad(mu[c0 + dl:c0 + dl + il], (0, LANES - il)),
                            mu[c0 + dl + il:]]).reshape(1, -1)
    q0 = rwkv_cols
    w_qkv = w_in[:, q0:q0 + q_width + 2 * kv_width].astype(BF16)
    w_gates = w_in[:, q0 + q_width + 2 * kv_width:].astype(BF16)

    g1 = row(norm1_w)
    p_rwkv = norm_proj(x2d, g1, w_rwkv, out_dtype=F32, sigmoid=False)
    qkv = norm_proj(x2d, g1, w_qkv, out_dtype=BF16, sigmoid=False)
    gates = norm_proj(x2d, g1, w_gates, out_dtype=BF16, sigmoid=True)

    rt, at, bt, kt, vv, g, bonus, pc = rwkv_prep(
        p_rwkv, mu_p, row(w0), _pad_rows(w2, LANES).astype(BF16), row(a0),
        _pad_rows(a2, LANES).astype(BF16), g2.astype(BF16), row(k_k), row(k_a), row(r_k),
        seq_len=seq)
    a_out = rwkv_scan(rt, at, bt, kt, vv, pc, g, bonus, row(gn_w), row(gn_b), batch=batch)

    b_out = swa_attention(qkv, q_norm_w, k_norm_w, sinks.astype(F32), batch=batch,
                          q_width=q_width, kv_width=kv_width)

    n_groups, n_experts = wc.shape[1], wf.shape[1]
    router_w = _pad_cols(jnp.concatenate([wc, wf], axis=1), LANES)
    router_b = _pad_cols(jnp.concatenate([bc, bf]).reshape(1, -1), LANES)
    assert n_groups == N_GROUPS and n_experts == N_GROUPS * GROUP_SIZE
    x1, h2, route = merge_route(a_out, b_out, gates, x2d, proj_rwkv.astype(BF16),
                                proj_attn.astype(BF16), w_out.astype(BF16), row(norm2_w),
                                router_w, router_b)
    return hierarchical_moe(x1, h2, route, wg, wu, wd)


def kernel(x, norm1_w, w_in, rwkv_mu, rwkv_w0, rwkv_w2, rwkv_a0, rwkv_a2, rwkv_g2, rwkv_k_k,
           rwkv_k_a, rwkv_r_k, rwkv_gn_w, rwkv_gn_b, q_norm_w, k_norm_w, attn_sinks, proj_rwkv,
           proj_attn, w_out, norm2_w, router_coarse_w, router_coarse_b, router_fine_w,
           router_fine_b, expert_w_gate, expert_w_up, expert_w_down):
    batch, seq, d = x.shape
    x2d = x.reshape(batch * seq, d)
    for layer in range(norm1_w.shape[0]):
        x2d = _layer(x2d, batch, norm1_w[layer], w_in[layer], rwkv_mu[layer], rwkv_w0[layer],
                     rwkv_w2[layer], rwkv_a0[layer], rwkv_a2[layer], rwkv_g2[layer],
                     rwkv_k_k[layer], rwkv_k_a[layer], rwkv_r_k[layer].reshape(-1),
                     rwkv_gn_w[layer], rwkv_gn_b[layer], q_norm_w[layer], k_norm_w[layer],
                     attn_sinks[layer], proj_rwkv[layer], proj_attn[layer], w_out[layer],
                     norm2_w[layer], router_coarse_w[layer], router_coarse_b[layer],
                     router_fine_w[layer], router_fine_b[layer], expert_w_gate[layer],
                     expert_w_up[layer], expert_w_down[layer])
    return x2d.reshape(batch, seq, d)
```

```python
import functools

import jax
import jax.numpy as jnp
from jax import lax
from jax.experimental import pallas as pl
from jax.experimental.pallas import tpu as pltpu

F32 = jnp.float32
BF16 = jnp.bfloat16
I32 = jnp.int32

NORM_EPS = 1e-6
GN_EPS = 64e-5
HEAD_DIM = 64
LANES = 128
CHUNK = 64
WINDOW = 128
MOE_BLOCK = 128
SUPER_ROWS = 1024
FF_TILE = 512
TOP_K = 2
N_GROUPS = 8
GROUP_SIZE = 8
VMEM_LIMIT = 56 * 1024 * 1024

_NT = (((1,), (1,)), ((), ()))
_TN = (((0,), (0,)), ((), ()))


def _dot(a, b):
    return jnp.dot(a, b, preferred_element_type=F32)


def _split2(x):
    hi = x.astype(BF16)
    mid = (x - hi.astype(F32)).astype(BF16)
    return hi, mid


def _split3(x):
    hi = x.astype(BF16)
    r1 = x - hi.astype(F32)
    mid = r1.astype(BF16)
    lo = (r1 - mid.astype(F32)).astype(BF16)
    return hi, mid, lo


def _dot_exact_rhs(x, m):
    hi, mid, lo = _split3(x)
    return _dot(hi, m) + _dot(mid, m) + _dot(lo, m)


def _exact_lhs_dot(m, x):
    hi, mid, lo = _split3(x)
    return _dot(m, hi) + _dot(m, mid) + _dot(m, lo)


def _pack_bf16_halves(x):
    w = x.shape[1] // 2
    lo = lax.bitcast_convert_type(x[:, :w].astype(BF16).astype(F32), jnp.uint32)
    hi = lax.bitcast_convert_type(x[:, w:].astype(BF16).astype(F32), jnp.uint32)
    return (lo >> 16) | (hi & jnp.uint32(0xFFFF0000))


def _unpack_bf16_halves(xp):
    lo = lax.bitcast_convert_type(xp << 16, F32).astype(BF16)
    hi = lax.bitcast_convert_type(xp & jnp.uint32(0xFFFF0000), F32).astype(BF16)
    return lo, hi


def _cparams(sem, vmem=VMEM_LIMIT):
    return pltpu.CompilerParams(dimension_semantics=sem, vmem_limit_bytes=vmem)


def _norm_proj_body(x_ref, g_ref, w_ref, o_ref, h_ref, *, sigmoid):
    @pl.when(pl.program_id(1) == 0)
    def _():
        x = x_ref[...]
        ms = jnp.mean(x * x, axis=-1, keepdims=True)
        h_ref[...] = (x * lax.rsqrt(ms + NORM_EPS) * g_ref[...]).astype(BF16)

    acc = _dot(h_ref[...], w_ref[...])
    if sigmoid:
        acc = jax.nn.sigmoid(acc)
    o_ref[...] = acc.astype(o_ref.dtype)


def norm_proj(x2d, gain, w, *, out_dtype, sigmoid, tm=1024, tn=512):
    n, d = x2d.shape
    c = w.shape[1]
    tm = min(tm, n)
    return pl.pallas_call(
        functools.partial(_norm_proj_body, sigmoid=sigmoid),
        grid=(n // tm, c // tn),
        in_specs=[pl.BlockSpec((tm, d), lambda i, j: (i, 0)),
                  pl.BlockSpec((1, d), lambda i, j: (0, 0)),
                  pl.BlockSpec((d, tn), lambda i, j: (0, j))],
        out_specs=pl.BlockSpec((tm, tn), lambda i, j: (i, j)),
        out_shape=jax.ShapeDtypeStruct((n, c), out_dtype),
        scratch_shapes=[pltpu.VMEM((tm, d), BF16)],
        compiler_params=_cparams(("parallel", "arbitrary")),
        name="norm_proj",
    )(x2d, gain, w)


def _head_sum(x, e, et):
    s = _dot(x.astype(BF16), e)
    hi, mid = _split2(s)
    return _dot(hi, et) + _dot(mid, et)


def _rwkv_prep_body(p_ref, pprev_ref, mu_ref, w0_ref, w2_ref, a0_ref, a2_ref, g2_ref,
                    kk_ref, ka_ref, rk_ref, e_ref, et_ref, tri_ref, sel_ref,
                    rt_ref, at_ref, bt_ref, kt_ref, v_ref, g_ref, bonus_ref, pc_ref,
                    *, seq_len, width):
    tm = p_ref.shape[0]
    w_ = width
    first = (pl.program_id(0) * tm) % seq_len == 0
    p = p_ref[...]
    prev_row = jnp.where(first, 0.0, pprev_ref[7:8, :])
    row = lax.broadcasted_iota(I32, (tm, 1), 0)
    shifted = jnp.where(row == 0, prev_row, pltpu.roll(p, 1, 0))
    m = p + (shifted - p) * mu_ref[...]
    r = m[:, 0:w_]
    k = m[:, w_:2 * w_]
    v = m[:, 2 * w_:3 * w_]
    xw = m[:, 3 * w_:3 * w_ + 128]
    xa = m[:, 3 * w_ + 128:3 * w_ + 256]
    xg = m[:, 3 * w_ + 256:]

    z = -(w0_ref[...] + _dot(jnp.tanh(xw).astype(BF16), w2_ref[...]))
    softplus = jnp.maximum(z, 0.0) + jnp.log1p(jnp.exp(-jnp.abs(z)))
    logw = -jnp.exp(-softplus - 0.5)
    a = jax.nn.sigmoid(a0_ref[...] + _dot(xa.astype(BF16), a2_ref[...]))
    g = _dot(jax.nn.sigmoid(xg).astype(BF16), g2_ref[...])

    e = e_ref[...]
    et = et_ref[...]
    kk = k * kk_ref[...]
    kk = kk / jnp.maximum(jnp.sqrt(_head_sum(kk * kk, e, et)), 1e-12)
    kmod = k * (1.0 + (a - 1.0) * ka_ref[...])
    bonus = _head_sum(r * kmod * rk_ref[...], e, et) * v

    cum = _exact_lhs_dot(tri_ref[...], logw)
    pc_ref[...] = jnp.exp(_exact_lhs_dot(sel_ref[...], logw))
    inv = jnp.exp(-cum)
    rt_ref[...] = (r * jnp.exp(cum)).astype(BF16)
    at_ref[...] = (-kk * jnp.exp(cum - logw)).astype(BF16)
    bt_ref[...] = (kk * a * inv).astype(BF16)
    kt_ref[...] = (kmod * inv).astype(BF16)
    v_ref[...] = v.astype(BF16)
    g_ref[...] = g.astype(BF16)
    bonus_ref[...] = bonus


def _head_indicator(width):
    heads = width // HEAD_DIM
    c = jnp.arange(width)[:, None] // HEAD_DIM
    h = jnp.arange(LANES)[None, :]
    e = (c == h).astype(BF16)
    assert heads <= LANES
    return e, e.T


def rwkv_prep(p, mu, w0, w2, a0, a2, g2, k_k, k_a, r_k, *, seq_len, tm=512):
    n, cols = p.shape
    width = w0.shape[1]
    nchunk = tm // CHUNK
    e, et = _head_indicator(width)
    t = jnp.arange(tm)
    same = (t[:, None] // CHUNK) == (t[None, :] // CHUNK)
    tri = (same & (t[:, None] >= t[None, :])).astype(BF16)
    sel = ((t[None, :] // CHUNK) == jnp.arange(nchunk)[:, None]).astype(BF16)
    const = lambda shape: pl.BlockSpec(shape, lambda i: (0, 0))
    stream = lambda dt: jax.ShapeDtypeStruct((n, width), dt)
    outs = pl.pallas_call(
        functools.partial(_rwkv_prep_body, seq_len=seq_len, width=width),
        grid=(n // tm,),
        in_specs=[pl.BlockSpec((tm, cols), lambda i: (i, 0)),
                  pl.BlockSpec((8, cols), lambda i: (jnp.maximum(i * (tm // 8) - 1, 0), 0)),
                  const((1, cols)), const((1, width)), const(w2.shape), const((1, width)),
                  const(a2.shape), const(g2.shape), const((1, width)), const((1, width)),
                  const((1, width)), const(e.shape), const(et.shape), const(tri.shape),
                  const(sel.shape)],
        out_specs=[pl.BlockSpec((tm, width), lambda i: (i, 0))] * 7
                  + [pl.BlockSpec((nchunk, width), lambda i: (i, 0))],
        out_shape=[stream(BF16)] * 6 + [stream(F32),
                   jax.ShapeDtypeStruct((n // CHUNK, width), F32)],
        compiler_params=_cparams(("parallel",)),
        name="rwkv_prep",
    )(p, p, mu, w0, w2, a0, a2, g2, k_k, k_a, r_k, e, et, tri, sel)
    return outs


def _chunk_pairs(rts, ats, bts, kts, vvs, pcs, s_prevs, masks):
    lo, strict, incl = masks
    c = rts[0].shape[0]
    c2 = 2 * c
    zero = jnp.zeros_like(rts[0])
    each = lambda f, *ls: [f(*a) for a in zip(*ls)]

    def stack(x):
        return jnp.concatenate([jnp.where(lo, x, zero), jnp.where(lo, zero, x)], axis=0)

    ar = each(lambda a, r: jnp.concatenate([stack(a), stack(r)], axis=0), ats, rts)
    bk = each(lambda b, k: jnp.concatenate([stack(b), stack(k)], axis=0), bts, kts)
    vb = each(stack, vvs)
    gram = each(lambda x, y: lax.dot_general(x, y, _NT, preferred_element_type=F32), ar, bk)
    ars = each(lambda x, s: lax.dot_general(x, s.astype(BF16), _NT, preferred_element_type=F32),
               ar, s_prevs)
    a_ab = each(lambda g: jnp.where(strict, g[:c2, :c2], 0.0), gram)
    rhs = each(lambda g, v, x: x[:c2] + _dot(jnp.where(strict, g[:c2, c2:], 0.0).astype(BF16), v),
               gram, vb, ars)

    u = rhs
    nk = a_ab
    steps = (c - 1).bit_length()
    for k in range(steps):
        nkb = each(lambda n: n.astype(BF16), nk)
        if k + 1 < steps:
            prod = each(lambda n, uu: _dot(n, jnp.concatenate([n, uu.astype(BF16)], axis=1)),
                        nkb, u)
            nk = each(lambda p: p[:, :c2], prod)
            u = each(lambda uu, p: uu + p[:, c2:], u, prod)
        else:
            u = each(lambda uu, n: uu + _dot(n, uu.astype(BF16)), u, nkb)

    uv = each(lambda uu, v: jnp.concatenate([uu.astype(BF16), v], axis=0), u, vb)
    a_r = each(lambda g: jnp.where(incl, g[c2:, :], 0.0).astype(BF16), gram)
    ybd = each(lambda x, a, w: x[c2:] + _dot(a, w), ars, a_r, uv)
    ys = each(lambda yb: yb[:c] + yb[c:], ybd)

    bk_end = each(lambda x, pc: (x.astype(F32) * pc).astype(BF16), bk, pcs)
    s_new = each(lambda s, pc, w, x: s * pc + lax.dot_general(w, x, _TN, preferred_element_type=F32),
                 s_prevs, pcs, uv, bk_end)
    return ys, s_new


def _rwkv_scan_body(rt_ref, at_ref, bt_ref, kt_ref, v_ref, pc_ref, g_ref, bonus_ref,
                    gnw_ref, gnb_ref, e_ref, et_ref, o_ref, s_ref, y_ref):
    tr, width = rt_ref.shape
    npairs = width // LANES

    @pl.when(pl.program_id(1) == 0)
    def _():
        s_ref[...] = jnp.zeros_like(s_ref)

    lane = lax.broadcasted_iota(I32, (CHUNK, LANES), 1)
    lo = lane < HEAD_DIM
    ri = lax.broadcasted_iota(I32, (2 * CHUNK, 2 * CHUNK), 0)
    ci = lax.broadcasted_iota(I32, (2 * CHUNK, 2 * CHUNK), 1)
    same = (ri // CHUNK) == (ci // CHUNK)
    ri2 = lax.broadcasted_iota(I32, (2 * CHUNK, 4 * CHUNK), 0)
    ci2 = lax.broadcasted_iota(I32, (2 * CHUNK, 4 * CHUNK), 1) % (2 * CHUNK)
    incl = ((ri2 // CHUNK) == (ci2 // CHUNK)) & (ri2 >= ci2)
    masks = (lo, same & (ri > ci), incl)

    def chunk_step(c, carry):
        rows = pl.ds(pl.multiple_of(c * CHUNK, CHUNK), CHUNK)
        this_chunk = lax.broadcasted_iota(I32, (pc_ref.shape[0], 1), 0) == c
        cols = [slice(pr * LANES, (pr + 1) * LANES) for pr in range(npairs)]
        pcs = [jnp.sum(jnp.where(this_chunk, pc_ref[:, cl], 0.0), axis=0, keepdims=True)
               for cl in cols]
        load = lambda ref: [ref[rows, cl] for cl in cols]
        ys, s_new = _chunk_pairs(load(rt_ref), load(at_ref), load(bt_ref), load(kt_ref),
                                 load(v_ref), pcs, [s_ref[pr] for pr in range(npairs)], masks)
        for pr in range(npairs):
            s_ref[pr] = s_new[pr]
            y_ref[rows, cols[pr]] = ys[pr]
        return carry

    lax.fori_loop(0, tr // CHUNK, chunk_step, 0)

    e = e_ref[...]
    et = et_ref[...]
    y = y_ref[...]
    mean = _head_sum(y, e, et) * (1.0 / HEAD_DIM)
    d = y - mean
    var = _head_sum(d * d, e, et) * (1.0 / HEAD_DIM)
    out = d * lax.rsqrt(var + GN_EPS) * gnw_ref[...] + gnb_ref[...]
    out = (out + bonus_ref[...]) * g_ref[...].astype(F32)
    o_ref[...] = out.astype(o_ref.dtype)


def rwkv_scan(rt, at, bt, kt, vv, pc, g, bonus, gn_w, gn_b, *, batch, tr=512):
    n, width = rt.shape
    seq = n // batch
    nst = seq // tr
    e, et = _head_indicator(width)
    blk = lambda rows: pl.BlockSpec((rows, width), lambda b, s: (b * nst + s, 0))
    const = lambda shape: pl.BlockSpec(shape, lambda b, s: (0, 0))
    return pl.pallas_call(
        _rwkv_scan_body,
        grid=(batch, nst),
        in_specs=[blk(tr)] * 5 + [blk(tr // CHUNK), blk(tr), blk(tr),
                  const((1, width)), const((1, width)), const(e.shape), const(et.shape)],
        out_specs=blk(tr),
        out_shape=jax.ShapeDtypeStruct((n, width), BF16),
        scratch_shapes=[pltpu.VMEM((width // LANES, LANES, LANES), F32),
                        pltpu.VMEM((tr, width), F32)],
        compiler_params=_cparams(("parallel", "arbitrary")),
        name="rwkv_scan",
    )(rt, at, bt, kt, vv, pc, g, bonus, gn_w, gn_b, e, et)


def _head_rmsnorm(x, gain, bd):
    hi, mid = _split2(x * x)
    ms = (_dot(hi, bd) + _dot(mid, bd)) * (1.0 / HEAD_DIM)
    return x * lax.rsqrt(ms + NORM_EPS) * gain


def _swa_body(sink_ref, q_ref, kp_ref, kc_ref, vp_ref, vc_ref, qg_ref, kg_ref, bd_ref, o_ref,
              *, group):
    blk = q_ref.shape[0]
    n = pl.program_id(1)
    bd = bd_ref[...]
    scale = HEAD_DIM ** -0.5
    lane = lax.broadcasted_iota(I32, (1, LANES), 1)
    lo = lane < HEAD_DIM
    ri = lax.broadcasted_iota(I32, (2 * blk, 2 * blk), 0)
    cj = lax.broadcasted_iota(I32, (2 * blk, 2 * blk), 1)
    rel = blk + (ri % blk) - cj
    jmin = jnp.where(n > 0, 0, blk)
    valid = (rel >= 0) & (rel < WINDOW) & (cj >= jmin)
    top_rows = lax.broadcasted_iota(I32, (2 * blk, 1), 0) < blk

    kcat = jnp.concatenate([kp_ref[...], kc_ref[...]], axis=0).astype(F32)
    vcat = jnp.concatenate([vp_ref[...], vc_ref[...]], axis=0).astype(F32)
    kv_heads = kcat.shape[1] // HEAD_DIM
    q_tiles = q_ref.shape[1] // LANES
    tiles_per_kv = group // 2
    each = lambda f, *ls: [f(*a) for a in zip(*ls)]

    kv_cols = [slice(jt * LANES, (jt + 1) * LANES) for jt in range(kv_heads // 2)]
    kn = [_head_rmsnorm(kcat[:, c], kg_ref[...], bd) for c in kv_cols]
    kn_r = [pltpu.roll(x, HEAD_DIM, 1) for x in kn]
    vt = [vcat[:, c] for c in kv_cols]
    vt_r = [pltpu.roll(x, HEAD_DIM, 1) for x in vt]
    own = lambda hk: lo if hk % 2 == 0 else jnp.logical_not(lo)
    k2 = [jnp.where(own(hk), kn[hk // 2], kn_r[hk // 2]).astype(BF16) for hk in range(kv_heads)]
    v2 = [jnp.where(own(hk), vt[hk // 2], vt_r[hk // 2]).astype(BF16) for hk in range(kv_heads)]

    q_cols = [slice(t * LANES, (t + 1) * LANES) for t in range(q_tiles)]
    qn = [_head_rmsnorm(q_ref[:, c].astype(F32), qg_ref[...] * scale, bd) for c in q_cols]
    qst = [jnp.concatenate([jnp.where(lo, x, 0.0), jnp.where(lo, 0.0, x)], axis=0).astype(BF16)
           for x in qn]
    s = [jnp.where(valid, lax.dot_general(x, k2[t // tiles_per_kv], _NT,
                                          preferred_element_type=F32), -jnp.inf)
         for t, x in enumerate(qst)]
    sink = [jnp.where(top_rows, sink_ref[2 * t], sink_ref[2 * t + 1]) for t in range(q_tiles)]
    mx = each(lambda x, sk: jnp.maximum(jnp.max(x, axis=-1, keepdims=True), sk), s, sink)
    pr = each(lambda x, m: jnp.exp(x - m), s, mx)
    inv = each(lambda p, sk, m: 1.0 / (jnp.sum(p, axis=-1, keepdims=True) + jnp.exp(sk - m)),
               pr, sink, mx)
    o = [_dot((p * r).astype(BF16), v2[t // tiles_per_kv])
         for t, (p, r) in enumerate(zip(pr, inv))]
    for t in range(q_tiles):
        o_ref[:, q_cols[t]] = jnp.where(lo, o[t][:blk], o[t][blk:]).astype(o_ref.dtype)


def swa_attention(qkv, q_gain, k_gain, sinks, *, batch, q_width, kv_width):
    n = qkv.shape[0]
    blk = WINDOW
    nb = n // batch // blk
    group = (q_width // HEAD_DIM) // (kv_width // HEAD_DIM)
    kcol = q_width // kv_width
    t = jnp.arange(LANES)
    bd = ((t[:, None] // HEAD_DIM) == (t[None, :] // HEAD_DIM)).astype(BF16)
    qg = jnp.tile(q_gain.reshape(1, HEAD_DIM), (1, 2))
    kg = jnp.tile(k_gain.reshape(1, HEAD_DIM), (1, 2))
    cur = lambda col: (lambda b, i, *_: (b * nb + i, col))
    prev = lambda col: (lambda b, i, *_: (b * nb + jnp.maximum(i - 1, 0), col))
    const = lambda shape: pl.BlockSpec(shape, lambda b, i, *_: (0, 0))
    grid_spec = pltpu.PrefetchScalarGridSpec(
        num_scalar_prefetch=1,
        grid=(batch, nb),
        in_specs=[pl.BlockSpec((blk, q_width), cur(0)),
                  pl.BlockSpec((blk, kv_width), prev(kcol)),
                  pl.BlockSpec((blk, kv_width), cur(kcol)),
                  pl.BlockSpec((blk, kv_width), prev(kcol + 1)),
                  pl.BlockSpec((blk, kv_width), cur(kcol + 1)),
                  const((1, LANES)), const((1, LANES)), const((LANES, LANES))],
        out_specs=pl.BlockSpec((blk, q_width), cur(0)),
    )
    return pl.pallas_call(
        functools.partial(_swa_body, group=group),
        grid_spec=grid_spec,
        out_shape=jax.ShapeDtypeStruct((n, q_width), BF16),
        compiler_params=_cparams(("parallel", "arbitrary")),
        name="swa_attn",
    )(sinks, qkv, qkv, qkv, qkv, qkv, qg, kg, bd)


def _merge_route_body(a_ref, b_ref, ga_ref, gb_ref, x_ref, pr_ref, pa_ref, wo_ref, n2_ref,
                      rhm_ref, rh_ref, rb_ref, x1_ref, h2_ref, route_ref):
    merged = (ga_ref[...].astype(F32) * _dot(a_ref[...], pr_ref[...])
              + gb_ref[...].astype(F32) * _dot(b_ref[...], pa_ref[...]))
    x1 = x_ref[...] + _dot(merged.astype(BF16), wo_ref[...])
    x1_ref[...] = x1
    ms = jnp.mean(x1 * x1, axis=-1, keepdims=True)
    h2 = x1 * lax.rsqrt(ms + NORM_EPS) * n2_ref[...]
    h2_ref[...] = _pack_bf16_halves(h2)

    hi, mid = _split2(h2)
    hw = _dot(hi, rhm_ref[...])
    logits = hw[:, :LANES] + (hw[:, LANES:] + _dot(mid, rh_ref[...])) + rb_ref[...]

    lane = lax.broadcasted_iota(I32, logits.shape, 1)
    big = jnp.int32(1 << 20)
    neg = -jnp.inf
    is_coarse = lane < N_GROUPS
    cl = jnp.where(is_coarse, logits, neg)
    ce = jnp.exp(cl - jnp.max(cl, axis=-1, keepdims=True))
    cp = ce / jnp.sum(ce, axis=-1, keepdims=True)
    g_prob = jnp.max(cp, axis=-1, keepdims=True)
    g_idx = jnp.min(jnp.where(is_coarse & (cp == g_prob), lane, big), axis=-1, keepdims=True)

    fine_lane = lane - N_GROUPS
    in_group = ((lane >= N_GROUPS) & (lane < N_GROUPS + N_GROUPS * GROUP_SIZE)
                & ((fine_lane // GROUP_SIZE) == g_idx))
    fl = jnp.where(in_group, logits, neg)
    fe = jnp.exp(fl - jnp.max(fl, axis=-1, keepdims=True))
    fp = fe / jnp.sum(fe, axis=-1, keepdims=True)
    p1 = jnp.max(jnp.where(in_group, fp, -1.0), axis=-1, keepdims=True)
    i1 = jnp.min(jnp.where(in_group & (fp == p1), lane, big), axis=-1, keepdims=True)
    rest = in_group & (lane != i1)
    p2 = jnp.max(jnp.where(rest, fp, -1.0), axis=-1, keepdims=True)
    i2 = jnp.min(jnp.where(rest & (fp == p2), lane, big), axis=-1, keepdims=True)
    den = p1 + p2
    w1 = g_prob * p1 / den
    w2 = g_prob * p2 / den
    e1 = (i1 - N_GROUPS).astype(F32)
    e2 = (i2 - N_GROUPS).astype(F32)
    route_ref[...] = jnp.where(lane == 0, w1, jnp.where(lane == 1, w2,
                     jnp.where(lane == 2, e1, jnp.where(lane == 3, e2, 0.0))))


def merge_route(a_out, b_out, gates, x2d, proj_r, proj_a, w_out, norm2_w, router_w, router_b,
                *, tm=256):
    n, d = x2d.shape
    wa = a_out.shape[1]
    rh, rm = _split2(router_w)
    rhm = jnp.concatenate([rh, rm], axis=1)
    const = lambda arr: pl.BlockSpec(arr.shape, lambda i: (0, 0), pipeline_mode=pl.Buffered(1))
    row = lambda w: pl.BlockSpec((tm, w), lambda i: (i, 0))
    return pl.pallas_call(
        _merge_route_body,
        grid=(n // tm,),
        in_specs=[row(wa), row(wa), pl.BlockSpec((tm, d), lambda i: (i, 0)),
                  pl.BlockSpec((tm, d), lambda i: (i, 1)), row(d),
                  const(proj_r), const(proj_a), const(w_out), const(norm2_w),
                  const(rhm), const(rh), const(router_b)],
        out_specs=[row(d), row(d // 2), row(LANES)],
        out_shape=[jax.ShapeDtypeStruct((n, d), F32), jax.ShapeDtypeStruct((n, d // 2), jnp.uint32),
                   jax.ShapeDtypeStruct((n, LANES), F32)],
        compiler_params=_cparams(("parallel",)),
        name="merge_route",
    )(a_out, b_out, gates, gates, x2d, proj_r, proj_a, w_out, norm2_w, rhm, rh, router_b)


def _moe_rank_body(ids_ref, ut_ref, rank_ref, counts_ref, carry_ref):
    @pl.when(pl.program_id(0) == 0)
    def _():
        carry_ref[...] = jnp.zeros_like(carry_ref)

    ids = ids_ref[...]
    n_exp = carry_ref.shape[0]
    sub = lax.broadcasted_iota(I32, (n_exp, ids.shape[1]), 0)
    onehot = (sub == ids).astype(F32)
    before = _dot(onehot.astype(BF16), ut_ref[...])
    carry = carry_ref[...]
    rank_ref[...] = jnp.sum(onehot * (before + carry), axis=0, keepdims=True).astype(I32)
    carry = carry + jnp.sum(onehot, axis=1, keepdims=True)
    carry_ref[...] = carry
    counts_ref[...] = jnp.broadcast_to(carry, counts_ref.shape)


def moe_rank(ids_row, n_experts, *, tb=512):
    m = ids_row.shape[1]
    t = jnp.arange(tb)
    ut = (t[:, None] < t[None, :]).astype(BF16)
    return pl.pallas_call(
        _moe_rank_body,
        grid=(m // tb,),
        in_specs=[pl.BlockSpec((1, tb), lambda i: (0, i)),
                  pl.BlockSpec((tb, tb), lambda i: (0, 0))],
        out_specs=[pl.BlockSpec((1, tb), lambda i: (0, i)),
                   pl.BlockSpec((n_experts, LANES), lambda i: (0, 0))],
        out_shape=[jax.ShapeDtypeStruct((1, m), I32),
                   jax.ShapeDtypeStruct((n_experts, LANES), F32)],
        scratch_shapes=[pltpu.VMEM((n_experts, 1), F32)],
        compiler_params=_cparams(("arbitrary",)),
        name="moe_rank",
    )(ids_row, ut)


def _moe_dest_body(ids_ref, rank_ref, start_ref, dest_ref):
    ids = ids_ref[...]
    sub = lax.broadcasted_iota(I32, (start_ref.shape[0], ids.shape[1]), 0)
    start = jnp.sum(jnp.where(sub == ids, start_ref[...], 0.0), axis=0, keepdims=True)
    dest_ref[...] = start.astype(I32) + rank_ref[...]


def moe_dest(ids_row, rank_row, pad_start_col, *, tb=512):
    m = ids_row.shape[1]
    row = pl.BlockSpec((1, tb), lambda i: (0, i))
    return pl.pallas_call(
        _moe_dest_body,
        grid=(m // tb,),
        in_specs=[row, row, pl.BlockSpec(pad_start_col.shape, lambda i: (0, 0))],
        out_specs=row,
        out_shape=jax.ShapeDtypeStruct((1, m), I32),
        compiler_params=_cparams(("parallel",)),
        name="moe_dest",
    )(ids_row, rank_row, pad_start_col)


def _moe_dispatch_body(dest_ref, h_ref, buf_in_ref, buf_ref, sem):
    del buf_in_ref
    tm = h_ref.shape[0]
    base = pl.program_id(0) * tm * TOP_K

    def row_copy(r, k):
        slot = dest_ref[base + r * TOP_K + k]
        return pltpu.make_async_copy(h_ref.at[pl.ds(r, 1)], buf_ref.at[pl.ds(slot, 1)], sem)

    def start(r, c):
        for k in range(TOP_K):
            row_copy(r, k).start()
        return c

    def wait(r, c):
        for k in range(TOP_K):
            row_copy(r, k).wait()
        return c

    lax.fori_loop(0, tm, start, 0, unroll=4)
    lax.fori_loop(0, tm, wait, 0, unroll=4)


def moe_dispatch(dest, h2, n_slots, *, tm=256):
    n, d = h2.shape
    buf0 = jnp.zeros((n_slots, d), h2.dtype)
    grid_spec = pltpu.PrefetchScalarGridSpec(
        num_scalar_prefetch=1,
        grid=(n // tm,),
        in_specs=[pl.BlockSpec((tm, d), lambda i, *_: (i, 0)),
                  pl.BlockSpec(memory_space=pl.ANY)],
        out_specs=pl.BlockSpec(memory_space=pl.ANY),
        scratch_shapes=[pltpu.SemaphoreType.DMA(())],
    )
    return pl.pallas_call(
        _moe_dispatch_body,
        grid_spec=grid_spec,
        out_shape=jax.ShapeDtypeStruct((n_slots, d), h2.dtype),
        input_output_aliases={2: 0},
        compiler_params=_cparams(("arbitrary",)),
        name="moe_dispatch",
    )(dest, h2, buf0)


def _moe_expert_body(sbe_ref, row0_ref, nb_ref, tot_ref, wg_ref, wu_ref, wd_ref, buf_ref, out_ref,
                     wgb, wub, wdb, xp, acc, pending, sem_in, sem_out):
    del sbe_ref
    s = pl.program_id(0)
    f = pl.program_id(1)
    n_sb = pl.num_programs(0)
    last_f = pl.num_programs(1) - 1
    nb = nb_ref[s]
    half = xp.shape[2]

    def blk(first_row, r):
        return pl.ds(pl.multiple_of(first_row + r * MOE_BLOCK, MOE_BLOCK), MOE_BLOCK)

    def rows_in(sb, r):
        return pltpu.make_async_copy(buf_ref.at[blk(row0_ref[sb], r)], xp.at[sb % 2, blk(0, r)],
                                     sem_in)

    def rows_out(r):
        return pltpu.make_async_copy(acc.at[blk(0, r)], out_ref.at[blk(row0_ref[s], r)], sem_out)

    def for_blocks(n, fn):
        def body(r, carry):
            fn(r)
            return carry
        lax.fori_loop(0, n, body, 0)

    def drain_out():
        for_blocks(pending[0], lambda r: rows_out(0).wait())
        pending[0] = 0

    def mlp(rows, wg_b, wu_b, wd_b):
        xlo, xhi = _unpack_bf16_halves(xp[s % 2, rows, :])
        hg = _dot(xlo, wg_b[:half, :]) + _dot(xhi, wg_b[half:, :])
        hu = _dot(xlo, wu_b[:half, :]) + _dot(xhi, wu_b[half:, :])
        return _dot((jax.nn.silu(hg) * hu).astype(BF16), wd_b)

    def block(r, first):
        rows = blk(0, r)
        y = mlp(rows, wgb, wub, wdb[...])
        if first:
            acc[rows, :] = y
        else:
            acc[rows, :] += y

    def later_blocks(first):
        def body(r, carry):
            block(r, first)
            return carry
        lax.fori_loop(1, nb, body, 0)

    @pl.when((s == 0) & (f == 0))
    def _():
        pending[0] = 0
        for_blocks(nb, lambda r: rows_in(0, r).start())

    @pl.when(nb > 0)
    def _():
        @pl.when(f == 0)
        def _():
            for_blocks(nb, lambda r: rows_in(s, r).wait())

        wg_b = wg_ref[0].astype(BF16)
        wu_b = wu_ref[0].astype(BF16)
        wd_b = wd_ref[0].astype(BF16)
        wgb[...] = wg_b
        wub[...] = wu_b
        wdb[...] = wd_b
        first_rows = pl.ds(0, MOE_BLOCK)
        y0 = mlp(first_rows, wg_b, wu_b, wd_b)

        @pl.when(f == 0)
        def _():
            drain_out()
            acc[first_rows, :] = y0
            later_blocks(True)

        @pl.when(f > 0)
        def _():
            acc[first_rows, :] += y0
            later_blocks(False)

        @pl.when(f == last_f)
        def _():
            for_blocks(nb, lambda r: rows_out(r).start())
            pending[0] = nb
            nxt = jnp.minimum(s + 1, n_sb - 1)
            nb_next = jnp.where(s + 1 < n_sb, nb_ref[nxt], 0)
            for_blocks(nb_next, lambda r: rows_in(nxt, r).start())

    @pl.when((s == n_sb - 1) & (f == last_f))
    def _():
        drain_out()
        acc[pl.ds(0, MOE_BLOCK), :] = jnp.zeros((MOE_BLOCK, acc.shape[1]), acc.dtype)
        tail0 = tot_ref[0]

        def zero_out(r):
            return pltpu.make_async_copy(acc.at[pl.ds(0, MOE_BLOCK)], out_ref.at[blk(tail0, r)],
                                         sem_out)

        ntail = (out_ref.shape[0] - tail0) // MOE_BLOCK
        for_blocks(ntail, lambda r: zero_out(r).start())
        for_blocks(ntail, lambda r: zero_out(r).wait())


def moe_experts(sb_expert, sb_row0, sb_blocks, total_rows, buf, wg, wu, wd):
    p, half = buf.shape
    d = 2 * half
    ff = wg.shape[2]
    nf = ff // FF_TILE
    f_eff = lambda s, f, nb: jnp.where(nb[s] > 0, f, nf - 1)
    grid_spec = pltpu.PrefetchScalarGridSpec(
        num_scalar_prefetch=4,
        grid=(sb_expert.shape[0], nf),
        in_specs=[pl.BlockSpec((1, d, FF_TILE), lambda s, f, e, r0, nb, t: (e[s], 0, f_eff(s, f, nb))),
                  pl.BlockSpec((1, d, FF_TILE), lambda s, f, e, r0, nb, t: (e[s], 0, f_eff(s, f, nb))),
                  pl.BlockSpec((1, FF_TILE, d), lambda s, f, e, r0, nb, t: (e[s], f_eff(s, f, nb), 0)),
                  pl.BlockSpec(memory_space=pl.ANY)],
        out_specs=pl.BlockSpec(memory_space=pl.ANY),
        scratch_shapes=[pltpu.VMEM((d, FF_TILE), BF16), pltpu.VMEM((d, FF_TILE), BF16),
                        pltpu.VMEM((FF_TILE, d), BF16), pltpu.VMEM((2, SUPER_ROWS, half), jnp.uint32),
                        pltpu.VMEM((SUPER_ROWS, d), F32), pltpu.SMEM((1,), I32),
                        pltpu.SemaphoreType.DMA(()), pltpu.SemaphoreType.DMA(())],
    )
    return pl.pallas_call(
        _moe_expert_body,
        grid_spec=grid_spec,
        out_shape=jax.ShapeDtypeStruct((p, d), F32),
        compiler_params=_cparams(("arbitrary", "arbitrary")),
        name="moe_experts",
    )(sb_expert, sb_row0, sb_blocks, total_rows, wg, wu, wd, buf)


def _moe_combine_body(dest_ref, x1_ref, route_ref, out_ref, y_ref, rows_ref, sem):
    tm = x1_ref.shape[0]
    base = pl.program_id(0) * tm * TOP_K

    def row_copy(r, k):
        slot = dest_ref[base + r * TOP_K + k]
        return pltpu.make_async_copy(out_ref.at[pl.ds(slot, 1)], rows_ref.at[k, pl.ds(r, 1)], sem)

    def start(r, c):
        for k in range(TOP_K):
            row_copy(r, k).start()
        return c

    def wait(r, c):
        for k in range(TOP_K):
            row_copy(r, k).wait()
        return c

    lax.fori_loop(0, tm, start, 0, unroll=4)
    lax.fori_loop(0, tm, wait, 0, unroll=4)
    route = route_ref[...]
    y_ref[...] = x1_ref[...] + (rows_ref[0] * route[:, 0:1] + rows_ref[1] * route[:, 1:2])


def moe_combine(dest, x1, route, expert_out, *, tm=256):
    n, d = x1.shape
    grid_spec = pltpu.PrefetchScalarGridSpec(
        num_scalar_prefetch=1,
        grid=(n // tm,),
        in_specs=[pl.BlockSpec((tm, d), lambda i, *_: (i, 0)),
                  pl.BlockSpec((tm, LANES), lambda i, *_: (i, 0)),
                  pl.BlockSpec(memory_space=pl.ANY)],
        out_specs=pl.BlockSpec((tm, d), lambda i, *_: (i, 0)),
        scratch_shapes=[pltpu.VMEM((TOP_K, tm, d), F32), pltpu.SemaphoreType.DMA(())],
    )
    return pl.pallas_call(
        _moe_combine_body,
        grid_spec=grid_spec,
        out_shape=jax.ShapeDtypeStruct((n, d), F32),
        compiler_params=_cparams(("arbitrary",)),
        name="moe_combine",
    )(dest, x1, route, expert_out)


def hierarchical_moe(x1, h2, route, wg, wu, wd):
    n = h2.shape[0]
    n_experts = wg.shape[0]
    m = n * TOP_K
    ids_row = route[:, 2:4].astype(I32).reshape(1, m)
    rank_row, counts = moe_rank(ids_row, n_experts)
    counts = counts[:, 0].astype(I32)
    padded = (counts + MOE_BLOCK - 1) // MOE_BLOCK * MOE_BLOCK
    pad_end = jnp.cumsum(padded)
    pad_start = pad_end - padded
    dest = moe_dest(ids_row, rank_row, pad_start.astype(F32).reshape(n_experts, 1)).reshape(m)
    n_slots = m + n_experts * MOE_BLOCK
    n_sb = (padded + SUPER_ROWS - 1) // SUPER_ROWS
    sb_end = jnp.cumsum(n_sb)
    sb_start = sb_end - n_sb
    s_idx = jnp.arange(n_experts + m // SUPER_ROWS, dtype=I32)
    used = s_idx < sb_end[-1]
    e_of = jnp.minimum(jnp.sum(sb_end[None, :] <= s_idx[:, None], axis=1), n_experts - 1)
    piece = s_idx - sb_start[e_of]
    sb_row0 = jnp.where(used, pad_start[e_of] + piece * SUPER_ROWS, 0).astype(I32)
    sb_rows = jnp.clip(padded[e_of] - piece * SUPER_ROWS, 0, SUPER_ROWS)
    sb_blocks = jnp.where(used, sb_rows // MOE_BLOCK, 0).astype(I32)
    sb_expert = jnp.where(used, e_of, e_of[jnp.maximum(sb_end[-1] - 1, 0)]).astype(I32)
    buf = moe_dispatch(dest, h2, n_slots)
    expert_out = moe_experts(sb_expert, sb_row0, sb_blocks, pad_end[-1:].astype(I32), buf,
                             wg, wu, wd)
    return moe_combine(dest, x1, route, expert_out)


def _pad_cols(w, to):
    return jnp.pad(w, ((0, 0), (0, to - w.shape[1])))


def _pad_rows(w, to):
    return jnp.pad(w, ((0, to - w.shape[0]), (0, 0)))


def _layer(x2d, batch, norm1_w, w_in, mu, w0, w2, a0, a2, g2, k_k, k_a, r_k, gn_w, gn_b,
           q_norm_w, k_norm_w, sinks, proj_rwkv, proj_attn, w_out, norm2_w, wc, bc, wf, bf,
           wg, wu, wd):
    n, d = x2d.shape
    seq = n // batch
    width = w0.shape[0]
    dl, il, gl = w2.shape[0], a2.shape[0], g2.shape[0]
    q_width = proj_attn.shape[0]
    rwkv_cols = 3 * width + dl + il + gl
    kv_width = (w_in.shape[1] - rwkv_cols - q_width - 2 * d) // 2
    row = lambda v: v.reshape(1, -1).astype(F32)

    c0 = 3 * width
    w_rwkv = jnp.concatenate([w_in[:, :c0], _pad_cols(w_in[:, c0:c0 + dl], LANES),
                              _pad_cols(w_in[:, c0 + dl:c0 + dl + il], LANES),
                              w_in[:, c0 + dl + il:rwkv_cols]], axis=1).astype(BF16)
    mu_p = jnp.concatenate([mu[:c0], jnp.pad(mu[c0:c0 + dl], (0, LANES - dl)),
                            jnp.pad(mu[c0 + dl:c0 + dl + il], (0, LANES - il)),
                            mu[c0 + dl + il:]]).reshape(1, -1)
    q0 = rwkv_cols
    w_qkv = w_in[:, q0:q0 + q_width + 2 * kv_width].astype(BF16)
    w_gates = w_in[:, q0 + q_width + 2 * kv_width:].astype(BF16)

    g1 = row(norm1_w)
    p_rwkv = norm_proj(x2d, g1, w_rwkv, out_dtype=F32, sigmoid=False)
    qkv = norm_proj(x2d, g1, w_qkv, out_dtype=BF16, sigmoid=False)
    gates = norm_proj(x2d, g1, w_gates, out_dtype=BF16, sigmoid=True)

    rt, at, bt, kt, vv, g, bonus, pc = rwkv_prep(
        p_rwkv, mu_p, row(w0), _pad_rows(w2, LANES).astype(BF16), row(a0),
        _pad_rows(a2, LANES).astype(BF16), g2.astype(BF16), row(k_k), row(k_a), row(r_k),
        seq_len=seq)
    a_out = rwkv_scan(rt, at, bt, kt, vv, pc, g, bonus, row(gn_w), row(gn_b), batch=batch)

    b_out = swa_attention(qkv, q_norm_w, k_norm_w, sinks.astype(F32), batch=batch,
                          q_width=q_width, kv_width=kv_width)

    n_groups, n_experts = wc.shape[1], wf.shape[1]
    router_w = _pad_cols(jnp.concatenate([wc, wf], axis=1), LANES)
    router_b = _pad_cols(jnp.concatenate([bc, bf]).reshape(1, -1), LANES)
    assert n_groups == N_GROUPS and n_experts == N_GROUPS * GROUP_SIZE
    x1, h2, route = merge_route(a_out, b_out, gates, x2d, proj_rwkv.astype(BF16),
                                proj_attn.astype(BF16), w_out.astype(BF16), row(norm2_w),
                                router_w, router_b)
    return hierarchical_moe(x1, h2, route, wg, wu, wd)


def kernel(x, norm1_w, w_in, rwkv_mu, rwkv_w0, rwkv_w2, rwkv_a0, rwkv_a2, rwkv_g2, rwkv_k_k,
           rwkv_k_a, rwkv_r_k, rwkv_gn_w, rwkv_gn_b, q_norm_w, k_norm_w, attn_sinks, proj_rwkv,
           proj_attn, w_out, norm2_w, router_coarse_w, router_coarse_b, router_fine_w,
           router_fine_b, expert_w_gate, expert_w_up, expert_w_down):
    batch, seq, d = x.shape
    x2d = x.reshape(batch * seq, d)
    for layer in range(norm1_w.shape[0]):
        x2d = _layer(x2d, batch, norm1_w[layer], w_in[layer], rwkv_mu[layer], rwkv_w0[layer],
                     rwkv_w2[layer], rwkv_a0[layer], rwkv_a2[layer], rwkv_g2[layer],
                     rwkv_k_k[layer], rwkv_k_a[layer], rwkv_r_k[layer].reshape(-1),
                     rwkv_gn_w[layer], rwkv_gn_b[layer], q_norm_w[layer], k_norm_w[layer],
                     attn_sinks[layer], proj_rwkv[layer], proj_attn[layer], w_out[layer],
                     norm2_w[layer], router_coarse_w[layer], router_coarse_b[layer],
                     router_fine_w[layer], router_fine_b[layer], expert_w_gate[layer],
                     expert_w_up[layer], expert_w_down[layer])
    return x2d.reshape(batch, seq, d)
```

```python
import functools

import jax
import jax.numpy as jnp
from jax import lax
from jax.experimental import pallas as pl
from jax.experimental.pallas import tpu as pltpu

F32 = jnp.float32
BF16 = jnp.bfloat16
I32 = jnp.int32

NORM_EPS = 1e-6
GN_EPS = 64e-5
HEAD_DIM = 64
LANES = 128
CHUNK = 64
WINDOW = 128
MOE_BLOCK = 128
SUPER_ROWS = 1024
FF_TILE = 512
TOP_K = 2
N_GROUPS = 8
GROUP_SIZE = 8
VMEM_LIMIT = 56 * 1024 * 1024

_NT = (((1,), (1,)), ((), ()))
_TN = (((0,), (0,)), ((), ()))


def _dot(a, b):
    return jnp.dot(a, b, preferred_element_type=F32)


def _split2(x):
    hi = x.astype(BF16)
    mid = (x - hi.astype(F32)).astype(BF16)
    return hi, mid


def _select_sum(m, hi, mid):
    return _dot(m, hi) + _dot(m, mid)


def _pack_bf16_halves(x):
    w = x.shape[1] // 2
    lo = lax.bitcast_convert_type(x[:, :w].astype(BF16).astype(F32), jnp.uint32)
    hi = lax.bitcast_convert_type(x[:, w:].astype(BF16).astype(F32), jnp.uint32)
    return (lo >> 16) | (hi & jnp.uint32(0xFFFF0000))


def _unpack_bf16_halves(xp):
    lo = lax.bitcast_convert_type(xp << 16, F32).astype(BF16)
    hi = lax.bitcast_convert_type(xp & jnp.uint32(0xFFFF0000), F32).astype(BF16)
    return lo, hi


def _cparams(sem, vmem=VMEM_LIMIT):
    return pltpu.CompilerParams(dimension_semantics=sem, vmem_limit_bytes=vmem)


def _norm_proj_body(x_ref, g_ref, w_ref, rwkv_ref, qkv_ref, gates_ref, h_ref, *, n_rwkv, n_qkv):
    j = pl.program_id(1)

    @pl.when(j == 0)
    def _():
        x = x_ref[...]
        ms = jnp.mean(x * x, axis=-1, keepdims=True)
        h_ref[...] = (x * lax.rsqrt(ms + NORM_EPS) * g_ref[...]).astype(BF16)

    acc = _dot(h_ref[...], w_ref[...])

    @pl.when(j < n_rwkv)
    def _():
        rwkv_ref[...] = acc

    @pl.when((j >= n_rwkv) & (j < n_rwkv + n_qkv))
    def _():
        qkv_ref[...] = acc.astype(qkv_ref.dtype)

    @pl.when(j >= n_rwkv + n_qkv)
    def _():
        gates_ref[...] = jax.nn.sigmoid(acc).astype(gates_ref.dtype)


def norm_proj(x2d, gain, w, widths, *, tm=1024, tn=512):
    n, d = x2d.shape
    tm = min(tm, n)
    n_rwkv, n_qkv, n_gates = (c // tn for c in widths)
    group = lambda first, count: (lambda i, j: (i, jnp.clip(j - first, 0, count - 1)))
    return pl.pallas_call(
        functools.partial(_norm_proj_body, n_rwkv=n_rwkv, n_qkv=n_qkv),
        grid=(n // tm, n_rwkv + n_qkv + n_gates),
        in_specs=[pl.BlockSpec((tm, d), lambda i, j: (i, 0)),
                  pl.BlockSpec((1, d), lambda i, j: (0, 0)),
                  pl.BlockSpec((d, tn), lambda i, j: (0, j))],
        out_specs=[pl.BlockSpec((tm, tn), group(0, n_rwkv)),
                   pl.BlockSpec((tm, tn), group(n_rwkv, n_qkv)),
                   pl.BlockSpec((tm, tn), group(n_rwkv + n_qkv, n_gates))],
        out_shape=[jax.ShapeDtypeStruct((n, widths[0]), F32),
                   jax.ShapeDtypeStruct((n, widths[1]), BF16),
                   jax.ShapeDtypeStruct((n, widths[2]), BF16)],
        scratch_shapes=[pltpu.VMEM((tm, d), BF16)],
        compiler_params=_cparams(("parallel", "arbitrary")),
        name="norm_proj",
    )(x2d, gain, w)


def _head_sum(x, e, et):
    s = _dot(x.astype(BF16), e)
    hi, mid = _split2(s)
    return _dot(hi, et) + _dot(mid, et)


def _rwkv_prep_body(p_ref, pprev_ref, mu_ref, w0_ref, w2_ref, a0_ref, a2_ref, g2_ref,
                    kk_ref, ka_ref, rk_ref, e_ref, et_ref, tri_ref, sel_ref,
                    rt_ref, at_ref, bt_ref, kt_ref, v_ref, g_ref, bonus_ref, pc_ref,
                    *, seq_len, width):
    tm = p_ref.shape[0]
    w_ = width
    first = (pl.program_id(0) * tm) % seq_len == 0
    p = p_ref[...]
    prev_row = jnp.where(first, 0.0, pprev_ref[7:8, :])
    row = lax.broadcasted_iota(I32, (tm, 1), 0)
    shifted = jnp.where(row == 0, prev_row, pltpu.roll(p, 1, 0))
    m = p + (shifted - p) * mu_ref[...]
    r = m[:, 0:w_]
    k = m[:, w_:2 * w_]
    v = m[:, 2 * w_:3 * w_]
    xw = m[:, 3 * w_:3 * w_ + 128]
    xa = m[:, 3 * w_ + 128:3 * w_ + 256]
    xg = m[:, 3 * w_ + 256:]

    z = -(w0_ref[...] + _dot(jnp.tanh(xw).astype(BF16), w2_ref[...]))
    softplus = jnp.maximum(z, 0.0) + jnp.log1p(jnp.exp(-jnp.abs(z)))
    logw = -jnp.exp(-softplus - 0.5)
    a = jax.nn.sigmoid(a0_ref[...] + _dot(xa.astype(BF16), a2_ref[...]))
    g = _dot(jax.nn.sigmoid(xg).astype(BF16), g2_ref[...])

    e = e_ref[...]
    et = et_ref[...]
    kk = k * kk_ref[...]
    kk = kk * jnp.minimum(lax.rsqrt(_head_sum(kk * kk, e, et)), 1e12)
    kmod = k * (1.0 + (a - 1.0) * ka_ref[...])
    bonus = _head_sum(r * kmod * rk_ref[...], e, et) * v

    lw_hi, lw_mid = _split2(logw)
    cum = _select_sum(tri_ref[...], lw_hi, lw_mid)
    pc_ref[...] = jnp.exp(_select_sum(sel_ref[...], lw_hi, lw_mid))
    inv = jnp.exp(-cum)
    rt_ref[...] = (r * jnp.exp(cum)).astype(BF16)
    at_ref[...] = (-kk * jnp.exp(cum - logw)).astype(BF16)
    bt_ref[...] = (kk * a * inv).astype(BF16)
    kt_ref[...] = (kmod * inv).astype(BF16)
    v_ref[...] = v.astype(BF16)
    g_ref[...] = g.astype(BF16)
    bonus_ref[...] = bonus


def _head_indicator(width):
    heads = width // HEAD_DIM
    c = jnp.arange(width)[:, None] // HEAD_DIM
    h = jnp.arange(LANES)[None, :]
    e = (c == h).astype(BF16)
    assert heads <= LANES
    return e, e.T


def rwkv_prep(p, mu, w0, w2, a0, a2, g2, k_k, k_a, r_k, *, seq_len, tm=512):
    n, cols = p.shape
    width = w0.shape[1]
    nchunk = tm // CHUNK
    e, et = _head_indicator(width)
    t = jnp.arange(tm)
    same = (t[:, None] // CHUNK) == (t[None, :] // CHUNK)
    tri = (same & (t[:, None] >= t[None, :])).astype(BF16)
    sel = ((t[None, :] // CHUNK) == jnp.arange(nchunk)[:, None]).astype(BF16)
    const = lambda shape: pl.BlockSpec(shape, lambda i: (0, 0))
    stream = lambda dt: jax.ShapeDtypeStruct((n, width), dt)
    outs = pl.pallas_call(
        functools.partial(_rwkv_prep_body, seq_len=seq_len, width=width),
        grid=(n // tm,),
        in_specs=[pl.BlockSpec((tm, cols), lambda i: (i, 0)),
                  pl.BlockSpec((8, cols), lambda i: (jnp.maximum(i * (tm // 8) - 1, 0), 0)),
                  const((1, cols)), const((1, width)), const(w2.shape), const((1, width)),
                  const(a2.shape), const(g2.shape), const((1, width)), const((1, width)),
                  const((1, width)), const(e.shape), const(et.shape), const(tri.shape),
                  const(sel.shape)],
        out_specs=[pl.BlockSpec((tm, width), lambda i: (i, 0))] * 7
                  + [pl.BlockSpec((nchunk, width), lambda i: (i, 0))],
        out_shape=[stream(BF16)] * 6 + [stream(F32),
                   jax.ShapeDtypeStruct((n // CHUNK, width), F32)],
        compiler_params=_cparams(("parallel",)),
        name="rwkv_prep",
    )(p, p, mu, w0, w2, a0, a2, g2, k_k, k_a, r_k, e, et, tri, sel)
    return outs


def _chunk_pairs(rts, ats, bts, kts, vvs, pcs, s_prevs, masks):
    lo, strict, incl = masks
    c = rts[0].shape[0]
    c2 = 2 * c
    zero = jnp.zeros_like(rts[0])
    each = lambda f, *ls: [f(*a) for a in zip(*ls)]

    def stack(x):
        return jnp.concatenate([jnp.where(lo, x, zero), jnp.where(lo, zero, x)], axis=0)

    ar = each(lambda a, r: jnp.concatenate([stack(a), stack(r)], axis=0), ats, rts)
    bk = each(lambda b, k: jnp.concatenate([stack(b), stack(k)], axis=0), bts, kts)
    vb = each(stack, vvs)
    gram = each(lambda x, y: lax.dot_general(x, y, _NT, preferred_element_type=F32), ar, bk)
    ars = each(lambda x, s: lax.dot_general(x, s.astype(BF16), _NT, preferred_element_type=F32),
               ar, s_prevs)
    a_ab = each(lambda g: jnp.where(strict, g[:c2, :c2], 0.0), gram)
    rhs = each(lambda g, v, x: x[:c2] + _dot(jnp.where(strict, g[:c2, c2:], 0.0).astype(BF16), v),
               gram, vb, ars)

    u = rhs
    nk = a_ab
    steps = (c - 1).bit_length()
    for k in range(steps):
        nkb = each(lambda n: n.astype(BF16), nk)
        if k + 1 < steps:
            prod = each(lambda n, uu: _dot(n, jnp.concatenate([n, uu.astype(BF16)], axis=1)),
                        nkb, u)
            nk = each(lambda p: p[:, :c2], prod)
            u = each(lambda uu, p: uu + p[:, c2:], u, prod)
        else:
            u = each(lambda uu, n: uu + _dot(n, uu.astype(BF16)), u, nkb)

    uv = each(lambda uu, v: jnp.concatenate([uu.astype(BF16), v], axis=0), u, vb)
    a_r = each(lambda g: jnp.where(incl, g[c2:, :], 0.0).astype(BF16), gram)
    ybd = each(lambda x, a, w: x[c2:] + _dot(a, w), ars, a_r, uv)
    ys = each(lambda yb: yb[:c] + yb[c:], ybd)

    bk_end = each(lambda x, pc: (x.astype(F32) * pc).astype(BF16), bk, pcs)
    s_new = each(lambda s, pc, w, x: s * pc + lax.dot_general(w, x, _TN, preferred_element_type=F32),
                 s_prevs, pcs, uv, bk_end)
    return ys, s_new


def _rwkv_scan_body(rt_ref, at_ref, bt_ref, kt_ref, v_ref, pc_ref, g_ref, bonus_ref,
                    gnw_ref, gnb_ref, e_ref, et_ref, o_ref, s_ref, y_ref):
    tr, width = rt_ref.shape
    npairs = width // LANES

    @pl.when(pl.program_id(1) == 0)
    def _():
        s_ref[...] = jnp.zeros_like(s_ref)

    lane = lax.broadcasted_iota(I32, (CHUNK, LANES), 1)
    lo = lane < HEAD_DIM
    ri = lax.broadcasted_iota(I32, (2 * CHUNK, 2 * CHUNK), 0)
    ci = lax.broadcasted_iota(I32, (2 * CHUNK, 2 * CHUNK), 1)
    same = (ri // CHUNK) == (ci // CHUNK)
    ri2 = lax.broadcasted_iota(I32, (2 * CHUNK, 4 * CHUNK), 0)
    ci2 = lax.broadcasted_iota(I32, (2 * CHUNK, 4 * CHUNK), 1) % (2 * CHUNK)
    incl = ((ri2 // CHUNK) == (ci2 // CHUNK)) & (ri2 >= ci2)
    masks = (lo, same & (ri > ci), incl)

    def chunk_step(c, carry):
        rows = pl.ds(pl.multiple_of(c * CHUNK, CHUNK), CHUNK)
        this_chunk = lax.broadcasted_iota(I32, (pc_ref.shape[0], 1), 0) == c
        cols = [slice(pr * LANES, (pr + 1) * LANES) for pr in range(npairs)]
        pcs = [jnp.sum(jnp.where(this_chunk, pc_ref[:, cl], 0.0), axis=0, keepdims=True)
               for cl in cols]
        load = lambda ref: [ref[rows, cl] for cl in cols]
        ys, s_new = _chunk_pairs(load(rt_ref), load(at_ref), load(bt_ref), load(kt_ref),
                                 load(v_ref), pcs, [s_ref[pr] for pr in range(npairs)], masks)
        for pr in range(npairs):
            s_ref[pr] = s_new[pr]
            y_ref[rows, cols[pr]] = ys[pr]
        return carry

    lax.fori_loop(0, tr // CHUNK, chunk_step, 0)

    e = e_ref[...]
    et = et_ref[...]
    y = y_ref[...]
    mean = _head_sum(y, e, et) * (1.0 / HEAD_DIM)
    d = y - mean
    var = _head_sum(d * d, e, et) * (1.0 / HEAD_DIM)
    out = d * lax.rsqrt(var + GN_EPS) * gnw_ref[...] + gnb_ref[...]
    out = (out + bonus_ref[...]) * g_ref[...].astype(F32)
    o_ref[...] = out.astype(o_ref.dtype)


def rwkv_scan(rt, at, bt, kt, vv, pc, g, bonus, gn_w, gn_b, *, batch, tr=512):
    n, width = rt.shape
    seq = n // batch
    nst = seq // tr
    e, et = _head_indicator(width)
    blk = lambda rows: pl.BlockSpec((rows, width), lambda b, s: (b * nst + s, 0))
    const = lambda shape: pl.BlockSpec(shape, lambda b, s: (0, 0))
    return pl.pallas_call(
        _rwkv_scan_body,
        grid=(batch, nst),
        in_specs=[blk(tr)] * 5 + [blk(tr // CHUNK), blk(tr), blk(tr),
                  const((1, width)), const((1, width)), const(e.shape), const(et.shape)],
        out_specs=blk(tr),
        out_shape=jax.ShapeDtypeStruct((n, width), BF16),
        scratch_shapes=[pltpu.VMEM((width // LANES, LANES, LANES), F32),
                        pltpu.VMEM((tr, width), F32)],
        compiler_params=_cparams(("parallel", "arbitrary")),
        name="rwkv_scan",
    )(rt, at, bt, kt, vv, pc, g, bonus, gn_w, gn_b, e, et)


def _head_rmsnorm(x, gain, bd):
    hi, mid = _split2(x * x)
    ms = (_dot(hi, bd) + _dot(mid, bd)) * (1.0 / HEAD_DIM)
    return x * lax.rsqrt(ms + NORM_EPS) * gain


def _swa_body(sink_ref, q_ref, kp_ref, kc_ref, vp_ref, vc_ref, qg_ref, kg_ref, bd_ref, o_ref,
              *, group):
    blk = q_ref.shape[0]
    n = pl.program_id(1)
    bd = bd_ref[...]
    scale = HEAD_DIM ** -0.5
    lane = lax.broadcasted_iota(I32, (1, LANES), 1)
    lo = lane < HEAD_DIM
    ri = lax.broadcasted_iota(I32, (2 * blk, 2 * blk), 0)
    cj = lax.broadcasted_iota(I32, (2 * blk, 2 * blk), 1)
    rel = blk + (ri % blk) - cj
    jmin = jnp.where(n > 0, 0, blk)
    valid = (rel >= 0) & (rel < WINDOW) & (cj >= jmin)
    top_rows = lax.broadcasted_iota(I32, (2 * blk, 1), 0) < blk

    kcat = jnp.concatenate([kp_ref[...], kc_ref[...]], axis=0).astype(F32)
    vcat = jnp.concatenate([vp_ref[...], vc_ref[...]], axis=0).astype(F32)
    kv_heads = kcat.shape[1] // HEAD_DIM
    q_tiles = q_ref.shape[1] // LANES
    tiles_per_kv = group // 2
    each = lambda f, *ls: [f(*a) for a in zip(*ls)]

    kv_cols = [slice(jt * LANES, (jt + 1) * LANES) for jt in range(kv_heads // 2)]
    kn = [_head_rmsnorm(kcat[:, c], kg_ref[...], bd) for c in kv_cols]
    kn_r = [pltpu.roll(x, HEAD_DIM, 1) for x in kn]
    vt = [vcat[:, c] for c in kv_cols]
    vt_r = [pltpu.roll(x, HEAD_DIM, 1) for x in vt]
    own = lambda hk: lo if hk % 2 == 0 else jnp.logical_not(lo)
    k2 = [jnp.where(own(hk), kn[hk // 2], kn_r[hk // 2]).astype(BF16) for hk in range(kv_heads)]
    v2 = [jnp.where(own(hk), vt[hk // 2], vt_r[hk // 2]).astype(BF16) for hk in range(kv_heads)]

    q_cols = [slice(t * LANES, (t + 1) * LANES) for t in range(q_tiles)]
    qn = [_head_rmsnorm(q_ref[:, c].astype(F32), qg_ref[...] * scale, bd) for c in q_cols]
    qst = [jnp.concatenate([jnp.where(lo, x, 0.0), jnp.where(lo, 0.0, x)], axis=0).astype(BF16)
           for x in qn]
    s = [jnp.where(valid, lax.dot_general(x, k2[t // tiles_per_kv], _NT,
                                          preferred_element_type=F32), -jnp.inf)
         for t, x in enumerate(qst)]
    sink = [jnp.where(top_rows, sink_ref[2 * t], sink_ref[2 * t + 1]) for t in range(q_tiles)]
    mx = each(lambda x, sk: jnp.maximum(jnp.max(x, axis=-1, keepdims=True), sk), s, sink)
    pr = each(lambda x, m: jnp.exp(x - m), s, mx)
    inv = each(lambda p, sk, m: 1.0 / (jnp.sum(p, axis=-1, keepdims=True) + jnp.exp(sk - m)),
               pr, sink, mx)
    o = [_dot((p * r).astype(BF16), v2[t // tiles_per_kv])
         for t, (p, r) in enumerate(zip(pr, inv))]
    for t in range(q_tiles):
        o_ref[:, q_cols[t]] = jnp.where(lo, o[t][:blk], o[t][blk:]).astype(o_ref.dtype)


def swa_attention(qkv, q_gain, k_gain, sinks, *, batch, q_width, kv_width):
    n = qkv.shape[0]
    blk = WINDOW
    nb = n // batch // blk
    group = (q_width // HEAD_DIM) // (kv_width // HEAD_DIM)
    kcol = q_width // kv_width
    t = jnp.arange(LANES)
    bd = ((t[:, None] // HEAD_DIM) == (t[None, :] // HEAD_DIM)).astype(BF16)
    qg = jnp.tile(q_gain.reshape(1, HEAD_DIM), (1, 2))
    kg = jnp.tile(k_gain.reshape(1, HEAD_DIM), (1, 2))
    cur = lambda col: (lambda b, i, *_: (b * nb + i, col))
    prev = lambda col: (lambda b, i, *_: (b * nb + jnp.maximum(i - 1, 0), col))
    const = lambda shape: pl.BlockSpec(shape, lambda b, i, *_: (0, 0))
    grid_spec = pltpu.PrefetchScalarGridSpec(
        num_scalar_prefetch=1,
        grid=(batch, nb),
        in_specs=[pl.BlockSpec((blk, q_width), cur(0)),
                  pl.BlockSpec((blk, kv_width), prev(kcol)),
                  pl.BlockSpec((blk, kv_width), cur(kcol)),
                  pl.BlockSpec((blk, kv_width), prev(kcol + 1)),
                  pl.BlockSpec((blk, kv_width), cur(kcol + 1)),
                  const((1, LANES)), const((1, LANES)), const((LANES, LANES))],
        out_specs=pl.BlockSpec((blk, q_width), cur(0)),
    )
    return pl.pallas_call(
        functools.partial(_swa_body, group=group),
        grid_spec=grid_spec,
        out_shape=jax.ShapeDtypeStruct((n, q_width), BF16),
        compiler_params=_cparams(("parallel", "arbitrary")),
        name="swa_attn",
    )(sinks, qkv, qkv, qkv, qkv, qkv, qg, kg, bd)


def _merge_route_body(a_ref, b_ref, ga_ref, gb_ref, x_ref, pr_ref, pa_ref, wo_ref, n2_ref,
                      rhm_ref, rh_ref, rb_ref, x1_ref, h2_ref, route_ref):
    merged = (ga_ref[...].astype(F32) * _dot(a_ref[...], pr_ref[...])
              + gb_ref[...].astype(F32) * _dot(b_ref[...], pa_ref[...]))
    x1 = x_ref[...] + _dot(merged.astype(BF16), wo_ref[...])
    x1_ref[...] = x1
    ms = jnp.mean(x1 * x1, axis=-1, keepdims=True)
    h2 = x1 * lax.rsqrt(ms + NORM_EPS) * n2_ref[...]
    h2_ref[...] = _pack_bf16_halves(h2)

    hi, mid = _split2(h2)
    hw = _dot(hi, rhm_ref[...])
    logits = hw[:, :LANES] + (hw[:, LANES:] + _dot(mid, rh_ref[...])) + rb_ref[...]

    lane = lax.broadcasted_iota(I32, logits.shape, 1)
    big = jnp.int32(1 << 20)
    neg = -jnp.inf
    is_coarse = lane < N_GROUPS
    cl = jnp.where(is_coarse, logits, neg)
    ce = jnp.exp(cl - jnp.max(cl, axis=-1, keepdims=True))
    cp = ce / jnp.sum(ce, axis=-1, keepdims=True)
    g_prob = jnp.max(cp, axis=-1, keepdims=True)
    g_idx = jnp.min(jnp.where(is_coarse & (cp == g_prob), lane, big), axis=-1, keepdims=True)

    fine_lane = lane - N_GROUPS
    in_group = ((lane >= N_GROUPS) & (lane < N_GROUPS + N_GROUPS * GROUP_SIZE)
                & ((fine_lane // GROUP_SIZE) == g_idx))
    fl = jnp.where(in_group, logits, neg)
    fe = jnp.exp(fl - jnp.max(fl, axis=-1, keepdims=True))
    fp = fe / jnp.sum(fe, axis=-1, keepdims=True)
    p1 = jnp.max(jnp.where(in_group, fp, -1.0), axis=-1, keepdims=True)
    i1 = jnp.min(jnp.where(in_group & (fp == p1), lane, big), axis=-1, keepdims=True)
    rest = in_group & (lane != i1)
    p2 = jnp.max(jnp.where(rest, fp, -1.0), axis=-1, keepdims=True)
    i2 = jnp.min(jnp.where(rest & (fp == p2), lane, big), axis=-1, keepdims=True)
    den = p1 + p2
    w1 = g_prob * p1 / den
    w2 = g_prob * p2 / den
    e1 = (i1 - N_GROUPS).astype(F32)
    e2 = (i2 - N_GROUPS).astype(F32)
    route_ref[...] = jnp.where(lane == 0, w1, jnp.where(lane == 1, w2,
                     jnp.where(lane == 2, e1, jnp.where(lane == 3, e2, 0.0))))


def merge_route(a_out, b_out, gates, x2d, proj_r, proj_a, w_out, norm2_w, router_w, router_b,
                *, tm=256):
    n, d = x2d.shape
    wa = a_out.shape[1]
    rh, rm = _split2(router_w)
    rhm = jnp.concatenate([rh, rm], axis=1)
    const = lambda arr: pl.BlockSpec(arr.shape, lambda i: (0, 0), pipeline_mode=pl.Buffered(1))
    row = lambda w: pl.BlockSpec((tm, w), lambda i: (i, 0))
    return pl.pallas_call(
        _merge_route_body,
        grid=(n // tm,),
        in_specs=[row(wa), row(wa), pl.BlockSpec((tm, d), lambda i: (i, 0)),
                  pl.BlockSpec((tm, d), lambda i: (i, 1)), row(d),
                  const(proj_r), const(proj_a), const(w_out), const(norm2_w),
                  const(rhm), const(rh), const(router_b)],
        out_specs=[row(d), row(d // 2), row(LANES)],
        out_shape=[jax.ShapeDtypeStruct((n, d), F32), jax.ShapeDtypeStruct((n, d // 2), jnp.uint32),
                   jax.ShapeDtypeStruct((n, LANES), F32)],
        compiler_params=_cparams(("parallel",)),
        name="merge_route",
    )(a_out, b_out, gates, gates, x2d, proj_r, proj_a, w_out, norm2_w, rhm, rh, router_b)


def _moe_rank_body(ids_ref, ut_ref, rank_ref, counts_ref, carry_ref):
    @pl.when(pl.program_id(0) == 0)
    def _():
        carry_ref[...] = jnp.zeros_like(carry_ref)

    ids = ids_ref[...]
    n_exp = carry_ref.shape[0]
    sub = lax.broadcasted_iota(I32, (n_exp, ids.shape[1]), 0)
    onehot = (sub == ids).astype(F32)
    before = _dot(onehot.astype(BF16), ut_ref[...])
    carry = carry_ref[...]
    rank_ref[...] = jnp.sum(onehot * (before + carry), axis=0, keepdims=True).astype(I32)
    carry = carry + jnp.sum(onehot, axis=1, keepdims=True)
    carry_ref[...] = carry
    counts_ref[...] = jnp.broadcast_to(carry, counts_ref.shape)


def moe_rank(ids_row, n_experts, *, tb=512):
    m = ids_row.shape[1]
    t = jnp.arange(tb)
    ut = (t[:, None] < t[None, :]).astype(BF16)
    return pl.pallas_call(
        _moe_rank_body,
        grid=(m // tb,),
        in_specs=[pl.BlockSpec((1, tb), lambda i: (0, i)),
                  pl.BlockSpec((tb, tb), lambda i: (0, 0))],
        out_specs=[pl.BlockSpec((1, tb), lambda i: (0, i)),
                   pl.BlockSpec((n_experts, LANES), lambda i: (0, 0))],
        out_shape=[jax.ShapeDtypeStruct((1, m), I32),
                   jax.ShapeDtypeStruct((n_experts, LANES), F32)],
        scratch_shapes=[pltpu.VMEM((n_experts, 1), F32)],
        compiler_params=_cparams(("arbitrary",)),
        name="moe_rank",
    )(ids_row, ut)


def _moe_dest_body(ids_ref, rank_ref, start_ref, dest_ref):
    ids = ids_ref[...]
    sub = lax.broadcasted_iota(I32, (start_ref.shape[0], ids.shape[1]), 0)
    start = jnp.sum(jnp.where(sub == ids, start_ref[...], 0.0), axis=0, keepdims=True)
    dest_ref[...] = start.astype(I32) + rank_ref[...]


def moe_dest(ids_row, rank_row, pad_start_col, *, tb=512):
    m = ids_row.shape[1]
    row = pl.BlockSpec((1, tb), lambda i: (0, i))
    return pl.pallas_call(
        _moe_dest_body,
        grid=(m // tb,),
        in_specs=[row, row, pl.BlockSpec(pad_start_col.shape, lambda i: (0, 0))],
        out_specs=row,
        out_shape=jax.ShapeDtypeStruct((1, m), I32),
        compiler_params=_cparams(("parallel",)),
        name="moe_dest",
    )(ids_row, rank_row, pad_start_col)


def _moe_dispatch_body(dest_ref, h_ref, buf_in_ref, buf_ref, sem):
    del buf_in_ref
    tm = h_ref.shape[0]
    base = pl.program_id(0) * tm * TOP_K

    def row_copy(r, k):
        slot = dest_ref[base + r * TOP_K + k]
        return pltpu.make_async_copy(h_ref.at[pl.ds(r, 1)], buf_ref.at[pl.ds(slot, 1)], sem)

    def start(r, c):
        for k in range(TOP_K):
            row_copy(r, k).start()
        return c

    def wait(r, c):
        for k in range(TOP_K):
            row_copy(r, k).wait()
        return c

    lax.fori_loop(0, tm, start, 0, unroll=4)
    lax.fori_loop(0, tm, wait, 0, unroll=4)


def moe_dispatch(dest, h2, n_slots, *, tm=256):
    n, d = h2.shape
    buf0 = jnp.zeros((n_slots, d), h2.dtype)
    grid_spec = pltpu.PrefetchScalarGridSpec(
        num_scalar_prefetch=1,
        grid=(n // tm,),
        in_specs=[pl.BlockSpec((tm, d), lambda i, *_: (i, 0)),
                  pl.BlockSpec(memory_space=pl.ANY)],
        out_specs=pl.BlockSpec(memory_space=pl.ANY),
        scratch_shapes=[pltpu.SemaphoreType.DMA(())],
    )
    return pl.pallas_call(
        _moe_dispatch_body,
        grid_spec=grid_spec,
        out_shape=jax.ShapeDtypeStruct((n_slots, d), h2.dtype),
        input_output_aliases={2: 0},
        compiler_params=_cparams(("arbitrary",)),
        name="moe_dispatch",
    )(dest, h2, buf0)


def _moe_expert_body(sbe_ref, row0_ref, nb_ref, tot_ref, wg_ref, wu_ref, wd_ref, buf_ref, out_ref,
                     wgb, wub, wdb, xp, acc, pending, sem_in, sem_out):
    del sbe_ref
    s = pl.program_id(0)
    f = pl.program_id(1)
    n_sb = pl.num_programs(0)
    last_f = pl.num_programs(1) - 1
    nb = nb_ref[s]
    half = xp.shape[2]

    def blk(first_row, r):
        return pl.ds(pl.multiple_of(first_row + r * MOE_BLOCK, MOE_BLOCK), MOE_BLOCK)

    def rows_in(sb, r):
        return pltpu.make_async_copy(buf_ref.at[blk(row0_ref[sb], r)], xp.at[sb % 2, blk(0, r)],
                                     sem_in.at[sb % 2])

    def rows_out(r):
        return pltpu.make_async_copy(acc.at[blk(0, r)], out_ref.at[blk(row0_ref[s], r)], sem_out)

    def for_blocks(n, fn):
        def body(r, carry):
            fn(r)
            return carry
        lax.fori_loop(0, n, body, 0)

    def drain_out():
        for_blocks(pending[0], lambda r: rows_out(0).wait())
        pending[0] = 0

    def mlp_rows(first_row, n_rows, first):
        rows = pl.ds(pl.multiple_of(first_row, MOE_BLOCK), n_rows)
        xlo, xhi = _unpack_bf16_halves(xp[s % 2, rows, :])
        hg = _dot(xlo, wgb[:half, :]) + _dot(xhi, wgb[half:, :])
        hu = _dot(xlo, wub[:half, :]) + _dot(xhi, wub[half:, :])
        y = _dot((jax.nn.silu(hg) * hu).astype(BF16), wdb[...])
        if first:
            acc[rows, :] = y
        else:
            acc[rows, :] += y

    def all_rows(first):
        for_blocks(nb // 2, lambda q: mlp_rows(q * (2 * MOE_BLOCK), 2 * MOE_BLOCK, first))

        @pl.when(nb % 2 == 1)
        def _():
            mlp_rows((nb - 1) * MOE_BLOCK, MOE_BLOCK, first)

    @pl.when((s == 0) & (f == 0))
    def _():
        pending[0] = 0
        for_blocks(nb, lambda r: rows_in(0, r).start())

    @pl.when(nb > 0)
    def _():
        @pl.when(f == 0)
        def _():
            nxt = jnp.minimum(s + 1, n_sb - 1)
            nb_next = jnp.where(s + 1 < n_sb, nb_ref[nxt], 0)
            for_blocks(nb_next, lambda r: rows_in(nxt, r).start())

        wgb[...] = wg_ref[0].astype(BF16)
        wub[...] = wu_ref[0].astype(BF16)
        wdb[...] = wd_ref[0].astype(BF16)

        @pl.when(f == 0)
        def _():
            for_blocks(nb, lambda r: rows_in(s, r).wait())
            drain_out()
            all_rows(True)

        @pl.when(f > 0)
        def _():
            all_rows(False)

        @pl.when(f == last_f)
        def _():
            for_blocks(nb, lambda r: rows_out(r).start())
            pending[0] = nb

    @pl.when((s == n_sb - 1) & (f == last_f))
    def _():
        drain_out()
        acc[pl.ds(0, MOE_BLOCK), :] = jnp.zeros((MOE_BLOCK, acc.shape[1]), acc.dtype)
        tail0 = tot_ref[0]

        def zero_out(r):
            return pltpu.make_async_copy(acc.at[pl.ds(0, MOE_BLOCK)], out_ref.at[blk(tail0, r)],
                                         sem_out)

        ntail = (out_ref.shape[0] - tail0) // MOE_BLOCK
        for_blocks(ntail, lambda r: zero_out(r).start())
        for_blocks(ntail, lambda r: zero_out(r).wait())


def moe_experts(sb_expert, sb_row0, sb_blocks, total_rows, buf, wg, wu, wd):
    p, half = buf.shape
    d = 2 * half
    ff = wg.shape[2]
    nf = ff // FF_TILE
    f_eff = lambda s, f, nb: jnp.where(nb[s] > 0, f, nf - 1)
    grid_spec = pltpu.PrefetchScalarGridSpec(
        num_scalar_prefetch=4,
        grid=(sb_expert.shape[0], nf),
        in_specs=[pl.BlockSpec((1, d, FF_TILE), lambda s, f, e, r0, nb, t: (e[s], 0, f_eff(s, f, nb))),
                  pl.BlockSpec((1, d, FF_TILE), lambda s, f, e, r0, nb, t: (e[s], 0, f_eff(s, f, nb))),
                  pl.BlockSpec((1, FF_TILE, d), lambda s, f, e, r0, nb, t: (e[s], f_eff(s, f, nb), 0)),
                  pl.BlockSpec(memory_space=pl.ANY)],
        out_specs=pl.BlockSpec(memory_space=pl.ANY),
        scratch_shapes=[pltpu.VMEM((d, FF_TILE), BF16), pltpu.VMEM((d, FF_TILE), BF16),
                        pltpu.VMEM((FF_TILE, d), BF16), pltpu.VMEM((2, SUPER_ROWS, half), jnp.uint32),
                        pltpu.VMEM((SUPER_ROWS, d), F32), pltpu.SMEM((1,), I32),
                        pltpu.SemaphoreType.DMA((2,)), pltpu.SemaphoreType.DMA(())],
    )
    return pl.pallas_call(
        _moe_expert_body,
        grid_spec=grid_spec,
        out_shape=jax.ShapeDtypeStruct((p, d), F32),
        compiler_params=_cparams(("arbitrary", "arbitrary")),
        name="moe_experts",
    )(sb_expert, sb_row0, sb_blocks, total_rows, wg, wu, wd, buf)


def _moe_combine_body(dest_ref, x1_ref, route_ref, out_ref, y_ref, rows_ref, sem):
    tm = x1_ref.shape[0]
    base = pl.program_id(0) * tm * TOP_K

    def row_copy(r, k):
        slot = dest_ref[base + r * TOP_K + k]
        return pltpu.make_async_copy(out_ref.at[pl.ds(slot, 1)], rows_ref.at[k, pl.ds(r, 1)], sem)

    def start(r, c):
        for k in range(TOP_K):
            row_copy(r, k).start()
        return c

    def wait(r, c):
        for k in range(TOP_K):
            row_copy(r, k).wait()
        return c

    lax.fori_loop(0, tm, start, 0, unroll=4)
    lax.fori_loop(0, tm, wait, 0, unroll=4)
    route = route_ref[...]
    y_ref[...] = x1_ref[...] + (rows_ref[0] * route[:, 0:1] + rows_ref[1] * route[:, 1:2])


def moe_combine(dest, x1, route, expert_out, *, tm=256):
    n, d = x1.shape
    grid_spec = pltpu.PrefetchScalarGridSpec(
        num_scalar_prefetch=1,
        grid=(n // tm,),
        in_specs=[pl.BlockSpec((tm, d), lambda i, *_: (i, 0)),
                  pl.BlockSpec((tm, LANES), lambda i, *_: (i, 0)),
                  pl.BlockSpec(memory_space=pl.ANY)],
        out_specs=pl.BlockSpec((tm, d), lambda i, *_: (i, 0)),
        scratch_shapes=[pltpu.VMEM((TOP_K, tm, d), F32), pltpu.SemaphoreType.DMA(())],
    )
    return pl.pallas_call(
        _moe_combine_body,
        grid_spec=grid_spec,
        out_shape=jax.ShapeDtypeStruct((n, d), F32),
        compiler_params=_cparams(("arbitrary",)),
        name="moe_combine",
    )(dest, x1, route, expert_out)


def hierarchical_moe(x1, h2, route, wg, wu, wd):
    n = h2.shape[0]
    n_experts = wg.shape[0]
    m = n * TOP_K
    ids_row = route[:, 2:4].astype(I32).reshape(1, m)
    rank_row, counts = moe_rank(ids_row, n_experts)
    counts = counts[:, 0].astype(I32)
    padded = (counts + MOE_BLOCK - 1) // MOE_BLOCK * MOE_BLOCK
    pad_end = jnp.cumsum(padded)
    pad_start = pad_end - padded
    dest = moe_dest(ids_row, rank_row, pad_start.astype(F32).reshape(n_experts, 1)).reshape(m)
    n_slots = m + n_experts * MOE_BLOCK
    n_sb = (padded + SUPER_ROWS - 1) // SUPER_ROWS
    sb_end = jnp.cumsum(n_sb)
    sb_start = sb_end - n_sb
    s_idx = jnp.arange(n_experts + m // SUPER_ROWS, dtype=I32)
    used = s_idx < sb_end[-1]
    e_of = jnp.minimum(jnp.sum(sb_end[None, :] <= s_idx[:, None], axis=1), n_experts - 1)
    piece = s_idx - sb_start[e_of]
    sb_row0 = jnp.where(used, pad_start[e_of] + piece * SUPER_ROWS, 0).astype(I32)
    sb_rows = jnp.clip(padded[e_of] - piece * SUPER_ROWS, 0, SUPER_ROWS)
    sb_blocks = jnp.where(used, sb_rows // MOE_BLOCK, 0).astype(I32)
    sb_expert = jnp.where(used, e_of, e_of[jnp.maximum(sb_end[-1] - 1, 0)]).astype(I32)
    buf = moe_dispatch(dest, h2, n_slots)
    expert_out = moe_experts(sb_expert, sb_row0, sb_blocks, pad_end[-1:].astype(I32), buf,
                             wg, wu, wd)
    return moe_combine(dest, x1, route, expert_out)


def _pad_cols(w, to):
    return jnp.pad(w, ((0, 0), (0, to - w.shape[1])))


def _pad_rows(w, to):
    return jnp.pad(w, ((0, to - w.shape[0]), (0, 0)))


def _layer(x2d, batch, norm1_w, w_in, mu, w0, w2, a0, a2, g2, k_k, k_a, r_k, gn_w, gn_b,
           q_norm_w, k_norm_w, sinks, proj_rwkv, proj_attn, w_out, norm2_w, wc, bc, wf, bf,
           wg, wu, wd):
    n, d = x2d.shape
    seq = n // batch
    width = w0.shape[0]
    dl, il, gl = w2.shape[0], a2.shape[0], g2.shape[0]
    q_width = proj_attn.shape[0]
    rwkv_cols = 3 * width + dl + il + gl
    kv_width = (w_in.shape[1] - rwkv_cols - q_width - 2 * d) // 2
    row = lambda v: v.reshape(1, -1).astype(F32)

    c0 = 3 * width
    w_rwkv = jnp.concatenate([w_in[:, :c0], _pad_cols(w_in[:, c0:c0 + dl], LANES),
                              _pad_cols(w_in[:, c0 + dl:c0 + dl + il], LANES),
                              w_in[:, c0 + dl + il:rwkv_cols]], axis=1).astype(BF16)
    mu_p = jnp.concatenate([mu[:c0], jnp.pad(mu[c0:c0 + dl], (0, LANES - dl)),
                            jnp.pad(mu[c0 + dl:c0 + dl + il], (0, LANES - il)),
                            mu[c0 + dl + il:]]).reshape(1, -1)
    q0 = rwkv_cols
    w_all = jnp.concatenate([w_rwkv, w_in[:, q0:].astype(BF16)], axis=1)
    p_rwkv, qkv, gates = norm_proj(x2d, row(norm1_w), w_all,
                                   (w_rwkv.shape[1], q_width + 2 * kv_width, 2 * d))

    rt, at, bt, kt, vv, g, bonus, pc = rwkv_prep(
        p_rwkv, mu_p, row(w0), _pad_rows(w2, LANES).astype(BF16), row(a0),
        _pad_rows(a2, LANES).astype(BF16), g2.astype(BF16), row(k_k), row(k_a), row(r_k),
        seq_len=seq)
    a_out = rwkv_scan(rt, at, bt, kt, vv, pc, g, bonus, row(gn_w), row(gn_b), batch=batch)

    b_out = swa_attention(qkv, q_norm_w, k_norm_w, sinks.astype(F32), batch=batch,
                          q_width=q_width, kv_width=kv_width)

    n_groups, n_experts = wc.shape[1], wf.shape[1]
    router_w = _pad_cols(jnp.concatenate([wc, wf], axis=1), LANES)
    router_b = _pad_cols(jnp.concatenate([bc, bf]).reshape(1, -1), LANES)
    assert n_groups == N_GROUPS and n_experts == N_GROUPS * GROUP_SIZE
    x1, h2, route = merge_route(a_out, b_out, gates, x2d, proj_rwkv.astype(BF16),
                                proj_attn.astype(BF16), w_out.astype(BF16), row(norm2_w),
                                router_w, router_b)
    return hierarchical_moe(x1, h2, route, wg, wu, wd)


def kernel(x, norm1_w, w_in, rwkv_mu, rwkv_w0, rwkv_w2, rwkv_a0, rwkv_a2, rwkv_g2, rwkv_k_k,
           rwkv_k_a, rwkv_r_k, rwkv_gn_w, rwkv_gn_b, q_norm_w, k_norm_w, attn_sinks, proj_rwkv,
           proj_attn, w_out, norm2_w, router_coarse_w, router_coarse_b, router_fine_w,
           router_fine_b, expert_w_gate, expert_w_up, expert_w_down):
    batch, seq, d = x.shape
    x2d = x.reshape(batch * seq, d)
    for layer in range(norm1_w.shape[0]):
        x2d = _layer(x2d, batch, norm1_w[layer], w_in[layer], rwkv_mu[layer], rwkv_w0[layer],
                     rwkv_w2[layer], rwkv_a0[layer], rwkv_a2[layer], rwkv_g2[layer],
                     rwkv_k_k[layer], rwkv_k_a[layer], rwkv_r_k[layer].reshape(-1),
                     rwkv_gn_w[layer], rwkv_gn_b[layer], q_norm_w[layer], k_norm_w[layer],
                     attn_sinks[layer], proj_rwkv[layer], proj_attn[layer], w_out[layer],
                     norm2_w[layer], router_coarse_w[layer], router_coarse_b[layer],
                     router_fine_w[layer], router_fine_b[layer], expert_w_gate[layer],
                     expert_w_up[layer], expert_w_down[layer])
    return x2d.reshape(batch, seq, d)
```

```python
import functools

import jax
import jax.numpy as jnp
from jax import lax
from jax.experimental import pallas as pl
from jax.experimental.pallas import tpu as pltpu

F32 = jnp.float32
BF16 = jnp.bfloat16
I32 = jnp.int32

NORM_EPS = 1e-6
GN_EPS = 64e-5
HEAD_DIM = 64
LANES = 128
CHUNK = 64
WINDOW = 128
MOE_BLOCK = 128
SUPER_ROWS = 1024
FF_TILE = 512
TOP_K = 2
N_GROUPS = 8
GROUP_SIZE = 8
VMEM_LIMIT = 56 * 1024 * 1024

_NT = (((1,), (1,)), ((), ()))
_TN = (((0,), (0,)), ((), ()))


def _dot(a, b):
    return jnp.dot(a, b, preferred_element_type=F32)


def _split2(x):
    hi = x.astype(BF16)
    mid = (x - hi.astype(F32)).astype(BF16)
    return hi, mid


def _select_sum(m, hi, mid):
    return _dot(m, hi) + _dot(m, mid)


def _pack_bf16_halves(x):
    w = x.shape[1] // 2
    lo = lax.bitcast_convert_type(x[:, :w].astype(BF16).astype(F32), jnp.uint32)
    hi = lax.bitcast_convert_type(x[:, w:].astype(BF16).astype(F32), jnp.uint32)
    return (lo >> 16) | (hi & jnp.uint32(0xFFFF0000))


def _unpack_bf16_halves(xp):
    lo = lax.bitcast_convert_type(xp << 16, F32).astype(BF16)
    hi = lax.bitcast_convert_type(xp & jnp.uint32(0xFFFF0000), F32).astype(BF16)
    return lo, hi


def _cparams(sem, vmem=VMEM_LIMIT):
    return pltpu.CompilerParams(dimension_semantics=sem, vmem_limit_bytes=vmem)


def _norm_proj_body(x_ref, g_ref, w_ref, rwkv_ref, qkv_ref, gates_ref, h_ref, *, n_rwkv, n_qkv):
    j = pl.program_id(1)

    @pl.when(j == 0)
    def _():
        x = x_ref[...]
        ms = jnp.mean(x * x, axis=-1, keepdims=True)
        h_ref[...] = (x * lax.rsqrt(ms + NORM_EPS) * g_ref[...]).astype(BF16)

    acc = _dot(h_ref[...], w_ref[...])

    @pl.when(j < n_rwkv)
    def _():
        rwkv_ref[...] = acc

    @pl.when((j >= n_rwkv) & (j < n_rwkv + n_qkv))
    def _():
        qkv_ref[...] = acc.astype(qkv_ref.dtype)

    @pl.when(j >= n_rwkv + n_qkv)
    def _():
        gates_ref[...] = jax.nn.sigmoid(acc).astype(gates_ref.dtype)


def norm_proj(x2d, gain, w, widths, *, tm=1024, tn=512):
    n, d = x2d.shape
    tm = min(tm, n)
    n_rwkv, n_qkv, n_gates = (c // tn for c in widths)
    group = lambda first, count: (lambda i, j: (i, jnp.clip(j - first, 0, count - 1)))
    return pl.pallas_call(
        functools.partial(_norm_proj_body, n_rwkv=n_rwkv, n_qkv=n_qkv),
        grid=(n // tm, n_rwkv + n_qkv + n_gates),
        in_specs=[pl.BlockSpec((tm, d), lambda i, j: (i, 0)),
                  pl.BlockSpec((1, d), lambda i, j: (0, 0)),
                  pl.BlockSpec((d, tn), lambda i, j: (0, j))],
        out_specs=[pl.BlockSpec((tm, tn), group(0, n_rwkv)),
                   pl.BlockSpec((tm, tn), group(n_rwkv, n_qkv)),
                   pl.BlockSpec((tm, tn), group(n_rwkv + n_qkv, n_gates))],
        out_shape=[jax.ShapeDtypeStruct((n, widths[0]), F32),
                   jax.ShapeDtypeStruct((n, widths[1]), BF16),
                   jax.ShapeDtypeStruct((n, widths[2]), BF16)],
        scratch_shapes=[pltpu.VMEM((tm, d), BF16)],
        compiler_params=_cparams(("parallel", "arbitrary")),
        name="norm_proj",
    )(x2d, gain, w)


def _head_sum(x, e, et):
    s = _dot(x.astype(BF16), e)
    hi, mid = _split2(s)
    return _dot(hi, et) + _dot(mid, et)


def _rwkv_prep_body(p_ref, pprev_ref, mu_ref, w0_ref, w2_ref, a0_ref, a2_ref, g2_ref,
                    kk_ref, ka_ref, rk_ref, e_ref, et_ref, tri_ref, sel_ref,
                    rt_ref, at_ref, bt_ref, kt_ref, v_ref, g_ref, bonus_ref, pc_ref,
                    *, seq_len, width):
    tm = p_ref.shape[0]
    w_ = width
    first = (pl.program_id(0) * tm) % seq_len == 0
    p = p_ref[...]
    prev_row = jnp.where(first, 0.0, pprev_ref[7:8, :])
    row = lax.broadcasted_iota(I32, (tm, 1), 0)
    shifted = jnp.where(row == 0, prev_row, pltpu.roll(p, 1, 0))
    m = p + (shifted - p) * mu_ref[...]
    r = m[:, 0:w_]
    k = m[:, w_:2 * w_]
    v = m[:, 2 * w_:3 * w_]
    xw = m[:, 3 * w_:3 * w_ + 128]
    xa = m[:, 3 * w_ + 128:3 * w_ + 256]
    xg = m[:, 3 * w_ + 256:]

    z = -(w0_ref[...] + _dot(jnp.tanh(xw).astype(BF16), w2_ref[...]))
    softplus = jnp.maximum(z, 0.0) + jnp.log1p(jnp.exp(-jnp.abs(z)))
    logw = -jnp.exp(-softplus - 0.5)
    a = jax.nn.sigmoid(a0_ref[...] + _dot(xa.astype(BF16), a2_ref[...]))
    g = _dot(jax.nn.sigmoid(xg).astype(BF16), g2_ref[...])

    e = e_ref[...]
    et = et_ref[...]
    kk = k * kk_ref[...]
    kk = kk * jnp.minimum(lax.rsqrt(_head_sum(kk * kk, e, et)), 1e12)
    kmod = k * (1.0 + (a - 1.0) * ka_ref[...])
    bonus = _head_sum(r * kmod * rk_ref[...], e, et) * v

    lw_hi, lw_mid = _split2(logw)
    cum = _select_sum(tri_ref[...], lw_hi, lw_mid)
    pc_ref[...] = jnp.exp(_select_sum(sel_ref[...], lw_hi, lw_mid))
    inv = jnp.exp(-cum)
    rt_ref[...] = (r * jnp.exp(cum)).astype(BF16)
    at_ref[...] = (-kk * jnp.exp(cum - logw)).astype(BF16)
    bt_ref[...] = (kk * a * inv).astype(BF16)
    kt_ref[...] = (kmod * inv).astype(BF16)
    v_ref[...] = v.astype(BF16)
    g_ref[...] = g.astype(BF16)
    bonus_ref[...] = bonus


def _head_indicator(width):
    heads = width // HEAD_DIM
    c = jnp.arange(width)[:, None] // HEAD_DIM
    h = jnp.arange(LANES)[None, :]
    e = (c == h).astype(BF16)
    assert heads <= LANES
    return e, e.T


def rwkv_prep(p, mu, w0, w2, a0, a2, g2, k_k, k_a, r_k, *, seq_len, tm=512):
    n, cols = p.shape
    width = w0.shape[1]
    nchunk = tm // CHUNK
    e, et = _head_indicator(width)
    t = jnp.arange(tm)
    same = (t[:, None] // CHUNK) == (t[None, :] // CHUNK)
    tri = (same & (t[:, None] >= t[None, :])).astype(BF16)
    sel = ((t[None, :] // CHUNK) == jnp.arange(nchunk)[:, None]).astype(BF16)
    const = lambda shape: pl.BlockSpec(shape, lambda i: (0, 0))
    stream = lambda dt: jax.ShapeDtypeStruct((n, width), dt)
    outs = pl.pallas_call(
        functools.partial(_rwkv_prep_body, seq_len=seq_len, width=width),
        grid=(n // tm,),
        in_specs=[pl.BlockSpec((tm, cols), lambda i: (i, 0)),
                  pl.BlockSpec((8, cols), lambda i: (jnp.maximum(i * (tm // 8) - 1, 0), 0)),
                  const((1, cols)), const((1, width)), const(w2.shape), const((1, width)),
                  const(a2.shape), const(g2.shape), const((1, width)), const((1, width)),
                  const((1, width)), const(e.shape), const(et.shape), const(tri.shape),
                  const(sel.shape)],
        out_specs=[pl.BlockSpec((tm, width), lambda i: (i, 0))] * 7
                  + [pl.BlockSpec((nchunk, width), lambda i: (i, 0))],
        out_shape=[stream(BF16)] * 6 + [stream(F32),
                   jax.ShapeDtypeStruct((n // CHUNK, width), F32)],
        compiler_params=_cparams(("parallel",)),
        name="rwkv_prep",
    )(p, p, mu, w0, w2, a0, a2, g2, k_k, k_a, r_k, e, et, tri, sel)
    return outs


def _chunk_pairs(rts, ats, bts, kts, vvs, pcs, s_prevs, masks):
    lo, strict, incl = masks
    c = rts[0].shape[0]
    c2 = 2 * c
    zero = jnp.zeros_like(rts[0])
    each = lambda f, *ls: [f(*a) for a in zip(*ls)]

    def stack(x):
        return jnp.concatenate([jnp.where(lo, x, zero), jnp.where(lo, zero, x)], axis=0)

    ar = each(lambda a, r: jnp.concatenate([stack(a), stack(r)], axis=0), ats, rts)
    bk = each(lambda b, k: jnp.concatenate([stack(b), stack(k)], axis=0), bts, kts)
    vb = each(stack, vvs)
    gram = each(lambda x, y: lax.dot_general(x, y, _NT, preferred_element_type=F32), ar, bk)
    ars = each(lambda x, s: lax.dot_general(x, s.astype(BF16), _NT, preferred_element_type=F32),
               ar, s_prevs)
    a_ab = each(lambda g: jnp.where(strict, g[:c2, :c2], 0.0), gram)
    rhs = each(lambda g, v, x: x[:c2] + _dot(jnp.where(strict, g[:c2, c2:], 0.0).astype(BF16), v),
               gram, vb, ars)

    u = rhs
    nk = a_ab
    steps = (c - 1).bit_length()
    for k in range(steps):
        nkb = each(lambda n: n.astype(BF16), nk)
        if k + 1 < steps:
            prod = each(lambda n, uu: _dot(n, jnp.concatenate([n, uu.astype(BF16)], axis=1)),
                        nkb, u)
            nk = each(lambda p: p[:, :c2], prod)
            u = each(lambda uu, p: uu + p[:, c2:], u, prod)
        else:
            u = each(lambda uu, n: uu + _dot(n, uu.astype(BF16)), u, nkb)

    uv = each(lambda uu, v: jnp.concatenate([uu.astype(BF16), v], axis=0), u, vb)
    a_r = each(lambda g: jnp.where(incl, g[c2:, :], 0.0).astype(BF16), gram)
    ybd = each(lambda x, a, w: x[c2:] + _dot(a, w), ars, a_r, uv)
    ys = each(lambda yb: yb[:c] + yb[c:], ybd)

    bk_end = each(lambda x, pc: (x.astype(F32) * pc).astype(BF16), bk, pcs)
    s_new = each(lambda s, pc, w, x: s * pc + lax.dot_general(w, x, _TN, preferred_element_type=F32),
                 s_prevs, pcs, uv, bk_end)
    return ys, s_new


def _rwkv_scan_body(rt_ref, at_ref, bt_ref, kt_ref, v_ref, pc_ref, g_ref, bonus_ref,
                    gnw_ref, gnb_ref, e_ref, et_ref, o_ref, s_ref, y_ref):
    tr, width = rt_ref.shape
    npairs = width // LANES

    @pl.when(pl.program_id(1) == 0)
    def _():
        s_ref[...] = jnp.zeros_like(s_ref)

    lane = lax.broadcasted_iota(I32, (CHUNK, LANES), 1)
    lo = lane < HEAD_DIM
    ri = lax.broadcasted_iota(I32, (2 * CHUNK, 2 * CHUNK), 0)
    ci = lax.broadcasted_iota(I32, (2 * CHUNK, 2 * CHUNK), 1)
    same = (ri // CHUNK) == (ci // CHUNK)
    ri2 = lax.broadcasted_iota(I32, (2 * CHUNK, 4 * CHUNK), 0)
    ci2 = lax.broadcasted_iota(I32, (2 * CHUNK, 4 * CHUNK), 1) % (2 * CHUNK)
    incl = ((ri2 // CHUNK) == (ci2 // CHUNK)) & (ri2 >= ci2)
    masks = (lo, same & (ri > ci), incl)

    def chunk_step(c, carry):
        rows = pl.ds(pl.multiple_of(c * CHUNK, CHUNK), CHUNK)
        this_chunk = lax.broadcasted_iota(I32, (pc_ref.shape[0], 1), 0) == c
        cols = [slice(pr * LANES, (pr + 1) * LANES) for pr in range(npairs)]
        pcs = [jnp.sum(jnp.where(this_chunk, pc_ref[:, cl], 0.0), axis=0, keepdims=True)
               for cl in cols]
        load = lambda ref: [ref[rows, cl] for cl in cols]
        ys, s_new = _chunk_pairs(load(rt_ref), load(at_ref), load(bt_ref), load(kt_ref),
                                 load(v_ref), pcs, [s_ref[pr] for pr in range(npairs)], masks)
        for pr in range(npairs):
            s_ref[pr] = s_new[pr]
            y_ref[rows, cols[pr]] = ys[pr]
        return carry

    lax.fori_loop(0, tr // CHUNK, chunk_step, 0)

    e = e_ref[...]
    et = et_ref[...]
    y = y_ref[...]
    mean = _head_sum(y, e, et) * (1.0 / HEAD_DIM)
    d = y - mean
    var = _head_sum(d * d, e, et) * (1.0 / HEAD_DIM)
    out = d * lax.rsqrt(var + GN_EPS) * gnw_ref[...] + gnb_ref[...]
    out = (out + bonus_ref[...]) * g_ref[...].astype(F32)
    o_ref[...] = out.astype(o_ref.dtype)


def rwkv_scan(rt, at, bt, kt, vv, pc, g, bonus, gn_w, gn_b, *, batch, tr=512):
    n, width = rt.shape
    seq = n // batch
    nst = seq // tr
    e, et = _head_indicator(width)
    blk = lambda rows: pl.BlockSpec((rows, width), lambda b, s: (b * nst + s, 0))
    const = lambda shape: pl.BlockSpec(shape, lambda b, s: (0, 0))
    return pl.pallas_call(
        _rwkv_scan_body,
        grid=(batch, nst),
        in_specs=[blk(tr)] * 5 + [blk(tr // CHUNK), blk(tr), blk(tr),
                  const((1, width)), const((1, width)), const(e.shape), const(et.shape)],
        out_specs=blk(tr),
        out_shape=jax.ShapeDtypeStruct((n, width), BF16),
        scratch_shapes=[pltpu.VMEM((width // LANES, LANES, LANES), F32),
                        pltpu.VMEM((tr, width), F32)],
        compiler_params=_cparams(("parallel", "arbitrary")),
        name="rwkv_scan",
    )(rt, at, bt, kt, vv, pc, g, bonus, gn_w, gn_b, e, et)


def _head_rmsnorm(x, gain, bd):
    hi, mid = _split2(x * x)
    ms = (_dot(hi, bd) + _dot(mid, bd)) * (1.0 / HEAD_DIM)
    return x * lax.rsqrt(ms + NORM_EPS) * gain


def _swa_body(sink_ref, q_ref, kp_ref, kc_ref, vp_ref, vc_ref, qg_ref, kg_ref, bd_ref, o_ref,
              *, group):
    blk = q_ref.shape[0]
    n = pl.program_id(1)
    bd = bd_ref[...]
    scale = HEAD_DIM ** -0.5
    lane = lax.broadcasted_iota(I32, (1, LANES), 1)
    lo = lane < HEAD_DIM
    ri = lax.broadcasted_iota(I32, (2 * blk, 2 * blk), 0)
    cj = lax.broadcasted_iota(I32, (2 * blk, 2 * blk), 1)
    rel = blk + (ri % blk) - cj
    jmin = jnp.where(n > 0, 0, blk)
    valid = (rel >= 0) & (rel < WINDOW) & (cj >= jmin)
    top_rows = lax.broadcasted_iota(I32, (2 * blk, 1), 0) < blk

    kcat = jnp.concatenate([kp_ref[...], kc_ref[...]], axis=0).astype(F32)
    vcat = jnp.concatenate([vp_ref[...], vc_ref[...]], axis=0).astype(F32)
    kv_heads = kcat.shape[1] // HEAD_DIM
    q_tiles = q_ref.shape[1] // LANES
    tiles_per_kv = group // 2
    each = lambda f, *ls: [f(*a) for a in zip(*ls)]

    kv_cols = [slice(jt * LANES, (jt + 1) * LANES) for jt in range(kv_heads // 2)]
    kn = [_head_rmsnorm(kcat[:, c], kg_ref[...], bd) for c in kv_cols]
    kn_r = [pltpu.roll(x, HEAD_DIM, 1) for x in kn]
    vt = [vcat[:, c] for c in kv_cols]
    vt_r = [pltpu.roll(x, HEAD_DIM, 1) for x in vt]
    own = lambda hk: lo if hk % 2 == 0 else jnp.logical_not(lo)
    k2 = [jnp.where(own(hk), kn[hk // 2], kn_r[hk // 2]).astype(BF16) for hk in range(kv_heads)]
    v2 = [jnp.where(own(hk), vt[hk // 2], vt_r[hk // 2]).astype(BF16) for hk in range(kv_heads)]

    q_cols = [slice(t * LANES, (t + 1) * LANES) for t in range(q_tiles)]
    qn = [_head_rmsnorm(q_ref[:, c].astype(F32), qg_ref[...] * scale, bd) for c in q_cols]
    qst = [jnp.concatenate([jnp.where(lo, x, 0.0), jnp.where(lo, 0.0, x)], axis=0).astype(BF16)
           for x in qn]
    s = [jnp.where(valid, lax.dot_general(x, k2[t // tiles_per_kv], _NT,
                                          preferred_element_type=F32), -jnp.inf)
         for t, x in enumerate(qst)]
    sink = [jnp.where(top_rows, sink_ref[2 * t], sink_ref[2 * t + 1]) for t in range(q_tiles)]
    mx = each(lambda x, sk: jnp.maximum(jnp.max(x, axis=-1, keepdims=True), sk), s, sink)
    pr = each(lambda x, m: jnp.exp(x - m), s, mx)
    inv = each(lambda p, sk, m: 1.0 / (jnp.sum(p, axis=-1, keepdims=True) + jnp.exp(sk - m)),
               pr, sink, mx)
    o = [_dot((p * r).astype(BF16), v2[t // tiles_per_kv])
         for t, (p, r) in enumerate(zip(pr, inv))]
    for t in range(q_tiles):
        o_ref[:, q_cols[t]] = jnp.where(lo, o[t][:blk], o[t][blk:]).astype(o_ref.dtype)


def swa_attention(qkv, q_gain, k_gain, sinks, *, batch, q_width, kv_width):
    n = qkv.shape[0]
    blk = WINDOW
    nb = n // batch // blk
    group = (q_width // HEAD_DIM) // (kv_width // HEAD_DIM)
    kcol = q_width // kv_width
    t = jnp.arange(LANES)
    bd = ((t[:, None] // HEAD_DIM) == (t[None, :] // HEAD_DIM)).astype(BF16)
    qg = jnp.tile(q_gain.reshape(1, HEAD_DIM), (1, 2))
    kg = jnp.tile(k_gain.reshape(1, HEAD_DIM), (1, 2))
    cur = lambda col: (lambda b, i, *_: (b * nb + i, col))
    prev = lambda col: (lambda b, i, *_: (b * nb + jnp.maximum(i - 1, 0), col))
    const = lambda shape: pl.BlockSpec(shape, lambda b, i, *_: (0, 0))
    grid_spec = pltpu.PrefetchScalarGridSpec(
        num_scalar_prefetch=1,
        grid=(batch, nb),
        in_specs=[pl.BlockSpec((blk, q_width), cur(0)),
                  pl.BlockSpec((blk, kv_width), prev(kcol)),
                  pl.BlockSpec((blk, kv_width), cur(kcol)),
                  pl.BlockSpec((blk, kv_width), prev(kcol + 1)),
                  pl.BlockSpec((blk, kv_width), cur(kcol + 1)),
                  const((1, LANES)), const((1, LANES)), const((LANES, LANES))],
        out_specs=pl.BlockSpec((blk, q_width), cur(0)),
    )
    return pl.pallas_call(
        functools.partial(_swa_body, group=group),
        grid_spec=grid_spec,
        out_shape=jax.ShapeDtypeStruct((n, q_width), BF16),
        compiler_params=_cparams(("parallel", "arbitrary")),
        name="swa_attn",
    )(sinks, qkv, qkv, qkv, qkv, qkv, qg, kg, bd)


def _merge_route_body(a_ref, b_ref, ga_ref, gb_ref, x_ref, pr_ref, pa_ref, wo_ref, n2_ref,
                      rhm_ref, rh_ref, rb_ref, x1_ref, h2_ref, route_ref, logits_ref):
    @pl.when(pl.program_id(0) == 0)
    def _():
        logits_ref[...] = jnp.zeros_like(logits_ref)

    _route(logits_ref[...], route_ref)

    merged = (ga_ref[...].astype(F32) * _dot(a_ref[...], pr_ref[...])
              + gb_ref[...].astype(F32) * _dot(b_ref[...], pa_ref[...]))
    x1 = x_ref[...] + _dot(merged.astype(BF16), wo_ref[...])
    x1_ref[...] = x1
    ms = jnp.mean(x1 * x1, axis=-1, keepdims=True)
    h2 = x1 * lax.rsqrt(ms + NORM_EPS) * n2_ref[...]
    h2_ref[...] = _pack_bf16_halves(h2)

    hi, mid = _split2(h2)
    hw = _dot(hi, rhm_ref[...])
    logits_ref[...] = hw[:, :LANES] + (hw[:, LANES:] + _dot(mid, rh_ref[...])) + rb_ref[...]


def _route(logits, route_ref):
    lane = lax.broadcasted_iota(I32, logits.shape, 1)
    big = jnp.int32(1 << 20)
    neg = -jnp.inf
    is_coarse = lane < N_GROUPS
    cl = jnp.where(is_coarse, logits, neg)
    ce = jnp.exp(cl - jnp.max(cl, axis=-1, keepdims=True))
    cp = ce / jnp.sum(ce, axis=-1, keepdims=True)
    g_prob = jnp.max(cp, axis=-1, keepdims=True)
    g_idx = jnp.min(jnp.where(is_coarse & (cp == g_prob), lane, big), axis=-1, keepdims=True)

    fine_lane = lane - N_GROUPS
    in_group = ((lane >= N_GROUPS) & (lane < N_GROUPS + N_GROUPS * GROUP_SIZE)
                & ((fine_lane // GROUP_SIZE) == g_idx))
    fl = jnp.where(in_group, logits, neg)
    fe = jnp.exp(fl - jnp.max(fl, axis=-1, keepdims=True))
    fp = fe / jnp.sum(fe, axis=-1, keepdims=True)
    p1 = jnp.max(jnp.where(in_group, fp, -1.0), axis=-1, keepdims=True)
    i1 = jnp.min(jnp.where(in_group & (fp == p1), lane, big), axis=-1, keepdims=True)
    rest = in_group & (lane != i1)
    p2 = jnp.max(jnp.where(rest, fp, -1.0), axis=-1, keepdims=True)
    i2 = jnp.min(jnp.where(rest & (fp == p2), lane, big), axis=-1, keepdims=True)
    den = p1 + p2
    w1 = g_prob * p1 / den
    w2 = g_prob * p2 / den
    e1 = (i1 - N_GROUPS).astype(F32)
    e2 = (i2 - N_GROUPS).astype(F32)
    route_ref[...] = jnp.where(lane == 0, w1, jnp.where(lane == 1, w2,
                     jnp.where(lane == 2, e1, jnp.where(lane == 3, e2, 0.0))))


def merge_route(a_out, b_out, gates, x2d, proj_r, proj_a, w_out, norm2_w, router_w, router_b,
                *, tm=256):
    n, d = x2d.shape
    wa = a_out.shape[1]
    rh, rm = _split2(router_w)
    rhm = jnp.concatenate([rh, rm], axis=1)
    nt = n // tm
    const = lambda arr: pl.BlockSpec(arr.shape, lambda i: (0, 0), pipeline_mode=pl.Buffered(1))
    row = lambda w, col=0: pl.BlockSpec((tm, w), lambda i: (jnp.minimum(i, nt - 1), col))
    return pl.pallas_call(
        _merge_route_body,
        grid=(nt + 1,),
        in_specs=[row(wa), row(wa), row(d), row(d, 1), row(d),
                  const(proj_r), const(proj_a), const(w_out), const(norm2_w),
                  const(rhm), const(rh), const(router_b)],
        out_specs=[row(d), row(d // 2),
                   pl.BlockSpec((tm, LANES), lambda i: (jnp.maximum(i - 1, 0), 0))],
        scratch_shapes=[pltpu.VMEM((tm, LANES), F32)],
        out_shape=[jax.ShapeDtypeStruct((n, d), F32), jax.ShapeDtypeStruct((n, d // 2), jnp.uint32),
                   jax.ShapeDtypeStruct((n, LANES), F32)],
        compiler_params=_cparams(("arbitrary",)),
        name="merge_route",
    )(a_out, b_out, gates, gates, x2d, proj_r, proj_a, w_out, norm2_w, rhm, rh, router_b)


def _moe_rank_body(ids_ref, ut_ref, rank_ref, counts_ref, carry_ref):
    @pl.when(pl.program_id(0) == 0)
    def _():
        carry_ref[...] = jnp.zeros_like(carry_ref)

    ids = ids_ref[...]
    n_exp = carry_ref.shape[0]
    sub = lax.broadcasted_iota(I32, (n_exp, ids.shape[1]), 0)
    onehot = (sub == ids).astype(F32)
    before = _dot(onehot.astype(BF16), ut_ref[...])
    carry = carry_ref[...]
    rank_ref[...] = jnp.sum(onehot * (before + carry), axis=0, keepdims=True).astype(I32)
    carry = carry + jnp.sum(onehot, axis=1, keepdims=True)
    carry_ref[...] = carry
    counts_ref[...] = jnp.broadcast_to(carry, counts_ref.shape)


def moe_rank(ids_row, n_experts, *, tb=512):
    m = ids_row.shape[1]
    t = jnp.arange(tb)
    ut = (t[:, None] < t[None, :]).astype(BF16)
    return pl.pallas_call(
        _moe_rank_body,
        grid=(m // tb,),
        in_specs=[pl.BlockSpec((1, tb), lambda i: (0, i)),
                  pl.BlockSpec((tb, tb), lambda i: (0, 0))],
        out_specs=[pl.BlockSpec((1, tb), lambda i: (0, i)),
                   pl.BlockSpec((n_experts, LANES), lambda i: (0, 0))],
        out_shape=[jax.ShapeDtypeStruct((1, m), I32),
                   jax.ShapeDtypeStruct((n_experts, LANES), F32)],
        scratch_shapes=[pltpu.VMEM((n_experts, 1), F32)],
        compiler_params=_cparams(("arbitrary",)),
        name="moe_rank",
    )(ids_row, ut)


def _moe_dest_body(ids_ref, rank_ref, start_ref, dest_ref):
    ids = ids_ref[...]
    sub = lax.broadcasted_iota(I32, (start_ref.shape[0], ids.shape[1]), 0)
    start = jnp.sum(jnp.where(sub == ids, start_ref[...], 0.0), axis=0, keepdims=True)
    dest_ref[...] = start.astype(I32) + rank_ref[...]


def moe_dest(ids_row, rank_row, pad_start_col, *, tb=512):
    m = ids_row.shape[1]
    row = pl.BlockSpec((1, tb), lambda i: (0, i))
    return pl.pallas_call(
        _moe_dest_body,
        grid=(m // tb,),
        in_specs=[row, row, pl.BlockSpec(pad_start_col.shape, lambda i: (0, 0))],
        out_specs=row,
        out_shape=jax.ShapeDtypeStruct((1, m), I32),
        compiler_params=_cparams(("parallel",)),
        name="moe_dest",
    )(ids_row, rank_row, pad_start_col)


def _moe_dispatch_body(dest_ref, start_ref, end_ref, h_ref, buf_ref, zeros_ref, sem, zsem):
    tm = h_ref.shape[0]
    base = pl.program_id(0) * tm * TOP_K

    @pl.when(pl.program_id(0) == 0)
    def _():
        zeros_ref[...] = jnp.zeros_like(zeros_ref)

        def zero_last_block(e):
            last = pl.multiple_of(end_ref[e] - MOE_BLOCK, MOE_BLOCK)
            return pltpu.make_async_copy(zeros_ref, buf_ref.at[pl.ds(last, MOE_BLOCK)], zsem)

        def zstart(e, c):
            @pl.when(end_ref[e] > start_ref[e])
            def _():
                zero_last_block(e).start()
            return c

        def zwait(e, c):
            @pl.when(end_ref[e] > start_ref[e])
            def _():
                zero_last_block(e).wait()
            return c

        lax.fori_loop(0, start_ref.shape[0], zstart, 0)
        lax.fori_loop(0, start_ref.shape[0], zwait, 0)

        tail0 = end_ref[end_ref.shape[0] - 1]

        def zero_tail_block(r):
            first = pl.multiple_of(tail0 + r * MOE_BLOCK, MOE_BLOCK)
            return pltpu.make_async_copy(zeros_ref, buf_ref.at[pl.ds(first, MOE_BLOCK)], zsem)

        def tstart(r, c):
            zero_tail_block(r).start()
            return c

        def twait(r, c):
            zero_tail_block(r).wait()
            return c

        n_tail = (buf_ref.shape[0] - tail0) // MOE_BLOCK
        lax.fori_loop(0, n_tail, tstart, 0)
        lax.fori_loop(0, n_tail, twait, 0)

    def row_copy(r, k):
        slot = dest_ref[base + r * TOP_K + k]
        return pltpu.make_async_copy(h_ref.at[pl.ds(r, 1)], buf_ref.at[pl.ds(slot, 1)], sem)

    def start(r, c):
        for k in range(TOP_K):
            row_copy(r, k).start()
        return c

    def wait(r, c):
        for k in range(TOP_K):
            row_copy(r, k).wait()
        return c

    lax.fori_loop(0, tm, start, 0, unroll=4)
    lax.fori_loop(0, tm, wait, 0, unroll=4)


def moe_dispatch(dest, pad_start, pad_end, h2, n_slots, *, tm=256):
    n, d = h2.shape
    grid_spec = pltpu.PrefetchScalarGridSpec(
        num_scalar_prefetch=3,
        grid=(n // tm,),
        in_specs=[pl.BlockSpec((tm, d), lambda i, *_: (i, 0))],
        out_specs=pl.BlockSpec(memory_space=pl.ANY),
        scratch_shapes=[pltpu.VMEM((MOE_BLOCK, d), h2.dtype),
                        pltpu.SemaphoreType.DMA(()), pltpu.SemaphoreType.DMA(())],
    )
    return pl.pallas_call(
        _moe_dispatch_body,
        grid_spec=grid_spec,
        out_shape=jax.ShapeDtypeStruct((n_slots, d), h2.dtype),
        compiler_params=_cparams(("arbitrary",)),
        name="moe_dispatch",
    )(dest, pad_start, pad_end, h2)


def _moe_expert_body(sbe_ref, row0_ref, nb_ref, tot_ref, wg_ref, wu_ref, wd_ref, buf_ref, out_ref,
                     wgb, wub, wdb, xp, acc, pending, sem_in, sem_out):
    del sbe_ref
    s = pl.program_id(0)
    f = pl.program_id(1)
    n_sb = pl.num_programs(0)
    last_f = pl.num_programs(1) - 1
    nb = nb_ref[s]
    half = xp.shape[2]

    def blk(first_row, r):
        return pl.ds(pl.multiple_of(first_row + r * MOE_BLOCK, MOE_BLOCK), MOE_BLOCK)

    def rows_in(sb, r):
        return pltpu.make_async_copy(buf_ref.at[blk(row0_ref[sb], r)], xp.at[sb % 2, blk(0, r)],
                                     sem_in.at[sb % 2])

    def rows_out(r):
        return pltpu.make_async_copy(acc.at[blk(0, r)], out_ref.at[blk(row0_ref[s], r)], sem_out)

    def for_blocks(n, fn):
        def body(r, carry):
            fn(r)
            return carry
        lax.fori_loop(0, n, body, 0)

    def drain_out():
        for_blocks(pending[0], lambda r: rows_out(0).wait())
        pending[0] = 0

    def mlp_rows(first_row, n_rows, first):
        rows = pl.ds(pl.multiple_of(first_row, MOE_BLOCK), n_rows)
        xlo, xhi = _unpack_bf16_halves(xp[s % 2, rows, :])
        hg = _dot(xlo, wgb[:half, :]) + _dot(xhi, wgb[half:, :])
        hu = _dot(xlo, wub[:half, :]) + _dot(xhi, wub[half:, :])
        y = _dot((jax.nn.silu(hg) * hu).astype(BF16), wdb[...])
        if first:
            acc[rows, :] = y
        else:
            acc[rows, :] += y

    def all_rows(first):
        for_blocks(nb // 2, lambda q: mlp_rows(q * (2 * MOE_BLOCK), 2 * MOE_BLOCK, first))

        @pl.when(nb % 2 == 1)
        def _():
            mlp_rows((nb - 1) * MOE_BLOCK, MOE_BLOCK, first)

    @pl.when((s == 0) & (f == 0))
    def _():
        pending[0] = 0
        for_blocks(nb, lambda r: rows_in(0, r).start())

    @pl.when(nb > 0)
    def _():
        @pl.when(f == 0)
        def _():
            nxt = jnp.minimum(s + 1, n_sb - 1)
            nb_next = jnp.where(s + 1 < n_sb, nb_ref[nxt], 0)
            for_blocks(nb_next, lambda r: rows_in(nxt, r).start())

        wgb[...] = wg_ref[0].astype(BF16)
        wub[...] = wu_ref[0].astype(BF16)
        wdb[...] = wd_ref[0].astype(BF16)

        @pl.when(f == 0)
        def _():
            for_blocks(nb, lambda r: rows_in(s, r).wait())
            drain_out()
            all_rows(True)

        @pl.when(f > 0)
        def _():
            all_rows(False)

        @pl.when(f == last_f)
        def _():
            for_blocks(nb, lambda r: rows_out(r).start())
            pending[0] = nb

    @pl.when((s == n_sb - 1) & (f == last_f))
    def _():
        drain_out()
        acc[pl.ds(0, MOE_BLOCK), :] = jnp.zeros((MOE_BLOCK, acc.shape[1]), acc.dtype)
        tail0 = tot_ref[0]

        def zero_out(r):
            return pltpu.make_async_copy(acc.at[pl.ds(0, MOE_BLOCK)], out_ref.at[blk(tail0, r)],
                                         sem_out)

        ntail = (out_ref.shape[0] - tail0) // MOE_BLOCK
        for_blocks(ntail, lambda r: zero_out(r).start())
        for_blocks(ntail, lambda r: zero_out(r).wait())


def moe_experts(sb_expert, sb_row0, sb_blocks, total_rows, buf, wg, wu, wd):
    p, half = buf.shape
    d = 2 * half
    ff = wg.shape[2]
    nf = ff // FF_TILE
    f_eff = lambda s, f, nb: jnp.where(nb[s] > 0, f, nf - 1)
    grid_spec = pltpu.PrefetchScalarGridSpec(
        num_scalar_prefetch=4,
        grid=(sb_expert.shape[0], nf),
        in_specs=[pl.BlockSpec((1, d, FF_TILE), lambda s, f, e, r0, nb, t: (e[s], 0, f_eff(s, f, nb))),
                  pl.BlockSpec((1, d, FF_TILE), lambda s, f, e, r0, nb, t: (e[s], 0, f_eff(s, f, nb))),
                  pl.BlockSpec((1, FF_TILE, d), lambda s, f, e, r0, nb, t: (e[s], f_eff(s, f, nb), 0)),
                  pl.BlockSpec(memory_space=pl.ANY)],
        out_specs=pl.BlockSpec(memory_space=pl.ANY),
        scratch_shapes=[pltpu.VMEM((d, FF_TILE), BF16), pltpu.VMEM((d, FF_TILE), BF16),
                        pltpu.VMEM((FF_TILE, d), BF16), pltpu.VMEM((2, SUPER_ROWS, half), jnp.uint32),
                        pltpu.VMEM((SUPER_ROWS, d), F32), pltpu.SMEM((1,), I32),
                        pltpu.SemaphoreType.DMA((2,)), pltpu.SemaphoreType.DMA(())],
    )
    return pl.pallas_call(
        _moe_expert_body,
        grid_spec=grid_spec,
        out_shape=jax.ShapeDtypeStruct((p, d), F32),
        compiler_params=_cparams(("arbitrary", "arbitrary")),
        name="moe_experts",
    )(sb_expert, sb_row0, sb_blocks, total_rows, wg, wu, wd, buf)


def _moe_combine_body(dest_ref, x1_ref, route_ref, out_ref, y_ref, rows_ref, sem):
    tm = x1_ref.shape[0]
    base = pl.program_id(0) * tm * TOP_K

    def row_copy(r, k):
        slot = dest_ref[base + r * TOP_K + k]
        return pltpu.make_async_copy(out_ref.at[pl.ds(slot, 1)], rows_ref.at[k, pl.ds(r, 1)], sem)

    def start(r, c):
        for k in range(TOP_K):
            row_copy(r, k).start()
        return c

    def wait(r, c):
        for k in range(TOP_K):
            row_copy(r, k).wait()
        return c

    lax.fori_loop(0, tm, start, 0, unroll=4)
    lax.fori_loop(0, tm, wait, 0, unroll=4)
    route = route_ref[...]
    y_ref[...] = x1_ref[...] + (rows_ref[0] * route[:, 0:1] + rows_ref[1] * route[:, 1:2])


def moe_combine(dest, x1, route, expert_out, *, tm=256):
    n, d = x1.shape
    grid_spec = pltpu.PrefetchScalarGridSpec(
        num_scalar_prefetch=1,
        grid=(n // tm,),
        in_specs=[pl.BlockSpec((tm, d), lambda i, *_: (i, 0)),
                  pl.BlockSpec((tm, LANES), lambda i, *_: (i, 0)),
                  pl.BlockSpec(memory_space=pl.ANY)],
        out_specs=pl.BlockSpec((tm, d), lambda i, *_: (i, 0)),
        scratch_shapes=[pltpu.VMEM((TOP_K, tm, d), F32), pltpu.SemaphoreType.DMA(())],
    )
    return pl.pallas_call(
        _moe_combine_body,
        grid_spec=grid_spec,
        out_shape=jax.ShapeDtypeStruct((n, d), F32),
        compiler_params=_cparams(("arbitrary",)),
        name="moe_combine",
    )(dest, x1, route, expert_out)


def hierarchical_moe(x1, h2, route, wg, wu, wd):
    n = h2.shape[0]
    n_experts = wg.shape[0]
    m = n * TOP_K
    ids_row = route[:, 2:4].astype(I32).reshape(1, m)
    rank_row, counts = moe_rank(ids_row, n_experts)
    counts = counts[:, 0].astype(I32)
    padded = (counts + MOE_BLOCK - 1) // MOE_BLOCK * MOE_BLOCK
    pad_end = jnp.cumsum(padded)
    pad_start = pad_end - padded
    dest = moe_dest(ids_row, rank_row, pad_start.astype(F32).reshape(n_experts, 1)).reshape(m)
    n_slots = m + n_experts * MOE_BLOCK
    n_sb = (padded + SUPER_ROWS - 1) // SUPER_ROWS
    sb_end = jnp.cumsum(n_sb)
    sb_start = sb_end - n_sb
    s_idx = jnp.arange(n_experts + m // SUPER_ROWS, dtype=I32)
    used = s_idx < sb_end[-1]
    e_of = jnp.minimum(jnp.sum(sb_end[None, :] <= s_idx[:, None], axis=1), n_experts - 1)
    piece = s_idx - sb_start[e_of]
    sb_row0 = jnp.where(used, pad_start[e_of] + piece * SUPER_ROWS, 0).astype(I32)
    sb_rows = jnp.clip(padded[e_of] - piece * SUPER_ROWS, 0, SUPER_ROWS)
    sb_blocks = jnp.where(used, sb_rows // MOE_BLOCK, 0).astype(I32)
    sb_expert = jnp.where(used, e_of, e_of[jnp.maximum(sb_end[-1] - 1, 0)]).astype(I32)
    buf = moe_dispatch(dest, pad_start.astype(I32), pad_end.astype(I32), h2, n_slots)
    expert_out = moe_experts(sb_expert, sb_row0, sb_blocks, pad_end[-1:].astype(I32), buf,
                             wg, wu, wd)
    return moe_combine(dest, x1, route, expert_out)


def _pad_cols(w, to):
    return jnp.pad(w, ((0, 0), (0, to - w.shape[1])))


def _pad_rows(w, to):
    return jnp.pad(w, ((0, to - w.shape[0]), (0, 0)))


def _layer(x2d, batch, norm1_w, w_in, mu, w0, w2, a0, a2, g2, k_k, k_a, r_k, gn_w, gn_b,
           q_norm_w, k_norm_w, sinks, proj_rwkv, proj_attn, w_out, norm2_w, wc, bc, wf, bf,
           wg, wu, wd):
    n, d = x2d.shape
    seq = n // batch
    width = w0.shape[0]
    dl, il, gl = w2.shape[0], a2.shape[0], g2.shape[0]
    q_width = proj_attn.shape[0]
    rwkv_cols = 3 * width + dl + il + gl
    kv_width = (w_in.shape[1] - rwkv_cols - q_width - 2 * d) // 2
    row = lambda v: v.reshape(1, -1).astype(F32)

    c0 = 3 * width
    w_rwkv = jnp.concatenate([w_in[:, :c0], _pad_cols(w_in[:, c0:c0 + dl], LANES),
                              _pad_cols(w_in[:, c0 + dl:c0 + dl + il], LANES),
                              w_in[:, c0 + dl + il:rwkv_cols]], axis=1).astype(BF16)
    mu_p = jnp.concatenate([mu[:c0], jnp.pad(mu[c0:c0 + dl], (0, LANES - dl)),
                            jnp.pad(mu[c0 + dl:c0 + dl + il], (0, LANES - il)),
                            mu[c0 + dl + il:]]).reshape(1, -1)
    q0 = rwkv_cols
    w_all = jnp.concatenate([w_rwkv, w_in[:, q0:].astype(BF16)], axis=1)
    p_rwkv, qkv, gates = norm_proj(x2d, row(norm1_w), w_all,
                                   (w_rwkv.shape[1], q_width + 2 * kv_width, 2 * d))

    rt, at, bt, kt, vv, g, bonus, pc = rwkv_prep(
        p_rwkv, mu_p, row(w0), _pad_rows(w2, LANES).astype(BF16), row(a0),
        _pad_rows(a2, LANES).astype(BF16), g2.astype(BF16), row(k_k), row(k_a), row(r_k),
        seq_len=seq)
    a_out = rwkv_scan(rt, at, bt, kt, vv, pc, g, bonus, row(gn_w), row(gn_b), batch=batch)

    b_out = swa_attention(qkv, q_norm_w, k_norm_w, sinks.astype(F32), batch=batch,
                          q_width=q_width, kv_width=kv_width)

    n_groups, n_experts = wc.shape[1], wf.shape[1]
    router_w = _pad_cols(jnp.concatenate([wc, wf], axis=1), LANES)
    router_b = _pad_cols(jnp.concatenate([bc, bf]).reshape(1, -1), LANES)
    assert n_groups == N_GROUPS and n_experts == N_GROUPS * GROUP_SIZE
    x1, h2, route = merge_route(a_out, b_out, gates, x2d, proj_rwkv.astype(BF16),
                                proj_attn.astype(BF16), w_out.astype(BF16), row(norm2_w),
                                router_w, router_b)
    return hierarchical_moe(x1, h2, route, wg, wu, wd)


def kernel(x, norm1_w, w_in, rwkv_mu, rwkv_w0, rwkv_w2, rwkv_a0, rwkv_a2, rwkv_g2, rwkv_k_k,
           rwkv_k_a, rwkv_r_k, rwkv_gn_w, rwkv_gn_b, q_norm_w, k_norm_w, attn_sinks, proj_rwkv,
           proj_attn, w_out, norm2_w, router_coarse_w, router_coarse_b, router_fine_w,
           router_fine_b, expert_w_gate, expert_w_up, expert_w_down):
    batch, seq, d = x.shape
    x2d = x.reshape(batch * seq, d)
    for layer in range(norm1_w.shape[0]):
        x2d = _layer(x2d, batch, norm1_w[layer], w_in[layer], rwkv_mu[layer], rwkv_w0[layer],
                     rwkv_w2[layer], rwkv_a0[layer], rwkv_a2[layer], rwkv_g2[layer],
                     rwkv_k_k[layer], rwkv_k_a[layer], rwkv_r_k[layer].reshape(-1),
                     rwkv_gn_w[layer], rwkv_gn_b[layer], q_norm_w[layer], k_norm_w[layer],
                     attn_sinks[layer], proj_rwkv[layer], proj_attn[layer], w_out[layer],
                     norm2_w[layer], router_coarse_w[layer], router_coarse_b[layer],
                     router_fine_w[layer], router_fine_b[layer], expert_w_gate[layer],
                     expert_w_up[layer], expert_w_down[layer])
    return x2d.reshape(batch, seq, d)
```

```python
import functools

import jax
import jax.numpy as jnp
from jax import lax
from jax.experimental import pallas as pl
from jax.experimental.pallas import tpu as pltpu

F32 = jnp.float32
BF16 = jnp.bfloat16
I32 = jnp.int32

NORM_EPS = 1e-6
GN_EPS = 64e-5
HEAD_DIM = 64
LANES = 128
SUBLANES = 8
CHUNK = 64
WINDOW = 128
MOE_BLOCK = 128
SUPER_ROWS = 1024
FF_TILE = 512
TOP_K = 2
N_GROUPS = 8
GROUP_SIZE = 8
VMEM_LIMIT = 56 * 1024 * 1024

_NT = (((1,), (1,)), ((), ()))
_TN = (((0,), (0,)), ((), ()))


def _dot(a, b):
    return jnp.dot(a, b, preferred_element_type=F32)


def _split2(x):
    hi = x.astype(BF16)
    mid = (x - hi.astype(F32)).astype(BF16)
    return hi, mid


def _select_sum(m, hi, mid):
    return _dot(m, hi) + _dot(m, mid)


def _pack_bf16_halves(x):
    w = x.shape[1] // 2
    lo = lax.bitcast_convert_type(x[:, :w].astype(BF16).astype(F32), jnp.uint32)
    hi = lax.bitcast_convert_type(x[:, w:].astype(BF16).astype(F32), jnp.uint32)
    return (lo >> 16) | (hi & jnp.uint32(0xFFFF0000))


def _unpack_bf16_halves(xp):
    lo = lax.bitcast_convert_type(xp << 16, F32).astype(BF16)
    hi = lax.bitcast_convert_type(xp & jnp.uint32(0xFFFF0000), F32).astype(BF16)
    return lo, hi


def _cparams(sem, vmem=VMEM_LIMIT):
    return pltpu.CompilerParams(dimension_semantics=sem, vmem_limit_bytes=vmem)


def _norm_proj_body(x_ref, g_ref, w_ref, rwkv_ref, qkv_ref, gates_ref, h_ref, *, n_rwkv, n_qkv):
    j = pl.program_id(1)

    @pl.when(j == 0)
    def _():
        x = x_ref[...]
        ms = jnp.mean(x * x, axis=-1, keepdims=True)
        h_ref[...] = (x * lax.rsqrt(ms + NORM_EPS) * g_ref[...]).astype(BF16)

    acc = _dot(h_ref[...], w_ref[...])

    @pl.when(j < n_rwkv)
    def _():
        rwkv_ref[...] = acc

    @pl.when((j >= n_rwkv) & (j < n_rwkv + n_qkv))
    def _():
        qkv_ref[...] = acc.astype(qkv_ref.dtype)

    @pl.when(j >= n_rwkv + n_qkv)
    def _():
        gates_ref[...] = jax.nn.sigmoid(acc).astype(gates_ref.dtype)


def norm_proj(x2d, gain, w, widths, *, tm=1024, tn=512):
    n, d = x2d.shape
    tm = min(tm, n)
    n_rwkv, n_qkv, n_gates = (c // tn for c in widths)
    group = lambda first, count: (lambda i, j: (i, jnp.clip(j - first, 0, count - 1)))
    return pl.pallas_call(
        functools.partial(_norm_proj_body, n_rwkv=n_rwkv, n_qkv=n_qkv),
        grid=(n // tm, n_rwkv + n_qkv + n_gates),
        in_specs=[pl.BlockSpec((tm, d), lambda i, j: (i, 0)),
                  pl.BlockSpec((1, d), lambda i, j: (0, 0)),
                  pl.BlockSpec((d, tn), lambda i, j: (0, j))],
        out_specs=[pl.BlockSpec((tm, tn), group(0, n_rwkv)),
                   pl.BlockSpec((tm, tn), group(n_rwkv, n_qkv)),
                   pl.BlockSpec((tm, tn), group(n_rwkv + n_qkv, n_gates))],
        out_shape=[jax.ShapeDtypeStruct((n, widths[0]), F32),
                   jax.ShapeDtypeStruct((n, widths[1]), BF16),
                   jax.ShapeDtypeStruct((n, widths[2]), BF16)],
        scratch_shapes=[pltpu.VMEM((tm, d), BF16)],
        compiler_params=_cparams(("parallel", "arbitrary")),
        name="norm_proj",
    )(x2d, gain, w)


def _head_sum(x, e, et):
    s = _dot(x.astype(BF16), e)
    hi, mid = _split2(s)
    return _dot(hi, et) + _dot(mid, et)


def _rwkv_prep_body(p_ref, pprev_ref, mu_ref, w0_ref, w2_ref, a0_ref, a2_ref, g2_ref,
                    kk_ref, ka_ref, rk_ref, e_ref, et_ref, tri_ref, sel_ref,
                    rt_ref, at_ref, bt_ref, kt_ref, v_ref, g_ref, bonus_ref, pc_ref,
                    *, seq_len, width):
    tm = p_ref.shape[0]
    w_ = width
    first = (pl.program_id(0) * tm) % seq_len == 0
    p = p_ref[...]
    prev_row = jnp.where(first, 0.0, pprev_ref[7:8, :])
    row = lax.broadcasted_iota(I32, (tm, 1), 0)
    shifted = jnp.where(row == 0, prev_row, pltpu.roll(p, 1, 0))
    m = p + (shifted - p) * mu_ref[...]
    r = m[:, 0:w_]
    k = m[:, w_:2 * w_]
    v = m[:, 2 * w_:3 * w_]
    xw = m[:, 3 * w_:3 * w_ + 128]
    xa = m[:, 3 * w_ + 128:3 * w_ + 256]
    xg = m[:, 3 * w_ + 256:]

    z = -(w0_ref[...] + _dot(jnp.tanh(xw).astype(BF16), w2_ref[...]))
    softplus = jnp.maximum(z, 0.0) + jnp.log1p(jnp.exp(-jnp.abs(z)))
    logw = -jnp.exp(-softplus - 0.5)
    a = jax.nn.sigmoid(a0_ref[...] + _dot(xa.astype(BF16), a2_ref[...]))
    g = _dot(jax.nn.sigmoid(xg).astype(BF16), g2_ref[...])

    e = e_ref[...]
    et = et_ref[...]
    kk = k * kk_ref[...]
    kk = kk * jnp.minimum(lax.rsqrt(_head_sum(kk * kk, e, et)), 1e12)
    kmod = k * (1.0 + (a - 1.0) * ka_ref[...])
    bonus = _head_sum(r * kmod * rk_ref[...], e, et) * v

    lw_hi, lw_mid = _split2(logw)
    cum = _select_sum(tri_ref[...], lw_hi, lw_mid)
    pc_ref[...] = jnp.exp(_select_sum(sel_ref[...], lw_hi, lw_mid))
    inv = jnp.exp(-cum)
    rt_ref[...] = (r * jnp.exp(cum)).astype(BF16)
    at_ref[...] = (-kk * jnp.exp(cum - logw)).astype(BF16)
    bt_ref[...] = (kk * a * inv).astype(BF16)
    kt_ref[...] = (kmod * inv).astype(BF16)
    v_ref[...] = v.astype(BF16)
    g_ref[...] = g.astype(BF16)
    bonus_ref[...] = bonus


def _head_indicator(width):
    heads = width // HEAD_DIM
    c = jnp.arange(width)[:, None] // HEAD_DIM
    h = jnp.arange(LANES)[None, :]
    e = (c == h).astype(BF16)
    assert heads <= LANES
    return e, e.T


def rwkv_prep(p, mu, w0, w2, a0, a2, g2, k_k, k_a, r_k, *, seq_len, tm=512):
    n, cols = p.shape
    width = w0.shape[1]
    nchunk = tm // CHUNK
    e, et = _head_indicator(width)
    t = jnp.arange(tm)
    same = (t[:, None] // CHUNK) == (t[None, :] // CHUNK)
    tri = (same & (t[:, None] >= t[None, :])).astype(BF16)
    sel = ((t[None, :] // CHUNK) == jnp.arange(nchunk)[:, None]).astype(BF16)
    const = lambda shape: pl.BlockSpec(shape, lambda i: (0, 0))
    stream = lambda dt: jax.ShapeDtypeStruct((n, width), dt)
    outs = pl.pallas_call(
        functools.partial(_rwkv_prep_body, seq_len=seq_len, width=width),
        grid=(n // tm,),
        in_specs=[pl.BlockSpec((tm, cols), lambda i: (i, 0)),
                  pl.BlockSpec((8, cols), lambda i: (jnp.maximum(i * (tm // 8) - 1, 0), 0)),
                  const((1, cols)), const((1, width)), const(w2.shape), const((1, width)),
                  const(a2.shape), const(g2.shape), const((1, width)), const((1, width)),
                  const((1, width)), const(e.shape), const(et.shape), const(tri.shape),
                  const(sel.shape)],
        out_specs=[pl.BlockSpec((tm, width), lambda i: (i, 0))] * 7
                  + [pl.BlockSpec((nchunk, width), lambda i: (i, 0))],
        out_shape=[stream(BF16)] * 6 + [stream(F32),
                   jax.ShapeDtypeStruct((n // CHUNK, width), F32)],
        compiler_params=_cparams(("parallel",)),
        name="rwkv_prep",
    )(p, p, mu, w0, w2, a0, a2, g2, k_k, k_a, r_k, e, et, tri, sel)
    return outs


def _chunk_pairs(rts, ats, bts, kts, vvs, pcs, s_prevs, masks):
    lo, strict, incl = masks
    c = rts[0].shape[0]
    c2 = 2 * c
    zero = jnp.zeros_like(rts[0])
    each = lambda f, *ls: [f(*a) for a in zip(*ls)]

    def stack(x):
        return jnp.concatenate([jnp.where(lo, x, zero), jnp.where(lo, zero, x)], axis=0)

    ar = each(lambda a, r: jnp.concatenate([stack(a), stack(r)], axis=0), ats, rts)
    bk = each(lambda b, k: jnp.concatenate([stack(b), stack(k)], axis=0), bts, kts)
    vb = each(stack, vvs)
    gram = each(lambda x, y: lax.dot_general(x, y, _NT, preferred_element_type=F32), ar, bk)
    ars = each(lambda x, s: lax.dot_general(x, s.astype(BF16), _NT, preferred_element_type=F32),
               ar, s_prevs)
    a_ab = each(lambda g: jnp.where(strict, g[:c2, :c2], 0.0), gram)
    rhs = each(lambda g, v, x: x[:c2] + _dot(jnp.where(strict, g[:c2, c2:], 0.0).astype(BF16), v),
               gram, vb, ars)

    u = rhs
    nk = a_ab
    steps = (c - 1).bit_length()
    for k in range(steps):
        nkb = each(lambda n: n.astype(BF16), nk)
        if k + 1 < steps:
            prod = each(lambda n, uu: _dot(n, jnp.concatenate([n, uu.astype(BF16)], axis=1)),
                        nkb, u)
            nk = each(lambda p: p[:, :c2], prod)
            u = each(lambda uu, p: uu + p[:, c2:], u, prod)
        else:
            u = each(lambda uu, n: uu + _dot(n, uu.astype(BF16)), u, nkb)

    uv = each(lambda uu, v: jnp.concatenate([uu.astype(BF16), v], axis=0), u, vb)
    a_r = each(lambda g: jnp.where(incl, g[c2:, :], 0.0).astype(BF16), gram)
    ybd = each(lambda x, a, w: x[c2:] + _dot(a, w), ars, a_r, uv)
    ys = each(lambda yb: yb[:c] + yb[c:], ybd)

    bk_end = each(lambda x, pc: (x.astype(F32) * pc).astype(BF16), bk, pcs)
    s_new = each(lambda s, pc, w, x: s * pc + lax.dot_general(w, x, _TN, preferred_element_type=F32),
                 s_prevs, pcs, uv, bk_end)
    return ys, s_new


def _rwkv_scan_body(rt_ref, at_ref, bt_ref, kt_ref, v_ref, pc_ref, g_ref, bonus_ref,
                    gnw_ref, gnb_ref, e_ref, et_ref, o_ref, s_ref, y_ref):
    tr, width = rt_ref.shape
    npairs = width // LANES

    @pl.when(pl.program_id(1) == 0)
    def _():
        s_ref[...] = jnp.zeros_like(s_ref)

    lane = lax.broadcasted_iota(I32, (CHUNK, LANES), 1)
    lo = lane < HEAD_DIM
    ri = lax.broadcasted_iota(I32, (2 * CHUNK, 2 * CHUNK), 0)
    ci = lax.broadcasted_iota(I32, (2 * CHUNK, 2 * CHUNK), 1)
    same = (ri // CHUNK) == (ci // CHUNK)
    ri2 = lax.broadcasted_iota(I32, (2 * CHUNK, 4 * CHUNK), 0)
    ci2 = lax.broadcasted_iota(I32, (2 * CHUNK, 4 * CHUNK), 1) % (2 * CHUNK)
    incl = ((ri2 // CHUNK) == (ci2 // CHUNK)) & (ri2 >= ci2)
    masks = (lo, same & (ri > ci), incl)

    def chunk_step(c, carry):
        rows = pl.ds(pl.multiple_of(c * CHUNK, CHUNK), CHUNK)
        this_chunk = lax.broadcasted_iota(I32, (pc_ref.shape[0], 1), 0) == c
        cols = [slice(pr * LANES, (pr + 1) * LANES) for pr in range(npairs)]
        pcs = [jnp.sum(jnp.where(this_chunk, pc_ref[:, cl], 0.0), axis=0, keepdims=True)
               for cl in cols]
        load = lambda ref: [ref[rows, cl] for cl in cols]
        ys, s_new = _chunk_pairs(load(rt_ref), load(at_ref), load(bt_ref), load(kt_ref),
                                 load(v_ref), pcs, [s_ref[pr] for pr in range(npairs)], masks)
        for pr in range(npairs):
            s_ref[pr] = s_new[pr]
            y_ref[rows, cols[pr]] = ys[pr]
        return carry

    lax.fori_loop(0, tr // CHUNK, chunk_step, 0)

    e = e_ref[...]
    et = et_ref[...]
    y = y_ref[...]
    mean = _head_sum(y, e, et) * (1.0 / HEAD_DIM)
    d = y - mean
    var = _head_sum(d * d, e, et) * (1.0 / HEAD_DIM)
    out = d * lax.rsqrt(var + GN_EPS) * gnw_ref[...] + gnb_ref[...]
    out = (out + bonus_ref[...]) * g_ref[...].astype(F32)
    o_ref[...] = out.astype(o_ref.dtype)


def rwkv_scan(rt, at, bt, kt, vv, pc, g, bonus, gn_w, gn_b, *, batch, tr=512):
    n, width = rt.shape
    seq = n // batch
    nst = seq // tr
    e, et = _head_indicator(width)
    blk = lambda rows: pl.BlockSpec((rows, width), lambda b, s: (b * nst + s, 0))
    const = lambda shape: pl.BlockSpec(shape, lambda b, s: (0, 0))
    return pl.pallas_call(
        _rwkv_scan_body,
        grid=(batch, nst),
        in_specs=[blk(tr)] * 5 + [blk(tr // CHUNK), blk(tr), blk(tr),
                  const((1, width)), const((1, width)), const(e.shape), const(et.shape)],
        out_specs=blk(tr),
        out_shape=jax.ShapeDtypeStruct((n, width), BF16),
        scratch_shapes=[pltpu.VMEM((width // LANES, LANES, LANES), F32),
                        pltpu.VMEM((tr, width), F32)],
        compiler_params=_cparams(("parallel", "arbitrary")),
        name="rwkv_scan",
    )(rt, at, bt, kt, vv, pc, g, bonus, gn_w, gn_b, e, et)


def _head_rmsnorm(x, gain, bd):
    hi, mid = _split2(x * x)
    ms = (_dot(hi, bd) + _dot(mid, bd)) * (1.0 / HEAD_DIM)
    return x * lax.rsqrt(ms + NORM_EPS) * gain


def _swa_body(sink_ref, q_ref, kp_ref, kc_ref, vp_ref, vc_ref, qg_ref, kg_ref, bd_ref, o_ref,
              *, group):
    blk = q_ref.shape[0]
    n = pl.program_id(1)
    bd = bd_ref[...]
    scale = HEAD_DIM ** -0.5
    lane = lax.broadcasted_iota(I32, (1, LANES), 1)
    lo = lane < HEAD_DIM
    ri = lax.broadcasted_iota(I32, (2 * blk, 2 * blk), 0)
    cj = lax.broadcasted_iota(I32, (2 * blk, 2 * blk), 1)
    rel = blk + (ri % blk) - cj
    jmin = jnp.where(n > 0, 0, blk)
    valid = (rel >= 0) & (rel < WINDOW) & (cj >= jmin)
    top_rows = lax.broadcasted_iota(I32, (2 * blk, 1), 0) < blk

    kcat = jnp.concatenate([kp_ref[...], kc_ref[...]], axis=0).astype(F32)
    vcat = jnp.concatenate([vp_ref[...], vc_ref[...]], axis=0).astype(F32)
    kv_heads = kcat.shape[1] // HEAD_DIM
    q_tiles = q_ref.shape[1] // LANES
    tiles_per_kv = group // 2
    each = lambda f, *ls: [f(*a) for a in zip(*ls)]

    kv_cols = [slice(jt * LANES, (jt + 1) * LANES) for jt in range(kv_heads // 2)]
    kn = [_head_rmsnorm(kcat[:, c], kg_ref[...], bd) for c in kv_cols]
    kn_r = [pltpu.roll(x, HEAD_DIM, 1) for x in kn]
    vt = [vcat[:, c] for c in kv_cols]
    vt_r = [pltpu.roll(x, HEAD_DIM, 1) for x in vt]
    own = lambda hk: lo if hk % 2 == 0 else jnp.logical_not(lo)
    k2 = [jnp.where(own(hk), kn[hk // 2], kn_r[hk // 2]).astype(BF16) for hk in range(kv_heads)]
    v2 = [jnp.where(own(hk), vt[hk // 2], vt_r[hk // 2]).astype(BF16) for hk in range(kv_heads)]

    q_cols = [slice(t * LANES, (t + 1) * LANES) for t in range(q_tiles)]
    qn = [_head_rmsnorm(q_ref[:, c].astype(F32), qg_ref[...] * scale, bd) for c in q_cols]
    qst = [jnp.concatenate([jnp.where(lo, x, 0.0), jnp.where(lo, 0.0, x)], axis=0).astype(BF16)
           for x in qn]
    s = [jnp.where(valid, lax.dot_general(x, k2[t // tiles_per_kv], _NT,
                                          preferred_element_type=F32), -jnp.inf)
         for t, x in enumerate(qst)]
    sink = [jnp.where(top_rows, sink_ref[2 * t], sink_ref[2 * t + 1]) for t in range(q_tiles)]
    mx = each(lambda x, sk: jnp.maximum(jnp.max(x, axis=-1, keepdims=True), sk), s, sink)
    pr = each(lambda x, m: jnp.exp(x - m), s, mx)
    inv = each(lambda p, sk, m: 1.0 / (jnp.sum(p, axis=-1, keepdims=True) + jnp.exp(sk - m)),
               pr, sink, mx)
    o = [_dot((p * r).astype(BF16), v2[t // tiles_per_kv])
         for t, (p, r) in enumerate(zip(pr, inv))]
    for t in range(q_tiles):
        o_ref[:, q_cols[t]] = jnp.where(lo, o[t][:blk], o[t][blk:]).astype(o_ref.dtype)


def swa_attention(qkv, q_gain, k_gain, sinks, *, batch, q_width, kv_width):
    n = qkv.shape[0]
    blk = WINDOW
    nb = n // batch // blk
    group = (q_width // HEAD_DIM) // (kv_width // HEAD_DIM)
    kcol = q_width // kv_width
    t = jnp.arange(LANES)
    bd = ((t[:, None] // HEAD_DIM) == (t[None, :] // HEAD_DIM)).astype(BF16)
    qg = jnp.tile(q_gain.reshape(1, HEAD_DIM), (1, 2))
    kg = jnp.tile(k_gain.reshape(1, HEAD_DIM), (1, 2))
    cur = lambda col: (lambda b, i, *_: (b * nb + i, col))
    prev = lambda col: (lambda b, i, *_: (b * nb + jnp.maximum(i - 1, 0), col))
    const = lambda shape: pl.BlockSpec(shape, lambda b, i, *_: (0, 0))
    grid_spec = pltpu.PrefetchScalarGridSpec(
        num_scalar_prefetch=1,
        grid=(batch, nb),
        in_specs=[pl.BlockSpec((blk, q_width), cur(0)),
                  pl.BlockSpec((blk, kv_width), prev(kcol)),
                  pl.BlockSpec((blk, kv_width), cur(kcol)),
                  pl.BlockSpec((blk, kv_width), prev(kcol + 1)),
                  pl.BlockSpec((blk, kv_width), cur(kcol + 1)),
                  const((1, LANES)), const((1, LANES)), const((LANES, LANES))],
        out_specs=pl.BlockSpec((blk, q_width), cur(0)),
    )
    return pl.pallas_call(
        functools.partial(_swa_body, group=group),
        grid_spec=grid_spec,
        out_shape=jax.ShapeDtypeStruct((n, q_width), BF16),
        compiler_params=_cparams(("parallel", "arbitrary")),
        name="swa_attn",
    )(sinks, qkv, qkv, qkv, qkv, qkv, qg, kg, bd)


def _merge_route_body(a_ref, b_ref, ga_ref, gb_ref, x_ref, pr_ref, pa_ref, wo_ref, n2_ref,
                      rhm_ref, rh_ref, rb_ref, x1_ref, h2_ref, route_ref, logits_ref):
    @pl.when(pl.program_id(0) == 0)
    def _():
        logits_ref[...] = jnp.zeros_like(logits_ref)

    _route(logits_ref[...], route_ref)

    merged = (ga_ref[...].astype(F32) * _dot(a_ref[...], pr_ref[...])
              + gb_ref[...].astype(F32) * _dot(b_ref[...], pa_ref[...]))
    x1 = x_ref[...] + _dot(merged.astype(BF16), wo_ref[...])
    x1_ref[...] = x1
    ms = jnp.mean(x1 * x1, axis=-1, keepdims=True)
    h2 = x1 * lax.rsqrt(ms + NORM_EPS) * n2_ref[...]
    h2_ref[...] = _pack_bf16_halves(h2)

    hi, mid = _split2(h2)
    hw = _dot(hi, rhm_ref[...])
    logits_ref[...] = hw[:, :LANES] + (hw[:, LANES:] + _dot(mid, rh_ref[...])) + rb_ref[...]


def _route(logits, route_ref):
    lane = lax.broadcasted_iota(I32, logits.shape, 1)
    big = jnp.int32(1 << 20)
    neg = -jnp.inf
    is_coarse = lane < N_GROUPS
    cl = jnp.where(is_coarse, logits, neg)
    ce = jnp.exp(cl - jnp.max(cl, axis=-1, keepdims=True))
    cp = ce / jnp.sum(ce, axis=-1, keepdims=True)
    g_prob = jnp.max(cp, axis=-1, keepdims=True)
    g_idx = jnp.min(jnp.where(is_coarse & (cp == g_prob), lane, big), axis=-1, keepdims=True)

    fine_lane = lane - N_GROUPS
    in_group = ((lane >= N_GROUPS) & (lane < N_GROUPS + N_GROUPS * GROUP_SIZE)
                & ((fine_lane // GROUP_SIZE) == g_idx))
    fl = jnp.where(in_group, logits, neg)
    fe = jnp.exp(fl - jnp.max(fl, axis=-1, keepdims=True))
    fp = fe / jnp.sum(fe, axis=-1, keepdims=True)
    p1 = jnp.max(jnp.where(in_group, fp, -1.0), axis=-1, keepdims=True)
    i1 = jnp.min(jnp.where(in_group & (fp == p1), lane, big), axis=-1, keepdims=True)
    rest = in_group & (lane != i1)
    p2 = jnp.max(jnp.where(rest, fp, -1.0), axis=-1, keepdims=True)
    i2 = jnp.min(jnp.where(rest & (fp == p2), lane, big), axis=-1, keepdims=True)
    den = p1 + p2
    w1 = g_prob * p1 / den
    w2 = g_prob * p2 / den
    e1 = (i1 - N_GROUPS).astype(F32)
    e2 = (i2 - N_GROUPS).astype(F32)
    route_ref[...] = jnp.where(lane == 0, w1, jnp.where(lane == 1, w2,
                     jnp.where(lane == 2, e1, jnp.where(lane == 3, e2, 0.0))))


def merge_route(a_out, b_out, gates, x2d, proj_r, proj_a, w_out, norm2_w, router_w, router_b,
                *, tm=256):
    n, d = x2d.shape
    wa = a_out.shape[1]
    rh, rm = _split2(router_w)
    rhm = jnp.concatenate([rh, rm], axis=1)
    nt = n // tm
    const = lambda arr: pl.BlockSpec(arr.shape, lambda i: (0, 0), pipeline_mode=pl.Buffered(1))
    row = lambda w, col=0: pl.BlockSpec((tm, w), lambda i: (jnp.minimum(i, nt - 1), col))
    return pl.pallas_call(
        _merge_route_body,
        grid=(nt + 1,),
        in_specs=[row(wa), row(wa), row(d), row(d, 1), row(d),
                  const(proj_r), const(proj_a), const(w_out), const(norm2_w),
                  const(rhm), const(rh), const(router_b)],
        out_specs=[row(d), row(d // 2),
                   pl.BlockSpec((tm, LANES), lambda i: (jnp.maximum(i - 1, 0), 0))],
        scratch_shapes=[pltpu.VMEM((tm, LANES), F32)],
        out_shape=[jax.ShapeDtypeStruct((n, d), F32), jax.ShapeDtypeStruct((n, d // 2), jnp.uint32),
                   jax.ShapeDtypeStruct((n, LANES), F32)],
        compiler_params=_cparams(("arbitrary",)),
        name="merge_route",
    )(a_out, b_out, gates, gates, x2d, proj_r, proj_a, w_out, norm2_w, rhm, rh, router_b)


def _moe_rank_body(ids_ref, ut_ref, rank_ref, counts_ref, carry_ref):
    @pl.when(pl.program_id(0) == 0)
    def _():
        carry_ref[...] = jnp.zeros_like(carry_ref)

    ids = ids_ref[...]
    n_exp = carry_ref.shape[0]
    sub = lax.broadcasted_iota(I32, (n_exp, ids.shape[1]), 0)
    onehot = (sub == ids).astype(F32)
    before = _dot(onehot.astype(BF16), ut_ref[...])
    carry = carry_ref[...]
    rank_ref[...] = jnp.sum(onehot * (before + carry), axis=0, keepdims=True).astype(I32)
    carry = carry + jnp.sum(onehot, axis=1, keepdims=True)
    carry_ref[...] = carry
    counts_ref[...] = jnp.broadcast_to(carry, counts_ref.shape)


def moe_rank(ids_row, n_experts, *, tb=512):
    m = ids_row.shape[1]
    t = jnp.arange(tb)
    ut = (t[:, None] < t[None, :]).astype(BF16)
    return pl.pallas_call(
        _moe_rank_body,
        grid=(m // tb,),
        in_specs=[pl.BlockSpec((1, tb), lambda i: (0, i)),
                  pl.BlockSpec((tb, tb), lambda i: (0, 0))],
        out_specs=[pl.BlockSpec((1, tb), lambda i: (0, i)),
                   pl.BlockSpec((n_experts, LANES), lambda i: (0, 0))],
        out_shape=[jax.ShapeDtypeStruct((1, m), I32),
                   jax.ShapeDtypeStruct((n_experts, LANES), F32)],
        scratch_shapes=[pltpu.VMEM((n_experts, 1), F32)],
        compiler_params=_cparams(("arbitrary",)),
        name="moe_rank",
    )(ids_row, ut)


def _moe_dest_body(ids_ref, rank_ref, start_ref, dest_ref):
    ids = ids_ref[...]
    sub = lax.broadcasted_iota(I32, (start_ref.shape[0], ids.shape[1]), 0)
    start = jnp.sum(jnp.where(sub == ids, start_ref[...], 0.0), axis=0, keepdims=True)
    dest_ref[...] = start.astype(I32) + rank_ref[...]


def moe_dest(ids_row, rank_row, pad_start_col, *, tb=512):
    m = ids_row.shape[1]
    row = pl.BlockSpec((1, tb), lambda i: (0, i))
    return pl.pallas_call(
        _moe_dest_body,
        grid=(m // tb,),
        in_specs=[row, row, pl.BlockSpec(pad_start_col.shape, lambda i: (0, 0))],
        out_specs=row,
        out_shape=jax.ShapeDtypeStruct((1, m), I32),
        compiler_params=_cparams(("parallel",)),
        name="moe_dest",
    )(ids_row, rank_row, pad_start_col)


def _moe_dispatch_body(dest_ref, start_ref, end_ref, h_ref, buf_ref, zeros_ref, sem, zsem):
    tm = h_ref.shape[0] * h_ref.shape[1]
    base = pl.program_id(0) * tm * TOP_K

    @pl.when(pl.program_id(0) == 0)
    def _():
        zeros_ref[...] = jnp.zeros_like(zeros_ref)

        def zero_last_block(e):
            last = pl.multiple_of(end_ref[e] - MOE_BLOCK, MOE_BLOCK)
            return pltpu.make_async_copy(zeros_ref, buf_ref.at[pl.ds(last, MOE_BLOCK)], zsem)

        def zstart(e, c):
            @pl.when(end_ref[e] > start_ref[e])
            def _():
                zero_last_block(e).start()
            return c

        def zwait(e, c):
            @pl.when(end_ref[e] > start_ref[e])
            def _():
                zero_last_block(e).wait()
            return c

        lax.fori_loop(0, start_ref.shape[0], zstart, 0)
        lax.fori_loop(0, start_ref.shape[0], zwait, 0)

        tail0 = end_ref[end_ref.shape[0] - 1]

        def zero_tail_block(r):
            first = pl.multiple_of(tail0 + r * MOE_BLOCK, MOE_BLOCK)
            return pltpu.make_async_copy(zeros_ref, buf_ref.at[pl.ds(first, MOE_BLOCK)], zsem)

        def tstart(r, c):
            zero_tail_block(r).start()
            return c

        def twait(r, c):
            zero_tail_block(r).wait()
            return c

        n_tail = (buf_ref.shape[0] - tail0) // MOE_BLOCK
        lax.fori_loop(0, n_tail, tstart, 0)
        lax.fori_loop(0, n_tail, twait, 0)

    def for_rows(fn):
        def body(q, c):
            for u in range(SUBLANES):
                for k in range(TOP_K):
                    slot = dest_ref[base + (q * SUBLANES + u) * TOP_K + k]
                    fn(pltpu.make_async_copy(h_ref.at[q, pl.ds(u, 1)], buf_ref.at[pl.ds(slot, 1)],
                                             sem))
            return c
        lax.fori_loop(0, h_ref.shape[0], body, 0)

    for_rows(lambda cp: cp.start())
    for_rows(lambda cp: cp.wait())


def moe_dispatch(dest, pad_start, pad_end, h2, n_slots, *, tm=256):
    n, d = h2.shape
    h2 = h2.reshape(n // SUBLANES, SUBLANES, d)
    grid_spec = pltpu.PrefetchScalarGridSpec(
        num_scalar_prefetch=3,
        grid=(n // tm,),
        in_specs=[pl.BlockSpec((tm // SUBLANES, SUBLANES, d), lambda i, *_: (i, 0, 0))],
        out_specs=pl.BlockSpec(memory_space=pl.ANY),
        scratch_shapes=[pltpu.VMEM((MOE_BLOCK, d), h2.dtype),
                        pltpu.SemaphoreType.DMA(()), pltpu.SemaphoreType.DMA(())],
    )
    return pl.pallas_call(
        _moe_dispatch_body,
        grid_spec=grid_spec,
        out_shape=jax.ShapeDtypeStruct((n_slots, d), h2.dtype),
        compiler_params=_cparams(("arbitrary",)),
        name="moe_dispatch",
    )(dest, pad_start, pad_end, h2)


def _moe_expert_body(sbe_ref, row0_ref, nb_ref, tot_ref, wg_ref, wu_ref, wd_ref, buf_ref, out_ref,
                     wgb, wub, wdb, xp, acc, pending, sem_in, sem_out):
    del sbe_ref
    s = pl.program_id(0)
    f = pl.program_id(1)
    n_sb = pl.num_programs(0)
    last_f = pl.num_programs(1) - 1
    nb = nb_ref[s]
    half = xp.shape[2]

    def blk(first_row, r):
        return pl.ds(pl.multiple_of(first_row + r * MOE_BLOCK, MOE_BLOCK), MOE_BLOCK)

    def rows_in(sb, r):
        return pltpu.make_async_copy(buf_ref.at[blk(row0_ref[sb], r)], xp.at[sb % 2, blk(0, r)],
                                     sem_in.at[sb % 2])

    def rows_out(r):
        return pltpu.make_async_copy(acc.at[blk(0, r)], out_ref.at[blk(row0_ref[s], r)], sem_out)

    def for_blocks(n, fn):
        def body(r, carry):
            fn(r)
            return carry
        lax.fori_loop(0, n, body, 0)

    def drain_out():
        for_blocks(pending[0], lambda r: rows_out(0).wait())
        pending[0] = 0

    def mlp_rows(first_row, n_rows, first):
        rows = pl.ds(pl.multiple_of(first_row, MOE_BLOCK), n_rows)
        xlo, xhi = _unpack_bf16_halves(xp[s % 2, rows, :])
        hg = _dot(xlo, wgb[:half, :]) + _dot(xhi, wgb[half:, :])
        hu = _dot(xlo, wub[:half, :]) + _dot(xhi, wub[half:, :])
        y = _dot((jax.nn.silu(hg) * hu).astype(BF16), wdb[...])
        if first:
            acc[rows, :] = y
        else:
            acc[rows, :] += y

    def all_rows(first):
        for_blocks(nb // 2, lambda q: mlp_rows(q * (2 * MOE_BLOCK), 2 * MOE_BLOCK, first))

        @pl.when(nb % 2 == 1)
        def _():
            mlp_rows((nb - 1) * MOE_BLOCK, MOE_BLOCK, first)

    @pl.when((s == 0) & (f == 0))
    def _():
        pending[0] = 0
        for_blocks(nb, lambda r: rows_in(0, r).start())

    @pl.when(nb > 0)
    def _():
        @pl.when(f == 0)
        def _():
            nxt = jnp.minimum(s + 1, n_sb - 1)
            nb_next = jnp.where(s + 1 < n_sb, nb_ref[nxt], 0)
            for_blocks(nb_next, lambda r: rows_in(nxt, r).start())

        wgb[...] = wg_ref[0].astype(BF16)
        wub[...] = wu_ref[0].astype(BF16)
        wdb[...] = wd_ref[0].astype(BF16)

        @pl.when(f == 0)
        def _():
            for_blocks(nb, lambda r: rows_in(s, r).wait())
            drain_out()
            all_rows(True)

        @pl.when(f > 0)
        def _():
            all_rows(False)

        @pl.when(f == last_f)
        def _():
            for_blocks(nb, lambda r: rows_out(r).start())
            pending[0] = nb

    @pl.when((s == n_sb - 1) & (f == last_f))
    def _():
        drain_out()
        acc[pl.ds(0, MOE_BLOCK), :] = jnp.zeros((MOE_BLOCK, acc.shape[1]), acc.dtype)
        tail0 = tot_ref[0]

        def zero_out(r):
            return pltpu.make_async_copy(acc.at[pl.ds(0, MOE_BLOCK)], out_ref.at[blk(tail0, r)],
                                         sem_out)

        ntail = (out_ref.shape[0] - tail0) // MOE_BLOCK
        for_blocks(ntail, lambda r: zero_out(r).start())
        for_blocks(ntail, lambda r: zero_out(r).wait())


def moe_experts(sb_expert, sb_row0, sb_blocks, total_rows, buf, wg, wu, wd):
    p, half = buf.shape
    d = 2 * half
    ff = wg.shape[2]
    nf = ff // FF_TILE
    f_eff = lambda s, f, nb: jnp.where(nb[s] > 0, f, nf - 1)
    grid_spec = pltpu.PrefetchScalarGridSpec(
        num_scalar_prefetch=4,
        grid=(sb_expert.shape[0], nf),
        in_specs=[pl.BlockSpec((1, d, FF_TILE), lambda s, f, e, r0, nb, t: (e[s], 0, f_eff(s, f, nb))),
                  pl.BlockSpec((1, d, FF_TILE), lambda s, f, e, r0, nb, t: (e[s], 0, f_eff(s, f, nb))),
                  pl.BlockSpec((1, FF_TILE, d), lambda s, f, e, r0, nb, t: (e[s], f_eff(s, f, nb), 0)),
                  pl.BlockSpec(memory_space=pl.ANY)],
        out_specs=pl.BlockSpec(memory_space=pl.ANY),
        scratch_shapes=[pltpu.VMEM((d, FF_TILE), BF16), pltpu.VMEM((d, FF_TILE), BF16),
                        pltpu.VMEM((FF_TILE, d), BF16), pltpu.VMEM((2, SUPER_ROWS, half), jnp.uint32),
                        pltpu.VMEM((SUPER_ROWS, d), F32), pltpu.SMEM((1,), I32),
                        pltpu.SemaphoreType.DMA((2,)), pltpu.SemaphoreType.DMA(())],
    )
    return pl.pallas_call(
        _moe_expert_body,
        grid_spec=grid_spec,
        out_shape=jax.ShapeDtypeStruct((p, d), F32),
        compiler_params=_cparams(("arbitrary", "arbitrary")),
        name="moe_experts",
    )(sb_expert, sb_row0, sb_blocks, total_rows, wg, wu, wd, buf)


def _moe_combine_body(dest_ref, x1_ref, route_ref, out_ref, y_ref, rows_ref, sem):
    tm, d = x1_ref.shape
    i = pl.program_id(0)
    sublanes = rows_ref.shape[3]

    def row_copy(tile, q, u, k):
        slot = dest_ref[(tile * tm + q * sublanes + u) * TOP_K + k]
        return pltpu.make_async_copy(out_ref.at[pl.ds(slot, 1)],
                                     rows_ref.at[tile % 2, k, q, pl.ds(u, 1)], sem.at[tile % 2])

    def for_rows(tile, fn):
        def body(q, c):
            for u in range(sublanes):
                for k in range(TOP_K):
                    fn(row_copy(tile, q, u, k))
            return c
        lax.fori_loop(0, tm // sublanes, body, 0)

    @pl.when(i == 0)
    def _():
        for_rows(0, lambda cp: cp.start())

    @pl.when(i + 1 < pl.num_programs(0))
    def _():
        for_rows(i + 1, lambda cp: cp.start())

    for_rows(i, lambda cp: cp.wait())
    route = route_ref[...]
    rows = rows_ref[i % 2]
    y_ref[...] = x1_ref[...] + (rows[0].reshape(tm, d) * route[:, 0:1]
                                + rows[1].reshape(tm, d) * route[:, 1:2])


def moe_combine(dest, x1, route, expert_out, *, tm=256):
    n, d = x1.shape
    grid_spec = pltpu.PrefetchScalarGridSpec(
        num_scalar_prefetch=1,
        grid=(n // tm,),
        in_specs=[pl.BlockSpec((tm, d), lambda i, *_: (i, 0)),
                  pl.BlockSpec((tm, LANES), lambda i, *_: (i, 0)),
                  pl.BlockSpec(memory_space=pl.ANY)],
        out_specs=pl.BlockSpec((tm, d), lambda i, *_: (i, 0)),
        scratch_shapes=[pltpu.VMEM((2, TOP_K, tm // SUBLANES, SUBLANES, d), F32),
                        pltpu.SemaphoreType.DMA((2,))],
    )
    return pl.pallas_call(
        _moe_combine_body,
        grid_spec=grid_spec,
        out_shape=jax.ShapeDtypeStruct((n, d), F32),
        compiler_params=_cparams(("arbitrary",)),
        name="moe_combine",
    )(dest, x1, route, expert_out)


def hierarchical_moe(x1, h2, route, wg, wu, wd):
    n = h2.shape[0]
    n_experts = wg.shape[0]
    m = n * TOP_K
    ids_row = route[:, 2:4].astype(I32).reshape(1, m)
    rank_row, counts = moe_rank(ids_row, n_experts)
    counts = counts[:, 0].astype(I32)
    padded = (counts + MOE_BLOCK - 1) // MOE_BLOCK * MOE_BLOCK
    pad_end = jnp.cumsum(padded)
    pad_start = pad_end - padded
    dest = moe_dest(ids_row, rank_row, pad_start.astype(F32).reshape(n_experts, 1)).reshape(m)
    n_slots = m + n_experts * MOE_BLOCK
    n_sb = (padded + SUPER_ROWS - 1) // SUPER_ROWS
    sb_end = jnp.cumsum(n_sb)
    sb_start = sb_end - n_sb
    s_idx = jnp.arange(n_experts + m // SUPER_ROWS, dtype=I32)
    used = s_idx < sb_end[-1]
    e_of = jnp.minimum(jnp.sum(sb_end[None, :] <= s_idx[:, None], axis=1), n_experts - 1)
    piece = s_idx - sb_start[e_of]
    sb_row0 = jnp.where(used, pad_start[e_of] + piece * SUPER_ROWS, 0).astype(I32)
    sb_rows = jnp.clip(padded[e_of] - piece * SUPER_ROWS, 0, SUPER_ROWS)
    sb_blocks = jnp.where(used, sb_rows // MOE_BLOCK, 0).astype(I32)
    sb_expert = jnp.where(used, e_of, e_of[jnp.maximum(sb_end[-1] - 1, 0)]).astype(I32)
    buf = moe_dispatch(dest, pad_start.astype(I32), pad_end.astype(I32), h2, n_slots)
    expert_out = moe_experts(sb_expert, sb_row0, sb_blocks, pad_end[-1:].astype(I32), buf,
                             wg, wu, wd)
    return moe_combine(dest, x1, route, expert_out)


def _pad_cols(w, to):
    return jnp.pad(w, ((0, 0), (0, to - w.shape[1])))


def _pad_rows(w, to):
    return jnp.pad(w, ((0, to - w.shape[0]), (0, 0)))


def _layer(x2d, batch, norm1_w, w_in, mu, w0, w2, a0, a2, g2, k_k, k_a, r_k, gn_w, gn_b,
           q_norm_w, k_norm_w, sinks, proj_rwkv, proj_attn, w_out, norm2_w, wc, bc, wf, bf,
           wg, wu, wd):
    n, d = x2d.shape
    seq = n // batch
    width = w0.shape[0]
    dl, il, gl = w2.shape[0], a2.shape[0], g2.shape[0]
    q_width = proj_attn.shape[0]
    rwkv_cols = 3 * width + dl + il + gl
    kv_width = (w_in.shape[1] - rwkv_cols - q_width - 2 * d) // 2
    row = lambda v: v.reshape(1, -1).astype(F32)

    c0 = 3 * width
    w_all = jnp.concatenate([w_in[:, :c0], _pad_cols(w_in[:, c0:c0 + dl], LANES),
                             _pad_cols(w_in[:, c0 + dl:c0 + dl + il], LANES),
                             w_in[:, c0 + dl + il:]], axis=1).astype(BF16)
    mu_p = jnp.concatenate([mu[:c0], jnp.pad(mu[c0:c0 + dl], (0, LANES - dl)),
                            jnp.pad(mu[c0 + dl:c0 + dl + il], (0, LANES - il)),
                            mu[c0 + dl + il:]]).reshape(1, -1)
    p_rwkv, qkv, gates = norm_proj(x2d, row(norm1_w), w_all,
                                   (c0 + 2 * LANES + gl, q_width + 2 * kv_width, 2 * d))

    rt, at, bt, kt, vv, g, bonus, pc = rwkv_prep(
        p_rwkv, mu_p, row(w0), _pad_rows(w2, LANES).astype(BF16), row(a0),
        _pad_rows(a2, LANES).astype(BF16), g2.astype(BF16), row(k_k), row(k_a), row(r_k),
        seq_len=seq)
    a_out = rwkv_scan(rt, at, bt, kt, vv, pc, g, bonus, row(gn_w), row(gn_b), batch=batch)

    b_out = swa_attention(qkv, q_norm_w, k_norm_w, sinks.astype(F32), batch=batch,
                          q_width=q_width, kv_width=kv_width)

    n_groups, n_experts = wc.shape[1], wf.shape[1]
    router_w = _pad_cols(jnp.concatenate([wc, wf], axis=1), LANES)
    router_b = _pad_cols(jnp.concatenate([bc, bf]).reshape(1, -1), LANES)
    assert n_groups == N_GROUPS and n_experts == N_GROUPS * GROUP_SIZE
    x1, h2, route = merge_route(a_out, b_out, gates, x2d, proj_rwkv.astype(BF16),
                                proj_attn.astype(BF16), w_out.astype(BF16), row(norm2_w),
                                router_w, router_b)
    return hierarchical_moe(x1, h2, route, wg, wu, wd)


def kernel(x, norm1_w, w_in, rwkv_mu, rwkv_w0, rwkv_w2, rwkv_a0, rwkv_a2, rwkv_g2, rwkv_k_k,
           rwkv_k_a, rwkv_r_k, rwkv_gn_w, rwkv_gn_b, q_norm_w, k_norm_w, attn_sinks, proj_rwkv,
           proj_attn, w_out, norm2_w, router_coarse_w, router_coarse_b, router_fine_w,
           router_fine_b, expert_w_gate, expert_w_up, expert_w_down):
    batch, seq, d = x.shape
    x2d = x.reshape(batch * seq, d)
    for layer in range(norm1_w.shape[0]):
        x2d = _layer(x2d, batch, norm1_w[layer], w_in[layer], rwkv_mu[layer], rwkv_w0[layer],
                     rwkv_w2[layer], rwkv_a0[layer], rwkv_a2[layer], rwkv_g2[layer],
                     rwkv_k_k[layer], rwkv_k_a[layer], rwkv_r_k[layer].reshape(-1),
                     rwkv_gn_w[layer], rwkv_gn_b[layer], q_norm_w[layer], k_norm_w[layer],
                     attn_sinks[layer], proj_rwkv[layer], proj_attn[layer], w_out[layer],
                     norm2_w[layer], router_coarse_w[layer], router_coarse_b[layer],
                     router_fine_w[layer], router_fine_b[layer], expert_w_gate[layer],
                     expert_w_up[layer], expert_w_down[layer])
    return x2d.reshape(batch, seq, d)
```

```python
import functools
import math

import jax
import jax.numpy as jnp
from jax import lax
from jax.experimental import pallas as pl
from jax.experimental.pallas import tpu as pltpu

F32 = jnp.float32
BF16 = jnp.bfloat16
I32 = jnp.int32

NORM_EPS = 1e-6
GN_EPS = 64e-5
HEAD_DIM = 64
LANES = 128
SUBLANES = 8
CHUNK = 64
WINDOW = 128
MOE_BLOCK = 128
SUPER_ROWS = 1024
FF_TILE = 512
TOP_K = 2
N_GROUPS = 8
GROUP_SIZE = 8
VMEM_LIMIT = 56 * 1024 * 1024

_NT = (((1,), (1,)), ((), ()))
_TN = (((0,), (0,)), ((), ()))


def _dot(a, b):
    return jnp.dot(a, b, preferred_element_type=F32)


def _split2(x):
    hi = x.astype(BF16)
    mid = (x - hi.astype(F32)).astype(BF16)
    return hi, mid


def _select_sum(m, hi, mid):
    return _dot(m, hi) + _dot(m, mid)


def _pack_bf16_halves(x):
    w = x.shape[1] // 2
    lo = lax.bitcast_convert_type(x[:, :w].astype(BF16).astype(F32), jnp.uint32)
    hi = lax.bitcast_convert_type(x[:, w:].astype(BF16).astype(F32), jnp.uint32)
    return (lo >> 16) | (hi & jnp.uint32(0xFFFF0000))


def _unpack_bf16_halves(xp):
    lo = lax.bitcast_convert_type(xp << 16, F32).astype(BF16)
    hi = lax.bitcast_convert_type(xp & jnp.uint32(0xFFFF0000), F32).astype(BF16)
    return lo, hi


def _cparams(sem, vmem=VMEM_LIMIT):
    return pltpu.CompilerParams(dimension_semantics=sem, vmem_limit_bytes=vmem)


def _norm_proj_body(x_ref, g_ref, w_ref, o_ref, h_ref, *, first_gate_tile):
    j = pl.program_id(1)

    @pl.when(j == 0)
    def _():
        x = x_ref[...]
        ms = jnp.mean(x * x, axis=-1, keepdims=True)
        h_ref[...] = (x * lax.rsqrt(ms + NORM_EPS) * g_ref[...]).astype(BF16)

    acc = _dot(h_ref[...], w_ref[...])

    @pl.when(j < first_gate_tile)
    def _():
        o_ref[...] = acc.astype(o_ref.dtype)

    @pl.when(j >= first_gate_tile)
    def _():
        o_ref[...] = jax.nn.sigmoid(acc).astype(o_ref.dtype)


def norm_proj(x2d, gain, w, gate_cols, *, tm=1024, tn=1024):
    n, d = x2d.shape
    c = w.shape[1]
    tm = min(tm, n)
    assert c % tn == 0 and gate_cols % tn == 0
    return pl.pallas_call(
        functools.partial(_norm_proj_body, first_gate_tile=(c - gate_cols) // tn),
        grid=(n // tm, c // tn),
        in_specs=[pl.BlockSpec((tm, d), lambda i, j: (i, 0)),
                  pl.BlockSpec((1, d), lambda i, j: (0, 0)),
                  pl.BlockSpec((d, tn), lambda i, j: (0, j))],
        out_specs=pl.BlockSpec((tm, tn), lambda i, j: (i, j)),
        out_shape=jax.ShapeDtypeStruct((n, c), BF16),
        scratch_shapes=[pltpu.VMEM((tm, d), BF16)],
        compiler_params=_cparams(("parallel", "arbitrary")),
        name="norm_proj",
    )(x2d, gain, w)


def _head_sum(x, e, et):
    s = _dot(x.astype(BF16), e)
    hi, mid = _split2(s)
    return _dot(hi, et) + _dot(mid, et)


def _rwkv_prep_body(p_ref, pprev_ref, mu_ref, w0_ref, w2_ref, a0_ref, a2_ref, g2_ref,
                    kk_ref, ka_ref, rk_ref, e_ref, et_ref, tri_ref, sel_ref,
                    rt_ref, at_ref, bt_ref, kt_ref, v_ref, g_ref, bonus_ref, pc_ref,
                    *, seq_len, width):
    tm = p_ref.shape[0]
    w_ = width
    first = (pl.program_id(0) * tm) % seq_len == 0
    p = p_ref[...].astype(F32)
    last = pprev_ref.shape[0] - 1
    prev_row = jnp.where(first, 0.0, pprev_ref[last:, :].astype(F32))
    row = lax.broadcasted_iota(I32, (tm, 1), 0)
    shifted = jnp.where(row == 0, prev_row, pltpu.roll(p, 1, 0))
    m = p + (shifted - p) * mu_ref[...]
    r = m[:, 0:w_]
    k = m[:, w_:2 * w_]
    v = m[:, 2 * w_:3 * w_]
    xw = m[:, 3 * w_:3 * w_ + 128]
    xa = m[:, 3 * w_ + 128:3 * w_ + 256]
    xg = m[:, 3 * w_ + 256:]

    z = -(w0_ref[...] + _dot(jnp.tanh(xw).astype(BF16), w2_ref[...]))
    softplus = jnp.maximum(z, 0.0) + jnp.log1p(jnp.exp(-jnp.abs(z)))
    logw = -jnp.exp(-softplus - 0.5)
    a = jax.nn.sigmoid(a0_ref[...] + _dot(xa.astype(BF16), a2_ref[...]))
    g = _dot(jax.nn.sigmoid(xg).astype(BF16), g2_ref[...])

    e = e_ref[...]
    et = et_ref[...]
    kk = k * kk_ref[...]
    kk = kk * jnp.minimum(lax.rsqrt(_head_sum(kk * kk, e, et)), 1e12)
    kmod = k * (1.0 + (a - 1.0) * ka_ref[...])
    bonus = _head_sum(r * kmod * rk_ref[...], e, et) * v

    lw_hi, lw_mid = _split2(logw)
    cum = _select_sum(tri_ref[...], lw_hi, lw_mid)
    pc_ref[...] = jnp.exp(_select_sum(sel_ref[...], lw_hi, lw_mid))
    inv = jnp.exp(-cum)
    rt_ref[...] = (r * jnp.exp(cum)).astype(BF16)
    at_ref[...] = (-kk * jnp.exp(cum - logw)).astype(BF16)
    bt_ref[...] = (kk * a * inv).astype(BF16)
    kt_ref[...] = (kmod * inv).astype(BF16)
    v_ref[...] = v.astype(BF16)
    g_ref[...] = g.astype(BF16)
    bonus_ref[...] = bonus


def _head_indicator(width):
    heads = width // HEAD_DIM
    c = jnp.arange(width)[:, None] // HEAD_DIM
    h = jnp.arange(LANES)[None, :]
    e = (c == h).astype(BF16)
    assert heads <= LANES
    return e, e.T


def rwkv_prep(p, mu, w0, w2, a0, a2, g2, k_k, k_a, r_k, *, seq_len, tm=512):
    n = p.shape[0]
    cols = mu.shape[1]
    prev_rows = 2 * SUBLANES
    width = w0.shape[1]
    nchunk = tm // CHUNK
    e, et = _head_indicator(width)
    t = jnp.arange(tm)
    same = (t[:, None] // CHUNK) == (t[None, :] // CHUNK)
    tri = (same & (t[:, None] >= t[None, :])).astype(BF16)
    sel = ((t[None, :] // CHUNK) == jnp.arange(nchunk)[:, None]).astype(BF16)
    const = lambda shape: pl.BlockSpec(shape, lambda i: (0, 0))
    stream = lambda dt: jax.ShapeDtypeStruct((n, width), dt)
    outs = pl.pallas_call(
        functools.partial(_rwkv_prep_body, seq_len=seq_len, width=width),
        grid=(n // tm,),
        in_specs=[pl.BlockSpec((tm, cols), lambda i: (i, 0)),
                  pl.BlockSpec((prev_rows, cols),
                               lambda i: (jnp.maximum(i * (tm // prev_rows) - 1, 0), 0)),
                  const((1, cols)), const((1, width)), const(w2.shape), const((1, width)),
                  const(a2.shape), const(g2.shape), const((1, width)), const((1, width)),
                  const((1, width)), const(e.shape), const(et.shape), const(tri.shape),
                  const(sel.shape)],
        out_specs=[pl.BlockSpec((tm, width), lambda i: (i, 0))] * 7
                  + [pl.BlockSpec((nchunk, width), lambda i: (i, 0))],
        out_shape=[stream(BF16)] * 6 + [stream(F32),
                   jax.ShapeDtypeStruct((n // CHUNK, width), F32)],
        compiler_params=_cparams(("parallel",)),
        name="rwkv_prep",
    )(p, p, mu, w0, w2, a0, a2, g2, k_k, k_a, r_k, e, et, tri, sel)
    return outs


def _chunk_pairs(rts, ats, bts, kts, vvs, pcs, s_prevs, masks):
    lo, strict, incl = masks
    c = rts[0].shape[0]
    c2 = 2 * c
    zero = jnp.zeros_like(rts[0])
    each = lambda f, *ls: [f(*a) for a in zip(*ls)]

    def stack(x):
        return jnp.concatenate([jnp.where(lo, x, zero), jnp.where(lo, zero, x)], axis=0)

    ar = each(lambda a, r: jnp.concatenate([stack(a), stack(r)], axis=0), ats, rts)
    bk = each(lambda b, k: jnp.concatenate([stack(b), stack(k)], axis=0), bts, kts)
    vb = each(stack, vvs)
    gram = each(lambda x, y: lax.dot_general(x, y, _NT, preferred_element_type=F32), ar, bk)
    ars = each(lambda x, s: lax.dot_general(x, s.astype(BF16), _NT, preferred_element_type=F32),
               ar, s_prevs)
    a_ab = each(lambda g: jnp.where(strict, g[:c2, :c2], 0.0), gram)
    rhs = each(lambda g, v, x: x[:c2] + _dot(jnp.where(strict, g[:c2, c2:], 0.0).astype(BF16), v),
               gram, vb, ars)

    u = rhs
    nk = a_ab
    steps = (c - 1).bit_length()
    for k in range(steps):
        nkb = each(lambda n: n.astype(BF16), nk)
        if k + 1 < steps:
            prod = each(lambda n, uu: _dot(n, jnp.concatenate([n, uu.astype(BF16)], axis=1)),
                        nkb, u)
            nk = each(lambda p: p[:, :c2], prod)
            u = each(lambda uu, p: uu + p[:, c2:], u, prod)
        else:
            u = each(lambda uu, n: uu + _dot(n, uu.astype(BF16)), u, nkb)

    uv = each(lambda uu, v: jnp.concatenate([uu.astype(BF16), v], axis=0), u, vb)
    a_r = each(lambda g: jnp.where(incl, g[c2:, :], 0.0).astype(BF16), gram)
    ybd = each(lambda x, a, w: x[c2:] + _dot(a, w), ars, a_r, uv)
    ys = each(lambda yb: yb[:c] + yb[c:], ybd)

    bk_end = each(lambda x, pc: (x.astype(F32) * pc).astype(BF16), bk, pcs)
    s_new = each(lambda s, pc, w, x: s * pc + lax.dot_general(w, x, _TN, preferred_element_type=F32),
                 s_prevs, pcs, uv, bk_end)
    return ys, s_new


def _rwkv_scan_body(rt_ref, at_ref, bt_ref, kt_ref, v_ref, pc_ref, g_ref, bonus_ref,
                    gnw_ref, gnb_ref, e_ref, et_ref, o_ref, s_ref, y_ref):
    tr, width = rt_ref.shape
    npairs = width // LANES

    @pl.when(pl.program_id(1) == 0)
    def _():
        s_ref[...] = jnp.zeros_like(s_ref)

    lane = lax.broadcasted_iota(I32, (CHUNK, LANES), 1)
    lo = lane < HEAD_DIM
    ri = lax.broadcasted_iota(I32, (2 * CHUNK, 2 * CHUNK), 0)
    ci = lax.broadcasted_iota(I32, (2 * CHUNK, 2 * CHUNK), 1)
    same = (ri // CHUNK) == (ci // CHUNK)
    ri2 = lax.broadcasted_iota(I32, (2 * CHUNK, 4 * CHUNK), 0)
    ci2 = lax.broadcasted_iota(I32, (2 * CHUNK, 4 * CHUNK), 1) % (2 * CHUNK)
    incl = ((ri2 // CHUNK) == (ci2 // CHUNK)) & (ri2 >= ci2)
    masks = (lo, same & (ri > ci), incl)

    def chunk_step(c, carry):
        rows = pl.ds(pl.multiple_of(c * CHUNK, CHUNK), CHUNK)
        this_chunk = lax.broadcasted_iota(I32, (pc_ref.shape[0], 1), 0) == c
        cols = [slice(pr * LANES, (pr + 1) * LANES) for pr in range(npairs)]
        pcs = [jnp.sum(jnp.where(this_chunk, pc_ref[:, cl], 0.0), axis=0, keepdims=True)
               for cl in cols]
        load = lambda ref: [ref[rows, cl] for cl in cols]
        ys, s_new = _chunk_pairs(load(rt_ref), load(at_ref), load(bt_ref), load(kt_ref),
                                 load(v_ref), pcs, [s_ref[pr] for pr in range(npairs)], masks)
        for pr in range(npairs):
            s_ref[pr] = s_new[pr]
            y_ref[rows, cols[pr]] = ys[pr]
        return carry

    lax.fori_loop(0, tr // CHUNK, chunk_step, 0)

    e = e_ref[...]
    et = et_ref[...]
    y = y_ref[...]
    mean = _head_sum(y, e, et) * (1.0 / HEAD_DIM)
    d = y - mean
    var = _head_sum(d * d, e, et) * (1.0 / HEAD_DIM)
    out = d * lax.rsqrt(var + GN_EPS) * gnw_ref[...] + gnb_ref[...]
    out = (out + bonus_ref[...]) * g_ref[...].astype(F32)
    o_ref[...] = out.astype(o_ref.dtype)


def rwkv_scan(rt, at, bt, kt, vv, pc, g, bonus, gn_w, gn_b, *, batch, tr=512):
    n, width = rt.shape
    seq = n // batch
    nst = seq // tr
    e, et = _head_indicator(width)
    blk = lambda rows: pl.BlockSpec((rows, width), lambda b, s: (b * nst + s, 0))
    const = lambda shape: pl.BlockSpec(shape, lambda b, s: (0, 0))
    return pl.pallas_call(
        _rwkv_scan_body,
        grid=(batch, nst),
        in_specs=[blk(tr)] * 5 + [blk(tr // CHUNK), blk(tr), blk(tr),
                  const((1, width)), const((1, width)), const(e.shape), const(et.shape)],
        out_specs=blk(tr),
        out_shape=jax.ShapeDtypeStruct((n, width), BF16),
        scratch_shapes=[pltpu.VMEM((width // LANES, LANES, LANES), F32),
                        pltpu.VMEM((tr, width), F32)],
        compiler_params=_cparams(("parallel", "arbitrary")),
        name="rwkv_scan",
    )(rt, at, bt, kt, vv, pc, g, bonus, gn_w, gn_b, e, et)


def _head_rmsnorm(x, gain, bd):
    hi, mid = _split2(x * x)
    ms = (_dot(hi, bd) + _dot(mid, bd)) * (1.0 / HEAD_DIM)
    return x * lax.rsqrt(ms + NORM_EPS) * gain


def _swa_body(sink_ref, *refs, group):
    kp_ref, kc_ref, vp_ref, vc_ref, qg_ref, kg_ref, bd_ref, o_ref = refs[-8:]
    q_pieces = refs[:-8]
    piece_tiles = q_pieces[0].shape[1] // LANES
    blk = o_ref.shape[0]
    n = pl.program_id(1)
    bd = bd_ref[...]
    scale = HEAD_DIM ** -0.5
    lane = lax.broadcasted_iota(I32, (1, LANES), 1)
    lo = lane < HEAD_DIM
    ri = lax.broadcasted_iota(I32, (2 * blk, 2 * blk), 0)
    cj = lax.broadcasted_iota(I32, (2 * blk, 2 * blk), 1)
    rel = blk + (ri % blk) - cj
    jmin = jnp.where(n > 0, 0, blk)
    valid = (rel >= 0) & (rel < WINDOW) & (cj >= jmin)
    top_rows = lax.broadcasted_iota(I32, (2 * blk, 1), 0) < blk

    kcat = jnp.concatenate([kp_ref[...], kc_ref[...]], axis=0).astype(F32)
    vcat = jnp.concatenate([vp_ref[...], vc_ref[...]], axis=0).astype(F32)
    kv_heads = kcat.shape[1] // HEAD_DIM
    q_tiles = o_ref.shape[1] // LANES
    tiles_per_kv = group // 2
    each = lambda f, *ls: [f(*a) for a in zip(*ls)]

    kv_cols = [slice(jt * LANES, (jt + 1) * LANES) for jt in range(kv_heads // 2)]
    kn = [_head_rmsnorm(kcat[:, c], kg_ref[...], bd) for c in kv_cols]
    kn_r = [pltpu.roll(x, HEAD_DIM, 1) for x in kn]
    vt = [vcat[:, c] for c in kv_cols]
    vt_r = [pltpu.roll(x, HEAD_DIM, 1) for x in vt]
    own = lambda hk: lo if hk % 2 == 0 else jnp.logical_not(lo)
    k2 = [jnp.where(own(hk), kn[hk // 2], kn_r[hk // 2]).astype(BF16) for hk in range(kv_heads)]
    v2 = [jnp.where(own(hk), vt[hk // 2], vt_r[hk // 2]).astype(BF16) for hk in range(kv_heads)]

    q_cols = [slice(t * LANES, (t + 1) * LANES) for t in range(q_tiles)]
    q_tile = lambda t: q_pieces[t // piece_tiles][:, q_cols[t % piece_tiles]]
    qn = [_head_rmsnorm(q_tile(t).astype(F32), qg_ref[...] * scale, bd) for t in range(q_tiles)]
    qst = [jnp.concatenate([jnp.where(lo, x, 0.0), jnp.where(lo, 0.0, x)], axis=0).astype(BF16)
           for x in qn]
    s = [jnp.where(valid, lax.dot_general(x, k2[t // tiles_per_kv], _NT,
                                          preferred_element_type=F32), -jnp.inf)
         for t, x in enumerate(qst)]
    sink = [jnp.where(top_rows, sink_ref[2 * t], sink_ref[2 * t + 1]) for t in range(q_tiles)]
    mx = each(lambda x, sk: jnp.maximum(jnp.max(x, axis=-1, keepdims=True), sk), s, sink)
    pr = each(lambda x, m: jnp.exp(x - m), s, mx)
    inv = each(lambda p, sk, m: 1.0 / (jnp.sum(p, axis=-1, keepdims=True) + jnp.exp(sk - m)),
               pr, sink, mx)
    o = [_dot((p * r).astype(BF16), v2[t // tiles_per_kv])
         for t, (p, r) in enumerate(zip(pr, inv))]
    for t in range(q_tiles):
        o_ref[:, q_cols[t]] = jnp.where(lo, o[t][:blk], o[t][blk:]).astype(o_ref.dtype)


def swa_attention(proj, q_gain, k_gain, sinks, *, batch, q_col0, q_width, kv_width):
    n = proj.shape[0]
    blk = WINDOW
    nb = n // batch // blk
    group = (q_width // HEAD_DIM) // (kv_width // HEAD_DIM)
    q_piece = math.gcd(q_col0, q_width)
    assert q_piece % LANES == 0 and (q_col0 + q_width) % kv_width == 0
    n_q = q_width // q_piece
    kcol = (q_col0 + q_width) // kv_width
    t = jnp.arange(LANES)
    bd = ((t[:, None] // HEAD_DIM) == (t[None, :] // HEAD_DIM)).astype(BF16)
    qg = jnp.tile(q_gain.reshape(1, HEAD_DIM), (1, 2))
    kg = jnp.tile(k_gain.reshape(1, HEAD_DIM), (1, 2))
    cur = lambda col: (lambda b, i, *_: (b * nb + i, col))
    prev = lambda col: (lambda b, i, *_: (b * nb + jnp.maximum(i - 1, 0), col))
    const = lambda shape: pl.BlockSpec(shape, lambda b, i, *_: (0, 0))
    grid_spec = pltpu.PrefetchScalarGridSpec(
        num_scalar_prefetch=1,
        grid=(batch, nb),
        in_specs=[pl.BlockSpec((blk, q_piece), cur(q_col0 // q_piece + t)) for t in range(n_q)]
                 + [pl.BlockSpec((blk, kv_width), prev(kcol)),
                  pl.BlockSpec((blk, kv_width), cur(kcol)),
                  pl.BlockSpec((blk, kv_width), prev(kcol + 1)),
                  pl.BlockSpec((blk, kv_width), cur(kcol + 1)),
                  const((1, LANES)), const((1, LANES)), const((LANES, LANES))],
        out_specs=pl.BlockSpec((blk, q_width), cur(0)),
    )
    return pl.pallas_call(
        functools.partial(_swa_body, group=group),
        grid_spec=grid_spec,
        out_shape=jax.ShapeDtypeStruct((n, q_width), BF16),
        compiler_params=_cparams(("parallel", "arbitrary")),
        name="swa_attn",
    )(sinks, *([proj] * (n_q + 4)), qg, kg, bd)


def _merge_route_body(a_ref, b_ref, *refs, gate_pieces):
    gates = refs[:2 * gate_pieces]
    (x_ref, pr_ref, pa_ref, wo_ref, n2_ref, rhm_ref, rh_ref, rb_ref,
     x1_ref, h2_ref, route_ref, logits_ref) = refs[2 * gate_pieces:]
    gate = lambda pieces: jnp.concatenate([g[...] for g in pieces], axis=1).astype(F32)
    @pl.when(pl.program_id(0) == 0)
    def _():
        logits_ref[...] = jnp.zeros_like(logits_ref)

    _route(logits_ref[...], route_ref)

    merged = (gate(gates[:gate_pieces]) * _dot(a_ref[...], pr_ref[...])
              + gate(gates[gate_pieces:]) * _dot(b_ref[...], pa_ref[...]))
    x1 = x_ref[...] + _dot(merged.astype(BF16), wo_ref[...])
    x1_ref[...] = x1
    ms = jnp.mean(x1 * x1, axis=-1, keepdims=True)
    h2 = x1 * lax.rsqrt(ms + NORM_EPS) * n2_ref[...]
    h2_ref[...] = _pack_bf16_halves(h2)

    hi, mid = _split2(h2)
    hw = _dot(hi, rhm_ref[...])
    logits_ref[...] = hw[:, :LANES] + (hw[:, LANES:] + _dot(mid, rh_ref[...])) + rb_ref[...]


def _route(logits, route_ref):
    lane = lax.broadcasted_iota(I32, logits.shape, 1)
    big = jnp.int32(1 << 20)
    neg = -jnp.inf
    is_coarse = lane < N_GROUPS
    cl = jnp.where(is_coarse, logits, neg)
    ce = jnp.exp(cl - jnp.max(cl, axis=-1, keepdims=True))
    cp = ce / jnp.sum(ce, axis=-1, keepdims=True)
    g_prob = jnp.max(cp, axis=-1, keepdims=True)
    g_idx = jnp.min(jnp.where(is_coarse & (cp == g_prob), lane, big), axis=-1, keepdims=True)

    fine_lane = lane - N_GROUPS
    in_group = ((lane >= N_GROUPS) & (lane < N_GROUPS + N_GROUPS * GROUP_SIZE)
                & ((fine_lane // GROUP_SIZE) == g_idx))
    fl = jnp.where(in_group, logits, neg)
    fe = jnp.exp(fl - jnp.max(fl, axis=-1, keepdims=True))
    fp = fe / jnp.sum(fe, axis=-1, keepdims=True)
    p1 = jnp.max(jnp.where(in_group, fp, -1.0), axis=-1, keepdims=True)
    i1 = jnp.min(jnp.where(in_group & (fp == p1), lane, big), axis=-1, keepdims=True)
    rest = in_group & (lane != i1)
    p2 = jnp.max(jnp.where(rest, fp, -1.0), axis=-1, keepdims=True)
    i2 = jnp.min(jnp.where(rest & (fp == p2), lane, big), axis=-1, keepdims=True)
    den = p1 + p2
    w1 = g_prob * p1 / den
    w2 = g_prob * p2 / den
    e1 = (i1 - N_GROUPS).astype(F32)
    e2 = (i2 - N_GROUPS).astype(F32)
    route_ref[...] = jnp.where(lane == 0, w1, jnp.where(lane == 1, w2,
                     jnp.where(lane == 2, e1, jnp.where(lane == 3, e2, 0.0))))


def merge_route(a_out, b_out, proj, gate_col0, x2d, proj_r, proj_a, w_out, norm2_w, router_w,
                router_b, *, tm=256):
    n, d = x2d.shape
    g_piece = math.gcd(gate_col0, d)
    assert g_piece % LANES == 0
    gate_pieces = d // g_piece
    wa = a_out.shape[1]
    rh, rm = _split2(router_w)
    rhm = jnp.concatenate([rh, rm], axis=1)
    nt = n // tm
    const = lambda arr: pl.BlockSpec(arr.shape, lambda i: (0, 0), pipeline_mode=pl.Buffered(1))
    row = lambda w, col=0: pl.BlockSpec((tm, w), lambda i: (jnp.minimum(i, nt - 1), col))
    return pl.pallas_call(
        functools.partial(_merge_route_body, gate_pieces=gate_pieces),
        grid=(nt + 1,),
        in_specs=[row(wa), row(wa)]
                 + [row(g_piece, gate_col0 // g_piece + t) for t in range(2 * gate_pieces)]
                 + [row(d),
                  const(proj_r), const(proj_a), const(w_out), const(norm2_w),
                  const(rhm), const(rh), const(router_b)],
        out_specs=[row(d), row(d // 2),
                   pl.BlockSpec((tm, LANES), lambda i: (jnp.maximum(i - 1, 0), 0))],
        scratch_shapes=[pltpu.VMEM((tm, LANES), F32)],
        out_shape=[jax.ShapeDtypeStruct((n, d), F32), jax.ShapeDtypeStruct((n, d // 2), jnp.uint32),
                   jax.ShapeDtypeStruct((n, LANES), F32)],
        compiler_params=_cparams(("arbitrary",)),
        name="merge_route",
    )(a_out, b_out, *([proj] * (2 * gate_pieces)), x2d, proj_r, proj_a, w_out, norm2_w, rhm, rh,
      router_b)


def _moe_rank_body(ids_ref, ut_ref, rank_ref, counts_ref, carry_ref):
    @pl.when(pl.program_id(0) == 0)
    def _():
        carry_ref[...] = jnp.zeros_like(carry_ref)

    ids = ids_ref[...]
    n_exp = carry_ref.shape[0]
    sub = lax.broadcasted_iota(I32, (n_exp, ids.shape[1]), 0)
    onehot = (sub == ids).astype(F32)
    before = _dot(onehot.astype(BF16), ut_ref[...])
    carry = carry_ref[...]
    rank_ref[...] = jnp.sum(onehot * (before + carry), axis=0, keepdims=True).astype(I32)
    carry = carry + jnp.sum(onehot, axis=1, keepdims=True)
    carry_ref[...] = carry
    counts_ref[...] = jnp.broadcast_to(carry, counts_ref.shape)


def moe_rank(ids_row, n_experts, *, tb=1024):
    m = ids_row.shape[1]
    t = jnp.arange(tb)
    ut = (t[:, None] < t[None, :]).astype(BF16)
    return pl.pallas_call(
        _moe_rank_body,
        grid=(m // tb,),
        in_specs=[pl.BlockSpec((1, tb), lambda i: (0, i)),
                  pl.BlockSpec((tb, tb), lambda i: (0, 0))],
        out_specs=[pl.BlockSpec((1, tb), lambda i: (0, i)),
                   pl.BlockSpec((n_experts, LANES), lambda i: (0, 0))],
        out_shape=[jax.ShapeDtypeStruct((1, m), I32),
                   jax.ShapeDtypeStruct((n_experts, LANES), F32)],
        scratch_shapes=[pltpu.VMEM((n_experts, 1), F32)],
        compiler_params=_cparams(("arbitrary",)),
        name="moe_rank",
    )(ids_row, ut)


def _moe_dest_body(ids_ref, rank_ref, start_ref, dest_ref):
    ids = ids_ref[...]
    sub = lax.broadcasted_iota(I32, (start_ref.shape[0], ids.shape[1]), 0)
    start = jnp.sum(jnp.where(sub == ids, start_ref[...], 0.0), axis=0, keepdims=True)
    dest_ref[...] = start.astype(I32) + rank_ref[...]


def moe_dest(ids_row, rank_row, pad_start_col, *, tb=2048):
    m = ids_row.shape[1]
    row = pl.BlockSpec((1, tb), lambda i: (0, i))
    return pl.pallas_call(
        _moe_dest_body,
        grid=(m // tb,),
        in_specs=[row, row, pl.BlockSpec(pad_start_col.shape, lambda i: (0, 0))],
        out_specs=row,
        out_shape=jax.ShapeDtypeStruct((1, m), I32),
        compiler_params=_cparams(("parallel",)),
        name="moe_dest",
    )(ids_row, rank_row, pad_start_col)


def _moe_dispatch_body(dest_ref, start_ref, end_ref, h_ref, buf_ref, zeros_ref, sem, zsem):
    tm = h_ref.shape[0] * h_ref.shape[1]
    base = pl.program_id(0) * tm * TOP_K

    @pl.when(pl.program_id(0) == 0)
    def _():
        zeros_ref[...] = jnp.zeros_like(zeros_ref)

        def zero_last_block(e):
            last = pl.multiple_of(end_ref[e] - MOE_BLOCK, MOE_BLOCK)
            return pltpu.make_async_copy(zeros_ref, buf_ref.at[pl.ds(last, MOE_BLOCK)], zsem)

        def zstart(e, c):
            @pl.when(end_ref[e] > start_ref[e])
            def _():
                zero_last_block(e).start()
            return c

        def zwait(e, c):
            @pl.when(end_ref[e] > start_ref[e])
            def _():
                zero_last_block(e).wait()
            return c

        lax.fori_loop(0, start_ref.shape[0], zstart, 0)
        lax.fori_loop(0, start_ref.shape[0], zwait, 0)

        tail0 = end_ref[end_ref.shape[0] - 1]

        def zero_tail_block(r):
            first = pl.multiple_of(tail0 + r * MOE_BLOCK, MOE_BLOCK)
            return pltpu.make_async_copy(zeros_ref, buf_ref.at[pl.ds(first, MOE_BLOCK)], zsem)

        def tstart(r, c):
            zero_tail_block(r).start()
            return c

        def twait(r, c):
            zero_tail_block(r).wait()
            return c

        n_tail = (buf_ref.shape[0] - tail0) // MOE_BLOCK
        lax.fori_loop(0, n_tail, tstart, 0)
        lax.fori_loop(0, n_tail, twait, 0)

    def for_rows(fn):
        def body(q, c):
            for u in range(SUBLANES):
                for k in range(TOP_K):
                    slot = dest_ref[base + (q * SUBLANES + u) * TOP_K + k]
                    fn(pltpu.make_async_copy(h_ref.at[q, pl.ds(u, 1)], buf_ref.at[pl.ds(slot, 1)],
                                             sem))
            return c
        lax.fori_loop(0, h_ref.shape[0], body, 0)

    for_rows(lambda cp: cp.start())
    for_rows(lambda cp: cp.wait())


def moe_dispatch(dest, pad_start, pad_end, h2, n_slots, *, tm=256):
    n, d = h2.shape
    h2 = h2.reshape(n // SUBLANES, SUBLANES, d)
    grid_spec = pltpu.PrefetchScalarGridSpec(
        num_scalar_prefetch=3,
        grid=(n // tm,),
        in_specs=[pl.BlockSpec((tm // SUBLANES, SUBLANES, d), lambda i, *_: (i, 0, 0))],
        out_specs=pl.BlockSpec(memory_space=pl.ANY),
        scratch_shapes=[pltpu.VMEM((MOE_BLOCK, d), h2.dtype),
                        pltpu.SemaphoreType.DMA(()), pltpu.SemaphoreType.DMA(())],
    )
    return pl.pallas_call(
        _moe_dispatch_body,
        grid_spec=grid_spec,
        out_shape=jax.ShapeDtypeStruct((n_slots, d), h2.dtype),
        compiler_params=_cparams(("arbitrary",)),
        name="moe_dispatch",
    )(dest, pad_start, pad_end, h2)


def _moe_expert_body(sbe_ref, row0_ref, nb_ref, tot_ref, wg_ref, wu_ref, wd_ref, buf_ref, out_ref,
                     wgb, wub, wdb, xp, acc, pending, sem_in, sem_out):
    del sbe_ref
    s = pl.program_id(0)
    f = pl.program_id(1)
    n_sb = pl.num_programs(0)
    last_f = pl.num_programs(1) - 1
    nb = nb_ref[s]
    half = xp.shape[2]

    def blk(first_row, r):
        return pl.ds(pl.multiple_of(first_row + r * MOE_BLOCK, MOE_BLOCK), MOE_BLOCK)

    def rows_in(sb, r):
        return pltpu.make_async_copy(buf_ref.at[blk(row0_ref[sb], r)], xp.at[sb % 2, blk(0, r)],
                                     sem_in.at[sb % 2])

    def rows_out(r):
        return pltpu.make_async_copy(acc.at[blk(0, r)], out_ref.at[blk(row0_ref[s], r)], sem_out)

    def for_blocks(n, fn):
        def body(r, carry):
            fn(r)
            return carry
        lax.fori_loop(0, n, body, 0)

    def drain_out():
        for_blocks(pending[0], lambda r: rows_out(0).wait())
        pending[0] = 0

    def mlp_rows(first_row, n_rows, first):
        rows = pl.ds(pl.multiple_of(first_row, MOE_BLOCK), n_rows)
        xlo, xhi = _unpack_bf16_halves(xp[s % 2, rows, :])
        hg = _dot(xlo, wgb[:half, :]) + _dot(xhi, wgb[half:, :])
        hu = _dot(xlo, wub[:half, :]) + _dot(xhi, wub[half:, :])
        y = _dot((jax.nn.silu(hg) * hu).astype(BF16), wdb[...])
        if first:
            acc[rows, :] = y
        else:
            acc[rows, :] += y

    def all_rows(first):
        for_blocks(nb // 2, lambda q: mlp_rows(q * (2 * MOE_BLOCK), 2 * MOE_BLOCK, first))

        @pl.when(nb % 2 == 1)
        def _():
            mlp_rows((nb - 1) * MOE_BLOCK, MOE_BLOCK, first)

    @pl.when((s == 0) & (f == 0))
    def _():
        pending[0] = 0
        for_blocks(nb, lambda r: rows_in(0, r).start())

    @pl.when(nb > 0)
    def _():
        @pl.when(f == 0)
        def _():
            nxt = jnp.minimum(s + 1, n_sb - 1)
            nb_next = jnp.where(s + 1 < n_sb, nb_ref[nxt], 0)
            for_blocks(nb_next, lambda r: rows_in(nxt, r).start())

        wgb[...] = wg_ref[0].astype(BF16)
        wub[...] = wu_ref[0].astype(BF16)
        wdb[...] = wd_ref[0].astype(BF16)

        @pl.when(f == 0)
        def _():
            for_blocks(nb, lambda r: rows_in(s, r).wait())
            drain_out()
            all_rows(True)

        @pl.when(f > 0)
        def _():
            all_rows(False)

        @pl.when(f == last_f)
        def _():
            for_blocks(nb, lambda r: rows_out(r).start())
            pending[0] = nb

    @pl.when((s == n_sb - 1) & (f == last_f))
    def _():
        drain_out()
        acc[pl.ds(0, MOE_BLOCK), :] = jnp.zeros((MOE_BLOCK, acc.shape[1]), acc.dtype)
        tail0 = tot_ref[0]

        def zero_out(r):
            return pltpu.make_async_copy(acc.at[pl.ds(0, MOE_BLOCK)], out_ref.at[blk(tail0, r)],
                                         sem_out)

        ntail = (out_ref.shape[0] - tail0) // MOE_BLOCK
        for_blocks(ntail, lambda r: zero_out(r).start())
        for_blocks(ntail, lambda r: zero_out(r).wait())


def moe_experts(sb_expert, sb_row0, sb_blocks, total_rows, buf, wg, wu, wd):
    p, half = buf.shape
    d = 2 * half
    ff = wg.shape[2]
    nf = ff // FF_TILE
    f_eff = lambda s, f, nb: jnp.where(nb[s] > 0, f, nf - 1)
    grid_spec = pltpu.PrefetchScalarGridSpec(
        num_scalar_prefetch=4,
        grid=(sb_expert.shape[0], nf),
        in_specs=[pl.BlockSpec((1, d, FF_TILE), lambda s, f, e, r0, nb, t: (e[s], 0, f_eff(s, f, nb))),
                  pl.BlockSpec((1, d, FF_TILE), lambda s, f, e, r0, nb, t: (e[s], 0, f_eff(s, f, nb))),
                  pl.BlockSpec((1, FF_TILE, d), lambda s, f, e, r0, nb, t: (e[s], f_eff(s, f, nb), 0)),
                  pl.BlockSpec(memory_space=pl.ANY)],
        out_specs=pl.BlockSpec(memory_space=pl.ANY),
        scratch_shapes=[pltpu.VMEM((d, FF_TILE), BF16), pltpu.VMEM((d, FF_TILE), BF16),
                        pltpu.VMEM((FF_TILE, d), BF16), pltpu.VMEM((2, SUPER_ROWS, half), jnp.uint32),
                        pltpu.VMEM((SUPER_ROWS, d), F32), pltpu.SMEM((1,), I32),
                        pltpu.SemaphoreType.DMA((2,)), pltpu.SemaphoreType.DMA(())],
    )
    return pl.pallas_call(
        _moe_expert_body,
        grid_spec=grid_spec,
        out_shape=jax.ShapeDtypeStruct((p, d), F32),
        compiler_params=_cparams(("arbitrary", "arbitrary")),
        name="moe_experts",
    )(sb_expert, sb_row0, sb_blocks, total_rows, wg, wu, wd, buf)


def _moe_combine_body(dest_ref, x1_ref, route_ref, out_ref, y_ref, rows_ref, sem):
    tm, d = x1_ref.shape
    i = pl.program_id(0)
    sublanes = rows_ref.shape[3]

    def row_copy(tile, q, u, k):
        slot = dest_ref[(tile * tm + q * sublanes + u) * TOP_K + k]
        return pltpu.make_async_copy(out_ref.at[pl.ds(slot, 1)],
                                     rows_ref.at[tile % 2, k, q, pl.ds(u, 1)], sem.at[tile % 2])

    def for_rows(tile, fn):
        def body(q, c):
            for u in range(sublanes):
                for k in range(TOP_K):
                    fn(row_copy(tile, q, u, k))
            return c
        lax.fori_loop(0, tm // sublanes, body, 0)

    @pl.when(i == 0)
    def _():
        for_rows(0, lambda cp: cp.start())

    @pl.when(i + 1 < pl.num_programs(0))
    def _():
        for_rows(i + 1, lambda cp: cp.start())

    for_rows(i, lambda cp: cp.wait())
    route = route_ref[...]
    rows = rows_ref[i % 2]
    y_ref[...] = x1_ref[...] + (rows[0].reshape(tm, d) * route[:, 0:1]
                                + rows[1].reshape(tm, d) * route[:, 1:2])


def moe_combine(dest, x1, route, expert_out, *, tm=256):
    n, d = x1.shape
    grid_spec = pltpu.PrefetchScalarGridSpec(
        num_scalar_prefetch=1,
        grid=(n // tm,),
        in_specs=[pl.BlockSpec((tm, d), lambda i, *_: (i, 0)),
                  pl.BlockSpec((tm, LANES), lambda i, *_: (i, 0)),
                  pl.BlockSpec(memory_space=pl.ANY)],
        out_specs=pl.BlockSpec((tm, d), lambda i, *_: (i, 0)),
        scratch_shapes=[pltpu.VMEM((2, TOP_K, tm // SUBLANES, SUBLANES, d), F32),
                        pltpu.SemaphoreType.DMA((2,))],
    )
    return pl.pallas_call(
        _moe_combine_body,
        grid_spec=grid_spec,
        out_shape=jax.ShapeDtypeStruct((n, d), F32),
        compiler_params=_cparams(("arbitrary",)),
        name="moe_combine",
    )(dest, x1, route, expert_out)


def hierarchical_moe(x1, h2, route, wg, wu, wd):
    n = h2.shape[0]
    n_experts = wg.shape[0]
    m = n * TOP_K
    ids_row = route[:, 2:4].astype(I32).reshape(1, m)
    rank_row, counts = moe_rank(ids_row, n_experts)
    counts = counts[:, 0].astype(I32)
    padded = (counts + MOE_BLOCK - 1) // MOE_BLOCK * MOE_BLOCK
    pad_end = jnp.cumsum(padded)
    pad_start = pad_end - padded
    dest = moe_dest(ids_row, rank_row, pad_start.astype(F32).reshape(n_experts, 1)).reshape(m)
    n_slots = m + n_experts * MOE_BLOCK
    n_sb = (padded + SUPER_ROWS - 1) // SUPER_ROWS
    sb_end = jnp.cumsum(n_sb)
    sb_start = sb_end - n_sb
    s_idx = jnp.arange(n_experts + m // SUPER_ROWS, dtype=I32)
    used = s_idx < sb_end[-1]
    e_of = jnp.minimum(jnp.sum(sb_end[None, :] <= s_idx[:, None], axis=1), n_experts - 1)
    piece = s_idx - sb_start[e_of]
    sb_row0 = jnp.where(used, pad_start[e_of] + piece * SUPER_ROWS, 0).astype(I32)
    sb_rows = jnp.clip(padded[e_of] - piece * SUPER_ROWS, 0, SUPER_ROWS)
    sb_blocks = jnp.where(used, sb_rows // MOE_BLOCK, 0).astype(I32)
    sb_expert = jnp.where(used, e_of, e_of[jnp.maximum(sb_end[-1] - 1, 0)]).astype(I32)
    buf = moe_dispatch(dest, pad_start.astype(I32), pad_end.astype(I32), h2, n_slots)
    expert_out = moe_experts(sb_expert, sb_row0, sb_blocks, pad_end[-1:].astype(I32), buf,
                             wg, wu, wd)
    return moe_combine(dest, x1, route, expert_out)


def _pad_cols(w, to):
    return jnp.pad(w, ((0, 0), (0, to - w.shape[1])))


def _pad_rows(w, to):
    return jnp.pad(w, ((0, to - w.shape[0]), (0, 0)))


def _layer(x2d, batch, norm1_w, w_in, mu, w0, w2, a0, a2, g2, k_k, k_a, r_k, gn_w, gn_b,
           q_norm_w, k_norm_w, sinks, proj_rwkv, proj_attn, w_out, norm2_w, wc, bc, wf, bf,
           wg, wu, wd):
    n, d = x2d.shape
    seq = n // batch
    width = w0.shape[0]
    dl, il, gl = w2.shape[0], a2.shape[0], g2.shape[0]
    q_width = proj_attn.shape[0]
    rwkv_cols = 3 * width + dl + il + gl
    kv_width = (w_in.shape[1] - rwkv_cols - q_width - 2 * d) // 2
    row = lambda v: v.reshape(1, -1).astype(F32)

    c0 = 3 * width
    w_all = jnp.concatenate([w_in[:, :c0], _pad_cols(w_in[:, c0:c0 + dl], LANES),
                             _pad_cols(w_in[:, c0 + dl:c0 + dl + il], LANES),
                             w_in[:, c0 + dl + il:]], axis=1).astype(BF16)
    mu_p = jnp.concatenate([mu[:c0], jnp.pad(mu[c0:c0 + dl], (0, LANES - dl)),
                            jnp.pad(mu[c0 + dl:c0 + dl + il], (0, LANES - il)),
                            mu[c0 + dl + il:]]).reshape(1, -1)
    rwkv_w = c0 + 2 * LANES + gl
    proj = norm_proj(x2d, row(norm1_w), w_all, 2 * d)

    rt, at, bt, kt, vv, g, bonus, pc = rwkv_prep(
        proj, mu_p, row(w0), _pad_rows(w2, LANES).astype(BF16), row(a0),
        _pad_rows(a2, LANES).astype(BF16), g2.astype(BF16), row(k_k), row(k_a), row(r_k),
        seq_len=seq)
    a_out = rwkv_scan(rt, at, bt, kt, vv, pc, g, bonus, row(gn_w), row(gn_b), batch=batch)

    b_out = swa_attention(proj, q_norm_w, k_norm_w, sinks.astype(F32), batch=batch,
                          q_col0=rwkv_w, q_width=q_width, kv_width=kv_width)

    n_groups, n_experts = wc.shape[1], wf.shape[1]
    router_w = _pad_cols(jnp.concatenate([wc, wf], axis=1), LANES)
    router_b = _pad_cols(jnp.concatenate([bc, bf]).reshape(1, -1), LANES)
    assert n_groups == N_GROUPS and n_experts == N_GROUPS * GROUP_SIZE
    x1, h2, route = merge_route(a_out, b_out, proj, rwkv_w + q_width + 2 * kv_width, x2d,
                                proj_rwkv.astype(BF16),
                                proj_attn.astype(BF16), w_out.astype(BF16), row(norm2_w),
                                router_w, router_b)
    return hierarchical_moe(x1, h2, route, wg, wu, wd)


def kernel(x, norm1_w, w_in, rwkv_mu, rwkv_w0, rwkv_w2, rwkv_a0, rwkv_a2, rwkv_g2, rwkv_k_k,
           rwkv_k_a, rwkv_r_k, rwkv_gn_w, rwkv_gn_b, q_norm_w, k_norm_w, attn_sinks, proj_rwkv,
           proj_attn, w_out, norm2_w, router_coarse_w, router_coarse_b, router_fine_w,
           router_fine_b, expert_w_gate, expert_w_up, expert_w_down):
    batch, seq, d = x.shape
    x2d = x.reshape(batch * seq, d)
    for layer in range(norm1_w.shape[0]):
        x2d = _layer(x2d, batch, norm1_w[layer], w_in[layer], rwkv_mu[layer], rwkv_w0[layer],
                     rwkv_w2[layer], rwkv_a0[layer], rwkv_a2[layer], rwkv_g2[layer],
                     rwkv_k_k[layer], rwkv_k_a[layer], rwkv_r_k[layer].reshape(-1),
                     rwkv_gn_w[layer], rwkv_gn_b[layer], q_norm_w[layer], k_norm_w[layer],
                     attn_sinks[layer], proj_rwkv[layer], proj_attn[layer], w_out[layer],
                     norm2_w[layer], router_coarse_w[layer], router_coarse_b[layer],
                     router_fine_w[layer], router_fine_b[layer], expert_w_gate[layer],
                     expert_w_up[layer], expert_w_down[layer])
    return x2d.reshape(batch, seq, d)
```

```python
import functools
import math

import jax
import jax.numpy as jnp
from jax import lax
from jax.experimental import pallas as pl
from jax.experimental.pallas import tpu as pltpu

F32 = jnp.float32
BF16 = jnp.bfloat16
I32 = jnp.int32

NORM_EPS = 1e-6
GN_EPS = 64e-5
HEAD_DIM = 64
LANES = 128
SUBLANES = 8
CHUNK = 64
WINDOW = 128
MOE_BLOCK = 128
SUPER_ROWS = 1024
FF_TILE = 512
TOP_K = 2
N_GROUPS = 8
GROUP_SIZE = 8
VMEM_LIMIT = 56 * 1024 * 1024

_NT = (((1,), (1,)), ((), ()))
_TN = (((0,), (0,)), ((), ()))


def _dot(a, b):
    return jnp.dot(a, b, preferred_element_type=F32)


def _split2(x):
    hi = x.astype(BF16)
    mid = (x - hi.astype(F32)).astype(BF16)
    return hi, mid


def _select_sum(m, hi, mid):
    return _dot(m, hi) + _dot(m, mid)


def _pack_bf16_halves(x):
    w = x.shape[1] // 2
    lo = lax.bitcast_convert_type(x[:, :w].astype(BF16).astype(F32), jnp.uint32)
    hi = lax.bitcast_convert_type(x[:, w:].astype(BF16).astype(F32), jnp.uint32)
    return (lo >> 16) | (hi & jnp.uint32(0xFFFF0000))


def _unpack_bf16_halves(xp):
    lo = lax.bitcast_convert_type(xp << 16, F32).astype(BF16)
    hi = lax.bitcast_convert_type(xp & jnp.uint32(0xFFFF0000), F32).astype(BF16)
    return lo, hi


def _cparams(sem, vmem=VMEM_LIMIT):
    return pltpu.CompilerParams(dimension_semantics=sem, vmem_limit_bytes=vmem)


def _norm_proj_body(x_ref, g_ref, w_ref, o_ref, h_ref, *, first_gate_tile):
    j = pl.program_id(1)

    @pl.when(j == 0)
    def _():
        x = x_ref[...]
        ms = jnp.mean(x * x, axis=-1, keepdims=True)
        h_ref[...] = (x * lax.rsqrt(ms + NORM_EPS) * g_ref[...]).astype(BF16)

    acc = _dot(h_ref[...], w_ref[...])

    @pl.when(j < first_gate_tile)
    def _():
        o_ref[...] = acc.astype(o_ref.dtype)

    @pl.when(j >= first_gate_tile)
    def _():
        o_ref[...] = jax.nn.sigmoid(acc).astype(o_ref.dtype)


def norm_proj(x2d, gain, w, gate_cols, *, tm=1024, tn=1024):
    n, d = x2d.shape
    c = w.shape[1]
    tm = min(tm, n)
    assert c % tn == 0 and gate_cols % tn == 0
    return pl.pallas_call(
        functools.partial(_norm_proj_body, first_gate_tile=(c - gate_cols) // tn),
        grid=(n // tm, c // tn),
        in_specs=[pl.BlockSpec((tm, d), lambda i, j: (i, 0)),
                  pl.BlockSpec((1, d), lambda i, j: (0, 0)),
                  pl.BlockSpec((d, tn), lambda i, j: (0, j))],
        out_specs=pl.BlockSpec((tm, tn), lambda i, j: (i, j)),
        out_shape=jax.ShapeDtypeStruct((n, c), BF16),
        scratch_shapes=[pltpu.VMEM((tm, d), BF16)],
        compiler_params=_cparams(("parallel", "arbitrary")),
        name="norm_proj",
    )(x2d, gain, w)


def _head_sum(x, e, et):
    s = _dot(x.astype(BF16), e)
    hi, mid = _split2(s)
    return _dot(hi, et) + _dot(mid, et)


def _rwkv_prep_body(p_ref, pprev_ref, mu_ref, w0_ref, w2_ref, a0_ref, a2_ref, g2_ref,
                    kk_ref, ka_ref, rk_ref, e_ref, et_ref, tri_ref, sel_ref,
                    rt_ref, at_ref, bt_ref, kt_ref, v_ref, g_ref, bonus_ref, pc_ref,
                    *, seq_len, width):
    tm = p_ref.shape[0]
    w_ = width
    first = (pl.program_id(0) * tm) % seq_len == 0
    p = p_ref[...].astype(F32)
    last = pprev_ref.shape[0] - 1
    prev_row = jnp.where(first, 0.0, pprev_ref[last:, :].astype(F32))
    row = lax.broadcasted_iota(I32, (tm, 1), 0)
    shifted = jnp.where(row == 0, prev_row, pltpu.roll(p, 1, 0))
    m = p + (shifted - p) * mu_ref[...]
    r = m[:, 0:w_]
    k = m[:, w_:2 * w_]
    v = m[:, 2 * w_:3 * w_]
    xw = m[:, 3 * w_:3 * w_ + 128]
    xa = m[:, 3 * w_ + 128:3 * w_ + 256]
    xg = m[:, 3 * w_ + 256:]

    z = -(w0_ref[...] + _dot(jnp.tanh(xw).astype(BF16), w2_ref[...]))
    softplus = jnp.maximum(z, 0.0) + jnp.log1p(jnp.exp(-jnp.abs(z)))
    logw = -jnp.exp(-softplus - 0.5)
    a = jax.nn.sigmoid(a0_ref[...] + _dot(xa.astype(BF16), a2_ref[...]))
    g = _dot(jax.nn.sigmoid(xg).astype(BF16), g2_ref[...])

    e = e_ref[...]
    et = et_ref[...]
    kk = k * kk_ref[...]
    kk = kk * jnp.minimum(lax.rsqrt(_head_sum(kk * kk, e, et)), 1e12)
    kmod = k * (1.0 + (a - 1.0) * ka_ref[...])
    bonus = _head_sum(r * kmod * rk_ref[...], e, et) * v

    lw_hi, lw_mid = _split2(logw)
    cum = _select_sum(tri_ref[...], lw_hi, lw_mid)
    pc_ref[...] = jnp.exp(_select_sum(sel_ref[...], lw_hi, lw_mid))
    inv = jnp.exp(-cum)
    rt_ref[...] = (r * jnp.exp(cum)).astype(BF16)
    at_ref[...] = (-kk * jnp.exp(cum - logw)).astype(BF16)
    bt_ref[...] = (kk * a * inv).astype(BF16)
    kt_ref[...] = (kmod * inv).astype(BF16)
    v_ref[...] = v.astype(BF16)
    g_ref[...] = g.astype(BF16)
    bonus_ref[...] = bonus


def _head_indicator(width):
    heads = width // HEAD_DIM
    c = jnp.arange(width)[:, None] // HEAD_DIM
    h = jnp.arange(LANES)[None, :]
    e = (c == h).astype(BF16)
    assert heads <= LANES
    return e, e.T


def rwkv_prep(p, mu, w0, w2, a0, a2, g2, k_k, k_a, r_k, *, seq_len, tm=512):
    n = p.shape[0]
    cols = mu.shape[1]
    prev_rows = 2 * SUBLANES
    width = w0.shape[1]
    nchunk = tm // CHUNK
    e, et = _head_indicator(width)
    t = jnp.arange(tm)
    same = (t[:, None] // CHUNK) == (t[None, :] // CHUNK)
    tri = (same & (t[:, None] >= t[None, :])).astype(BF16)
    sel = ((t[None, :] // CHUNK) == jnp.arange(nchunk)[:, None]).astype(BF16)
    const = lambda shape: pl.BlockSpec(shape, lambda i: (0, 0))
    stream = lambda dt: jax.ShapeDtypeStruct((n, width), dt)
    outs = pl.pallas_call(
        functools.partial(_rwkv_prep_body, seq_len=seq_len, width=width),
        grid=(n // tm,),
        in_specs=[pl.BlockSpec((tm, cols), lambda i: (i, 0)),
                  pl.BlockSpec((prev_rows, cols),
                               lambda i: (jnp.maximum(i * (tm // prev_rows) - 1, 0), 0)),
                  const((1, cols)), const((1, width)), const(w2.shape), const((1, width)),
                  const(a2.shape), const(g2.shape), const((1, width)), const((1, width)),
                  const((1, width)), const(e.shape), const(et.shape), const(tri.shape),
                  const(sel.shape)],
        out_specs=[pl.BlockSpec((tm, width), lambda i: (i, 0))] * 7
                  + [pl.BlockSpec((nchunk, width), lambda i: (i, 0))],
        out_shape=[stream(BF16)] * 6 + [stream(F32),
                   jax.ShapeDtypeStruct((n // CHUNK, width), F32)],
        compiler_params=_cparams(("parallel",)),
        name="rwkv_prep",
    )(p, p, mu, w0, w2, a0, a2, g2, k_k, k_a, r_k, e, et, tri, sel)
    return outs


def _chunk_pairs(rts, ats, bts, kts, vvs, pcs, s_prevs, masks):
    lo, strict, incl = masks
    c = rts[0].shape[0]
    c2 = 2 * c
    zero = jnp.zeros_like(rts[0])
    each = lambda f, *ls: [f(*a) for a in zip(*ls)]

    def stack(x):
        return jnp.concatenate([jnp.where(lo, x, zero), jnp.where(lo, zero, x)], axis=0)

    ar = each(lambda a, r: jnp.concatenate([stack(a), stack(r)], axis=0), ats, rts)
    bk = each(lambda b, k: jnp.concatenate([stack(b), stack(k)], axis=0), bts, kts)
    vb = each(stack, vvs)
    gram = each(lambda x, y: lax.dot_general(x, y, _NT, preferred_element_type=F32), ar, bk)
    ars = each(lambda x, s: lax.dot_general(x, s.astype(BF16), _NT, preferred_element_type=F32),
               ar, s_prevs)
    a_ab = each(lambda g: jnp.where(strict, g[:c2, :c2], 0.0), gram)
    rhs = each(lambda g, v, x: x[:c2] + _dot(jnp.where(strict, g[:c2, c2:], 0.0).astype(BF16), v),
               gram, vb, ars)

    u = rhs
    nk = a_ab
    steps = (c - 1).bit_length()
    for k in range(steps):
        nkb = each(lambda n: n.astype(BF16), nk)
        if k + 1 < steps:
            prod = each(lambda n, uu: _dot(n, jnp.concatenate([n, uu.astype(BF16)], axis=1)),
                        nkb, u)
            nk = each(lambda p: p[:, :c2], prod)
            u = each(lambda uu, p: uu + p[:, c2:], u, prod)
        else:
            u = each(lambda uu, n: uu + _dot(n, uu.astype(BF16)), u, nkb)

    uv = each(lambda uu, v: jnp.concatenate([uu.astype(BF16), v], axis=0), u, vb)
    a_r = each(lambda g: jnp.where(incl, g[c2:, :], 0.0).astype(BF16), gram)
    ybd = each(lambda x, a, w: x[c2:] + _dot(a, w), ars, a_r, uv)
    ys = each(lambda yb: yb[:c] + yb[c:], ybd)

    bk_end = each(lambda x, pc: (x.astype(F32) * pc).astype(BF16), bk, pcs)
    s_new = each(lambda s, pc, w, x: s * pc + lax.dot_general(w, x, _TN, preferred_element_type=F32),
                 s_prevs, pcs, uv, bk_end)
    return ys, s_new


def _attn_kv(kcat, vcat, k_gain, bd, lo):
    kv_heads = kcat.shape[1] // HEAD_DIM
    kv_cols = [slice(jt * LANES, (jt + 1) * LANES) for jt in range(kv_heads // 2)]
    kn = [_head_rmsnorm(kcat[:, c], k_gain, bd) for c in kv_cols]
    kn_r = [pltpu.roll(x, HEAD_DIM, 1) for x in kn]
    vt = [vcat[:, c] for c in kv_cols]
    vt_r = [pltpu.roll(x, HEAD_DIM, 1) for x in vt]
    own = lambda hk: lo if hk % 2 == 0 else jnp.logical_not(lo)
    k2 = [jnp.where(own(hk), kn[hk // 2], kn_r[hk // 2]).astype(BF16) for hk in range(kv_heads)]
    v2 = [jnp.where(own(hk), vt[hk // 2], vt_r[hk // 2]).astype(BF16) for hk in range(kv_heads)]
    return k2, v2


def _attn_tiles(tiles, q_of, k2, v2, q_gain, bd, valid, lo, top_rows, sink_ref, tiles_per_kv):
    each = lambda f, *ls: [f(*a) for a in zip(*ls)]
    blk = valid.shape[0] // 2
    qn = [_head_rmsnorm(q_of(t).astype(F32), q_gain, bd) for t in tiles]
    qst = [jnp.concatenate([jnp.where(lo, x, 0.0), jnp.where(lo, 0.0, x)], axis=0).astype(BF16)
           for x in qn]
    s = [jnp.where(valid, lax.dot_general(x, k2[t // tiles_per_kv], _NT,
                                          preferred_element_type=F32), -jnp.inf)
         for t, x in zip(tiles, qst)]
    sink = [jnp.where(top_rows, sink_ref[2 * t], sink_ref[2 * t + 1]) for t in tiles]
    mx = each(lambda x, sk: jnp.maximum(jnp.max(x, axis=-1, keepdims=True), sk), s, sink)
    pr = each(lambda x, m: jnp.exp(x - m), s, mx)
    inv = each(lambda p, sk, m: 1.0 / (jnp.sum(p, axis=-1, keepdims=True) + jnp.exp(sk - m)),
               pr, sink, mx)
    o = [_dot((p * r).astype(BF16), v2[t // tiles_per_kv]) for t, p, r in zip(tiles, pr, inv)]
    return [jnp.where(lo, x[:blk], x[blk:]) for x in o]


def _mixers_body(sink_ref, rt_ref, at_ref, bt_ref, kt_ref, v_ref, pc_ref, g_ref, bonus_ref,
                 gnw_ref, gnb_ref, e_ref, et_ref, *refs, group):
    (kp_ref, vp_ref, kc_ref, vc_ref, qg_ref, kg_ref, bd_ref,
     oa_ref, ob_ref, s_ref, y_ref) = refs[-11:]
    q_pieces = refs[:-11]
    tr, width = rt_ref.shape
    npairs = width // LANES
    blk = WINDOW
    seq_first = pl.program_id(1) == 0

    @pl.when(seq_first)
    def _():
        s_ref[...] = jnp.zeros_like(s_ref)

    lo_c = lax.broadcasted_iota(I32, (CHUNK, LANES), 1) < HEAD_DIM
    ri = lax.broadcasted_iota(I32, (2 * CHUNK, 2 * CHUNK), 0)
    ci = lax.broadcasted_iota(I32, (2 * CHUNK, 2 * CHUNK), 1)
    same = (ri // CHUNK) == (ci // CHUNK)
    ri2 = lax.broadcasted_iota(I32, (2 * CHUNK, 4 * CHUNK), 0)
    ci2 = lax.broadcasted_iota(I32, (2 * CHUNK, 4 * CHUNK), 1) % (2 * CHUNK)
    incl = ((ri2 // CHUNK) == (ci2 // CHUNK)) & (ri2 >= ci2)
    masks = (lo_c, same & (ri > ci), incl)
    cols = [slice(pr * LANES, (pr + 1) * LANES) for pr in range(npairs)]

    def chunk_step(c):
        rows = pl.ds(pl.multiple_of(c * CHUNK, CHUNK), CHUNK)
        this_chunk = lax.broadcasted_iota(I32, (pc_ref.shape[0], 1), 0) == c
        pcs = [jnp.sum(jnp.where(this_chunk, pc_ref[:, cl], 0.0), axis=0, keepdims=True)
               for cl in cols]
        load = lambda ref: [ref[rows, cl] for cl in cols]
        ys, s_new = _chunk_pairs(load(rt_ref), load(at_ref), load(bt_ref), load(kt_ref),
                                 load(v_ref), pcs, [s_ref[pr] for pr in range(npairs)], masks)
        for pr in range(npairs):
            s_ref[pr] = s_new[pr]
            y_ref[rows, cols[pr]] = ys[pr]

    bd = bd_ref[...]
    q_gain = qg_ref[...] * (HEAD_DIM ** -0.5)
    lo = lax.broadcasted_iota(I32, (1, LANES), 1) < HEAD_DIM
    qi = lax.broadcasted_iota(I32, (2 * blk, 2 * blk), 0)
    kj = lax.broadcasted_iota(I32, (2 * blk, 2 * blk), 1)
    rel = blk + (qi % blk) - kj
    in_window = (rel >= 0) & (rel < WINDOW)
    top_rows = lax.broadcasted_iota(I32, (2 * blk, 1), 0) < blk
    q_tiles = ob_ref.shape[1] // LANES
    piece_tiles = q_pieces[0].shape[1] // LANES
    tiles_per_kv = group // 2
    chunks_per_blk = blk // CHUNK

    def block_step(jb, carry):
        cur = pl.ds(pl.multiple_of(jb * blk, blk), blk)
        prv = pl.ds(pl.multiple_of(jnp.maximum(jb - 1, 0) * blk, blk), blk)
        first_blk = jb == 0
        kprev = jnp.where(first_blk, kp_ref[...], kc_ref[prv, :])
        vprev = jnp.where(first_blk, vp_ref[...], vc_ref[prv, :])
        kcat = jnp.concatenate([kprev, kc_ref[cur, :]], axis=0).astype(F32)
        vcat = jnp.concatenate([vprev, vc_ref[cur, :]], axis=0).astype(F32)
        k2, v2 = _attn_kv(kcat, vcat, kg_ref[...], bd, lo)
        first_key = jnp.where(seq_first & first_blk, blk, 0)
        valid = in_window & (kj >= first_key)
        q_of = lambda t: q_pieces[t // piece_tiles][cur, cols[t % piece_tiles]]
        for part in range(chunks_per_blk):
            chunk_step(jb * chunks_per_blk + part)
            tiles = list(range(part * q_tiles // chunks_per_blk,
                               (part + 1) * q_tiles // chunks_per_blk))
            outs = _attn_tiles(tiles, q_of, k2, v2, q_gain, bd, valid, lo, top_rows, sink_ref,
                               tiles_per_kv)
            for t, o in zip(tiles, outs):
                ob_ref[cur, cols[t]] = o.astype(ob_ref.dtype)
        return carry

    lax.fori_loop(0, tr // blk, block_step, 0)

    e = e_ref[...]
    et = et_ref[...]
    y = y_ref[...]
    mean = _head_sum(y, e, et) * (1.0 / HEAD_DIM)
    d = y - mean
    var = _head_sum(d * d, e, et) * (1.0 / HEAD_DIM)
    out = d * lax.rsqrt(var + GN_EPS) * gnw_ref[...] + gnb_ref[...]
    out = (out + bonus_ref[...]) * g_ref[...].astype(F32)
    oa_ref[...] = out.astype(oa_ref.dtype)


def mixers(rt, at, bt, kt, vv, pc, g, bonus, gn_w, gn_b, proj, q_gain, k_gain, sinks, *, batch,
           q_col0, q_width, kv_width, tr=512):
    n, width = rt.shape
    nst = n // batch // tr
    assert q_width == width and WINDOW % CHUNK == 0 and tr % WINDOW == 0
    group = (q_width // HEAD_DIM) // (kv_width // HEAD_DIM)
    q_piece = math.gcd(q_col0, q_width)
    assert q_piece % LANES == 0 and (q_col0 + q_width) % kv_width == 0
    n_q = q_width // q_piece
    kcol = (q_col0 + q_width) // kv_width
    e, et = _head_indicator(width)
    t = jnp.arange(LANES)
    bd = ((t[:, None] // HEAD_DIM) == (t[None, :] // HEAD_DIM)).astype(BF16)
    qg = jnp.tile(q_gain.reshape(1, HEAD_DIM), (1, 2))
    kg = jnp.tile(k_gain.reshape(1, HEAD_DIM), (1, 2))
    tile = lambda rows, w, col=0: pl.BlockSpec((rows, w), lambda b, s, *_: (b * nst + s, col))
    prev = lambda col: pl.BlockSpec(
        (WINDOW, kv_width),
        lambda b, s, *_: (jnp.maximum((b * nst + s) * (tr // WINDOW) - 1, 0), col))
    const = lambda shape: pl.BlockSpec(shape, lambda b, s, *_: (0, 0))
    grid_spec = pltpu.PrefetchScalarGridSpec(
        num_scalar_prefetch=1,
        grid=(batch, nst),
        in_specs=[tile(tr, width)] * 5 + [tile(tr // CHUNK, width), tile(tr, width),
                  tile(tr, width), const((1, width)), const((1, width)), const(e.shape),
                  const(et.shape)]
                 + [tile(tr, q_piece, q_col0 // q_piece + i) for i in range(n_q)]
                 + [prev(kcol), prev(kcol + 1), tile(tr, kv_width, kcol),
                    tile(tr, kv_width, kcol + 1), const((1, LANES)), const((1, LANES)),
                    const((LANES, LANES))],
        out_specs=[tile(tr, width), tile(tr, q_width)],
        scratch_shapes=[pltpu.VMEM((width // LANES, LANES, LANES), F32),
                        pltpu.VMEM((tr, width), F32)],
    )
    return pl.pallas_call(
        functools.partial(_mixers_body, group=group),
        grid_spec=grid_spec,
        out_shape=[jax.ShapeDtypeStruct((n, width), BF16),
                   jax.ShapeDtypeStruct((n, q_width), BF16)],
        compiler_params=_cparams(("parallel", "arbitrary")),
        name="mixers",
    )(sinks, rt, at, bt, kt, vv, pc, g, bonus, gn_w, gn_b, e, et, *([proj] * (n_q + 4)), qg, kg, bd)


def _head_rmsnorm(x, gain, bd):
    hi, mid = _split2(x * x)
    ms = (_dot(hi, bd) + _dot(mid, bd)) * (1.0 / HEAD_DIM)
    return x * lax.rsqrt(ms + NORM_EPS) * gain


def _merge_route_body(a_ref, b_ref, *refs, gate_pieces):
    gates = refs[:2 * gate_pieces]
    (x_ref, pr_ref, pa_ref, wo_ref, n2_ref, rhm_ref, rh_ref, rb_ref,
     x1_ref, h2_ref, route_ref, logits_ref) = refs[2 * gate_pieces:]
    gate = lambda pieces: jnp.concatenate([g[...] for g in pieces], axis=1).astype(F32)
    @pl.when(pl.program_id(0) == 0)
    def _():
        logits_ref[...] = jnp.zeros_like(logits_ref)

    _route(logits_ref[...], route_ref)

    merged = (gate(gates[:gate_pieces]) * _dot(a_ref[...], pr_ref[...])
              + gate(gates[gate_pieces:]) * _dot(b_ref[...], pa_ref[...]))
    x1 = x_ref[...] + _dot(merged.astype(BF16), wo_ref[...])
    x1_ref[...] = x1
    ms = jnp.mean(x1 * x1, axis=-1, keepdims=True)
    h2 = x1 * lax.rsqrt(ms + NORM_EPS) * n2_ref[...]
    h2_ref[...] = _pack_bf16_halves(h2)

    hi, mid = _split2(h2)
    hw = _dot(hi, rhm_ref[...])
    logits_ref[...] = hw[:, :LANES] + (hw[:, LANES:] + _dot(mid, rh_ref[...])) + rb_ref[...]


def _route(logits, route_ref):
    lane = lax.broadcasted_iota(I32, logits.shape, 1)
    big = jnp.int32(1 << 20)
    neg = -jnp.inf
    is_coarse = lane < N_GROUPS
    cl = jnp.where(is_coarse, logits, neg)
    ce = jnp.exp(cl - jnp.max(cl, axis=-1, keepdims=True))
    cp = ce / jnp.sum(ce, axis=-1, keepdims=True)
    g_prob = jnp.max(cp, axis=-1, keepdims=True)
    g_idx = jnp.min(jnp.where(is_coarse & (cp == g_prob), lane, big), axis=-1, keepdims=True)

    fine_lane = lane - N_GROUPS
    in_group = ((lane >= N_GROUPS) & (lane < N_GROUPS + N_GROUPS * GROUP_SIZE)
                & ((fine_lane // GROUP_SIZE) == g_idx))
    fl = jnp.where(in_group, logits, neg)
    fe = jnp.exp(fl - jnp.max(fl, axis=-1, keepdims=True))
    fp = fe / jnp.sum(fe, axis=-1, keepdims=True)
    p1 = jnp.max(jnp.where(in_group, fp, -1.0), axis=-1, keepdims=True)
    i1 = jnp.min(jnp.where(in_group & (fp == p1), lane, big), axis=-1, keepdims=True)
    rest = in_group & (lane != i1)
    p2 = jnp.max(jnp.where(rest, fp, -1.0), axis=-1, keepdims=True)
    i2 = jnp.min(jnp.where(rest & (fp == p2), lane, big), axis=-1, keepdims=True)
    den = p1 + p2
    w1 = g_prob * p1 / den
    w2 = g_prob * p2 / den
    e1 = (i1 - N_GROUPS).astype(F32)
    e2 = (i2 - N_GROUPS).astype(F32)
    route_ref[...] = jnp.where(lane == 0, w1, jnp.where(lane == 1, w2,
                     jnp.where(lane == 2, e1, jnp.where(lane == 3, e2, 0.0))))


def merge_route(a_out, b_out, proj, gate_col0, x2d, proj_r, proj_a, w_out, norm2_w, router_w,
                router_b, *, tm=256):
    n, d = x2d.shape
    g_piece = math.gcd(gate_col0, d)
    assert g_piece % LANES == 0
    gate_pieces = d // g_piece
    wa = a_out.shape[1]
    rh, rm = _split2(router_w)
    rhm = jnp.concatenate([rh, rm], axis=1)
    nt = n // tm
    const = lambda arr: pl.BlockSpec(arr.shape, lambda i: (0, 0), pipeline_mode=pl.Buffered(1))
    row = lambda w, col=0: pl.BlockSpec((tm, w), lambda i: (jnp.minimum(i, nt - 1), col))
    return pl.pallas_call(
        functools.partial(_merge_route_body, gate_pieces=gate_pieces),
        grid=(nt + 1,),
        in_specs=[row(wa), row(wa)]
                 + [row(g_piece, gate_col0 // g_piece + t) for t in range(2 * gate_pieces)]
                 + [row(d),
                  const(proj_r), const(proj_a), const(w_out), const(norm2_w),
                  const(rhm), const(rh), const(router_b)],
        out_specs=[row(d), row(d // 2),
                   pl.BlockSpec((tm, LANES), lambda i: (jnp.maximum(i - 1, 0), 0))],
        scratch_shapes=[pltpu.VMEM((tm, LANES), F32)],
        out_shape=[jax.ShapeDtypeStruct((n, d), F32), jax.ShapeDtypeStruct((n, d // 2), jnp.uint32),
                   jax.ShapeDtypeStruct((n, LANES), F32)],
        compiler_params=_cparams(("arbitrary",)),
        name="merge_route",
    )(a_out, b_out, *([proj] * (2 * gate_pieces)), x2d, proj_r, proj_a, w_out, norm2_w, rhm, rh,
      router_b)


def _moe_rank_body(ids_ref, ut_ref, rank_ref, counts_ref, carry_ref):
    @pl.when(pl.program_id(0) == 0)
    def _():
        carry_ref[...] = jnp.zeros_like(carry_ref)

    ids = ids_ref[...]
    n_exp = carry_ref.shape[0]
    sub = lax.broadcasted_iota(I32, (n_exp, ids.shape[1]), 0)
    onehot = (sub == ids).astype(F32)
    before = _dot(onehot.astype(BF16), ut_ref[...])
    carry = carry_ref[...]
    rank_ref[...] = jnp.sum(onehot * (before + carry), axis=0, keepdims=True).astype(I32)
    carry = carry + jnp.sum(onehot, axis=1, keepdims=True)
    carry_ref[...] = carry
    counts_ref[...] = jnp.broadcast_to(carry, counts_ref.shape)


def moe_rank(ids_row, n_experts, *, tb=1024):
    m = ids_row.shape[1]
    t = jnp.arange(tb)
    ut = (t[:, None] < t[None, :]).astype(BF16)
    return pl.pallas_call(
        _moe_rank_body,
        grid=(m // tb,),
        in_specs=[pl.BlockSpec((1, tb), lambda i: (0, i)),
                  pl.BlockSpec((tb, tb), lambda i: (0, 0))],
        out_specs=[pl.BlockSpec((1, tb), lambda i: (0, i)),
                   pl.BlockSpec((n_experts, LANES), lambda i: (0, 0))],
        out_shape=[jax.ShapeDtypeStruct((1, m), I32),
                   jax.ShapeDtypeStruct((n_experts, LANES), F32)],
        scratch_shapes=[pltpu.VMEM((n_experts, 1), F32)],
        compiler_params=_cparams(("arbitrary",)),
        name="moe_rank",
    )(ids_row, ut)


def _moe_dest_body(ids_ref, rank_ref, start_ref, dest_ref):
    ids = ids_ref[...]
    sub = lax.broadcasted_iota(I32, (start_ref.shape[0], ids.shape[1]), 0)
    start = jnp.sum(jnp.where(sub == ids, start_ref[...], 0.0), axis=0, keepdims=True)
    dest_ref[...] = start.astype(I32) + rank_ref[...]


def moe_dest(ids_row, rank_row, pad_start_col, *, tb=2048):
    m = ids_row.shape[1]
    row = pl.BlockSpec((1, tb), lambda i: (0, i))
    return pl.pallas_call(
        _moe_dest_body,
        grid=(m // tb,),
        in_specs=[row, row, pl.BlockSpec(pad_start_col.shape, lambda i: (0, 0))],
        out_specs=row,
        out_shape=jax.ShapeDtypeStruct((1, m), I32),
        compiler_params=_cparams(("parallel",)),
        name="moe_dest",
    )(ids_row, rank_row, pad_start_col)


def _moe_dispatch_body(dest_ref, start_ref, end_ref, h_ref, buf_ref, zeros_ref, sem, zsem):
    tm = h_ref.shape[0] * h_ref.shape[1]
    base = pl.program_id(0) * tm * TOP_K

    @pl.when(pl.program_id(0) == 0)
    def _():
        zeros_ref[...] = jnp.zeros_like(zeros_ref)

        def zero_last_block(e):
            last = pl.multiple_of(end_ref[e] - MOE_BLOCK, MOE_BLOCK)
            return pltpu.make_async_copy(zeros_ref, buf_ref.at[pl.ds(last, MOE_BLOCK)], zsem)

        def zstart(e, c):
            @pl.when(end_ref[e] > start_ref[e])
            def _():
                zero_last_block(e).start()
            return c

        def zwait(e, c):
            @pl.when(end_ref[e] > start_ref[e])
            def _():
                zero_last_block(e).wait()
            return c

        lax.fori_loop(0, start_ref.shape[0], zstart, 0)
        lax.fori_loop(0, start_ref.shape[0], zwait, 0)

        tail0 = end_ref[end_ref.shape[0] - 1]

        def zero_tail_block(r):
            first = pl.multiple_of(tail0 + r * MOE_BLOCK, MOE_BLOCK)
            return pltpu.make_async_copy(zeros_ref, buf_ref.at[pl.ds(first, MOE_BLOCK)], zsem)

        def tstart(r, c):
            zero_tail_block(r).start()
            return c

        def twait(r, c):
            zero_tail_block(r).wait()
            return c

        n_tail = (buf_ref.shape[0] - tail0) // MOE_BLOCK
        lax.fori_loop(0, n_tail, tstart, 0)
        lax.fori_loop(0, n_tail, twait, 0)

    def for_rows(fn):
        def body(q, c):
            for u in range(SUBLANES):
                for k in range(TOP_K):
                    slot = dest_ref[base + (q * SUBLANES + u) * TOP_K + k]
                    fn(pltpu.make_async_copy(h_ref.at[q, pl.ds(u, 1)], buf_ref.at[pl.ds(slot, 1)],
                                             sem))
            return c
        lax.fori_loop(0, h_ref.shape[0], body, 0)

    for_rows(lambda cp: cp.start())
    for_rows(lambda cp: cp.wait())


def moe_dispatch(dest, pad_start, pad_end, h2, n_slots, *, tm=256):
    n, d = h2.shape
    h2 = h2.reshape(n // SUBLANES, SUBLANES, d)
    grid_spec = pltpu.PrefetchScalarGridSpec(
        num_scalar_prefetch=3,
        grid=(n // tm,),
        in_specs=[pl.BlockSpec((tm // SUBLANES, SUBLANES, d), lambda i, *_: (i, 0, 0))],
        out_specs=pl.BlockSpec(memory_space=pl.ANY),
        scratch_shapes=[pltpu.VMEM((MOE_BLOCK, d), h2.dtype),
                        pltpu.SemaphoreType.DMA(()), pltpu.SemaphoreType.DMA(())],
    )
    return pl.pallas_call(
        _moe_dispatch_body,
        grid_spec=grid_spec,
        out_shape=jax.ShapeDtypeStruct((n_slots, d), h2.dtype),
        compiler_params=_cparams(("arbitrary",)),
        name="moe_dispatch",
    )(dest, pad_start, pad_end, h2)


def _moe_expert_body(sbe_ref, row0_ref, nb_ref, tot_ref, wg_ref, wu_ref, wd_ref, buf_ref, out_ref,
                     wgb, wub, wdb, xp, acc, pending, sem_in, sem_out):
    del sbe_ref
    s = pl.program_id(0)
    f = pl.program_id(1)
    n_sb = pl.num_programs(0)
    last_f = pl.num_programs(1) - 1
    nb = nb_ref[s]
    half = xp.shape[2]

    def blk(first_row, r):
        return pl.ds(pl.multiple_of(first_row + r * MOE_BLOCK, MOE_BLOCK), MOE_BLOCK)

    def rows_in(sb, r):
        return pltpu.make_async_copy(buf_ref.at[blk(row0_ref[sb], r)], xp.at[sb % 2, blk(0, r)],
                                     sem_in.at[sb % 2])

    def rows_out(r):
        return pltpu.make_async_copy(acc.at[blk(0, r)], out_ref.at[blk(row0_ref[s], r)], sem_out)

    def for_blocks(n, fn):
        def body(r, carry):
            fn(r)
            return carry
        lax.fori_loop(0, n, body, 0)

    def drain_out():
        for_blocks(pending[0], lambda r: rows_out(0).wait())
        pending[0] = 0

    def mlp_rows(first_row, n_rows, first):
        rows = pl.ds(pl.multiple_of(first_row, MOE_BLOCK), n_rows)
        xlo, xhi = _unpack_bf16_halves(xp[s % 2, rows, :])
        hg = _dot(xlo, wgb[:half, :]) + _dot(xhi, wgb[half:, :])
        hu = _dot(xlo, wub[:half, :]) + _dot(xhi, wub[half:, :])
        y = _dot((jax.nn.silu(hg) * hu).astype(BF16), wdb[...])
        if first:
            acc[rows, :] = y
        else:
            acc[rows, :] += y

    def all_rows(first):
        for_blocks(nb // 2, lambda q: mlp_rows(q * (2 * MOE_BLOCK), 2 * MOE_BLOCK, first))

        @pl.when(nb % 2 == 1)
        def _():
            mlp_rows((nb - 1) * MOE_BLOCK, MOE_BLOCK, first)

    @pl.when((s == 0) & (f == 0))
    def _():
        pending[0] = 0
        for_blocks(nb, lambda r: rows_in(0, r).start())

    @pl.when(nb > 0)
    def _():
        @pl.when(f == 0)
        def _():
            nxt = jnp.minimum(s + 1, n_sb - 1)
            nb_next = jnp.where(s + 1 < n_sb, nb_ref[nxt], 0)
            for_blocks(nb_next, lambda r: rows_in(nxt, r).start())

        wgb[...] = wg_ref[0].astype(BF16)
        wub[...] = wu_ref[0].astype(BF16)
        wdb[...] = wd_ref[0].astype(BF16)

        @pl.when(f == 0)
        def _():
            for_blocks(nb, lambda r: rows_in(s, r).wait())
            drain_out()
            all_rows(True)

        @pl.when(f > 0)
        def _():
            all_rows(False)

        @pl.when(f == last_f)
        def _():
            for_blocks(nb, lambda r: rows_out(r).start())
            pending[0] = nb

    @pl.when((s == n_sb - 1) & (f == last_f))
    def _():
        drain_out()
        acc[pl.ds(0, MOE_BLOCK), :] = jnp.zeros((MOE_BLOCK, acc.shape[1]), acc.dtype)
        tail0 = tot_ref[0]

        def zero_out(r):
            return pltpu.make_async_copy(acc.at[pl.ds(0, MOE_BLOCK)], out_ref.at[blk(tail0, r)],
                                         sem_out)

        ntail = (out_ref.shape[0] - tail0) // MOE_BLOCK
        for_blocks(ntail, lambda r: zero_out(r).start())
        for_blocks(ntail, lambda r: zero_out(r).wait())


def moe_experts(sb_expert, sb_row0, sb_blocks, total_rows, buf, wg, wu, wd):
    p, half = buf.shape
    d = 2 * half
    ff = wg.shape[2]
    nf = ff // FF_TILE
    f_eff = lambda s, f, nb: jnp.where(nb[s] > 0, f, nf - 1)
    grid_spec = pltpu.PrefetchScalarGridSpec(
        num_scalar_prefetch=4,
        grid=(sb_expert.shape[0], nf),
        in_specs=[pl.BlockSpec((1, d, FF_TILE), lambda s, f, e, r0, nb, t: (e[s], 0, f_eff(s, f, nb))),
                  pl.BlockSpec((1, d, FF_TILE), lambda s, f, e, r0, nb, t: (e[s], 0, f_eff(s, f, nb))),
                  pl.BlockSpec((1, FF_TILE, d), lambda s, f, e, r0, nb, t: (e[s], f_eff(s, f, nb), 0)),
                  pl.BlockSpec(memory_space=pl.ANY)],
        out_specs=pl.BlockSpec(memory_space=pl.ANY),
        scratch_shapes=[pltpu.VMEM((d, FF_TILE), BF16), pltpu.VMEM((d, FF_TILE), BF16),
                        pltpu.VMEM((FF_TILE, d), BF16), pltpu.VMEM((2, SUPER_ROWS, half), jnp.uint32),
                        pltpu.VMEM((SUPER_ROWS, d), F32), pltpu.SMEM((1,), I32),
                        pltpu.SemaphoreType.DMA((2,)), pltpu.SemaphoreType.DMA(())],
    )
    return pl.pallas_call(
        _moe_expert_body,
        grid_spec=grid_spec,
        out_shape=jax.ShapeDtypeStruct((p, d), F32),
        compiler_params=_cparams(("arbitrary", "arbitrary")),
        name="moe_experts",
    )(sb_expert, sb_row0, sb_blocks, total_rows, wg, wu, wd, buf)


def _moe_combine_body(dest_ref, x1_ref, route_ref, out_ref, y_ref, rows_ref, sem):
    tm, d = x1_ref.shape
    i = pl.program_id(0)
    sublanes = rows_ref.shape[3]

    def row_copy(tile, q, u, k):
        slot = dest_ref[(tile * tm + q * sublanes + u) * TOP_K + k]
        return pltpu.make_async_copy(out_ref.at[pl.ds(slot, 1)],
                                     rows_ref.at[tile % 2, k, q, pl.ds(u, 1)], sem.at[tile % 2])

    def for_rows(tile, fn):
        def body(q, c):
            for u in range(sublanes):
                for k in range(TOP_K):
                    fn(row_copy(tile, q, u, k))
            return c
        lax.fori_loop(0, tm // sublanes, body, 0)

    @pl.when(i == 0)
    def _():
        for_rows(0, lambda cp: cp.start())

    @pl.when(i + 1 < pl.num_programs(0))
    def _():
        for_rows(i + 1, lambda cp: cp.start())

    for_rows(i, lambda cp: cp.wait())
    route = route_ref[...]
    rows = rows_ref[i % 2]
    y_ref[...] = x1_ref[...] + (rows[0].reshape(tm, d) * route[:, 0:1]
                                + rows[1].reshape(tm, d) * route[:, 1:2])


def moe_combine(dest, x1, route, expert_out, *, tm=256):
    n, d = x1.shape
    grid_spec = pltpu.PrefetchScalarGridSpec(
        num_scalar_prefetch=1,
        grid=(n // tm,),
        in_specs=[pl.BlockSpec((tm, d), lambda i, *_: (i, 0)),
                  pl.BlockSpec((tm, LANES), lambda i, *_: (i, 0)),
                  pl.BlockSpec(memory_space=pl.ANY)],
        out_specs=pl.BlockSpec((tm, d), lambda i, *_: (i, 0)),
        scratch_shapes=[pltpu.VMEM((2, TOP_K, tm // SUBLANES, SUBLANES, d), F32),
                        pltpu.SemaphoreType.DMA((2,))],
    )
    return pl.pallas_call(
        _moe_combine_body,
        grid_spec=grid_spec,
        out_shape=jax.ShapeDtypeStruct((n, d), F32),
        compiler_params=_cparams(("arbitrary",)),
        name="moe_combine",
    )(dest, x1, route, expert_out)


def hierarchical_moe(x1, h2, route, wg, wu, wd):
    n = h2.shape[0]
    n_experts = wg.shape[0]
    m = n * TOP_K
    ids_row = route[:, 2:4].astype(I32).reshape(1, m)
    rank_row, counts = moe_rank(ids_row, n_experts)
    counts = counts[:, 0].astype(I32)
    padded = (counts + MOE_BLOCK - 1) // MOE_BLOCK * MOE_BLOCK
    pad_end = jnp.cumsum(padded)
    pad_start = pad_end - padded
    dest = moe_dest(ids_row, rank_row, pad_start.astype(F32).reshape(n_experts, 1)).reshape(m)
    n_slots = m + n_experts * MOE_BLOCK
    n_sb = (padded + SUPER_ROWS - 1) // SUPER_ROWS
    sb_end = jnp.cumsum(n_sb)
    sb_start = sb_end - n_sb
    s_idx = jnp.arange(n_experts + m // SUPER_ROWS, dtype=I32)
    used = s_idx < sb_end[-1]
    e_of = jnp.minimum(jnp.sum(sb_end[None, :] <= s_idx[:, None], axis=1), n_experts - 1)
    piece = s_idx - sb_start[e_of]
    sb_row0 = jnp.where(used, pad_start[e_of] + piece * SUPER_ROWS, 0).astype(I32)
    sb_rows = jnp.clip(padded[e_of] - piece * SUPER_ROWS, 0, SUPER_ROWS)
    sb_blocks = jnp.where(used, sb_rows // MOE_BLOCK, 0).astype(I32)
    sb_expert = jnp.where(used, e_of, e_of[jnp.maximum(sb_end[-1] - 1, 0)]).astype(I32)
    buf = moe_dispatch(dest, pad_start.astype(I32), pad_end.astype(I32), h2, n_slots)
    expert_out = moe_experts(sb_expert, sb_row0, sb_blocks, pad_end[-1:].astype(I32), buf,
                             wg, wu, wd)
    return moe_combine(dest, x1, route, expert_out)


def _pad_cols(w, to):
    return jnp.pad(w, ((0, 0), (0, to - w.shape[1])))


def _pad_rows(w, to):
    return jnp.pad(w, ((0, to - w.shape[0]), (0, 0)))


def _layer(x2d, batch, norm1_w, w_in, mu, w0, w2, a0, a2, g2, k_k, k_a, r_k, gn_w, gn_b,
           q_norm_w, k_norm_w, sinks, proj_rwkv, proj_attn, w_out, norm2_w, wc, bc, wf, bf,
           wg, wu, wd):
    n, d = x2d.shape
    seq = n // batch
    width = w0.shape[0]
    dl, il, gl = w2.shape[0], a2.shape[0], g2.shape[0]
    q_width = proj_attn.shape[0]
    rwkv_cols = 3 * width + dl + il + gl
    kv_width = (w_in.shape[1] - rwkv_cols - q_width - 2 * d) // 2
    row = lambda v: v.reshape(1, -1).astype(F32)

    c0 = 3 * width
    w_all = jnp.concatenate([w_in[:, :c0], _pad_cols(w_in[:, c0:c0 + dl], LANES),
                             _pad_cols(w_in[:, c0 + dl:c0 + dl + il], LANES),
                             w_in[:, c0 + dl + il:]], axis=1).astype(BF16)
    mu_p = jnp.concatenate([mu[:c0], jnp.pad(mu[c0:c0 + dl], (0, LANES - dl)),
                            jnp.pad(mu[c0 + dl:c0 + dl + il], (0, LANES - il)),
                            mu[c0 + dl + il:]]).reshape(1, -1)
    rwkv_w = c0 + 2 * LANES + gl
    proj = norm_proj(x2d, row(norm1_w), w_all, 2 * d)

    rt, at, bt, kt, vv, g, bonus, pc = rwkv_prep(
        proj, mu_p, row(w0), _pad_rows(w2, LANES).astype(BF16), row(a0),
        _pad_rows(a2, LANES).astype(BF16), g2.astype(BF16), row(k_k), row(k_a), row(r_k),
        seq_len=seq)
    a_out, b_out = mixers(rt, at, bt, kt, vv, pc, g, bonus, row(gn_w), row(gn_b), proj, q_norm_w,
                          k_norm_w, sinks.astype(F32), batch=batch, q_col0=rwkv_w,
                          q_width=q_width, kv_width=kv_width)

    n_groups, n_experts = wc.shape[1], wf.shape[1]
    router_w = _pad_cols(jnp.concatenate([wc, wf], axis=1), LANES)
    router_b = _pad_cols(jnp.concatenate([bc, bf]).reshape(1, -1), LANES)
    assert n_groups == N_GROUPS and n_experts == N_GROUPS * GROUP_SIZE
    x1, h2, route = merge_route(a_out, b_out, proj, rwkv_w + q_width + 2 * kv_width, x2d,
                                proj_rwkv.astype(BF16),
                                proj_attn.astype(BF16), w_out.astype(BF16), row(norm2_w),
                                router_w, router_b)
    return hierarchical_moe(x1, h2, route, wg, wu, wd)


def kernel(x, norm1_w, w_in, rwkv_mu, rwkv_w0, rwkv_w2, rwkv_a0, rwkv_a2, rwkv_g2, rwkv_k_k,
           rwkv_k_a, rwkv_r_k, rwkv_gn_w, rwkv_gn_b, q_norm_w, k_norm_w, attn_sinks, proj_rwkv,
           proj_attn, w_out, norm2_w, router_coarse_w, router_coarse_b, router_fine_w,
           router_fine_b, expert_w_gate, expert_w_up, expert_w_down):
    batch, seq, d = x.shape
    x2d = x.reshape(batch * seq, d)
    for layer in range(norm1_w.shape[0]):
        x2d = _layer(x2d, batch, norm1_w[layer], w_in[layer], rwkv_mu[layer], rwkv_w0[layer],
                     rwkv_w2[layer], rwkv_a0[layer], rwkv_a2[layer], rwkv_g2[layer],
                     rwkv_k_k[layer], rwkv_k_a[layer], rwkv_r_k[layer].reshape(-1),
                     rwkv_gn_w[layer], rwkv_gn_b[layer], q_norm_w[layer], k_norm_w[layer],
                     attn_sinks[layer], proj_rwkv[layer], proj_attn[layer], w_out[layer],
                     norm2_w[layer], router_coarse_w[layer], router_coarse_b[layer],
                     router_fine_w[layer], router_fine_b[layer], expert_w_gate[layer],
                     expert_w_up[layer], expert_w_down[layer])
    return x2d.reshape(batch, seq, d)
```

```python
import functools
import math

import jax
import jax.numpy as jnp
from jax import lax
from jax.experimental import pallas as pl
from jax.experimental.pallas import tpu as pltpu

F32 = jnp.float32
BF16 = jnp.bfloat16
I32 = jnp.int32

NORM_EPS = 1e-6
GN_EPS = 64e-5
HEAD_DIM = 64
LANES = 128
SUBLANES = 8
CHUNK = 64
WINDOW = 128
MOE_BLOCK = 128
SUPER_ROWS = 1024
FF_TILE = 512
TOP_K = 2
N_GROUPS = 8
GROUP_SIZE = 8
VMEM_LIMIT = 56 * 1024 * 1024

_NT = (((1,), (1,)), ((), ()))
_TN = (((0,), (0,)), ((), ()))


def _dot(a, b):
    return jnp.dot(a, b, preferred_element_type=F32)


def _split2(x):
    hi = x.astype(BF16)
    mid = (x - hi.astype(F32)).astype(BF16)
    return hi, mid


def _select_sum(m, hi, mid):
    return _dot(m, hi) + _dot(m, mid)


def _pack_bf16_halves(x):
    w = x.shape[1] // 2
    lo = lax.bitcast_convert_type(x[:, :w].astype(BF16).astype(F32), jnp.uint32)
    hi = lax.bitcast_convert_type(x[:, w:].astype(BF16).astype(F32), jnp.uint32)
    return (lo >> 16) | (hi & jnp.uint32(0xFFFF0000))


def _unpack_bf16_halves(xp):
    lo = lax.bitcast_convert_type(xp << 16, F32).astype(BF16)
    hi = lax.bitcast_convert_type(xp & jnp.uint32(0xFFFF0000), F32).astype(BF16)
    return lo, hi


def _cparams(sem, vmem=VMEM_LIMIT):
    return pltpu.CompilerParams(dimension_semantics=sem, vmem_limit_bytes=vmem)


def _norm_proj_body(x_ref, g_ref, w_ref, o_ref, h_ref, *, first_gate_tile):
    j = pl.program_id(1)

    @pl.when(j == 0)
    def _():
        x = x_ref[...]
        ms = jnp.mean(x * x, axis=-1, keepdims=True)
        h_ref[...] = (x * lax.rsqrt(ms + NORM_EPS) * g_ref[...]).astype(BF16)

    acc = _dot(h_ref[...], w_ref[...])

    @pl.when(j < first_gate_tile)
    def _():
        o_ref[...] = acc.astype(o_ref.dtype)

    @pl.when(j >= first_gate_tile)
    def _():
        o_ref[...] = jax.nn.sigmoid(acc).astype(o_ref.dtype)


def norm_proj(x2d, gain, w, gate_cols, *, tm=1024, tn=1024):
    n, d = x2d.shape
    c = w.shape[1]
    tm = min(tm, n)
    assert c % tn == 0 and gate_cols % tn == 0
    return pl.pallas_call(
        functools.partial(_norm_proj_body, first_gate_tile=(c - gate_cols) // tn),
        grid=(n // tm, c // tn),
        in_specs=[pl.BlockSpec((tm, d), lambda i, j: (i, 0)),
                  pl.BlockSpec((1, d), lambda i, j: (0, 0)),
                  pl.BlockSpec((d, tn), lambda i, j: (0, j))],
        out_specs=pl.BlockSpec((tm, tn), lambda i, j: (i, j)),
        out_shape=jax.ShapeDtypeStruct((n, c), BF16),
        scratch_shapes=[pltpu.VMEM((tm, d), BF16)],
        compiler_params=_cparams(("parallel", "arbitrary")),
        name="norm_proj",
    )(x2d, gain, w)


def _head_sum(x, e, et):
    s = _dot(x.astype(BF16), e)
    hi, mid = _split2(s)
    return _dot(hi, et) + _dot(mid, et)


def _rwkv_prep_body(p_ref, pprev_ref, mu_ref, w0_ref, w2_ref, a0_ref, a2_ref, g2_ref,
                    kk_ref, ka_ref, rk_ref, e_ref, et_ref, tri_ref, sel_ref,
                    rt_ref, at_ref, bt_ref, kt_ref, v_ref, g_ref, bonus_ref, pc_ref,
                    *, seq_len, width):
    tm = p_ref.shape[0]
    w_ = width
    first = (pl.program_id(0) * tm) % seq_len == 0
    p = p_ref[...].astype(F32)
    last = pprev_ref.shape[0] - 1
    prev_row = jnp.where(first, 0.0, pprev_ref[last:, :].astype(F32))
    row = lax.broadcasted_iota(I32, (tm, 1), 0)
    shifted = jnp.where(row == 0, prev_row, pltpu.roll(p, 1, 0))
    m = p + (shifted - p) * mu_ref[...]
    r = m[:, 0:w_]
    k = m[:, w_:2 * w_]
    v = m[:, 2 * w_:3 * w_]
    xw = m[:, 3 * w_:3 * w_ + 128]
    xa = m[:, 3 * w_ + 128:3 * w_ + 256]
    xg = m[:, 3 * w_ + 256:]

    z = -(w0_ref[...] + _dot(jnp.tanh(xw).astype(BF16), w2_ref[...]))
    softplus = jnp.maximum(z, 0.0) + jnp.log1p(jnp.exp(-jnp.abs(z)))
    logw = -jnp.exp(-softplus - 0.5)
    a = jax.nn.sigmoid(a0_ref[...] + _dot(xa.astype(BF16), a2_ref[...]))
    g = _dot(jax.nn.sigmoid(xg).astype(BF16), g2_ref[...])

    e = e_ref[...]
    et = et_ref[...]
    kk = k * kk_ref[...]
    kk = kk * jnp.minimum(lax.rsqrt(_head_sum(kk * kk, e, et)), 1e12)
    kmod = k * (1.0 + (a - 1.0) * ka_ref[...])
    bonus = _head_sum(r * kmod * rk_ref[...], e, et) * v

    lw_hi, lw_mid = _split2(logw)
    cum = _select_sum(tri_ref[...], lw_hi, lw_mid)
    pc_ref[...] = jnp.exp(_select_sum(sel_ref[...], lw_hi, lw_mid))
    inv = jnp.exp(-cum)
    rt_ref[...] = (r * jnp.exp(cum)).astype(BF16)
    at_ref[...] = (-kk * jnp.exp(cum - logw)).astype(BF16)
    bt_ref[...] = (kk * a * inv).astype(BF16)
    kt_ref[...] = (kmod * inv).astype(BF16)
    v_ref[...] = v.astype(BF16)
    g_ref[...] = g.astype(BF16)
    bonus_ref[...] = bonus


def _head_indicator(width):
    heads = width // HEAD_DIM
    c = jnp.arange(width)[:, None] // HEAD_DIM
    h = jnp.arange(LANES)[None, :]
    e = (c == h).astype(BF16)
    assert heads <= LANES
    return e, e.T


def rwkv_prep(p, mu, w0, w2, a0, a2, g2, k_k, k_a, r_k, *, seq_len, tm=512):
    n = p.shape[0]
    cols = mu.shape[1]
    prev_rows = 2 * SUBLANES
    width = w0.shape[1]
    nchunk = tm // CHUNK
    e, et = _head_indicator(width)
    t = jnp.arange(tm)
    same = (t[:, None] // CHUNK) == (t[None, :] // CHUNK)
    tri = (same & (t[:, None] >= t[None, :])).astype(BF16)
    sel = ((t[None, :] // CHUNK) == jnp.arange(nchunk)[:, None]).astype(BF16)
    const = lambda shape: pl.BlockSpec(shape, lambda i: (0, 0))
    stream = lambda dt: jax.ShapeDtypeStruct((n, width), dt)
    outs = pl.pallas_call(
        functools.partial(_rwkv_prep_body, seq_len=seq_len, width=width),
        grid=(n // tm,),
        in_specs=[pl.BlockSpec((tm, cols), lambda i: (i, 0)),
                  pl.BlockSpec((prev_rows, cols),
                               lambda i: (jnp.maximum(i * (tm // prev_rows) - 1, 0), 0)),
                  const((1, cols)), const((1, width)), const(w2.shape), const((1, width)),
                  const(a2.shape), const(g2.shape), const((1, width)), const((1, width)),
                  const((1, width)), const(e.shape), const(et.shape), const(tri.shape),
                  const(sel.shape)],
        out_specs=[pl.BlockSpec((tm, width), lambda i: (i, 0))] * 7
                  + [pl.BlockSpec((nchunk, width), lambda i: (i, 0))],
        out_shape=[stream(BF16)] * 6 + [stream(F32),
                   jax.ShapeDtypeStruct((n // CHUNK, width), F32)],
        compiler_params=_cparams(("parallel",)),
        name="rwkv_prep",
    )(p, p, mu, w0, w2, a0, a2, g2, k_k, k_a, r_k, e, et, tri, sel)
    return outs


def _chunk_pairs(rts, ats, bts, kts, vvs, pcs, s_prevs, masks):
    lo, strict, incl = masks
    c = rts[0].shape[0]
    c2 = 2 * c
    zero = jnp.zeros_like(rts[0])
    each = lambda f, *ls: [f(*a) for a in zip(*ls)]

    def stack(x):
        return jnp.concatenate([jnp.where(lo, x, zero), jnp.where(lo, zero, x)], axis=0)

    ar = each(lambda a, r: jnp.concatenate([stack(a), stack(r)], axis=0), ats, rts)
    bk = each(lambda b, k: jnp.concatenate([stack(b), stack(k)], axis=0), bts, kts)
    vb = each(stack, vvs)
    gram = each(lambda x, y: lax.dot_general(x, y, _NT, preferred_element_type=F32), ar, bk)
    ars = each(lambda x, s: lax.dot_general(x, s.astype(BF16), _NT, preferred_element_type=F32),
               ar, s_prevs)
    a_ab = each(lambda g: jnp.where(strict, g[:c2, :c2], 0.0), gram)
    rhs = each(lambda g, v, x: x[:c2] + _dot(jnp.where(strict, g[:c2, c2:], 0.0).astype(BF16), v),
               gram, vb, ars)

    u = rhs
    nk = a_ab
    steps = (c - 1).bit_length()
    for k in range(steps):
        nkb = each(lambda n: n.astype(BF16), nk)
        if k + 1 < steps:
            prod = each(lambda n, uu: _dot(n, jnp.concatenate([n, uu.astype(BF16)], axis=1)),
                        nkb, u)
            nk = each(lambda p: p[:, :c2], prod)
            u = each(lambda uu, p: uu + p[:, c2:], u, prod)
        else:
            u = each(lambda uu, n: uu + _dot(n, uu.astype(BF16)), u, nkb)

    uv = each(lambda uu, v: jnp.concatenate([uu.astype(BF16), v], axis=0), u, vb)
    a_r = each(lambda g: jnp.where(incl, g[c2:, :], 0.0).astype(BF16), gram)
    ybd = each(lambda x, a, w: x[c2:] + _dot(a, w), ars, a_r, uv)
    ys = each(lambda yb: yb[:c] + yb[c:], ybd)

    bk_end = each(lambda x, pc: (x.astype(F32) * pc).astype(BF16), bk, pcs)
    s_new = each(lambda s, pc, w, x: s * pc + lax.dot_general(w, x, _TN, preferred_element_type=F32),
                 s_prevs, pcs, uv, bk_end)
    return ys, s_new


def _attn_kv(kcat, vcat, k_gain, bd, lo):
    kv_heads = kcat.shape[1] // HEAD_DIM
    kv_cols = [slice(jt * LANES, (jt + 1) * LANES) for jt in range(kv_heads // 2)]
    kn = [_head_rmsnorm(kcat[:, c], k_gain, bd) for c in kv_cols]
    kn_r = [pltpu.roll(x, HEAD_DIM, 1) for x in kn]
    vt = [vcat[:, c] for c in kv_cols]
    vt_r = [pltpu.roll(x, HEAD_DIM, 1) for x in vt]
    own = lambda hk: lo if hk % 2 == 0 else jnp.logical_not(lo)
    k2 = [jnp.where(own(hk), kn[hk // 2], kn_r[hk // 2]).astype(BF16) for hk in range(kv_heads)]
    v2 = [jnp.where(own(hk), vt[hk // 2], vt_r[hk // 2]).astype(BF16) for hk in range(kv_heads)]
    return k2, v2


def _attn_tiles(tiles, q_of, k2, v2, q_gain, bd, valid, lo, top_rows, sink_ref, tiles_per_kv):
    each = lambda f, *ls: [f(*a) for a in zip(*ls)]
    blk = valid.shape[0] // 2
    qn = [_head_rmsnorm(q_of(t).astype(F32), q_gain, bd) for t in tiles]
    qst = [jnp.concatenate([jnp.where(lo, x, 0.0), jnp.where(lo, 0.0, x)], axis=0).astype(BF16)
           for x in qn]
    s = [jnp.where(valid, lax.dot_general(x, k2[t // tiles_per_kv], _NT,
                                          preferred_element_type=F32), -jnp.inf)
         for t, x in zip(tiles, qst)]
    sink = [jnp.where(top_rows, sink_ref[2 * t], sink_ref[2 * t + 1]) for t in tiles]
    mx = each(lambda x, sk: jnp.maximum(jnp.max(x, axis=-1, keepdims=True), sk), s, sink)
    pr = each(lambda x, m: jnp.exp(x - m), s, mx)
    inv = each(lambda p, sk, m: 1.0 / (jnp.sum(p, axis=-1, keepdims=True) + jnp.exp(sk - m)),
               pr, sink, mx)
    o = [_dot((p * r).astype(BF16), v2[t // tiles_per_kv]) for t, p, r in zip(tiles, pr, inv)]
    return [jnp.where(lo, x[:blk], x[blk:]) for x in o]


def _mixers_body(sink_ref, rt_ref, at_ref, bt_ref, kt_ref, v_ref, pc_ref, g_ref, bonus_ref,
                 gnw_ref, gnb_ref, e_ref, et_ref, *refs, group):
    (kp_ref, vp_ref, kc_ref, vc_ref, qg_ref, kg_ref, bd_ref,
     oa_ref, ob_ref, s_ref, y_ref) = refs[-11:]
    q_pieces = refs[:-11]
    tr, width = rt_ref.shape
    npairs = width // LANES
    blk = WINDOW
    seq_first = pl.program_id(1) == 0

    @pl.when(seq_first)
    def _():
        s_ref[...] = jnp.zeros_like(s_ref)

    lo_c = lax.broadcasted_iota(I32, (CHUNK, LANES), 1) < HEAD_DIM
    ri = lax.broadcasted_iota(I32, (2 * CHUNK, 2 * CHUNK), 0)
    ci = lax.broadcasted_iota(I32, (2 * CHUNK, 2 * CHUNK), 1)
    same = (ri // CHUNK) == (ci // CHUNK)
    ri2 = lax.broadcasted_iota(I32, (2 * CHUNK, 4 * CHUNK), 0)
    ci2 = lax.broadcasted_iota(I32, (2 * CHUNK, 4 * CHUNK), 1) % (2 * CHUNK)
    incl = ((ri2 // CHUNK) == (ci2 // CHUNK)) & (ri2 >= ci2)
    masks = (lo_c, same & (ri > ci), incl)
    cols = [slice(pr * LANES, (pr + 1) * LANES) for pr in range(npairs)]

    def chunk_step(c):
        rows = pl.ds(pl.multiple_of(c * CHUNK, CHUNK), CHUNK)
        this_chunk = lax.broadcasted_iota(I32, (pc_ref.shape[0], 1), 0) == c
        pcs = [jnp.sum(jnp.where(this_chunk, pc_ref[:, cl], 0.0), axis=0, keepdims=True)
               for cl in cols]
        load = lambda ref: [ref[rows, cl] for cl in cols]
        ys, s_new = _chunk_pairs(load(rt_ref), load(at_ref), load(bt_ref), load(kt_ref),
                                 load(v_ref), pcs, [s_ref[pr] for pr in range(npairs)], masks)
        for pr in range(npairs):
            s_ref[pr] = s_new[pr]
            y_ref[rows, cols[pr]] = ys[pr]

    bd = bd_ref[...]
    q_gain = qg_ref[...] * (HEAD_DIM ** -0.5)
    lo = lax.broadcasted_iota(I32, (1, LANES), 1) < HEAD_DIM
    qi = lax.broadcasted_iota(I32, (2 * blk, 2 * blk), 0)
    kj = lax.broadcasted_iota(I32, (2 * blk, 2 * blk), 1)
    rel = blk + (qi % blk) - kj
    in_window = (rel >= 0) & (rel < WINDOW)
    top_rows = lax.broadcasted_iota(I32, (2 * blk, 1), 0) < blk
    q_tiles = ob_ref.shape[1] // LANES
    piece_tiles = q_pieces[0].shape[1] // LANES
    tiles_per_kv = group // 2
    chunks_per_blk = blk // CHUNK

    def block_step(jb, carry):
        cur = pl.ds(pl.multiple_of(jb * blk, blk), blk)
        prv = pl.ds(pl.multiple_of(jnp.maximum(jb - 1, 0) * blk, blk), blk)
        first_blk = jb == 0
        kprev = jnp.where(first_blk, kp_ref[...], kc_ref[prv, :])
        vprev = jnp.where(first_blk, vp_ref[...], vc_ref[prv, :])
        kcat = jnp.concatenate([kprev, kc_ref[cur, :]], axis=0).astype(F32)
        vcat = jnp.concatenate([vprev, vc_ref[cur, :]], axis=0).astype(F32)
        k2, v2 = _attn_kv(kcat, vcat, kg_ref[...], bd, lo)
        first_key = jnp.where(seq_first & first_blk, blk, 0)
        valid = in_window & (kj >= first_key)
        q_of = lambda t: q_pieces[t // piece_tiles][cur, cols[t % piece_tiles]]
        for part in range(chunks_per_blk):
            chunk_step(jb * chunks_per_blk + part)
            tiles = list(range(part * q_tiles // chunks_per_blk,
                               (part + 1) * q_tiles // chunks_per_blk))
            outs = _attn_tiles(tiles, q_of, k2, v2, q_gain, bd, valid, lo, top_rows, sink_ref,
                               tiles_per_kv)
            for t, o in zip(tiles, outs):
                ob_ref[cur, cols[t]] = o.astype(ob_ref.dtype)
        return carry

    lax.fori_loop(0, tr // blk, block_step, 0)

    e = e_ref[...]
    et = et_ref[...]
    y = y_ref[...]
    mean = _head_sum(y, e, et) * (1.0 / HEAD_DIM)
    d = y - mean
    var = _head_sum(d * d, e, et) * (1.0 / HEAD_DIM)
    out = d * lax.rsqrt(var + GN_EPS) * gnw_ref[...] + gnb_ref[...]
    out = (out + bonus_ref[...]) * g_ref[...].astype(F32)
    oa_ref[...] = out.astype(oa_ref.dtype)


def mixers(rt, at, bt, kt, vv, pc, g, bonus, gn_w, gn_b, proj, q_gain, k_gain, sinks, *, batch,
           q_col0, q_width, kv_width, tr=512):
    n, width = rt.shape
    nst = n // batch // tr
    assert q_width == width and WINDOW % CHUNK == 0 and tr % WINDOW == 0
    group = (q_width // HEAD_DIM) // (kv_width // HEAD_DIM)
    q_piece = math.gcd(q_col0, q_width)
    assert q_piece % LANES == 0 and (q_col0 + q_width) % kv_width == 0
    n_q = q_width // q_piece
    kcol = (q_col0 + q_width) // kv_width
    e, et = _head_indicator(width)
    t = jnp.arange(LANES)
    bd = ((t[:, None] // HEAD_DIM) == (t[None, :] // HEAD_DIM)).astype(BF16)
    qg = jnp.tile(q_gain.reshape(1, HEAD_DIM), (1, 2))
    kg = jnp.tile(k_gain.reshape(1, HEAD_DIM), (1, 2))
    tile = lambda rows, w, col=0: pl.BlockSpec((rows, w), lambda b, s, *_: (b * nst + s, col))
    prev = lambda col: pl.BlockSpec(
        (WINDOW, kv_width),
        lambda b, s, *_: (jnp.maximum((b * nst + s) * (tr // WINDOW) - 1, 0), col))
    const = lambda shape: pl.BlockSpec(shape, lambda b, s, *_: (0, 0))
    grid_spec = pltpu.PrefetchScalarGridSpec(
        num_scalar_prefetch=1,
        grid=(batch, nst),
        in_specs=[tile(tr, width)] * 5 + [tile(tr // CHUNK, width), tile(tr, width),
                  tile(tr, width), const((1, width)), const((1, width)), const(e.shape),
                  const(et.shape)]
                 + [tile(tr, q_piece, q_col0 // q_piece + i) for i in range(n_q)]
                 + [prev(kcol), prev(kcol + 1), tile(tr, kv_width, kcol),
                    tile(tr, kv_width, kcol + 1), const((1, LANES)), const((1, LANES)),
                    const((LANES, LANES))],
        out_specs=[tile(tr, width), tile(tr, q_width)],
        scratch_shapes=[pltpu.VMEM((width // LANES, LANES, LANES), F32),
                        pltpu.VMEM((tr, width), F32)],
    )
    return pl.pallas_call(
        functools.partial(_mixers_body, group=group),
        grid_spec=grid_spec,
        out_shape=[jax.ShapeDtypeStruct((n, width), BF16),
                   jax.ShapeDtypeStruct((n, q_width), BF16)],
        compiler_params=_cparams(("parallel", "arbitrary")),
        name="mixers",
    )(sinks, rt, at, bt, kt, vv, pc, g, bonus, gn_w, gn_b, e, et, *([proj] * (n_q + 4)), qg, kg, bd)


def _head_rmsnorm(x, gain, bd):
    hi, mid = _split2(x * x)
    ms = (_dot(hi, bd) + _dot(mid, bd)) * (1.0 / HEAD_DIM)
    return x * lax.rsqrt(ms + NORM_EPS) * gain


def _merge_route_body(a_ref, b_ref, *refs, gate_pieces):
    gates = refs[:2 * gate_pieces]
    (x_ref, pr_ref, pa_ref, wo_ref, n2_ref, rhm_ref, rh_ref, rb_ref,
     x1_ref, h2_ref, route_ref, logits_ref) = refs[2 * gate_pieces:]
    gate = lambda pieces: jnp.concatenate([g[...] for g in pieces], axis=1).astype(F32)
    @pl.when(pl.program_id(0) == 0)
    def _():
        logits_ref[...] = jnp.zeros_like(logits_ref)

    _route(logits_ref[...], route_ref)

    merged = (gate(gates[:gate_pieces]) * _dot(a_ref[...], pr_ref[...])
              + gate(gates[gate_pieces:]) * _dot(b_ref[...], pa_ref[...]))
    x1 = x_ref[...] + _dot(merged.astype(BF16), wo_ref[...])
    x1_ref[...] = x1
    ms = jnp.mean(x1 * x1, axis=-1, keepdims=True)
    h2 = x1 * lax.rsqrt(ms + NORM_EPS) * n2_ref[...]
    h2_ref[...] = _pack_bf16_halves(h2)

    hi, mid = _split2(h2)
    hw = _dot(hi, rhm_ref[...])
    logits_ref[...] = hw[:, :LANES] + (hw[:, LANES:] + _dot(mid, rh_ref[...])) + rb_ref[...]


def _route(logits, route_ref):
    lane = lax.broadcasted_iota(I32, logits.shape, 1)
    big = jnp.int32(1 << 20)
    neg = -jnp.inf
    is_coarse = lane < N_GROUPS
    cl = jnp.where(is_coarse, logits, neg)
    ce = jnp.exp(cl - jnp.max(cl, axis=-1, keepdims=True))
    cp = ce / jnp.sum(ce, axis=-1, keepdims=True)
    g_prob = jnp.max(cp, axis=-1, keepdims=True)
    g_idx = jnp.min(jnp.where(is_coarse & (cp == g_prob), lane, big), axis=-1, keepdims=True)

    fine_lane = lane - N_GROUPS
    in_group = ((lane >= N_GROUPS) & (lane < N_GROUPS + N_GROUPS * GROUP_SIZE)
                & ((fine_lane // GROUP_SIZE) == g_idx))
    fl = jnp.where(in_group, logits, neg)
    fe = jnp.exp(fl - jnp.max(fl, axis=-1, keepdims=True))
    fp = fe / jnp.sum(fe, axis=-1, keepdims=True)
    p1 = jnp.max(jnp.where(in_group, fp, -1.0), axis=-1, keepdims=True)
    i1 = jnp.min(jnp.where(in_group & (fp == p1), lane, big), axis=-1, keepdims=True)
    rest = in_group & (lane != i1)
    p2 = jnp.max(jnp.where(rest, fp, -1.0), axis=-1, keepdims=True)
    i2 = jnp.min(jnp.where(rest & (fp == p2), lane, big), axis=-1, keepdims=True)
    den = p1 + p2
    w1 = g_prob * p1 / den
    w2 = g_prob * p2 / den
    e1 = (i1 - N_GROUPS).astype(F32)
    e2 = (i2 - N_GROUPS).astype(F32)
    route_ref[...] = jnp.where(lane == 0, w1, jnp.where(lane == 1, w2,
                     jnp.where(lane == 2, e1, jnp.where(lane == 3, e2, 0.0))))


def merge_route(a_out, b_out, proj, gate_col0, x2d, proj_r, proj_a, w_out, norm2_w, router_w,
                router_b, *, tm=256):
    n, d = x2d.shape
    g_piece = math.gcd(gate_col0, d)
    assert g_piece % LANES == 0
    gate_pieces = d // g_piece
    wa = a_out.shape[1]
    rh, rm = _split2(router_w)
    rhm = jnp.concatenate([rh, rm], axis=1)
    nt = n // tm
    const = lambda arr: pl.BlockSpec(arr.shape, lambda i: (0, 0), pipeline_mode=pl.Buffered(1))
    row = lambda w, col=0: pl.BlockSpec((tm, w), lambda i: (jnp.minimum(i, nt - 1), col))
    return pl.pallas_call(
        functools.partial(_merge_route_body, gate_pieces=gate_pieces),
        grid=(nt + 1,),
        in_specs=[row(wa), row(wa)]
                 + [row(g_piece, gate_col0 // g_piece + t) for t in range(2 * gate_pieces)]
                 + [row(d),
                  const(proj_r), const(proj_a), const(w_out), const(norm2_w),
                  const(rhm), const(rh), const(router_b)],
        out_specs=[row(d), row(d // 2),
                   pl.BlockSpec((tm, LANES), lambda i: (jnp.maximum(i - 1, 0), 0))],
        scratch_shapes=[pltpu.VMEM((tm, LANES), F32)],
        out_shape=[jax.ShapeDtypeStruct((n, d), F32), jax.ShapeDtypeStruct((n, d // 2), jnp.uint32),
                   jax.ShapeDtypeStruct((n, LANES), F32)],
        compiler_params=_cparams(("arbitrary",)),
        name="merge_route",
    )(a_out, b_out, *([proj] * (2 * gate_pieces)), x2d, proj_r, proj_a, w_out, norm2_w, rhm, rh,
      router_b)


def _moe_rank_body(ids_ref, ut_ref, rank_ref, counts_ref, carry_ref):
    @pl.when(pl.program_id(0) == 0)
    def _():
        carry_ref[...] = jnp.zeros_like(carry_ref)

    ids = ids_ref[...]
    n_exp = carry_ref.shape[0]
    sub = lax.broadcasted_iota(I32, (n_exp, ids.shape[1]), 0)
    onehot = (sub == ids).astype(F32)
    before = _dot(onehot.astype(BF16), ut_ref[...])
    carry = carry_ref[...]
    rank_ref[...] = jnp.sum(onehot * (before + carry), axis=0, keepdims=True).astype(I32)
    carry = carry + jnp.sum(onehot, axis=1, keepdims=True)
    carry_ref[...] = carry
    counts_ref[...] = jnp.broadcast_to(carry, counts_ref.shape)


def moe_rank(ids_row, n_experts, *, tb=1024):
    m = ids_row.shape[1]
    t = jnp.arange(tb)
    ut = (t[:, None] < t[None, :]).astype(BF16)
    return pl.pallas_call(
        _moe_rank_body,
        grid=(m // tb,),
        in_specs=[pl.BlockSpec((1, tb), lambda i: (0, i)),
                  pl.BlockSpec((tb, tb), lambda i: (0, 0))],
        out_specs=[pl.BlockSpec((1, tb), lambda i: (0, i)),
                   pl.BlockSpec((n_experts, LANES), lambda i: (0, 0))],
        out_shape=[jax.ShapeDtypeStruct((1, m), I32),
                   jax.ShapeDtypeStruct((n_experts, LANES), F32)],
        scratch_shapes=[pltpu.VMEM((n_experts, 1), F32)],
        compiler_params=_cparams(("arbitrary",)),
        name="moe_rank",
    )(ids_row, ut)


def _moe_dest_body(ids_ref, rank_ref, start_ref, dest_ref):
    ids = ids_ref[...]
    sub = lax.broadcasted_iota(I32, (start_ref.shape[0], ids.shape[1]), 0)
    start = jnp.sum(jnp.where(sub == ids, start_ref[...], 0.0), axis=0, keepdims=True)
    dest_ref[...] = start.astype(I32) + rank_ref[...]


def moe_dest(ids_row, rank_row, pad_start_col, *, tb=2048):
    m = ids_row.shape[1]
    row = pl.BlockSpec((1, tb), lambda i: (0, i))
    return pl.pallas_call(
        _moe_dest_body,
        grid=(m // tb,),
        in_specs=[row, row, pl.BlockSpec(pad_start_col.shape, lambda i: (0, 0))],
        out_specs=row,
        out_shape=jax.ShapeDtypeStruct((1, m), I32),
        compiler_params=_cparams(("parallel",)),
        name="moe_dest",
    )(ids_row, rank_row, pad_start_col)


def _moe_dispatch_body(dest_ref, start_ref, end_ref, h_ref, buf_ref, zeros_ref, sem, zsem):
    tm = h_ref.shape[0] * h_ref.shape[1]
    base = pl.program_id(0) * tm * TOP_K

    @pl.when(pl.program_id(0) == 0)
    def _():
        zeros_ref[...] = jnp.zeros_like(zeros_ref)

        def zero_last_block(e):
            last = pl.multiple_of(end_ref[e] - MOE_BLOCK, MOE_BLOCK)
            return pltpu.make_async_copy(zeros_ref, buf_ref.at[pl.ds(last, MOE_BLOCK)], zsem)

        def zstart(e, c):
            @pl.when(end_ref[e] > start_ref[e])
            def _():
                zero_last_block(e).start()
            return c

        def zwait(e, c):
            @pl.when(end_ref[e] > start_ref[e])
            def _():
                zero_last_block(e).wait()
            return c

        lax.fori_loop(0, start_ref.shape[0], zstart, 0)
        lax.fori_loop(0, start_ref.shape[0], zwait, 0)

        tail0 = end_ref[end_ref.shape[0] - 1]

        def zero_tail_block(r):
            first = pl.multiple_of(tail0 + r * MOE_BLOCK, MOE_BLOCK)
            return pltpu.make_async_copy(zeros_ref, buf_ref.at[pl.ds(first, MOE_BLOCK)], zsem)

        def tstart(r, c):
            zero_tail_block(r).start()
            return c

        def twait(r, c):
            zero_tail_block(r).wait()
            return c

        n_tail = (buf_ref.shape[0] - tail0) // MOE_BLOCK
        lax.fori_loop(0, n_tail, tstart, 0)
        lax.fori_loop(0, n_tail, twait, 0)

    def for_rows(fn):
        def body(q, c):
            for u in range(SUBLANES):
                for k in range(TOP_K):
                    slot = dest_ref[base + (q * SUBLANES + u) * TOP_K + k]
                    fn(pltpu.make_async_copy(h_ref.at[q, pl.ds(u, 1)], buf_ref.at[pl.ds(slot, 1)],
                                             sem))
            return c
        lax.fori_loop(0, h_ref.shape[0], body, 0)

    for_rows(lambda cp: cp.start())
    for_rows(lambda cp: cp.wait())


def moe_dispatch(dest, pad_start, pad_end, h2, n_slots, *, tm=256):
    n, d = h2.shape
    h2 = h2.reshape(n // SUBLANES, SUBLANES, d)
    grid_spec = pltpu.PrefetchScalarGridSpec(
        num_scalar_prefetch=3,
        grid=(n // tm,),
        in_specs=[pl.BlockSpec((tm // SUBLANES, SUBLANES, d), lambda i, *_: (i, 0, 0))],
        out_specs=pl.BlockSpec(memory_space=pl.ANY),
        scratch_shapes=[pltpu.VMEM((MOE_BLOCK, d), h2.dtype),
                        pltpu.SemaphoreType.DMA(()), pltpu.SemaphoreType.DMA(())],
    )
    return pl.pallas_call(
        _moe_dispatch_body,
        grid_spec=grid_spec,
        out_shape=jax.ShapeDtypeStruct((n_slots, d), h2.dtype),
        compiler_params=_cparams(("arbitrary",)),
        name="moe_dispatch",
    )(dest, pad_start, pad_end, h2)


def _moe_expert_body(sbe_ref, row0_ref, nb_ref, tot_ref, wg_ref, wu_ref, wd_ref, buf_ref, out_ref,
                     wgb, wub, wdb, xp, acc, pending, sem_in, sem_out):
    del sbe_ref
    s = pl.program_id(0)
    f = pl.program_id(1)
    n_sb = pl.num_programs(0)
    last_f = pl.num_programs(1) - 1
    nb = nb_ref[s]
    half = xp.shape[2]

    def blk(first_row, r):
        return pl.ds(pl.multiple_of(first_row + r * MOE_BLOCK, MOE_BLOCK), MOE_BLOCK)

    def rows_in(sb, r):
        return pltpu.make_async_copy(buf_ref.at[blk(row0_ref[sb], r)], xp.at[sb % 2, blk(0, r)],
                                     sem_in.at[sb % 2])

    def rows_out(r):
        return pltpu.make_async_copy(acc.at[blk(0, r)], out_ref.at[blk(row0_ref[s], r)], sem_out)

    def for_blocks(n, fn):
        def body(r, carry):
            fn(r)
            return carry
        lax.fori_loop(0, n, body, 0)

    def drain_out():
        for_blocks(pending[0], lambda r: rows_out(0).wait())
        pending[0] = 0

    def mlp_rows(first_row, n_rows, first):
        rows = pl.ds(pl.multiple_of(first_row, MOE_BLOCK), n_rows)
        xlo, xhi = _unpack_bf16_halves(xp[s % 2, rows, :])
        hg = _dot(xlo, wgb[:half, :]) + _dot(xhi, wgb[half:, :])
        hu = _dot(xlo, wub[:half, :]) + _dot(xhi, wub[half:, :])
        y = _dot((jax.nn.silu(hg) * hu).astype(BF16), wdb[...])
        if first:
            acc[rows, :] = y
        else:
            acc[rows, :] += y

    def all_rows(first):
        for_blocks(nb // 2, lambda q: mlp_rows(q * (2 * MOE_BLOCK), 2 * MOE_BLOCK, first))

        @pl.when(nb % 2 == 1)
        def _():
            mlp_rows((nb - 1) * MOE_BLOCK, MOE_BLOCK, first)

    @pl.when((s == 0) & (f == 0))
    def _():
        pending[0] = 0
        for_blocks(nb, lambda r: rows_in(0, r).start())

    @pl.when(nb > 0)
    def _():
        @pl.when(f == 0)
        def _():
            nxt = jnp.minimum(s + 1, n_sb - 1)
            nb_next = jnp.where(s + 1 < n_sb, nb_ref[nxt], 0)
            for_blocks(nb_next, lambda r: rows_in(nxt, r).start())

        wgb[...] = wg_ref[0].astype(BF16)
        wub[...] = wu_ref[0].astype(BF16)
        wdb[...] = wd_ref[0].astype(BF16)

        @pl.when(f == 0)
        def _():
            for_blocks(nb, lambda r: rows_in(s, r).wait())
            drain_out()
            all_rows(True)

        @pl.when(f > 0)
        def _():
            all_rows(False)

        @pl.when(f == last_f)
        def _():
            for_blocks(nb, lambda r: rows_out(r).start())
            pending[0] = nb

    @pl.when((s == n_sb - 1) & (f == last_f))
    def _():
        drain_out()
        acc[pl.ds(0, MOE_BLOCK), :] = jnp.zeros((MOE_BLOCK, acc.shape[1]), acc.dtype)
        tail0 = tot_ref[0]

        def zero_out(r):
            return pltpu.make_async_copy(acc.at[pl.ds(0, MOE_BLOCK)], out_ref.at[blk(tail0, r)],
                                         sem_out)

        ntail = (out_ref.shape[0] - tail0) // MOE_BLOCK
        for_blocks(ntail, lambda r: zero_out(r).start())
        for_blocks(ntail, lambda r: zero_out(r).wait())


def moe_experts(sb_expert, sb_row0, sb_blocks, total_rows, buf, wg, wu, wd):
    p, half = buf.shape
    d = 2 * half
    ff = wg.shape[2]
    nf = ff // FF_TILE
    f_eff = lambda s, f, nb: jnp.where(nb[s] > 0, f, nf - 1)
    grid_spec = pltpu.PrefetchScalarGridSpec(
        num_scalar_prefetch=4,
        grid=(sb_expert.shape[0], nf),
        in_specs=[pl.BlockSpec((1, d, FF_TILE), lambda s, f, e, r0, nb, t: (e[s], 0, f_eff(s, f, nb))),
                  pl.BlockSpec((1, d, FF_TILE), lambda s, f, e, r0, nb, t: (e[s], 0, f_eff(s, f, nb))),
                  pl.BlockSpec((1, FF_TILE, d), lambda s, f, e, r0, nb, t: (e[s], f_eff(s, f, nb), 0)),
                  pl.BlockSpec(memory_space=pl.ANY)],
        out_specs=pl.BlockSpec(memory_space=pl.ANY),
        scratch_shapes=[pltpu.VMEM((d, FF_TILE), BF16), pltpu.VMEM((d, FF_TILE), BF16),
                        pltpu.VMEM((FF_TILE, d), BF16), pltpu.VMEM((2, SUPER_ROWS, half), jnp.uint32),
                        pltpu.VMEM((SUPER_ROWS, d), F32), pltpu.SMEM((1,), I32),
                        pltpu.SemaphoreType.DMA((2,)), pltpu.SemaphoreType.DMA(())],
    )
    return pl.pallas_call(
        _moe_expert_body,
        grid_spec=grid_spec,
        out_shape=jax.ShapeDtypeStruct((p, d), F32),
        compiler_params=_cparams(("arbitrary", "arbitrary")),
        name="moe_experts",
    )(sb_expert, sb_row0, sb_blocks, total_rows, wg, wu, wd, buf)


def _moe_combine_body(dest_ref, x1_ref, route_ref, out_ref, y_ref, rows_ref, sem):
    tm, d = x1_ref.shape
    i = pl.program_id(0)
    sublanes = rows_ref.shape[3]

    def row_copy(tile, q, u, k):
        slot = dest_ref[(tile * tm + q * sublanes + u) * TOP_K + k]
        return pltpu.make_async_copy(out_ref.at[pl.ds(slot, 1)],
                                     rows_ref.at[tile % 2, k, q, pl.ds(u, 1)], sem.at[tile % 2])

    def for_rows(tile, fn):
        def body(q, c):
            for u in range(sublanes):
                for k in range(TOP_K):
                    fn(row_copy(tile, q, u, k))
            return c
        lax.fori_loop(0, tm // sublanes, body, 0)

    @pl.when(i == 0)
    def _():
        for_rows(0, lambda cp: cp.start())

    @pl.when(i + 1 < pl.num_programs(0))
    def _():
        for_rows(i + 1, lambda cp: cp.start())

    for_rows(i, lambda cp: cp.wait())
    route = route_ref[...]
    rows = rows_ref[i % 2]
    y_ref[...] = x1_ref[...] + (rows[0].reshape(tm, d) * route[:, 0:1]
                                + rows[1].reshape(tm, d) * route[:, 1:2])


def moe_combine(dest, x1, route, expert_out, *, tm=256):
    n, d = x1.shape
    grid_spec = pltpu.PrefetchScalarGridSpec(
        num_scalar_prefetch=1,
        grid=(n // tm,),
        in_specs=[pl.BlockSpec((tm, d), lambda i, *_: (i, 0)),
                  pl.BlockSpec((tm, LANES), lambda i, *_: (i, 0)),
                  pl.BlockSpec(memory_space=pl.ANY)],
        out_specs=pl.BlockSpec((tm, d), lambda i, *_: (i, 0)),
        scratch_shapes=[pltpu.VMEM((2, TOP_K, tm // SUBLANES, SUBLANES, d), F32),
                        pltpu.SemaphoreType.DMA((2,))],
    )
    return pl.pallas_call(
        _moe_combine_body,
        grid_spec=grid_spec,
        out_shape=jax.ShapeDtypeStruct((n, d), F32),
        compiler_params=_cparams(("arbitrary",)),
        name="moe_combine",
    )(dest, x1, route, expert_out)


def hierarchical_moe(x1, h2, route, wg, wu, wd):
    n = h2.shape[0]
    n_experts = wg.shape[0]
    m = n * TOP_K
    ids_row = route[:, 2:4].astype(I32).reshape(1, m)
    rank_row, counts = moe_rank(ids_row, n_experts)
    counts = counts[:, 0].astype(I32)
    padded = (counts + MOE_BLOCK - 1) // MOE_BLOCK * MOE_BLOCK
    pad_end = jnp.cumsum(padded)
    pad_start = pad_end - padded
    dest = moe_dest(ids_row, rank_row, pad_start.astype(F32).reshape(n_experts, 1)).reshape(m)
    n_slots = m + n_experts * MOE_BLOCK
    n_sb = (padded + SUPER_ROWS - 1) // SUPER_ROWS
    sb_end = jnp.cumsum(n_sb)
    sb_start = sb_end - n_sb
    s_idx = jnp.arange(n_experts + m // SUPER_ROWS, dtype=I32)
    used = s_idx < sb_end[-1]
    e_of = jnp.minimum(jnp.sum(sb_end[None, :] <= s_idx[:, None], axis=1), n_experts - 1)
    piece = s_idx - sb_start[e_of]
    sb_row0 = jnp.where(used, pad_start[e_of] + piece * SUPER_ROWS, 0).astype(I32)
    sb_rows = jnp.clip(padded[e_of] - piece * SUPER_ROWS, 0, SUPER_ROWS)
    sb_blocks = jnp.where(used, sb_rows // MOE_BLOCK, 0).astype(I32)
    sb_expert = jnp.where(used, e_of, e_of[jnp.maximum(sb_end[-1] - 1, 0)]).astype(I32)
    buf = moe_dispatch(dest, pad_start.astype(I32), pad_end.astype(I32), h2, n_slots)
    expert_out = moe_experts(sb_expert, sb_row0, sb_blocks, pad_end[-1:].astype(I32), buf,
                             wg, wu, wd)
    return moe_combine(dest, x1, route, expert_out)


def _w_regroup_body(w_ref, o_ref, *, c0, dl, il):
    w = w_ref[...]
    lane = lax.broadcasted_iota(I32, (1, LANES), 1)
    o_ref[:, :c0] = w[:, :c0].astype(BF16)
    o_ref[:, c0:c0 + LANES] = jnp.where(lane < dl, w[:, c0:c0 + LANES], 0.0).astype(BF16)
    o_ref[:, c0 + LANES:c0 + 2 * LANES] = jnp.where(
        lane < il, w[:, c0 + dl:c0 + dl + LANES], 0.0).astype(BF16)
    o_ref[:, c0 + 2 * LANES:] = w[:, c0 + dl + il:].astype(BF16)


def regroup_w_in(w_in, c0, dl, il, *, tr=256):
    d, cols = w_in.shape
    out_cols = cols + 2 * LANES - dl - il
    return pl.pallas_call(
        functools.partial(_w_regroup_body, c0=c0, dl=dl, il=il),
        grid=(d // tr,),
        in_specs=[pl.BlockSpec((tr, cols), lambda i: (i, 0))],
        out_specs=pl.BlockSpec((tr, out_cols), lambda i: (i, 0)),
        out_shape=jax.ShapeDtypeStruct((d, out_cols), BF16),
        compiler_params=_cparams(("parallel",)),
        name="regroup_w_in",
    )(w_in)


def _pad_cols(w, to):
    return jnp.pad(w, ((0, 0), (0, to - w.shape[1])))


def _pad_rows(w, to):
    return jnp.pad(w, ((0, to - w.shape[0]), (0, 0)))


def _layer(x2d, batch, norm1_w, w_in, mu, w0, w2, a0, a2, g2, k_k, k_a, r_k, gn_w, gn_b,
           q_norm_w, k_norm_w, sinks, proj_rwkv, proj_attn, w_out, norm2_w, wc, bc, wf, bf,
           wg, wu, wd):
    n, d = x2d.shape
    seq = n // batch
    width = w0.shape[0]
    dl, il, gl = w2.shape[0], a2.shape[0], g2.shape[0]
    q_width = proj_attn.shape[0]
    rwkv_cols = 3 * width + dl + il + gl
    kv_width = (w_in.shape[1] - rwkv_cols - q_width - 2 * d) // 2
    row = lambda v: v.reshape(1, -1).astype(F32)

    c0 = 3 * width
    w_all = regroup_w_in(w_in, c0, dl, il)
    mu_p = jnp.concatenate([mu[:c0], jnp.pad(mu[c0:c0 + dl], (0, LANES - dl)),
                            jnp.pad(mu[c0 + dl:c0 + dl + il], (0, LANES - il)),
                            mu[c0 + dl + il:]]).reshape(1, -1)
    rwkv_w = c0 + 2 * LANES + gl
    proj = norm_proj(x2d, row(norm1_w), w_all, 2 * d)

    rt, at, bt, kt, vv, g, bonus, pc = rwkv_prep(
        proj, mu_p, row(w0), _pad_rows(w2, LANES).astype(BF16), row(a0),
        _pad_rows(a2, LANES).astype(BF16), g2.astype(BF16), row(k_k), row(k_a), row(r_k),
        seq_len=seq)
    a_out, b_out = mixers(rt, at, bt, kt, vv, pc, g, bonus, row(gn_w), row(gn_b), proj, q_norm_w,
                          k_norm_w, sinks.astype(F32), batch=batch, q_col0=rwkv_w,
                          q_width=q_width, kv_width=kv_width)

    n_groups, n_experts = wc.shape[1], wf.shape[1]
    router_w = _pad_cols(jnp.concatenate([wc, wf], axis=1), LANES)
    router_b = _pad_cols(jnp.concatenate([bc, bf]).reshape(1, -1), LANES)
    assert n_groups == N_GROUPS and n_experts == N_GROUPS * GROUP_SIZE
    x1, h2, route = merge_route(a_out, b_out, proj, rwkv_w + q_width + 2 * kv_width, x2d,
                                proj_rwkv.astype(BF16),
                                proj_attn.astype(BF16), w_out.astype(BF16), row(norm2_w),
                                router_w, router_b)
    return hierarchical_moe(x1, h2, route, wg, wu, wd)


def kernel(x, norm1_w, w_in, rwkv_mu, rwkv_w0, rwkv_w2, rwkv_a0, rwkv_a2, rwkv_g2, rwkv_k_k,
           rwkv_k_a, rwkv_r_k, rwkv_gn_w, rwkv_gn_b, q_norm_w, k_norm_w, attn_sinks, proj_rwkv,
           proj_attn, w_out, norm2_w, router_coarse_w, router_coarse_b, router_fine_w,
           router_fine_b, expert_w_gate, expert_w_up, expert_w_down):
    batch, seq, d = x.shape
    x2d = x.reshape(batch * seq, d)
    for layer in range(norm1_w.shape[0]):
        x2d = _layer(x2d, batch, norm1_w[layer], w_in[layer], rwkv_mu[layer], rwkv_w0[layer],
                     rwkv_w2[layer], rwkv_a0[layer], rwkv_a2[layer], rwkv_g2[layer],
                     rwkv_k_k[layer], rwkv_k_a[layer], rwkv_r_k[layer].reshape(-1),
                     rwkv_gn_w[layer], rwkv_gn_b[layer], q_norm_w[layer], k_norm_w[layer],
                     attn_sinks[layer], proj_rwkv[layer], proj_attn[layer], w_out[layer],
                     norm2_w[layer], router_coarse_w[layer], router_coarse_b[layer],
                     router_fine_w[layer], router_fine_b[layer], expert_w_gate[layer],
                     expert_w_up[layer], expert_w_down[layer])
    return x2d.reshape(batch, seq, d)
```

```python
import functools
import math

import jax
import jax.numpy as jnp
from jax import lax
from jax.experimental import pallas as pl
from jax.experimental.pallas import tpu as pltpu

F32 = jnp.float32
BF16 = jnp.bfloat16
I32 = jnp.int32

NORM_EPS = 1e-6
GN_EPS = 64e-5
HEAD_DIM = 64
LANES = 128
SUBLANES = 8
CHUNK = 64
WINDOW = 128
MOE_BLOCK = 128
SUPER_ROWS = 1024
FF_TILE = 512
TOP_K = 2
N_GROUPS = 8
GROUP_SIZE = 8
VMEM_LIMIT = 56 * 1024 * 1024

_NT = (((1,), (1,)), ((), ()))
_TN = (((0,), (0,)), ((), ()))


def _dot(a, b):
    return jnp.dot(a, b, preferred_element_type=F32)


def _split2(x):
    hi = x.astype(BF16)
    mid = (x - hi.astype(F32)).astype(BF16)
    return hi, mid


def _select_sum(m, hi, mid):
    return _dot(m, hi) + _dot(m, mid)


def _pack_bf16_halves(x):
    w = x.shape[1] // 2
    lo = lax.bitcast_convert_type(x[:, :w].astype(BF16).astype(F32), jnp.uint32)
    hi = lax.bitcast_convert_type(x[:, w:].astype(BF16).astype(F32), jnp.uint32)
    return (lo >> 16) | (hi & jnp.uint32(0xFFFF0000))


def _unpack_bf16_halves(xp):
    lo = lax.bitcast_convert_type(xp << 16, F32).astype(BF16)
    hi = lax.bitcast_convert_type(xp & jnp.uint32(0xFFFF0000), F32).astype(BF16)
    return lo, hi


def _cparams(sem, vmem=VMEM_LIMIT):
    return pltpu.CompilerParams(dimension_semantics=sem, vmem_limit_bytes=vmem)


def _norm_proj_body(x_ref, g_ref, w_ref, o_ref, h_ref, *, first_gate_tile):
    j = pl.program_id(1)

    @pl.when(j == 0)
    def _():
        x = x_ref[...]
        ms = jnp.mean(x * x, axis=-1, keepdims=True)
        h_ref[...] = (x * lax.rsqrt(ms + NORM_EPS) * g_ref[...]).astype(BF16)

    acc = _dot(h_ref[...], w_ref[...])

    @pl.when(j < first_gate_tile)
    def _():
        o_ref[...] = acc.astype(o_ref.dtype)

    @pl.when(j >= first_gate_tile)
    def _():
        o_ref[...] = jax.nn.sigmoid(acc).astype(o_ref.dtype)


def norm_proj(x2d, gain, w, gate_cols, *, tm=1024, tn=1024):
    n, d = x2d.shape
    c = w.shape[1]
    tm = min(tm, n)
    assert c % tn == 0 and gate_cols % tn == 0
    return pl.pallas_call(
        functools.partial(_norm_proj_body, first_gate_tile=(c - gate_cols) // tn),
        grid=(n // tm, c // tn),
        in_specs=[pl.BlockSpec((tm, d), lambda i, j: (i, 0)),
                  pl.BlockSpec((1, d), lambda i, j: (0, 0)),
                  pl.BlockSpec((d, tn), lambda i, j: (0, j))],
        out_specs=pl.BlockSpec((tm, tn), lambda i, j: (i, j)),
        out_shape=jax.ShapeDtypeStruct((n, c), BF16),
        scratch_shapes=[pltpu.VMEM((tm, d), BF16)],
        compiler_params=_cparams(("parallel", "arbitrary")),
        name="norm_proj",
    )(x2d, gain, w)


def _head_sum(x, e, et):
    s = _dot(x.astype(BF16), e)
    hi, mid = _split2(s)
    return _dot(hi, et) + _dot(mid, et)


def _rwkv_prep_body(p_ref, pprev_ref, mu_ref, w0_ref, w2_ref, a0_ref, a2_ref, g2_ref,
                    kk_ref, ka_ref, rk_ref, e_ref, et_ref, tri_ref, sel_ref,
                    rt_ref, at_ref, bt_ref, kt_ref, v_ref, g_ref, bonus_ref, pc_ref,
                    *, seq_len, width):
    tm = p_ref.shape[0]
    w_ = width
    first = (pl.program_id(0) * tm) % seq_len == 0
    p = p_ref[...].astype(F32)
    last = pprev_ref.shape[0] - 1
    prev_row = jnp.where(first, 0.0, pprev_ref[last:, :].astype(F32))
    row = lax.broadcasted_iota(I32, (tm, 1), 0)
    shifted = jnp.where(row == 0, prev_row, pltpu.roll(p, 1, 0))
    m = p + (shifted - p) * mu_ref[...]
    r = m[:, 0:w_]
    k = m[:, w_:2 * w_]
    v = m[:, 2 * w_:3 * w_]
    xw = m[:, 3 * w_:3 * w_ + 128]
    xa = m[:, 3 * w_ + 128:3 * w_ + 256]
    xg = m[:, 3 * w_ + 256:]

    z = -(w0_ref[...] + _dot(jnp.tanh(xw).astype(BF16), w2_ref[...]))
    softplus = jnp.maximum(z, 0.0) + jnp.log1p(jnp.exp(-jnp.abs(z)))
    logw = -jnp.exp(-softplus - 0.5)
    a = jax.nn.sigmoid(a0_ref[...] + _dot(xa.astype(BF16), a2_ref[...]))
    g = _dot(jax.nn.sigmoid(xg).astype(BF16), g2_ref[...])

    e = e_ref[...]
    et = et_ref[...]
    kk = k * kk_ref[...]
    kk = kk * jnp.minimum(lax.rsqrt(_head_sum(kk * kk, e, et)), 1e12)
    kmod = k * (1.0 + (a - 1.0) * ka_ref[...])
    bonus = _head_sum(r * kmod * rk_ref[...], e, et) * v

    lw_hi, lw_mid = _split2(logw)
    cum = _select_sum(tri_ref[...], lw_hi, lw_mid)
    pc_ref[...] = jnp.exp(_select_sum(sel_ref[...], lw_hi, lw_mid))
    inv = jnp.exp(-cum)
    rt_ref[...] = (r * jnp.exp(cum)).astype(BF16)
    at_ref[...] = (-kk * jnp.exp(cum - logw)).astype(BF16)
    bt_ref[...] = (kk * a * inv).astype(BF16)
    kt_ref[...] = (kmod * inv).astype(BF16)
    v_ref[...] = v.astype(BF16)
    g_ref[...] = g.astype(BF16)
    bonus_ref[...] = bonus


def _head_indicator(width):
    heads = width // HEAD_DIM
    c = jnp.arange(width)[:, None] // HEAD_DIM
    h = jnp.arange(LANES)[None, :]
    e = (c == h).astype(BF16)
    assert heads <= LANES
    return e, e.T


def rwkv_prep(p, mu, w0, w2, a0, a2, g2, k_k, k_a, r_k, *, seq_len, tm=512):
    n = p.shape[0]
    cols = mu.shape[1]
    prev_rows = 2 * SUBLANES
    width = w0.shape[1]
    nchunk = tm // CHUNK
    e, et = _head_indicator(width)
    t = jnp.arange(tm)
    same = (t[:, None] // CHUNK) == (t[None, :] // CHUNK)
    tri = (same & (t[:, None] >= t[None, :])).astype(BF16)
    sel = ((t[None, :] // CHUNK) == jnp.arange(nchunk)[:, None]).astype(BF16)
    const = lambda shape: pl.BlockSpec(shape, lambda i: (0, 0))
    stream = lambda dt: jax.ShapeDtypeStruct((n, width), dt)
    outs = pl.pallas_call(
        functools.partial(_rwkv_prep_body, seq_len=seq_len, width=width),
        grid=(n // tm,),
        in_specs=[pl.BlockSpec((tm, cols), lambda i: (i, 0)),
                  pl.BlockSpec((prev_rows, cols),
                               lambda i: (jnp.maximum(i * (tm // prev_rows) - 1, 0), 0)),
                  const((1, cols)), const((1, width)), const(w2.shape), const((1, width)),
                  const(a2.shape), const(g2.shape), const((1, width)), const((1, width)),
                  const((1, width)), const(e.shape), const(et.shape), const(tri.shape),
                  const(sel.shape)],
        out_specs=[pl.BlockSpec((tm, width), lambda i: (i, 0))] * 7
                  + [pl.BlockSpec((nchunk, width), lambda i: (i, 0))],
        out_shape=[stream(BF16)] * 6 + [stream(F32),
                   jax.ShapeDtypeStruct((n // CHUNK, width), F32)],
        compiler_params=_cparams(("parallel",)),
        name="rwkv_prep",
    )(p, p, mu, w0, w2, a0, a2, g2, k_k, k_a, r_k, e, et, tri, sel)
    return outs


def _chunk_pairs(rts, ats, bts, kts, vvs, pcs, s_prevs, masks):
    lo, strict, incl = masks
    c = rts[0].shape[0]
    c2 = 2 * c
    zero = jnp.zeros_like(rts[0])
    each = lambda f, *ls: [f(*a) for a in zip(*ls)]

    def stack(x):
        return jnp.concatenate([jnp.where(lo, x, zero), jnp.where(lo, zero, x)], axis=0)

    ar = each(lambda a, r: jnp.concatenate([stack(a), stack(r)], axis=0), ats, rts)
    bk = each(lambda b, k: jnp.concatenate([stack(b), stack(k)], axis=0), bts, kts)
    vb = each(stack, vvs)
    gram = each(lambda x, y: lax.dot_general(x, y, _NT, preferred_element_type=F32), ar, bk)
    ars = each(lambda x, s: lax.dot_general(x, s.astype(BF16), _NT, preferred_element_type=F32),
               ar, s_prevs)
    a_ab = each(lambda g: jnp.where(strict, g[:c2, :c2], 0.0), gram)
    rhs = each(lambda g, v, x: x[:c2] + _dot(jnp.where(strict, g[:c2, c2:], 0.0).astype(BF16), v),
               gram, vb, ars)

    u = rhs
    nk = a_ab
    steps = (c - 1).bit_length()
    for k in range(steps):
        nkb = each(lambda n: n.astype(BF16), nk)
        if k + 1 < steps:
            prod = each(lambda n, uu: _dot(n, jnp.concatenate([n, uu.astype(BF16)], axis=1)),
                        nkb, u)
            nk = each(lambda p: p[:, :c2], prod)
            u = each(lambda uu, p: uu + p[:, c2:], u, prod)
        else:
            u = each(lambda uu, n: uu + _dot(n, uu.astype(BF16)), u, nkb)

    uv = each(lambda uu, v: jnp.concatenate([uu.astype(BF16), v], axis=0), u, vb)
    a_r = each(lambda g: jnp.where(incl, g[c2:, :], 0.0).astype(BF16), gram)
    ybd = each(lambda x, a, w: x[c2:] + _dot(a, w), ars, a_r, uv)
    ys = each(lambda yb: yb[:c] + yb[c:], ybd)

    bk_end = each(lambda x, pc: (x.astype(F32) * pc).astype(BF16), bk, pcs)
    s_new = each(lambda s, pc, w, x: s * pc + lax.dot_general(w, x, _TN, preferred_element_type=F32),
                 s_prevs, pcs, uv, bk_end)
    return ys, s_new


def _attn_kv(kcat, vcat, k_gain, bd, lo):
    kv_heads = kcat.shape[1] // HEAD_DIM
    kv_cols = [slice(jt * LANES, (jt + 1) * LANES) for jt in range(kv_heads // 2)]
    kn = [_head_rmsnorm(kcat[:, c], k_gain, bd) for c in kv_cols]
    kn_r = [pltpu.roll(x, HEAD_DIM, 1) for x in kn]
    vt = [vcat[:, c] for c in kv_cols]
    vt_r = [pltpu.roll(x, HEAD_DIM, 1) for x in vt]
    own = lambda hk: lo if hk % 2 == 0 else jnp.logical_not(lo)
    k2 = [jnp.where(own(hk), kn[hk // 2], kn_r[hk // 2]).astype(BF16) for hk in range(kv_heads)]
    v2 = [jnp.where(own(hk), vt[hk // 2], vt_r[hk // 2]).astype(BF16) for hk in range(kv_heads)]
    return k2, v2


def _attn_tiles(tiles, q_of, k2, v2, q_gain, bd, valid, lo, top_rows, sink_ref, tiles_per_kv):
    each = lambda f, *ls: [f(*a) for a in zip(*ls)]
    blk = valid.shape[0] // 2
    qn = [_head_rmsnorm(q_of(t).astype(F32), q_gain, bd) for t in tiles]
    qst = [jnp.concatenate([jnp.where(lo, x, 0.0), jnp.where(lo, 0.0, x)], axis=0).astype(BF16)
           for x in qn]
    s = [jnp.where(valid, lax.dot_general(x, k2[t // tiles_per_kv], _NT,
                                          preferred_element_type=F32), -jnp.inf)
         for t, x in zip(tiles, qst)]
    sink = [jnp.where(top_rows, sink_ref[2 * t], sink_ref[2 * t + 1]) for t in tiles]
    mx = each(lambda x, sk: jnp.maximum(jnp.max(x, axis=-1, keepdims=True), sk), s, sink)
    pr = each(lambda x, m: jnp.exp(x - m), s, mx)
    inv = each(lambda p, sk, m: 1.0 / (jnp.sum(p, axis=-1, keepdims=True) + jnp.exp(sk - m)),
               pr, sink, mx)
    o = [_dot((p * r).astype(BF16), v2[t // tiles_per_kv]) for t, p, r in zip(tiles, pr, inv)]
    return [jnp.where(lo, x[:blk], x[blk:]) for x in o]


def _mixers_body(sink_ref, rt_ref, at_ref, bt_ref, kt_ref, v_ref, pc_ref, g_ref, bonus_ref,
                 gnw_ref, gnb_ref, e_ref, et_ref, *refs, group):
    (kp_ref, vp_ref, kc_ref, vc_ref, qg_ref, kg_ref, bd_ref,
     oa_ref, ob_ref, s_ref, y_ref) = refs[-11:]
    q_pieces = refs[:-11]
    tr, width = rt_ref.shape
    npairs = width // LANES
    blk = WINDOW
    seq_first = pl.program_id(1) == 0

    @pl.when(seq_first)
    def _():
        s_ref[...] = jnp.zeros_like(s_ref)

    lo_c = lax.broadcasted_iota(I32, (CHUNK, LANES), 1) < HEAD_DIM
    ri = lax.broadcasted_iota(I32, (2 * CHUNK, 2 * CHUNK), 0)
    ci = lax.broadcasted_iota(I32, (2 * CHUNK, 2 * CHUNK), 1)
    same = (ri // CHUNK) == (ci // CHUNK)
    ri2 = lax.broadcasted_iota(I32, (2 * CHUNK, 4 * CHUNK), 0)
    ci2 = lax.broadcasted_iota(I32, (2 * CHUNK, 4 * CHUNK), 1) % (2 * CHUNK)
    incl = ((ri2 // CHUNK) == (ci2 // CHUNK)) & (ri2 >= ci2)
    masks = (lo_c, same & (ri > ci), incl)
    cols = [slice(pr * LANES, (pr + 1) * LANES) for pr in range(npairs)]

    def chunk_step(c):
        rows = pl.ds(pl.multiple_of(c * CHUNK, CHUNK), CHUNK)
        this_chunk = lax.broadcasted_iota(I32, (pc_ref.shape[0], 1), 0) == c
        pcs = [jnp.sum(jnp.where(this_chunk, pc_ref[:, cl], 0.0), axis=0, keepdims=True)
               for cl in cols]
        load = lambda ref: [ref[rows, cl] for cl in cols]
        ys, s_new = _chunk_pairs(load(rt_ref), load(at_ref), load(bt_ref), load(kt_ref),
                                 load(v_ref), pcs, [s_ref[pr] for pr in range(npairs)], masks)
        for pr in range(npairs):
            s_ref[pr] = s_new[pr]
            y_ref[rows, cols[pr]] = ys[pr]

    bd = bd_ref[...]
    q_gain = qg_ref[...] * (HEAD_DIM ** -0.5)
    lo = lax.broadcasted_iota(I32, (1, LANES), 1) < HEAD_DIM
    qi = lax.broadcasted_iota(I32, (2 * blk, 2 * blk), 0)
    kj = lax.broadcasted_iota(I32, (2 * blk, 2 * blk), 1)
    rel = blk + (qi % blk) - kj
    in_window = (rel >= 0) & (rel < WINDOW)
    top_rows = lax.broadcasted_iota(I32, (2 * blk, 1), 0) < blk
    q_tiles = ob_ref.shape[1] // LANES
    piece_tiles = q_pieces[0].shape[1] // LANES
    tiles_per_kv = group // 2
    chunks_per_blk = blk // CHUNK

    def block_step(jb, carry):
        cur = pl.ds(pl.multiple_of(jb * blk, blk), blk)
        prv = pl.ds(pl.multiple_of(jnp.maximum(jb - 1, 0) * blk, blk), blk)
        first_blk = jb == 0
        kprev = jnp.where(first_blk, kp_ref[...], kc_ref[prv, :])
        vprev = jnp.where(first_blk, vp_ref[...], vc_ref[prv, :])
        kcat = jnp.concatenate([kprev, kc_ref[cur, :]], axis=0).astype(F32)
        vcat = jnp.concatenate([vprev, vc_ref[cur, :]], axis=0).astype(F32)
        k2, v2 = _attn_kv(kcat, vcat, kg_ref[...], bd, lo)
        first_key = jnp.where(seq_first & first_blk, blk, 0)
        valid = in_window & (kj >= first_key)
        q_of = lambda t: q_pieces[t // piece_tiles][cur, cols[t % piece_tiles]]
        for part in range(chunks_per_blk):
            chunk_step(jb * chunks_per_blk + part)
            tiles = list(range(part * q_tiles // chunks_per_blk,
                               (part + 1) * q_tiles // chunks_per_blk))
            outs = _attn_tiles(tiles, q_of, k2, v2, q_gain, bd, valid, lo, top_rows, sink_ref,
                               tiles_per_kv)
            for t, o in zip(tiles, outs):
                ob_ref[cur, cols[t]] = o.astype(ob_ref.dtype)
        return carry

    lax.fori_loop(0, tr // blk, block_step, 0)

    e = e_ref[...]
    et = et_ref[...]
    y = y_ref[...]
    mean = _head_sum(y, e, et) * (1.0 / HEAD_DIM)
    d = y - mean
    var = _head_sum(d * d, e, et) * (1.0 / HEAD_DIM)
    out = d * lax.rsqrt(var + GN_EPS) * gnw_ref[...] + gnb_ref[...]
    out = (out + bonus_ref[...]) * g_ref[...].astype(F32)
    oa_ref[...] = out.astype(oa_ref.dtype)


def mixers(rt, at, bt, kt, vv, pc, g, bonus, gn_w, gn_b, proj, q_gain, k_gain, sinks, *, batch,
           q_col0, q_width, kv_width, tr=512):
    n, width = rt.shape
    nst = n // batch // tr
    assert q_width == width and WINDOW % CHUNK == 0 and tr % WINDOW == 0
    group = (q_width // HEAD_DIM) // (kv_width // HEAD_DIM)
    q_piece = math.gcd(q_col0, q_width)
    assert q_piece % LANES == 0 and (q_col0 + q_width) % kv_width == 0
    n_q = q_width // q_piece
    kcol = (q_col0 + q_width) // kv_width
    e, et = _head_indicator(width)
    t = jnp.arange(LANES)
    bd = ((t[:, None] // HEAD_DIM) == (t[None, :] // HEAD_DIM)).astype(BF16)
    qg = jnp.tile(q_gain.reshape(1, HEAD_DIM), (1, 2))
    kg = jnp.tile(k_gain.reshape(1, HEAD_DIM), (1, 2))
    tile = lambda rows, w, col=0: pl.BlockSpec((rows, w), lambda b, s, *_: (b * nst + s, col))
    prev = lambda col: pl.BlockSpec(
        (WINDOW, kv_width),
        lambda b, s, *_: (jnp.maximum((b * nst + s) * (tr // WINDOW) - 1, 0), col))
    const = lambda shape: pl.BlockSpec(shape, lambda b, s, *_: (0, 0))
    grid_spec = pltpu.PrefetchScalarGridSpec(
        num_scalar_prefetch=1,
        grid=(batch, nst),
        in_specs=[tile(tr, width)] * 5 + [tile(tr // CHUNK, width), tile(tr, width),
                  tile(tr, width), const((1, width)), const((1, width)), const(e.shape),
                  const(et.shape)]
                 + [tile(tr, q_piece, q_col0 // q_piece + i) for i in range(n_q)]
                 + [prev(kcol), prev(kcol + 1), tile(tr, kv_width, kcol),
                    tile(tr, kv_width, kcol + 1), const((1, LANES)), const((1, LANES)),
                    const((LANES, LANES))],
        out_specs=[tile(tr, width), tile(tr, q_width)],
        scratch_shapes=[pltpu.VMEM((width // LANES, LANES, LANES), F32),
                        pltpu.VMEM((tr, width), F32)],
    )
    return pl.pallas_call(
        functools.partial(_mixers_body, group=group),
        grid_spec=grid_spec,
        out_shape=[jax.ShapeDtypeStruct((n, width), BF16),
                   jax.ShapeDtypeStruct((n, q_width), BF16)],
        compiler_params=_cparams(("parallel", "arbitrary")),
        name="mixers",
    )(sinks, rt, at, bt, kt, vv, pc, g, bonus, gn_w, gn_b, e, et, *([proj] * (n_q + 4)), qg, kg, bd)


def _head_rmsnorm(x, gain, bd):
    hi, mid = _split2(x * x)
    ms = (_dot(hi, bd) + _dot(mid, bd)) * (1.0 / HEAD_DIM)
    return x * lax.rsqrt(ms + NORM_EPS) * gain


def _merge_route_body(a_ref, b_ref, *refs, gate_pieces):
    gates = refs[:2 * gate_pieces]
    (x_ref, pr_ref, pa_ref, wo_ref, n2_ref, rhm_ref, rh_ref, rb_ref,
     x1_ref, h2_ref, route_ref, logits_ref) = refs[2 * gate_pieces:]
    gate = lambda pieces: jnp.concatenate([g[...] for g in pieces], axis=1).astype(F32)
    @pl.when(pl.program_id(0) == 0)
    def _():
        logits_ref[...] = jnp.zeros_like(logits_ref)

    _route(logits_ref[...], route_ref)

    merged = (gate(gates[:gate_pieces]) * _dot(a_ref[...], pr_ref[...])
              + gate(gates[gate_pieces:]) * _dot(b_ref[...], pa_ref[...]))
    x1 = x_ref[...] + _dot(merged.astype(BF16), wo_ref[...])
    x1_ref[...] = x1
    ms = jnp.mean(x1 * x1, axis=-1, keepdims=True)
    h2 = x1 * lax.rsqrt(ms + NORM_EPS) * n2_ref[...]
    h2_ref[...] = _pack_bf16_halves(h2)

    hi, mid = _split2(h2)
    hw = _dot(hi, rhm_ref[...])
    logits_ref[...] = hw[:, :LANES] + (hw[:, LANES:] + _dot(mid, rh_ref[...])) + rb_ref[...]


def _route(logits, route_ref):
    lane = lax.broadcasted_iota(I32, logits.shape, 1)
    big = jnp.int32(1 << 20)
    neg = -jnp.inf
    is_coarse = lane < N_GROUPS
    cl = jnp.where(is_coarse, logits, neg)
    ce = jnp.exp(cl - jnp.max(cl, axis=-1, keepdims=True))
    cp = ce / jnp.sum(ce, axis=-1, keepdims=True)
    g_prob = jnp.max(cp, axis=-1, keepdims=True)
    g_idx = jnp.min(jnp.where(is_coarse & (cp == g_prob), lane, big), axis=-1, keepdims=True)

    fine_lane = lane - N_GROUPS
    in_group = ((lane >= N_GROUPS) & (lane < N_GROUPS + N_GROUPS * GROUP_SIZE)
                & ((fine_lane // GROUP_SIZE) == g_idx))
    fl = jnp.where(in_group, logits, neg)
    fe = jnp.exp(fl - jnp.max(fl, axis=-1, keepdims=True))
    fp = fe / jnp.sum(fe, axis=-1, keepdims=True)
    p1 = jnp.max(jnp.where(in_group, fp, -1.0), axis=-1, keepdims=True)
    i1 = jnp.min(jnp.where(in_group & (fp == p1), lane, big), axis=-1, keepdims=True)
    rest = in_group & (lane != i1)
    p2 = jnp.max(jnp.where(rest, fp, -1.0), axis=-1, keepdims=True)
    i2 = jnp.min(jnp.where(rest & (fp == p2), lane, big), axis=-1, keepdims=True)
    den = p1 + p2
    w1 = g_prob * p1 / den
    w2 = g_prob * p2 / den
    e1 = (i1 - N_GROUPS).astype(F32)
    e2 = (i2 - N_GROUPS).astype(F32)
    route_ref[...] = jnp.where(lane == 0, w1, jnp.where(lane == 1, w2,
                     jnp.where(lane == 2, e1, jnp.where(lane == 3, e2, 0.0))))


def merge_route(a_out, b_out, proj, gate_col0, x2d, proj_r, proj_a, w_out, norm2_w, router_w,
                router_b, *, tm=256):
    n, d = x2d.shape
    g_piece = math.gcd(gate_col0, d)
    assert g_piece % LANES == 0
    gate_pieces = d // g_piece
    wa = a_out.shape[1]
    rh, rm = _split2(router_w)
    rhm = jnp.concatenate([rh, rm], axis=1)
    nt = n // tm
    const = lambda arr: pl.BlockSpec(arr.shape, lambda i: (0, 0), pipeline_mode=pl.Buffered(1))
    row = lambda w, col=0: pl.BlockSpec((tm, w), lambda i: (jnp.minimum(i, nt - 1), col))
    return pl.pallas_call(
        functools.partial(_merge_route_body, gate_pieces=gate_pieces),
        grid=(nt + 1,),
        in_specs=[row(wa), row(wa)]
                 + [row(g_piece, gate_col0 // g_piece + t) for t in range(2 * gate_pieces)]
                 + [row(d),
                  const(proj_r), const(proj_a), const(w_out), const(norm2_w),
                  const(rhm), const(rh), const(router_b)],
        out_specs=[row(d), row(d // 2),
                   pl.BlockSpec((tm, LANES), lambda i: (jnp.maximum(i - 1, 0), 0))],
        scratch_shapes=[pltpu.VMEM((tm, LANES), F32)],
        out_shape=[jax.ShapeDtypeStruct((n, d), F32), jax.ShapeDtypeStruct((n, d // 2), jnp.uint32),
                   jax.ShapeDtypeStruct((n, LANES), F32)],
        compiler_params=_cparams(("arbitrary",)),
        name="merge_route",
    )(a_out, b_out, *([proj] * (2 * gate_pieces)), x2d, proj_r, proj_a, w_out, norm2_w, rhm, rh,
      router_b)


def _moe_rank_body(ids_ref, ut_ref, rank_ref, counts_ref, carry_ref):
    @pl.when(pl.program_id(0) == 0)
    def _():
        carry_ref[...] = jnp.zeros_like(carry_ref)

    ids = ids_ref[...]
    n_exp = carry_ref.shape[0]
    sub = lax.broadcasted_iota(I32, (n_exp, ids.shape[1]), 0)
    onehot = (sub == ids).astype(F32)
    before = _dot(onehot.astype(BF16), ut_ref[...])
    carry = carry_ref[...]
    rank_ref[...] = jnp.sum(onehot * (before + carry), axis=0, keepdims=True).astype(I32)
    carry = carry + jnp.sum(onehot, axis=1, keepdims=True)
    carry_ref[...] = carry
    counts_ref[...] = jnp.broadcast_to(carry, counts_ref.shape)


def moe_rank(ids_row, n_experts, *, tb=1024):
    m = ids_row.shape[1]
    t = jnp.arange(tb)
    ut = (t[:, None] < t[None, :]).astype(BF16)
    return pl.pallas_call(
        _moe_rank_body,
        grid=(m // tb,),
        in_specs=[pl.BlockSpec((1, tb), lambda i: (0, i)),
                  pl.BlockSpec((tb, tb), lambda i: (0, 0))],
        out_specs=[pl.BlockSpec((1, tb), lambda i: (0, i)),
                   pl.BlockSpec((n_experts, LANES), lambda i: (0, 0))],
        out_shape=[jax.ShapeDtypeStruct((1, m), I32),
                   jax.ShapeDtypeStruct((n_experts, LANES), F32)],
        scratch_shapes=[pltpu.VMEM((n_experts, 1), F32)],
        compiler_params=_cparams(("arbitrary",)),
        name="moe_rank",
    )(ids_row, ut)


def _moe_dest_body(ids_ref, rank_ref, start_ref, dest_ref):
    ids = ids_ref[...]
    sub = lax.broadcasted_iota(I32, (start_ref.shape[0], ids.shape[1]), 0)
    start = jnp.sum(jnp.where(sub == ids, start_ref[...], 0.0), axis=0, keepdims=True)
    dest_ref[...] = start.astype(I32) + rank_ref[...]


def moe_dest(ids_row, rank_row, pad_start_col, *, tb=2048):
    m = ids_row.shape[1]
    row = pl.BlockSpec((1, tb), lambda i: (0, i))
    return pl.pallas_call(
        _moe_dest_body,
        grid=(m // tb,),
        in_specs=[row, row, pl.BlockSpec(pad_start_col.shape, lambda i: (0, 0))],
        out_specs=row,
        out_shape=jax.ShapeDtypeStruct((1, m), I32),
        compiler_params=_cparams(("parallel",)),
        name="moe_dest",
    )(ids_row, rank_row, pad_start_col)


def _moe_dispatch_body(dest_ref, start_ref, end_ref, h_ref, buf_ref, zeros_ref, sem, zsem):
    tm = h_ref.shape[0] * h_ref.shape[1]
    base = pl.program_id(0) * tm * TOP_K

    @pl.when(pl.program_id(0) == 0)
    def _():
        zeros_ref[...] = jnp.zeros_like(zeros_ref)

        def zero_last_block(e):
            last = pl.multiple_of(end_ref[e] - MOE_BLOCK, MOE_BLOCK)
            return pltpu.make_async_copy(zeros_ref, buf_ref.at[pl.ds(last, MOE_BLOCK)], zsem)

        def zstart(e, c):
            @pl.when(end_ref[e] > start_ref[e])
            def _():
                zero_last_block(e).start()
            return c

        def zwait(e, c):
            @pl.when(end_ref[e] > start_ref[e])
            def _():
                zero_last_block(e).wait()
            return c

        lax.fori_loop(0, start_ref.shape[0], zstart, 0)
        lax.fori_loop(0, start_ref.shape[0], zwait, 0)

        tail0 = end_ref[end_ref.shape[0] - 1]

        def zero_tail_block(r):
            first = pl.multiple_of(tail0 + r * MOE_BLOCK, MOE_BLOCK)
            return pltpu.make_async_copy(zeros_ref, buf_ref.at[pl.ds(first, MOE_BLOCK)], zsem)

        def tstart(r, c):
            zero_tail_block(r).start()
            return c

        def twait(r, c):
            zero_tail_block(r).wait()
            return c

        n_tail = (buf_ref.shape[0] - tail0) // MOE_BLOCK
        lax.fori_loop(0, n_tail, tstart, 0)
        lax.fori_loop(0, n_tail, twait, 0)

    def for_rows(fn):
        def body(q, c):
            for u in range(SUBLANES):
                for k in range(TOP_K):
                    slot = dest_ref[base + (q * SUBLANES + u) * TOP_K + k]
                    fn(pltpu.make_async_copy(h_ref.at[q, pl.ds(u, 1)], buf_ref.at[pl.ds(slot, 1)],
                                             sem))
            return c
        lax.fori_loop(0, h_ref.shape[0], body, 0)

    for_rows(lambda cp: cp.start())
    for_rows(lambda cp: cp.wait())


def moe_dispatch(dest, pad_start, pad_end, h2, n_slots, *, tm=256):
    n, d = h2.shape
    h2 = h2.reshape(n // SUBLANES, SUBLANES, d)
    grid_spec = pltpu.PrefetchScalarGridSpec(
        num_scalar_prefetch=3,
        grid=(n // tm,),
        in_specs=[pl.BlockSpec((tm // SUBLANES, SUBLANES, d), lambda i, *_: (i, 0, 0))],
        out_specs=pl.BlockSpec(memory_space=pl.ANY),
        scratch_shapes=[pltpu.VMEM((MOE_BLOCK, d), h2.dtype),
                        pltpu.SemaphoreType.DMA(()), pltpu.SemaphoreType.DMA(())],
    )
    return pl.pallas_call(
        _moe_dispatch_body,
        grid_spec=grid_spec,
        out_shape=jax.ShapeDtypeStruct((n_slots, d), h2.dtype),
        compiler_params=_cparams(("arbitrary",)),
        name="moe_dispatch",
    )(dest, pad_start, pad_end, h2)


def _moe_expert_body(sbe_ref, row0_ref, nb_ref, tot_ref, wg_ref, wu_ref, wd_ref, buf_ref, out_ref,
                     wgb, wub, wdb, xp, acc, pending, sem_in, sem_out):
    del sbe_ref
    s = pl.program_id(0)
    f = pl.program_id(1)
    n_sb = pl.num_programs(0)
    last_f = pl.num_programs(1) - 1
    nb = nb_ref[s]
    half = xp.shape[2]

    def blk(first_row, r):
        return pl.ds(pl.multiple_of(first_row + r * MOE_BLOCK, MOE_BLOCK), MOE_BLOCK)

    def rows_in(sb, r):
        return pltpu.make_async_copy(buf_ref.at[blk(row0_ref[sb], r)], xp.at[sb % 2, blk(0, r)],
                                     sem_in.at[sb % 2])

    def rows_out(r):
        return pltpu.make_async_copy(acc.at[blk(0, r)], out_ref.at[blk(row0_ref[s], r)], sem_out)

    def for_blocks(n, fn):
        def body(r, carry):
            fn(r)
            return carry
        lax.fori_loop(0, n, body, 0)

    def drain_out():
        for_blocks(pending[0], lambda r: rows_out(0).wait())
        pending[0] = 0

    def mlp_rows(first_row, n_rows, first):
        rows = pl.ds(pl.multiple_of(first_row, MOE_BLOCK), n_rows)
        xlo, xhi = _unpack_bf16_halves(xp[s % 2, rows, :])
        hg = _dot(xlo, wgb[:half, :]) + _dot(xhi, wgb[half:, :])
        hu = _dot(xlo, wub[:half, :]) + _dot(xhi, wub[half:, :])
        y = _dot((jax.nn.silu(hg) * hu).astype(BF16), wdb[...])
        if first:
            acc[rows, :] = y
        else:
            acc[rows, :] += y

    def all_rows(first):
        for_blocks(nb // 2, lambda q: mlp_rows(q * (2 * MOE_BLOCK), 2 * MOE_BLOCK, first))

        @pl.when(nb % 2 == 1)
        def _():
            mlp_rows((nb - 1) * MOE_BLOCK, MOE_BLOCK, first)

    @pl.when((s == 0) & (f == 0))
    def _():
        pending[0] = 0
        for_blocks(nb, lambda r: rows_in(0, r).start())

    @pl.when(nb > 0)
    def _():
        @pl.when(f == 0)
        def _():
            nxt = jnp.minimum(s + 1, n_sb - 1)
            nb_next = jnp.where(s + 1 < n_sb, nb_ref[nxt], 0)
            for_blocks(nb_next, lambda r: rows_in(nxt, r).start())

        wgb[...] = wg_ref[0].astype(BF16)
        wub[...] = wu_ref[0].astype(BF16)
        wdb[...] = wd_ref[0].astype(BF16)

        @pl.when(f == 0)
        def _():
            for_blocks(nb, lambda r: rows_in(s, r).wait())
            drain_out()
            all_rows(True)

        @pl.when(f > 0)
        def _():
            all_rows(False)

        @pl.when(f == last_f)
        def _():
            for_blocks(nb, lambda r: rows_out(r).start())
            pending[0] = nb

    @pl.when((s == n_sb - 1) & (f == last_f))
    def _():
        drain_out()
        acc[pl.ds(0, MOE_BLOCK), :] = jnp.zeros((MOE_BLOCK, acc.shape[1]), acc.dtype)
        tail0 = tot_ref[0]

        def zero_out(r):
            return pltpu.make_async_copy(acc.at[pl.ds(0, MOE_BLOCK)], out_ref.at[blk(tail0, r)],
                                         sem_out)

        ntail = (out_ref.shape[0] - tail0) // MOE_BLOCK
        for_blocks(ntail, lambda r: zero_out(r).start())
        for_blocks(ntail, lambda r: zero_out(r).wait())


def moe_experts(sb_expert, sb_row0, sb_blocks, total_rows, buf, wg, wu, wd):
    p, half = buf.shape
    d = 2 * half
    ff = wg.shape[2]
    nf = ff // FF_TILE
    f_eff = lambda s, f, nb: jnp.where(nb[s] > 0, f, nf - 1)
    grid_spec = pltpu.PrefetchScalarGridSpec(
        num_scalar_prefetch=4,
        grid=(sb_expert.shape[0], nf),
        in_specs=[pl.BlockSpec((1, d, FF_TILE), lambda s, f, e, r0, nb, t: (e[s], 0, f_eff(s, f, nb))),
                  pl.BlockSpec((1, d, FF_TILE), lambda s, f, e, r0, nb, t: (e[s], 0, f_eff(s, f, nb))),
                  pl.BlockSpec((1, FF_TILE, d), lambda s, f, e, r0, nb, t: (e[s], f_eff(s, f, nb), 0)),
                  pl.BlockSpec(memory_space=pl.ANY)],
        out_specs=pl.BlockSpec(memory_space=pl.ANY),
        scratch_shapes=[pltpu.VMEM((d, FF_TILE), BF16), pltpu.VMEM((d, FF_TILE), BF16),
                        pltpu.VMEM((FF_TILE, d), BF16), pltpu.VMEM((2, SUPER_ROWS, half), jnp.uint32),
                        pltpu.VMEM((SUPER_ROWS, d), F32), pltpu.SMEM((1,), I32),
                        pltpu.SemaphoreType.DMA((2,)), pltpu.SemaphoreType.DMA(())],
    )
    return pl.pallas_call(
        _moe_expert_body,
        grid_spec=grid_spec,
        out_shape=jax.ShapeDtypeStruct((p, d), F32),
        compiler_params=_cparams(("arbitrary", "arbitrary")),
        name="moe_experts",
    )(sb_expert, sb_row0, sb_blocks, total_rows, wg, wu, wd, buf)


def _moe_combine_body(dest_ref, x1_ref, route_ref, out_ref, y_ref, rows_ref, sem):
    tm, d = x1_ref.shape
    i = pl.program_id(0)
    sublanes = rows_ref.shape[3]

    def row_copy(tile, q, u, k):
        slot = dest_ref[(tile * tm + q * sublanes + u) * TOP_K + k]
        return pltpu.make_async_copy(out_ref.at[pl.ds(slot, 1)],
                                     rows_ref.at[tile % 2, k, q, pl.ds(u, 1)], sem.at[tile % 2])

    def for_rows(tile, fn):
        def body(q, c):
            for u in range(sublanes):
                for k in range(TOP_K):
                    fn(row_copy(tile, q, u, k))
            return c
        lax.fori_loop(0, tm // sublanes, body, 0)

    @pl.when(i == 0)
    def _():
        for_rows(0, lambda cp: cp.start())

    @pl.when(i + 1 < pl.num_programs(0))
    def _():
        for_rows(i + 1, lambda cp: cp.start())

    for_rows(i, lambda cp: cp.wait())
    route = route_ref[...]
    rows = rows_ref[i % 2]
    y_ref[...] = x1_ref[...] + (rows[0].reshape(tm, d) * route[:, 0:1]
                                + rows[1].reshape(tm, d) * route[:, 1:2])


def moe_combine(dest, x1, route, expert_out, *, tm=256):
    n, d = x1.shape
    grid_spec = pltpu.PrefetchScalarGridSpec(
        num_scalar_prefetch=1,
        grid=(n // tm,),
        in_specs=[pl.BlockSpec((tm, d), lambda i, *_: (i, 0)),
                  pl.BlockSpec((tm, LANES), lambda i, *_: (i, 0)),
                  pl.BlockSpec(memory_space=pl.ANY)],
        out_specs=pl.BlockSpec((tm, d), lambda i, *_: (i, 0)),
        scratch_shapes=[pltpu.VMEM((2, TOP_K, tm // SUBLANES, SUBLANES, d), F32),
                        pltpu.SemaphoreType.DMA((2,))],
    )
    return pl.pallas_call(
        _moe_combine_body,
        grid_spec=grid_spec,
        out_shape=jax.ShapeDtypeStruct((n, d), F32),
        compiler_params=_cparams(("arbitrary",)),
        name="moe_combine",
    )(dest, x1, route, expert_out)


def hierarchical_moe(x1, h2, route, wg, wu, wd):
    n = h2.shape[0]
    n_experts = wg.shape[0]
    m = n * TOP_K
    ids_row = route[:, 2:4].astype(I32).reshape(1, m)
    rank_row, counts = moe_rank(ids_row, n_experts)
    counts = counts[:, 0].astype(I32)
    padded = (counts + MOE_BLOCK - 1) // MOE_BLOCK * MOE_BLOCK
    pad_end = jnp.cumsum(padded)
    pad_start = pad_end - padded
    dest = moe_dest(ids_row, rank_row, pad_start.astype(F32).reshape(n_experts, 1)).reshape(m)
    n_slots = m + n_experts * MOE_BLOCK
    n_sb = (padded + SUPER_ROWS - 1) // SUPER_ROWS
    sb_end = jnp.cumsum(n_sb)
    sb_start = sb_end - n_sb
    s_idx = jnp.arange(n_experts + m // SUPER_ROWS, dtype=I32)
    used = s_idx < sb_end[-1]
    e_of = jnp.minimum(jnp.sum(sb_end[None, :] <= s_idx[:, None], axis=1), n_experts - 1)
    piece = s_idx - sb_start[e_of]
    sb_row0 = jnp.where(used, pad_start[e_of] + piece * SUPER_ROWS, 0).astype(I32)
    sb_rows = jnp.clip(padded[e_of] - piece * SUPER_ROWS, 0, SUPER_ROWS)
    sb_blocks = jnp.where(used, sb_rows // MOE_BLOCK, 0).astype(I32)
    sb_expert = jnp.where(used, e_of, e_of[jnp.maximum(sb_end[-1] - 1, 0)]).astype(I32)
    buf = moe_dispatch(dest, pad_start.astype(I32), pad_end.astype(I32), h2, n_slots)
    expert_out = moe_experts(sb_expert, sb_row0, sb_blocks, pad_end[-1:].astype(I32), buf,
                             wg, wu, wd)
    return moe_combine(dest, x1, route, expert_out)


def _w_regroup_body(w_ref, o_ref, *, c0, dl, il):
    w = w_ref[...]
    lane = lax.broadcasted_iota(I32, (1, LANES), 1)
    o_ref[:, :c0] = w[:, :c0].astype(BF16)
    o_ref[:, c0:c0 + LANES] = jnp.where(lane < dl, w[:, c0:c0 + LANES], 0.0).astype(BF16)
    o_ref[:, c0 + LANES:c0 + 2 * LANES] = jnp.where(
        lane < il, w[:, c0 + dl:c0 + dl + LANES], 0.0).astype(BF16)
    o_ref[:, c0 + 2 * LANES:] = w[:, c0 + dl + il:].astype(BF16)


def regroup_w_in(w_in, layer, c0, dl, il, *, tr=256):
    _, d, cols = w_in.shape
    out_cols = cols + 2 * LANES - dl - il
    return pl.pallas_call(
        functools.partial(_w_regroup_body, c0=c0, dl=dl, il=il),
        grid=(d // tr,),
        in_specs=[pl.BlockSpec((None, tr, cols), lambda i: (layer, i, 0))],
        out_specs=pl.BlockSpec((tr, out_cols), lambda i: (i, 0)),
        out_shape=jax.ShapeDtypeStruct((d, out_cols), BF16),
        compiler_params=_cparams(("parallel",)),
        name="regroup_w_in",
    )(w_in)


def _pad_cols(w, to):
    return jnp.pad(w, ((0, 0), (0, to - w.shape[1])))


def _pad_rows(w, to):
    return jnp.pad(w, ((0, to - w.shape[0]), (0, 0)))


def _layer(x2d, batch, norm1_w, w_in, mu, w0, w2, a0, a2, g2, k_k, k_a, r_k, gn_w, gn_b,
           q_norm_w, k_norm_w, sinks, proj_rwkv, proj_attn, w_out, norm2_w, wc, bc, wf, bf,
           wg, wu, wd):
    n, d = x2d.shape
    seq = n // batch
    width = w0.shape[0]
    dl, il, gl = w2.shape[0], a2.shape[0], g2.shape[0]
    q_width = proj_attn.shape[0]
    rwkv_cols = 3 * width + dl + il + gl
    w_in_all, layer = w_in
    kv_width = (w_in_all.shape[2] - rwkv_cols - q_width - 2 * d) // 2
    row = lambda v: v.reshape(1, -1).astype(F32)

    c0 = 3 * width
    w_all = regroup_w_in(w_in_all, layer, c0, dl, il)
    mu_p = jnp.concatenate([mu[:c0], jnp.pad(mu[c0:c0 + dl], (0, LANES - dl)),
                            jnp.pad(mu[c0 + dl:c0 + dl + il], (0, LANES - il)),
                            mu[c0 + dl + il:]]).reshape(1, -1)
    rwkv_w = c0 + 2 * LANES + gl
    proj = norm_proj(x2d, row(norm1_w), w_all, 2 * d)

    rt, at, bt, kt, vv, g, bonus, pc = rwkv_prep(
        proj, mu_p, row(w0), _pad_rows(w2, LANES).astype(BF16), row(a0),
        _pad_rows(a2, LANES).astype(BF16), g2.astype(BF16), row(k_k), row(k_a), row(r_k),
        seq_len=seq)
    a_out, b_out = mixers(rt, at, bt, kt, vv, pc, g, bonus, row(gn_w), row(gn_b), proj, q_norm_w,
                          k_norm_w, sinks.astype(F32), batch=batch, q_col0=rwkv_w,
                          q_width=q_width, kv_width=kv_width)

    n_groups, n_experts = wc.shape[1], wf.shape[1]
    router_w = _pad_cols(jnp.concatenate([wc, wf], axis=1), LANES)
    router_b = _pad_cols(jnp.concatenate([bc, bf]).reshape(1, -1), LANES)
    assert n_groups == N_GROUPS and n_experts == N_GROUPS * GROUP_SIZE
    x1, h2, route = merge_route(a_out, b_out, proj, rwkv_w + q_width + 2 * kv_width, x2d,
                                proj_rwkv.astype(BF16),
                                proj_attn.astype(BF16), w_out.astype(BF16), row(norm2_w),
                                router_w, router_b)
    return hierarchical_moe(x1, h2, route, wg, wu, wd)


def kernel(x, norm1_w, w_in, rwkv_mu, rwkv_w0, rwkv_w2, rwkv_a0, rwkv_a2, rwkv_g2, rwkv_k_k,
           rwkv_k_a, rwkv_r_k, rwkv_gn_w, rwkv_gn_b, q_norm_w, k_norm_w, attn_sinks, proj_rwkv,
           proj_attn, w_out, norm2_w, router_coarse_w, router_coarse_b, router_fine_w,
           router_fine_b, expert_w_gate, expert_w_up, expert_w_down):
    batch, seq, d = x.shape
    x2d = x.reshape(batch * seq, d)
    for layer in range(norm1_w.shape[0]):
        x2d = _layer(x2d, batch, norm1_w[layer], (w_in, layer), rwkv_mu[layer], rwkv_w0[layer],
                     rwkv_w2[layer], rwkv_a0[layer], rwkv_a2[layer], rwkv_g2[layer],
                     rwkv_k_k[layer], rwkv_k_a[layer], rwkv_r_k[layer].reshape(-1),
                     rwkv_gn_w[layer], rwkv_gn_b[layer], q_norm_w[layer], k_norm_w[layer],
                     attn_sinks[layer], proj_rwkv[layer], proj_attn[layer], w_out[layer],
                     norm2_w[layer], router_coarse_w[layer], router_coarse_b[layer],
                     router_fine_w[layer], router_fine_b[layer], expert_w_gate[layer],
                     expert_w_up[layer], expert_w_down[layer])
    return x2d.reshape(batch, seq, d)
```

```python
import functools
import math

import jax
import jax.numpy as jnp
from jax import lax
from jax.experimental import pallas as pl
from jax.experimental.pallas import tpu as pltpu

F32 = jnp.float32
BF16 = jnp.bfloat16
I32 = jnp.int32

NORM_EPS = 1e-6
GN_EPS = 64e-5
HEAD_DIM = 64
LANES = 128
SUBLANES = 8
CHUNK = 64
WINDOW = 128
MOE_BLOCK = 128
SUPER_ROWS = 1024
FF_TILE = 512
TOP_K = 2
N_GROUPS = 8
GROUP_SIZE = 8
VMEM_LIMIT = 56 * 1024 * 1024

_NT = (((1,), (1,)), ((), ()))
_TN = (((0,), (0,)), ((), ()))


def _dot(a, b):
    return jnp.dot(a, b, preferred_element_type=F32)


def _split2(x):
    hi = x.astype(BF16)
    mid = (x - hi.astype(F32)).astype(BF16)
    return hi, mid


def _select_sum(m, hi, mid):
    return _dot(m, hi) + _dot(m, mid)


def _pack_bf16_halves(x):
    w = x.shape[1] // 2
    lo = lax.bitcast_convert_type(x[:, :w].astype(BF16).astype(F32), jnp.uint32)
    hi = lax.bitcast_convert_type(x[:, w:].astype(BF16).astype(F32), jnp.uint32)
    return (lo >> 16) | (hi & jnp.uint32(0xFFFF0000))


def _unpack_bf16_halves(xp):
    lo = lax.bitcast_convert_type(xp << 16, F32).astype(BF16)
    hi = lax.bitcast_convert_type(xp & jnp.uint32(0xFFFF0000), F32).astype(BF16)
    return lo, hi


def _cparams(sem, vmem=VMEM_LIMIT):
    return pltpu.CompilerParams(dimension_semantics=sem, vmem_limit_bytes=vmem)


def _norm_proj_body(x_ref, g_ref, w_ref, o_ref, h_ref, *, first_gate_tile):
    j = pl.program_id(1)

    @pl.when(j == 0)
    def _():
        x = x_ref[...]
        ms = jnp.mean(x * x, axis=-1, keepdims=True)
        h_ref[...] = (x * lax.rsqrt(ms + NORM_EPS) * g_ref[...]).astype(BF16)

    acc = _dot(h_ref[...], w_ref[...])

    @pl.when(j < first_gate_tile)
    def _():
        o_ref[...] = acc.astype(o_ref.dtype)

    @pl.when(j >= first_gate_tile)
    def _():
        o_ref[...] = jax.nn.sigmoid(acc).astype(o_ref.dtype)


def norm_proj(x2d, gain, w, gate_cols, *, tm=1024, tn=1024):
    n, d = x2d.shape
    c = w.shape[1]
    tm = min(tm, n)
    assert c % tn == 0 and gate_cols % tn == 0
    return pl.pallas_call(
        functools.partial(_norm_proj_body, first_gate_tile=(c - gate_cols) // tn),
        grid=(n // tm, c // tn),
        in_specs=[pl.BlockSpec((tm, d), lambda i, j: (i, 0)),
                  pl.BlockSpec((1, d), lambda i, j: (0, 0)),
                  pl.BlockSpec((d, tn), lambda i, j: (0, j))],
        out_specs=pl.BlockSpec((tm, tn), lambda i, j: (i, j)),
        out_shape=jax.ShapeDtypeStruct((n, c), BF16),
        scratch_shapes=[pltpu.VMEM((tm, d), BF16)],
        compiler_params=_cparams(("parallel", "arbitrary")),
        name="norm_proj",
    )(x2d, gain, w)


def _head_sum(x, e, et):
    s = _dot(x.astype(BF16), e)
    hi, mid = _split2(s)
    return _dot(hi, et) + _dot(mid, et)


def _rwkv_prep_body(p_ref, pprev_ref, mu_ref, w0_ref, w2_ref, a0_ref, a2_ref, g2_ref,
                    kk_ref, ka_ref, rk_ref, e_ref, et_ref, tri_ref, sel_ref,
                    rt_ref, at_ref, bt_ref, kt_ref, v_ref, g_ref, bonus_ref, pc_ref,
                    *, seq_len, width):
    tm = p_ref.shape[0]
    w_ = width
    first = (pl.program_id(0) * tm) % seq_len == 0
    p = p_ref[...].astype(F32)
    last = pprev_ref.shape[0] - 1
    prev_row = jnp.where(first, 0.0, pprev_ref[last:, :].astype(F32))
    row = lax.broadcasted_iota(I32, (tm, 1), 0)
    shifted = jnp.where(row == 0, prev_row, pltpu.roll(p, 1, 0))
    m = p + (shifted - p) * mu_ref[...]
    r = m[:, 0:w_]
    k = m[:, w_:2 * w_]
    v = m[:, 2 * w_:3 * w_]
    xw = m[:, 3 * w_:3 * w_ + 128]
    xa = m[:, 3 * w_ + 128:3 * w_ + 256]
    xg = m[:, 3 * w_ + 256:]

    z = -(w0_ref[...] + _dot(jnp.tanh(xw).astype(BF16), w2_ref[...]))
    softplus = jnp.maximum(z, 0.0) + jnp.log1p(jnp.exp(-jnp.abs(z)))
    logw = -jnp.exp(-softplus - 0.5)
    a = jax.nn.sigmoid(a0_ref[...] + _dot(xa.astype(BF16), a2_ref[...]))
    g = _dot(jax.nn.sigmoid(xg).astype(BF16), g2_ref[...])

    e = e_ref[...]
    et = et_ref[...]
    kk = k * kk_ref[...]
    kk = kk * jnp.minimum(lax.rsqrt(_head_sum(kk * kk, e, et)), 1e12)
    kmod = k * (1.0 + (a - 1.0) * ka_ref[...])
    bonus = _head_sum(r * kmod * rk_ref[...], e, et) * v

    lw_hi, lw_mid = _split2(logw)
    cum = _select_sum(tri_ref[...], lw_hi, lw_mid)
    pc_ref[...] = jnp.exp(_select_sum(sel_ref[...], lw_hi, lw_mid))
    inv = jnp.exp(-cum)
    rt_ref[...] = (r * jnp.exp(cum)).astype(BF16)
    at_ref[...] = (-kk * jnp.exp(cum - logw)).astype(BF16)
    bt_ref[...] = (kk * a * inv).astype(BF16)
    kt_ref[...] = (kmod * inv).astype(BF16)
    v_ref[...] = v.astype(BF16)
    g_ref[...] = g.astype(BF16)
    bonus_ref[...] = bonus


def _head_indicator(width):
    heads = width // HEAD_DIM
    c = jnp.arange(width)[:, None] // HEAD_DIM
    h = jnp.arange(LANES)[None, :]
    e = (c == h).astype(BF16)
    assert heads <= LANES
    return e, e.T


def rwkv_prep(p, mu, w0, w2, a0, a2, g2, k_k, k_a, r_k, *, seq_len, tm=512):
    n = p.shape[0]
    cols = mu.shape[1]
    prev_rows = 2 * SUBLANES
    width = w0.shape[1]
    nchunk = tm // CHUNK
    e, et = _head_indicator(width)
    t = jnp.arange(tm)
    same = (t[:, None] // CHUNK) == (t[None, :] // CHUNK)
    tri = (same & (t[:, None] >= t[None, :])).astype(BF16)
    sel = ((t[None, :] // CHUNK) == jnp.arange(nchunk)[:, None]).astype(BF16)
    const = lambda shape: pl.BlockSpec(shape, lambda i: (0, 0))
    stream = lambda dt: jax.ShapeDtypeStruct((n, width), dt)
    outs = pl.pallas_call(
        functools.partial(_rwkv_prep_body, seq_len=seq_len, width=width),
        grid=(n // tm,),
        in_specs=[pl.BlockSpec((tm, cols), lambda i: (i, 0)),
                  pl.BlockSpec((prev_rows, cols),
                               lambda i: (jnp.maximum(i * (tm // prev_rows) - 1, 0), 0)),
                  const((1, cols)), const((1, width)), const(w2.shape), const((1, width)),
                  const(a2.shape), const(g2.shape), const((1, width)), const((1, width)),
                  const((1, width)), const(e.shape), const(et.shape), const(tri.shape),
                  const(sel.shape)],
        out_specs=[pl.BlockSpec((tm, width), lambda i: (i, 0))] * 7
                  + [pl.BlockSpec((nchunk, width), lambda i: (i, 0))],
        out_shape=[stream(BF16)] * 6 + [stream(F32),
                   jax.ShapeDtypeStruct((n // CHUNK, width), F32)],
        compiler_params=_cparams(("parallel",)),
        name="rwkv_prep",
    )(p, p, mu, w0, w2, a0, a2, g2, k_k, k_a, r_k, e, et, tri, sel)
    return outs


def _chunk_pairs(rts, ats, bts, kts, vvs, pcs, s_prevs, masks):
    lo, strict, incl = masks
    c = rts[0].shape[0]
    c2 = 2 * c
    zero = jnp.zeros_like(rts[0])
    each = lambda f, *ls: [f(*a) for a in zip(*ls)]

    def stack(x):
        return jnp.concatenate([jnp.where(lo, x, zero), jnp.where(lo, zero, x)], axis=0)

    ar = each(lambda a, r: jnp.concatenate([stack(a), stack(r)], axis=0), ats, rts)
    bk = each(lambda b, k: jnp.concatenate([stack(b), stack(k)], axis=0), bts, kts)
    vb = each(stack, vvs)
    gram = each(lambda x, y: lax.dot_general(x, y, _NT, preferred_element_type=F32), ar, bk)
    ars = each(lambda x, s: lax.dot_general(x, s.astype(BF16), _NT, preferred_element_type=F32),
               ar, s_prevs)
    a_ab = each(lambda g: jnp.where(strict, g[:c2, :c2], 0.0), gram)
    rhs = each(lambda g, v, x: x[:c2] + _dot(jnp.where(strict, g[:c2, c2:], 0.0).astype(BF16), v),
               gram, vb, ars)

    u = rhs
    nk = a_ab
    steps = (c - 1).bit_length()
    for k in range(steps):
        nkb = each(lambda n: n.astype(BF16), nk)
        if k + 1 < steps:
            prod = each(lambda n, uu: _dot(n, jnp.concatenate([n, uu.astype(BF16)], axis=1)),
                        nkb, u)
            nk = each(lambda p: p[:, :c2], prod)
            u = each(lambda uu, p: uu + p[:, c2:], u, prod)
        else:
            u = each(lambda uu, n: uu + _dot(n, uu.astype(BF16)), u, nkb)

    uv = each(lambda uu, v: jnp.concatenate([uu.astype(BF16), v], axis=0), u, vb)
    a_r = each(lambda g: jnp.where(incl, g[c2:, :], 0.0).astype(BF16), gram)
    ybd = each(lambda x, a, w: x[c2:] + _dot(a, w), ars, a_r, uv)
    ys = each(lambda yb: yb[:c] + yb[c:], ybd)

    bk_end = each(lambda x, pc: (x.astype(F32) * pc).astype(BF16), bk, pcs)
    s_new = each(lambda s, pc, w, x: s * pc + lax.dot_general(w, x, _TN, preferred_element_type=F32),
                 s_prevs, pcs, uv, bk_end)
    return ys, s_new


def _attn_kv(kcat, vcat, k_gain, bd, lo):
    kv_heads = kcat.shape[1] // HEAD_DIM
    kv_cols = [slice(jt * LANES, (jt + 1) * LANES) for jt in range(kv_heads // 2)]
    kn = [_head_rmsnorm(kcat[:, c], k_gain, bd) for c in kv_cols]
    kn_r = [pltpu.roll(x, HEAD_DIM, 1) for x in kn]
    vt = [vcat[:, c] for c in kv_cols]
    vt_r = [pltpu.roll(x, HEAD_DIM, 1) for x in vt]
    own = lambda hk: lo if hk % 2 == 0 else jnp.logical_not(lo)
    k2 = [jnp.where(own(hk), kn[hk // 2], kn_r[hk // 2]).astype(BF16) for hk in range(kv_heads)]
    v2 = [jnp.where(own(hk), vt[hk // 2], vt_r[hk // 2]).astype(BF16) for hk in range(kv_heads)]
    return k2, v2


def _attn_tiles(tiles, q_of, k2, v2, q_gain, bd, valid, lo, top_rows, sink_ref, tiles_per_kv):
    each = lambda f, *ls: [f(*a) for a in zip(*ls)]
    blk = valid.shape[0] // 2
    qn = [_head_rmsnorm(q_of(t).astype(F32), q_gain, bd) for t in tiles]
    qst = [jnp.concatenate([jnp.where(lo, x, 0.0), jnp.where(lo, 0.0, x)], axis=0).astype(BF16)
           for x in qn]
    s = [jnp.where(valid, lax.dot_general(x, k2[t // tiles_per_kv], _NT,
                                          preferred_element_type=F32), -jnp.inf)
         for t, x in zip(tiles, qst)]
    sink = [jnp.where(top_rows, sink_ref[2 * t], sink_ref[2 * t + 1]) for t in tiles]
    mx = each(lambda x, sk: jnp.maximum(jnp.max(x, axis=-1, keepdims=True), sk), s, sink)
    pr = each(lambda x, m: jnp.exp(x - m), s, mx)
    inv = each(lambda p, sk, m: 1.0 / (jnp.sum(p, axis=-1, keepdims=True) + jnp.exp(sk - m)),
               pr, sink, mx)
    o = [_dot((p * r).astype(BF16), v2[t // tiles_per_kv]) for t, p, r in zip(tiles, pr, inv)]
    return [jnp.where(lo, x[:blk], x[blk:]) for x in o]


def _mixers_body(sink_ref, rt_ref, at_ref, bt_ref, kt_ref, v_ref, pc_ref, g_ref, bonus_ref,
                 gnw_ref, gnb_ref, e_ref, et_ref, *refs, group):
    (kp_ref, vp_ref, kc_ref, vc_ref, qg_ref, kg_ref, bd_ref,
     oa_ref, ob_ref, s_ref, y_ref) = refs[-11:]
    q_pieces = refs[:-11]
    tr, width = rt_ref.shape
    npairs = width // LANES
    blk = WINDOW
    seq_first = pl.program_id(1) == 0

    @pl.when(seq_first)
    def _():
        s_ref[...] = jnp.zeros_like(s_ref)

    lo_c = lax.broadcasted_iota(I32, (CHUNK, LANES), 1) < HEAD_DIM
    ri = lax.broadcasted_iota(I32, (2 * CHUNK, 2 * CHUNK), 0)
    ci = lax.broadcasted_iota(I32, (2 * CHUNK, 2 * CHUNK), 1)
    same = (ri // CHUNK) == (ci // CHUNK)
    ri2 = lax.broadcasted_iota(I32, (2 * CHUNK, 4 * CHUNK), 0)
    ci2 = lax.broadcasted_iota(I32, (2 * CHUNK, 4 * CHUNK), 1) % (2 * CHUNK)
    incl = ((ri2 // CHUNK) == (ci2 // CHUNK)) & (ri2 >= ci2)
    masks = (lo_c, same & (ri > ci), incl)
    cols = [slice(pr * LANES, (pr + 1) * LANES) for pr in range(npairs)]

    def chunk_step(c):
        rows = pl.ds(pl.multiple_of(c * CHUNK, CHUNK), CHUNK)
        this_chunk = lax.broadcasted_iota(I32, (pc_ref.shape[0], 1), 0) == c
        pcs = [jnp.sum(jnp.where(this_chunk, pc_ref[:, cl], 0.0), axis=0, keepdims=True)
               for cl in cols]
        load = lambda ref: [ref[rows, cl] for cl in cols]
        ys, s_new = _chunk_pairs(load(rt_ref), load(at_ref), load(bt_ref), load(kt_ref),
                                 load(v_ref), pcs, [s_ref[pr] for pr in range(npairs)], masks)
        for pr in range(npairs):
            s_ref[pr] = s_new[pr]
            y_ref[rows, cols[pr]] = ys[pr]

    bd = bd_ref[...]
    q_gain = qg_ref[...] * (HEAD_DIM ** -0.5)
    lo = lax.broadcasted_iota(I32, (1, LANES), 1) < HEAD_DIM
    qi = lax.broadcasted_iota(I32, (2 * blk, 2 * blk), 0)
    kj = lax.broadcasted_iota(I32, (2 * blk, 2 * blk), 1)
    rel = blk + (qi % blk) - kj
    in_window = (rel >= 0) & (rel < WINDOW)
    top_rows = lax.broadcasted_iota(I32, (2 * blk, 1), 0) < blk
    q_tiles = ob_ref.shape[1] // LANES
    piece_tiles = q_pieces[0].shape[1] // LANES
    tiles_per_kv = group // 2
    chunks_per_blk = blk // CHUNK

    def block_step(jb, carry):
        cur = pl.ds(pl.multiple_of(jb * blk, blk), blk)
        prv = pl.ds(pl.multiple_of(jnp.maximum(jb - 1, 0) * blk, blk), blk)
        first_blk = jb == 0
        kprev = jnp.where(first_blk, kp_ref[...], kc_ref[prv, :])
        vprev = jnp.where(first_blk, vp_ref[...], vc_ref[prv, :])
        kcat = jnp.concatenate([kprev, kc_ref[cur, :]], axis=0).astype(F32)
        vcat = jnp.concatenate([vprev, vc_ref[cur, :]], axis=0).astype(F32)
        k2, v2 = _attn_kv(kcat, vcat, kg_ref[...], bd, lo)
        first_key = jnp.where(seq_first & first_blk, blk, 0)
        valid = in_window & (kj >= first_key)
        q_of = lambda t: q_pieces[t // piece_tiles][cur, cols[t % piece_tiles]]
        for part in range(chunks_per_blk):
            chunk_step(jb * chunks_per_blk + part)
            tiles = list(range(part * q_tiles // chunks_per_blk,
                               (part + 1) * q_tiles // chunks_per_blk))
            outs = _attn_tiles(tiles, q_of, k2, v2, q_gain, bd, valid, lo, top_rows, sink_ref,
                               tiles_per_kv)
            for t, o in zip(tiles, outs):
                ob_ref[cur, cols[t]] = o.astype(ob_ref.dtype)
        return carry

    lax.fori_loop(0, tr // blk, block_step, 0)

    e = e_ref[...]
    et = et_ref[...]
    y = y_ref[...]
    mean = _head_sum(y, e, et) * (1.0 / HEAD_DIM)
    d = y - mean
    var = _head_sum(d * d, e, et) * (1.0 / HEAD_DIM)
    out = d * lax.rsqrt(var + GN_EPS) * gnw_ref[...] + gnb_ref[...]
    out = (out + bonus_ref[...]) * g_ref[...].astype(F32)
    oa_ref[...] = out.astype(oa_ref.dtype)


def mixers(rt, at, bt, kt, vv, pc, g, bonus, gn_w, gn_b, proj, q_gain, k_gain, sinks, *, batch,
           q_col0, q_width, kv_width, tr=512):
    n, width = rt.shape
    nst = n // batch // tr
    assert q_width == width and WINDOW % CHUNK == 0 and tr % WINDOW == 0
    group = (q_width // HEAD_DIM) // (kv_width // HEAD_DIM)
    q_piece = math.gcd(q_col0, q_width)
    assert q_piece % LANES == 0 and (q_col0 + q_width) % kv_width == 0
    n_q = q_width // q_piece
    kcol = (q_col0 + q_width) // kv_width
    e, et = _head_indicator(width)
    t = jnp.arange(LANES)
    bd = ((t[:, None] // HEAD_DIM) == (t[None, :] // HEAD_DIM)).astype(BF16)
    qg = jnp.tile(q_gain.reshape(1, HEAD_DIM), (1, 2))
    kg = jnp.tile(k_gain.reshape(1, HEAD_DIM), (1, 2))
    tile = lambda rows, w, col=0: pl.BlockSpec((rows, w), lambda b, s, *_: (b * nst + s, col))
    prev = lambda col: pl.BlockSpec(
        (WINDOW, kv_width),
        lambda b, s, *_: (jnp.maximum((b * nst + s) * (tr // WINDOW) - 1, 0), col))
    const = lambda shape: pl.BlockSpec(shape, lambda b, s, *_: (0, 0))
    grid_spec = pltpu.PrefetchScalarGridSpec(
        num_scalar_prefetch=1,
        grid=(batch, nst),
        in_specs=[tile(tr, width)] * 5 + [tile(tr // CHUNK, width), tile(tr, width),
                  tile(tr, width), const((1, width)), const((1, width)), const(e.shape),
                  const(et.shape)]
                 + [tile(tr, q_piece, q_col0 // q_piece + i) for i in range(n_q)]
                 + [prev(kcol), prev(kcol + 1), tile(tr, kv_width, kcol),
                    tile(tr, kv_width, kcol + 1), const((1, LANES)), const((1, LANES)),
                    const((LANES, LANES))],
        out_specs=[tile(tr, width), tile(tr, q_width)],
        scratch_shapes=[pltpu.VMEM((width // LANES, LANES, LANES), F32),
                        pltpu.VMEM((tr, width), F32)],
    )
    return pl.pallas_call(
        functools.partial(_mixers_body, group=group),
        grid_spec=grid_spec,
        out_shape=[jax.ShapeDtypeStruct((n, width), BF16),
                   jax.ShapeDtypeStruct((n, q_width), BF16)],
        compiler_params=_cparams(("parallel", "arbitrary")),
        name="mixers",
    )(sinks, rt, at, bt, kt, vv, pc, g, bonus, gn_w, gn_b, e, et, *([proj] * (n_q + 4)), qg, kg, bd)


def _head_rmsnorm(x, gain, bd):
    hi, mid = _split2(x * x)
    ms = (_dot(hi, bd) + _dot(mid, bd)) * (1.0 / HEAD_DIM)
    return x * lax.rsqrt(ms + NORM_EPS) * gain


def _merge_route_body(a_ref, b_ref, *refs, gate_pieces):
    gates = refs[:2 * gate_pieces]
    (x_ref, pr_ref, pa_ref, wo_ref, n2_ref, rhm_ref, rh_ref, rb_ref,
     x1_ref, h2_ref, route_ref, logits_ref) = refs[2 * gate_pieces:]
    gate = lambda pieces: jnp.concatenate([g[...] for g in pieces], axis=1).astype(F32)
    @pl.when(pl.program_id(0) == 0)
    def _():
        logits_ref[...] = jnp.zeros_like(logits_ref)

    _route(logits_ref[...], route_ref)

    merged = (gate(gates[:gate_pieces]) * _dot(a_ref[...], pr_ref[...])
              + gate(gates[gate_pieces:]) * _dot(b_ref[...], pa_ref[...]))
    x1 = x_ref[...] + _dot(merged.astype(BF16), wo_ref[...])
    x1_ref[...] = x1
    ms = jnp.mean(x1 * x1, axis=-1, keepdims=True)
    h2 = x1 * lax.rsqrt(ms + NORM_EPS) * n2_ref[...]
    h2_ref[...] = _pack_bf16_halves(h2)

    hi, mid = _split2(h2)
    hw = _dot(hi, rhm_ref[...])
    logits_ref[...] = hw[:, :LANES] + (hw[:, LANES:] + _dot(mid, rh_ref[...])) + rb_ref[...]


def _route(logits, route_ref):
    lane = lax.broadcasted_iota(I32, logits.shape, 1)
    big = jnp.int32(1 << 20)
    neg = -jnp.inf
    is_coarse = lane < N_GROUPS
    cl = jnp.where(is_coarse, logits, neg)
    ce = jnp.exp(cl - jnp.max(cl, axis=-1, keepdims=True))
    cp = ce / jnp.sum(ce, axis=-1, keepdims=True)
    g_prob = jnp.max(cp, axis=-1, keepdims=True)
    g_idx = jnp.min(jnp.where(is_coarse & (cp == g_prob), lane, big), axis=-1, keepdims=True)

    fine_lane = lane - N_GROUPS
    in_group = ((lane >= N_GROUPS) & (lane < N_GROUPS + N_GROUPS * GROUP_SIZE)
                & ((fine_lane // GROUP_SIZE) == g_idx))
    fl = jnp.where(in_group, logits, neg)
    fe = jnp.exp(fl - jnp.max(fl, axis=-1, keepdims=True))
    fp = fe / jnp.sum(fe, axis=-1, keepdims=True)
    p1 = jnp.max(jnp.where(in_group, fp, -1.0), axis=-1, keepdims=True)
    i1 = jnp.min(jnp.where(in_group & (fp == p1), lane, big), axis=-1, keepdims=True)
    rest = in_group & (lane != i1)
    p2 = jnp.max(jnp.where(rest, fp, -1.0), axis=-1, keepdims=True)
    i2 = jnp.min(jnp.where(rest & (fp == p2), lane, big), axis=-1, keepdims=True)
    den = p1 + p2
    w1 = g_prob * p1 / den
    w2 = g_prob * p2 / den
    e1 = (i1 - N_GROUPS).astype(F32)
    e2 = (i2 - N_GROUPS).astype(F32)
    route_ref[...] = jnp.where(lane == 0, w1, jnp.where(lane == 1, w2,
                     jnp.where(lane == 2, e1, jnp.where(lane == 3, e2, 0.0))))


def merge_route(a_out, b_out, proj, gate_col0, x2d, proj_r, proj_a, w_out, norm2_w, router_w,
                router_b, *, tm=256):
    n, d = x2d.shape
    g_piece = math.gcd(gate_col0, d)
    assert g_piece % LANES == 0
    gate_pieces = d // g_piece
    wa = a_out.shape[1]
    rh, rm = _split2(router_w)
    rhm = jnp.concatenate([rh, rm], axis=1)
    nt = n // tm
    const = lambda arr: pl.BlockSpec(arr.shape, lambda i: (0, 0), pipeline_mode=pl.Buffered(1))
    row = lambda w, col=0: pl.BlockSpec((tm, w), lambda i: (jnp.minimum(i, nt - 1), col))
    return pl.pallas_call(
        functools.partial(_merge_route_body, gate_pieces=gate_pieces),
        grid=(nt + 1,),
        in_specs=[row(wa), row(wa)]
                 + [row(g_piece, gate_col0 // g_piece + t) for t in range(2 * gate_pieces)]
                 + [row(d),
                  const(proj_r), const(proj_a), const(w_out), const(norm2_w),
                  const(rhm), const(rh), const(router_b)],
        out_specs=[row(d), row(d // 2),
                   pl.BlockSpec((tm, LANES), lambda i: (jnp.maximum(i - 1, 0), 0))],
        scratch_shapes=[pltpu.VMEM((tm, LANES), F32)],
        out_shape=[jax.ShapeDtypeStruct((n, d), F32), jax.ShapeDtypeStruct((n, d // 2), jnp.uint32),
                   jax.ShapeDtypeStruct((n, LANES), F32)],
        compiler_params=_cparams(("arbitrary",)),
        name="merge_route",
    )(a_out, b_out, *([proj] * (2 * gate_pieces)), x2d, proj_r, proj_a, w_out, norm2_w, rhm, rh,
      router_b)


def _moe_rank_body(ids_ref, ut_ref, rank_ref, counts_ref, carry_ref):
    @pl.when(pl.program_id(0) == 0)
    def _():
        carry_ref[...] = jnp.zeros_like(carry_ref)

    ids = ids_ref[...]
    n_exp = carry_ref.shape[0]
    sub = lax.broadcasted_iota(I32, (n_exp, ids.shape[1]), 0)
    onehot = (sub == ids).astype(F32)
    before = _dot(onehot.astype(BF16), ut_ref[...])
    carry = carry_ref[...]
    rank_ref[...] = jnp.sum(onehot * (before + carry), axis=0, keepdims=True).astype(I32)
    carry = carry + jnp.sum(onehot, axis=1, keepdims=True)
    carry_ref[...] = carry
    counts_ref[...] = jnp.broadcast_to(carry, counts_ref.shape)


def moe_rank(ids_row, n_experts, *, tb=1024):
    m = ids_row.shape[1]
    t = jnp.arange(tb)
    ut = (t[:, None] < t[None, :]).astype(BF16)
    return pl.pallas_call(
        _moe_rank_body,
        grid=(m // tb,),
        in_specs=[pl.BlockSpec((1, tb), lambda i: (0, i)),
                  pl.BlockSpec((tb, tb), lambda i: (0, 0))],
        out_specs=[pl.BlockSpec((1, tb), lambda i: (0, i)),
                   pl.BlockSpec((n_experts, LANES), lambda i: (0, 0))],
        out_shape=[jax.ShapeDtypeStruct((1, m), I32),
                   jax.ShapeDtypeStruct((n_experts, LANES), F32)],
        scratch_shapes=[pltpu.VMEM((n_experts, 1), F32)],
        compiler_params=_cparams(("arbitrary",)),
        name="moe_rank",
    )(ids_row, ut)


def _moe_dest_body(ids_ref, rank_ref, start_ref, dest_ref):
    ids = ids_ref[...]
    sub = lax.broadcasted_iota(I32, (start_ref.shape[0], ids.shape[1]), 0)
    start = jnp.sum(jnp.where(sub == ids, start_ref[...], 0.0), axis=0, keepdims=True)
    dest_ref[...] = start.astype(I32) + rank_ref[...]


def moe_dest(ids_row, rank_row, pad_start_col, *, tb=2048):
    m = ids_row.shape[1]
    row = pl.BlockSpec((1, tb), lambda i: (0, i))
    return pl.pallas_call(
        _moe_dest_body,
        grid=(m // tb,),
        in_specs=[row, row, pl.BlockSpec(pad_start_col.shape, lambda i: (0, 0))],
        out_specs=row,
        out_shape=jax.ShapeDtypeStruct((1, m), I32),
        compiler_params=_cparams(("parallel",)),
        name="moe_dest",
    )(ids_row, rank_row, pad_start_col)


def _moe_dispatch_body(dest_ref, start_ref, end_ref, h_ref, buf_ref, zeros_ref, sem, zsem):
    tm = h_ref.shape[0] * h_ref.shape[1]
    base = pl.program_id(0) * tm * TOP_K

    @pl.when(pl.program_id(0) == 0)
    def _():
        zeros_ref[...] = jnp.zeros_like(zeros_ref)

        def zero_last_block(e):
            last = pl.multiple_of(end_ref[e] - MOE_BLOCK, MOE_BLOCK)
            return pltpu.make_async_copy(zeros_ref, buf_ref.at[pl.ds(last, MOE_BLOCK)], zsem)

        def zstart(e, c):
            @pl.when(end_ref[e] > start_ref[e])
            def _():
                zero_last_block(e).start()
            return c

        def zwait(e, c):
            @pl.when(end_ref[e] > start_ref[e])
            def _():
                zero_last_block(e).wait()
            return c

        lax.fori_loop(0, start_ref.shape[0], zstart, 0)
        lax.fori_loop(0, start_ref.shape[0], zwait, 0)

        tail0 = end_ref[end_ref.shape[0] - 1]

        def zero_tail_block(r):
            first = pl.multiple_of(tail0 + r * MOE_BLOCK, MOE_BLOCK)
            return pltpu.make_async_copy(zeros_ref, buf_ref.at[pl.ds(first, MOE_BLOCK)], zsem)

        def tstart(r, c):
            zero_tail_block(r).start()
            return c

        def twait(r, c):
            zero_tail_block(r).wait()
            return c

        n_tail = (buf_ref.shape[0] - tail0) // MOE_BLOCK
        lax.fori_loop(0, n_tail, tstart, 0)
        lax.fori_loop(0, n_tail, twait, 0)

    def for_rows(fn):
        def body(q, c):
            for u in range(SUBLANES):
                for k in range(TOP_K):
                    slot = dest_ref[base + (q * SUBLANES + u) * TOP_K + k]
                    fn(pltpu.make_async_copy(h_ref.at[q, pl.ds(u, 1)], buf_ref.at[pl.ds(slot, 1)],
                                             sem))
            return c
        lax.fori_loop(0, h_ref.shape[0], body, 0)

    for_rows(lambda cp: cp.start())
    for_rows(lambda cp: cp.wait())


def moe_dispatch(dest, pad_start, pad_end, h2, n_slots, *, tm=512):
    n, d = h2.shape
    h2 = h2.reshape(n // SUBLANES, SUBLANES, d)
    grid_spec = pltpu.PrefetchScalarGridSpec(
        num_scalar_prefetch=3,
        grid=(n // tm,),
        in_specs=[pl.BlockSpec((tm // SUBLANES, SUBLANES, d), lambda i, *_: (i, 0, 0))],
        out_specs=pl.BlockSpec(memory_space=pl.ANY),
        scratch_shapes=[pltpu.VMEM((MOE_BLOCK, d), h2.dtype),
                        pltpu.SemaphoreType.DMA(()), pltpu.SemaphoreType.DMA(())],
    )
    return pl.pallas_call(
        _moe_dispatch_body,
        grid_spec=grid_spec,
        out_shape=jax.ShapeDtypeStruct((n_slots, d), h2.dtype),
        compiler_params=_cparams(("arbitrary",)),
        name="moe_dispatch",
    )(dest, pad_start, pad_end, h2)


def _moe_expert_body(sbe_ref, row0_ref, nb_ref, tot_ref, wg_ref, wu_ref, wd_ref, buf_ref, out_ref,
                     wgb, wub, wdb, xp, acc, pending, sem_in, sem_out):
    del sbe_ref
    s = pl.program_id(0)
    f = pl.program_id(1)
    n_sb = pl.num_programs(0)
    last_f = pl.num_programs(1) - 1
    nb = nb_ref[s]
    half = xp.shape[2]

    def blk(first_row, r):
        return pl.ds(pl.multiple_of(first_row + r * MOE_BLOCK, MOE_BLOCK), MOE_BLOCK)

    def rows_in(sb, r):
        return pltpu.make_async_copy(buf_ref.at[blk(row0_ref[sb], r)], xp.at[sb % 2, blk(0, r)],
                                     sem_in.at[sb % 2])

    def rows_out(r):
        return pltpu.make_async_copy(acc.at[blk(0, r)], out_ref.at[blk(row0_ref[s], r)], sem_out)

    def for_blocks(n, fn):
        def body(r, carry):
            fn(r)
            return carry
        lax.fori_loop(0, n, body, 0)

    def drain_out():
        for_blocks(pending[0], lambda r: rows_out(0).wait())
        pending[0] = 0

    def mlp_rows(first_row, n_rows, first):
        rows = pl.ds(pl.multiple_of(first_row, MOE_BLOCK), n_rows)
        xlo, xhi = _unpack_bf16_halves(xp[s % 2, rows, :])
        hg = _dot(xlo, wgb[:half, :]) + _dot(xhi, wgb[half:, :])
        hu = _dot(xlo, wub[:half, :]) + _dot(xhi, wub[half:, :])
        y = _dot((jax.nn.silu(hg) * hu).astype(BF16), wdb[...])
        if first:
            acc[rows, :] = y
        else:
            acc[rows, :] += y

    def all_rows(first):
        for_blocks(nb // 2, lambda q: mlp_rows(q * (2 * MOE_BLOCK), 2 * MOE_BLOCK, first))

        @pl.when(nb % 2 == 1)
        def _():
            mlp_rows((nb - 1) * MOE_BLOCK, MOE_BLOCK, first)

    @pl.when((s == 0) & (f == 0))
    def _():
        pending[0] = 0
        for_blocks(nb, lambda r: rows_in(0, r).start())

    @pl.when(nb > 0)
    def _():
        @pl.when(f == 0)
        def _():
            nxt = jnp.minimum(s + 1, n_sb - 1)
            nb_next = jnp.where(s + 1 < n_sb, nb_ref[nxt], 0)
            for_blocks(nb_next, lambda r: rows_in(nxt, r).start())

        wgb[...] = wg_ref[0].astype(BF16)
        wub[...] = wu_ref[0].astype(BF16)
        wdb[...] = wd_ref[0].astype(BF16)

        @pl.when(f == 0)
        def _():
            for_blocks(nb, lambda r: rows_in(s, r).wait())
            drain_out()
            all_rows(True)

        @pl.when(f > 0)
        def _():
            all_rows(False)

        @pl.when(f == last_f)
        def _():
            for_blocks(nb, lambda r: rows_out(r).start())
            pending[0] = nb

    @pl.when((s == n_sb - 1) & (f == last_f))
    def _():
        drain_out()
        acc[pl.ds(0, MOE_BLOCK), :] = jnp.zeros((MOE_BLOCK, acc.shape[1]), acc.dtype)
        tail0 = tot_ref[0]

        def zero_out(r):
            return pltpu.make_async_copy(acc.at[pl.ds(0, MOE_BLOCK)], out_ref.at[blk(tail0, r)],
                                         sem_out)

        ntail = (out_ref.shape[0] - tail0) // MOE_BLOCK
        for_blocks(ntail, lambda r: zero_out(r).start())
        for_blocks(ntail, lambda r: zero_out(r).wait())


def moe_experts(sb_expert, sb_row0, sb_blocks, total_rows, buf, wg, wu, wd):
    p, half = buf.shape
    d = 2 * half
    ff = wg.shape[2]
    nf = ff // FF_TILE
    f_eff = lambda s, f, nb: jnp.where(nb[s] > 0, f, nf - 1)
    grid_spec = pltpu.PrefetchScalarGridSpec(
        num_scalar_prefetch=4,
        grid=(sb_expert.shape[0], nf),
        in_specs=[pl.BlockSpec((1, d, FF_TILE), lambda s, f, e, r0, nb, t: (e[s], 0, f_eff(s, f, nb))),
                  pl.BlockSpec((1, d, FF_TILE), lambda s, f, e, r0, nb, t: (e[s], 0, f_eff(s, f, nb))),
                  pl.BlockSpec((1, FF_TILE, d), lambda s, f, e, r0, nb, t: (e[s], f_eff(s, f, nb), 0)),
                  pl.BlockSpec(memory_space=pl.ANY)],
        out_specs=pl.BlockSpec(memory_space=pl.ANY),
        scratch_shapes=[pltpu.VMEM((d, FF_TILE), BF16), pltpu.VMEM((d, FF_TILE), BF16),
                        pltpu.VMEM((FF_TILE, d), BF16), pltpu.VMEM((2, SUPER_ROWS, half), jnp.uint32),
                        pltpu.VMEM((SUPER_ROWS, d), F32), pltpu.SMEM((1,), I32),
                        pltpu.SemaphoreType.DMA((2,)), pltpu.SemaphoreType.DMA(())],
    )
    return pl.pallas_call(
        _moe_expert_body,
        grid_spec=grid_spec,
        out_shape=jax.ShapeDtypeStruct((p, d), F32),
        compiler_params=_cparams(("arbitrary", "arbitrary")),
        name="moe_experts",
    )(sb_expert, sb_row0, sb_blocks, total_rows, wg, wu, wd, buf)


def _moe_combine_body(dest_ref, x1_ref, route_ref, out_ref, y_ref, rows_ref, sem):
    tm, d = x1_ref.shape
    i = pl.program_id(0)
    sublanes = rows_ref.shape[3]

    def row_copy(tile, q, u, k):
        slot = dest_ref[(tile * tm + q * sublanes + u) * TOP_K + k]
        return pltpu.make_async_copy(out_ref.at[pl.ds(slot, 1)],
                                     rows_ref.at[tile % 2, k, q, pl.ds(u, 1)], sem.at[tile % 2])

    def for_rows(tile, fn):
        def body(q, c):
            for u in range(sublanes):
                for k in range(TOP_K):
                    fn(row_copy(tile, q, u, k))
            return c
        lax.fori_loop(0, tm // sublanes, body, 0)

    @pl.when(i == 0)
    def _():
        for_rows(0, lambda cp: cp.start())

    @pl.when(i + 1 < pl.num_programs(0))
    def _():
        for_rows(i + 1, lambda cp: cp.start())

    for_rows(i, lambda cp: cp.wait())
    route = route_ref[...]
    rows = rows_ref[i % 2]
    y_ref[...] = x1_ref[...] + (rows[0].reshape(tm, d) * route[:, 0:1]
                                + rows[1].reshape(tm, d) * route[:, 1:2])


def moe_combine(dest, x1, route, expert_out, *, tm=512):
    n, d = x1.shape
    grid_spec = pltpu.PrefetchScalarGridSpec(
        num_scalar_prefetch=1,
        grid=(n // tm,),
        in_specs=[pl.BlockSpec((tm, d), lambda i, *_: (i, 0)),
                  pl.BlockSpec((tm, LANES), lambda i, *_: (i, 0)),
                  pl.BlockSpec(memory_space=pl.ANY)],
        out_specs=pl.BlockSpec((tm, d), lambda i, *_: (i, 0)),
        scratch_shapes=[pltpu.VMEM((2, TOP_K, tm // SUBLANES, SUBLANES, d), F32),
                        pltpu.SemaphoreType.DMA((2,))],
    )
    return pl.pallas_call(
        _moe_combine_body,
        grid_spec=grid_spec,
        out_shape=jax.ShapeDtypeStruct((n, d), F32),
        compiler_params=_cparams(("arbitrary",)),
        name="moe_combine",
    )(dest, x1, route, expert_out)


def hierarchical_moe(x1, h2, route, wg, wu, wd):
    n = h2.shape[0]
    n_experts = wg.shape[0]
    m = n * TOP_K
    ids_row = route[:, 2:4].astype(I32).reshape(1, m)
    rank_row, counts = moe_rank(ids_row, n_experts)
    counts = counts[:, 0].astype(I32)
    padded = (counts + MOE_BLOCK - 1) // MOE_BLOCK * MOE_BLOCK
    pad_end = jnp.cumsum(padded)
    pad_start = pad_end - padded
    dest = moe_dest(ids_row, rank_row, pad_start.astype(F32).reshape(n_experts, 1)).reshape(m)
    n_slots = m + n_experts * MOE_BLOCK
    n_sb = (padded + SUPER_ROWS - 1) // SUPER_ROWS
    sb_end = jnp.cumsum(n_sb)
    sb_start = sb_end - n_sb
    s_idx = jnp.arange(n_experts + m // SUPER_ROWS, dtype=I32)
    used = s_idx < sb_end[-1]
    e_of = jnp.minimum(jnp.sum(sb_end[None, :] <= s_idx[:, None], axis=1), n_experts - 1)
    piece = s_idx - sb_start[e_of]
    sb_row0 = jnp.where(used, pad_start[e_of] + piece * SUPER_ROWS, 0).astype(I32)
    sb_rows = jnp.clip(padded[e_of] - piece * SUPER_ROWS, 0, SUPER_ROWS)
    sb_blocks = jnp.where(used, sb_rows // MOE_BLOCK, 0).astype(I32)
    sb_expert = jnp.where(used, e_of, e_of[jnp.maximum(sb_end[-1] - 1, 0)]).astype(I32)
    buf = moe_dispatch(dest, pad_start.astype(I32), pad_end.astype(I32), h2, n_slots)
    expert_out = moe_experts(sb_expert, sb_row0, sb_blocks, pad_end[-1:].astype(I32), buf,
                             wg, wu, wd)
    return moe_combine(dest, x1, route, expert_out)


def _w_regroup_body(w_ref, o_ref, *, c0, dl, il):
    w = w_ref[...]
    lane = lax.broadcasted_iota(I32, (1, LANES), 1)
    o_ref[:, :c0] = w[:, :c0].astype(BF16)
    o_ref[:, c0:c0 + LANES] = jnp.where(lane < dl, w[:, c0:c0 + LANES], 0.0).astype(BF16)
    o_ref[:, c0 + LANES:c0 + 2 * LANES] = jnp.where(
        lane < il, w[:, c0 + dl:c0 + dl + LANES], 0.0).astype(BF16)
    o_ref[:, c0 + 2 * LANES:] = w[:, c0 + dl + il:].astype(BF16)


def regroup_w_in(w_in, layer, c0, dl, il, *, tr=256):
    _, d, cols = w_in.shape
    out_cols = cols + 2 * LANES - dl - il
    return pl.pallas_call(
        functools.partial(_w_regroup_body, c0=c0, dl=dl, il=il),
        grid=(d // tr,),
        in_specs=[pl.BlockSpec((None, tr, cols), lambda i: (layer, i, 0))],
        out_specs=pl.BlockSpec((tr, out_cols), lambda i: (i, 0)),
        out_shape=jax.ShapeDtypeStruct((d, out_cols), BF16),
        compiler_params=_cparams(("parallel",)),
        name="regroup_w_in",
    )(w_in)


def _pad_cols(w, to):
    return jnp.pad(w, ((0, 0), (0, to - w.shape[1])))


def _pad_rows(w, to):
    return jnp.pad(w, ((0, to - w.shape[0]), (0, 0)))


def _layer(x2d, batch, norm1_w, w_in, mu, w0, w2, a0, a2, g2, k_k, k_a, r_k, gn_w, gn_b,
           q_norm_w, k_norm_w, sinks, proj_rwkv, proj_attn, w_out, norm2_w, wc, bc, wf, bf,
           wg, wu, wd):
    n, d = x2d.shape
    seq = n // batch
    width = w0.shape[0]
    dl, il, gl = w2.shape[0], a2.shape[0], g2.shape[0]
    q_width = proj_attn.shape[0]
    rwkv_cols = 3 * width + dl + il + gl
    w_in_all, layer = w_in
    kv_width = (w_in_all.shape[2] - rwkv_cols - q_width - 2 * d) // 2
    row = lambda v: v.reshape(1, -1).astype(F32)

    c0 = 3 * width
    w_all = regroup_w_in(w_in_all, layer, c0, dl, il)
    mu_p = jnp.concatenate([mu[:c0], jnp.pad(mu[c0:c0 + dl], (0, LANES - dl)),
                            jnp.pad(mu[c0 + dl:c0 + dl + il], (0, LANES - il)),
                            mu[c0 + dl + il:]]).reshape(1, -1)
    rwkv_w = c0 + 2 * LANES + gl
    proj = norm_proj(x2d, row(norm1_w), w_all, 2 * d)

    rt, at, bt, kt, vv, g, bonus, pc = rwkv_prep(
        proj, mu_p, row(w0), _pad_rows(w2, LANES).astype(BF16), row(a0),
        _pad_rows(a2, LANES).astype(BF16), g2.astype(BF16), row(k_k), row(k_a), row(r_k),
        seq_len=seq)
    a_out, b_out = mixers(rt, at, bt, kt, vv, pc, g, bonus, row(gn_w), row(gn_b), proj, q_norm_w,
                          k_norm_w, sinks.astype(F32), batch=batch, q_col0=rwkv_w,
                          q_width=q_width, kv_width=kv_width)

    n_groups, n_experts = wc.shape[1], wf.shape[1]
    router_w = _pad_cols(jnp.concatenate([wc, wf], axis=1), LANES)
    router_b = _pad_cols(jnp.concatenate([bc, bf]).reshape(1, -1), LANES)
    assert n_groups == N_GROUPS and n_experts == N_GROUPS * GROUP_SIZE
    x1, h2, route = merge_route(a_out, b_out, proj, rwkv_w + q_width + 2 * kv_width, x2d,
                                proj_rwkv.astype(BF16),
                                proj_attn.astype(BF16), w_out.astype(BF16), row(norm2_w),
                                router_w, router_b)
    return hierarchical_moe(x1, h2, route, wg, wu, wd)


def kernel(x, norm1_w, w_in, rwkv_mu, rwkv_w0, rwkv_w2, rwkv_a0, rwkv_a2, rwkv_g2, rwkv_k_k,
           rwkv_k_a, rwkv_r_k, rwkv_gn_w, rwkv_gn_b, q_norm_w, k_norm_w, attn_sinks, proj_rwkv,
           proj_attn, w_out, norm2_w, router_coarse_w, router_coarse_b, router_fine_w,
           router_fine_b, expert_w_gate, expert_w_up, expert_w_down):
    batch, seq, d = x.shape
    x2d = x.reshape(batch * seq, d)
    for layer in range(norm1_w.shape[0]):
        x2d = _layer(x2d, batch, norm1_w[layer], (w_in, layer), rwkv_mu[layer], rwkv_w0[layer],
                     rwkv_w2[layer], rwkv_a0[layer], rwkv_a2[layer], rwkv_g2[layer],
                     rwkv_k_k[layer], rwkv_k_a[layer], rwkv_r_k[layer].reshape(-1),
                     rwkv_gn_w[layer], rwkv_gn_b[layer], q_norm_w[layer], k_norm_w[layer],
                     attn_sinks[layer], proj_rwkv[layer], proj_attn[layer], w_out[layer],
                     norm2_w[layer], router_coarse_w[layer], router_coarse_b[layer],
                     router_fine_w[layer], router_fine_b[layer], expert_w_gate[layer],
                     expert_w_up[layer], expert_w_down[layer])
    return x2d.reshape(batch, seq, d)
```

```python
import functools
import math

import jax
import jax.numpy as jnp
from jax import lax
from jax.experimental import pallas as pl
from jax.experimental.pallas import tpu as pltpu

F32 = jnp.float32
BF16 = jnp.bfloat16
I32 = jnp.int32

NORM_EPS = 1e-6
GN_EPS = 64e-5
HEAD_DIM = 64
LANES = 128
SUBLANES = 8
CHUNK = 64
WINDOW = 128
MOE_BLOCK = 128
SUPER_ROWS = 1024
FF_TILE = 512
TOP_K = 2
N_GROUPS = 8
GROUP_SIZE = 8
VMEM_LIMIT = 56 * 1024 * 1024

_NT = (((1,), (1,)), ((), ()))
_TN = (((0,), (0,)), ((), ()))


def _dot(a, b):
    return jnp.dot(a, b, preferred_element_type=F32)


def _split2(x):
    hi = x.astype(BF16)
    mid = (x - hi.astype(F32)).astype(BF16)
    return hi, mid


def _select_sum(m, hi, mid):
    return _dot(m, hi) + _dot(m, mid)


def _pack_bf16_halves(x):
    w = x.shape[1] // 2
    lo = lax.bitcast_convert_type(x[:, :w].astype(BF16).astype(F32), jnp.uint32)
    hi = lax.bitcast_convert_type(x[:, w:].astype(BF16).astype(F32), jnp.uint32)
    return (lo >> 16) | (hi & jnp.uint32(0xFFFF0000))


def _unpack_bf16_halves(xp):
    lo = lax.bitcast_convert_type(xp << 16, F32).astype(BF16)
    hi = lax.bitcast_convert_type(xp & jnp.uint32(0xFFFF0000), F32).astype(BF16)
    return lo, hi


def _cparams(sem, vmem=VMEM_LIMIT):
    return pltpu.CompilerParams(dimension_semantics=sem, vmem_limit_bytes=vmem)


def _norm_proj_body(x_ref, g_ref, w_ref, o_ref, h_ref, *, first_gate_tile):
    j = pl.program_id(1)

    @pl.when(j == 0)
    def _():
        x = x_ref[...]
        ms = jnp.mean(x * x, axis=-1, keepdims=True)
        h_ref[...] = (x * lax.rsqrt(ms + NORM_EPS) * g_ref[...]).astype(BF16)

    acc = _dot(h_ref[...], w_ref[...])

    @pl.when(j < first_gate_tile)
    def _():
        o_ref[...] = acc.astype(o_ref.dtype)

    @pl.when(j >= first_gate_tile)
    def _():
        o_ref[...] = jax.nn.sigmoid(acc).astype(o_ref.dtype)


def norm_proj(x2d, gain, w, gate_cols, *, tm=1024, tn=1024):
    n, d = x2d.shape
    c = w.shape[1]
    tm = min(tm, n)
    assert c % tn == 0 and gate_cols % tn == 0
    return pl.pallas_call(
        functools.partial(_norm_proj_body, first_gate_tile=(c - gate_cols) // tn),
        grid=(n // tm, c // tn),
        in_specs=[pl.BlockSpec((tm, d), lambda i, j: (i, 0)),
                  pl.BlockSpec((1, d), lambda i, j: (0, 0)),
                  pl.BlockSpec((d, tn), lambda i, j: (0, j))],
        out_specs=pl.BlockSpec((tm, tn), lambda i, j: (i, j)),
        out_shape=jax.ShapeDtypeStruct((n, c), BF16),
        scratch_shapes=[pltpu.VMEM((tm, d), BF16)],
        compiler_params=_cparams(("parallel", "arbitrary")),
        name="norm_proj",
    )(x2d, gain, w)


def _head_sum(x, e, et):
    s = _dot(x.astype(BF16), e)
    hi, mid = _split2(s)
    return _dot(hi, et) + _dot(mid, et)


def _rwkv_prep_body(p_ref, pprev_ref, mu_ref, w0_ref, w2_ref, a0_ref, a2_ref, g2_ref,
                    kk_ref, ka_ref, rk_ref, e_ref, et_ref, tri_ref, sel_ref,
                    rt_ref, at_ref, bt_ref, kt_ref, v_ref, g_ref, bonus_ref, pc_ref,
                    *, seq_len, width):
    tm = p_ref.shape[0]
    w_ = width
    first = (pl.program_id(0) * tm) % seq_len == 0
    p = p_ref[...].astype(F32)
    last = pprev_ref.shape[0] - 1
    prev_row = jnp.where(first, 0.0, pprev_ref[last:, :].astype(F32))
    row = lax.broadcasted_iota(I32, (tm, 1), 0)
    shifted = jnp.where(row == 0, prev_row, pltpu.roll(p, 1, 0))
    m = p + (shifted - p) * mu_ref[...]
    r = m[:, 0:w_]
    k = m[:, w_:2 * w_]
    v = m[:, 2 * w_:3 * w_]
    xw = m[:, 3 * w_:3 * w_ + 128]
    xa = m[:, 3 * w_ + 128:3 * w_ + 256]
    xg = m[:, 3 * w_ + 256:]

    z = -(w0_ref[...] + _dot(jnp.tanh(xw).astype(BF16), w2_ref[...]))
    softplus = jnp.maximum(z, 0.0) + jnp.log1p(jnp.exp(-jnp.abs(z)))
    logw = -jnp.exp(-softplus - 0.5)
    a = jax.nn.sigmoid(a0_ref[...] + _dot(xa.astype(BF16), a2_ref[...]))
    g = _dot(jax.nn.sigmoid(xg).astype(BF16), g2_ref[...])

    e = e_ref[...]
    et = et_ref[...]
    kk = k * kk_ref[...]
    kk = kk * jnp.minimum(lax.rsqrt(_head_sum(kk * kk, e, et)), 1e12)
    kmod = k * (1.0 + (a - 1.0) * ka_ref[...])
    bonus = _head_sum(r * kmod * rk_ref[...], e, et) * v

    lw_hi, lw_mid = _split2(logw)
    cum = _select_sum(tri_ref[...], lw_hi, lw_mid)
    pc_ref[...] = jnp.exp(_select_sum(sel_ref[...], lw_hi, lw_mid))
    inv = jnp.exp(-cum)
    rt_ref[...] = (r * jnp.exp(cum)).astype(BF16)
    at_ref[...] = (-kk * jnp.exp(cum - logw)).astype(BF16)
    bt_ref[...] = (kk * a * inv).astype(BF16)
    kt_ref[...] = (kmod * inv).astype(BF16)
    v_ref[...] = v.astype(BF16)
    g_ref[...] = g.astype(BF16)
    bonus_ref[...] = bonus


def _head_indicator(width):
    heads = width // HEAD_DIM
    c = jnp.arange(width)[:, None] // HEAD_DIM
    h = jnp.arange(LANES)[None, :]
    e = (c == h).astype(BF16)
    assert heads <= LANES
    return e, e.T


def rwkv_prep(p, mu, w0, w2, a0, a2, g2, k_k, k_a, r_k, *, seq_len, tm=512):
    n = p.shape[0]
    cols = mu.shape[1]
    prev_rows = 2 * SUBLANES
    width = w0.shape[1]
    nchunk = tm // CHUNK
    e, et = _head_indicator(width)
    t = jnp.arange(tm)
    same = (t[:, None] // CHUNK) == (t[None, :] // CHUNK)
    tri = (same & (t[:, None] >= t[None, :])).astype(BF16)
    sel = ((t[None, :] // CHUNK) == jnp.arange(nchunk)[:, None]).astype(BF16)
    const = lambda shape: pl.BlockSpec(shape, lambda i: (0, 0))
    stream = lambda dt: jax.ShapeDtypeStruct((n, width), dt)
    outs = pl.pallas_call(
        functools.partial(_rwkv_prep_body, seq_len=seq_len, width=width),
        grid=(n // tm,),
        in_specs=[pl.BlockSpec((tm, cols), lambda i: (i, 0)),
                  pl.BlockSpec((prev_rows, cols),
                               lambda i: (jnp.maximum(i * (tm // prev_rows) - 1, 0), 0)),
                  const((1, cols)), const((1, width)), const(w2.shape), const((1, width)),
                  const(a2.shape), const(g2.shape), const((1, width)), const((1, width)),
                  const((1, width)), const(e.shape), const(et.shape), const(tri.shape),
                  const(sel.shape)],
        out_specs=[pl.BlockSpec((tm, width), lambda i: (i, 0))] * 7
                  + [pl.BlockSpec((nchunk, width), lambda i: (i, 0))],
        out_shape=[stream(BF16)] * 6 + [stream(F32),
                   jax.ShapeDtypeStruct((n // CHUNK, width), F32)],
        compiler_params=_cparams(("parallel",)),
        name="rwkv_prep",
    )(p, p, mu, w0, w2, a0, a2, g2, k_k, k_a, r_k, e, et, tri, sel)
    return outs


def _chunk_pairs(rts, ats, bts, kts, vvs, pcs, s_prevs, masks):
    lo, strict, incl = masks
    c = rts[0].shape[0]
    c2 = 2 * c
    zero = jnp.zeros_like(rts[0])
    each = lambda f, *ls: [f(*a) for a in zip(*ls)]

    def stack(x):
        return jnp.concatenate([jnp.where(lo, x, zero), jnp.where(lo, zero, x)], axis=0)

    ar = each(lambda a, r: jnp.concatenate([stack(a), stack(r)], axis=0), ats, rts)
    bk = each(lambda b, k: jnp.concatenate([stack(b), stack(k)], axis=0), bts, kts)
    vb = each(stack, vvs)
    gram = each(lambda x, y: lax.dot_general(x, y, _NT, preferred_element_type=F32), ar, bk)
    ars = each(lambda x, s: lax.dot_general(x, s.astype(BF16), _NT, preferred_element_type=F32),
               ar, s_prevs)
    a_ab = each(lambda g: jnp.where(strict, g[:c2, :c2], 0.0), gram)
    rhs = each(lambda g, v, x: x[:c2] + _dot(jnp.where(strict, g[:c2, c2:], 0.0).astype(BF16), v),
               gram, vb, ars)

    u = rhs
    nk = a_ab
    steps = (c - 1).bit_length()
    for k in range(steps):
        nkb = each(lambda n: n.astype(BF16), nk)
        if k + 1 < steps:
            prod = each(lambda n, uu: _dot(n, jnp.concatenate([n, uu.astype(BF16)], axis=1)),
                        nkb, u)
            nk = each(lambda p: p[:, :c2], prod)
            u = each(lambda uu, p: uu + p[:, c2:], u, prod)
        else:
            u = each(lambda uu, n: uu + _dot(n, uu.astype(BF16)), u, nkb)

    uv = each(lambda uu, v: jnp.concatenate([uu.astype(BF16), v], axis=0), u, vb)
    a_r = each(lambda g: jnp.where(incl, g[c2:, :], 0.0).astype(BF16), gram)
    ybd = each(lambda x, a, w: x[c2:] + _dot(a, w), ars, a_r, uv)
    ys = each(lambda yb: yb[:c] + yb[c:], ybd)

    bk_end = each(lambda x, pc: (x.astype(F32) * pc).astype(BF16), bk, pcs)
    s_new = each(lambda s, pc, w, x: s * pc + lax.dot_general(w, x, _TN, preferred_element_type=F32),
                 s_prevs, pcs, uv, bk_end)
    return ys, s_new


def _attn_kv(kcat, vcat, k_gain, bd, lo):
    kv_heads = kcat.shape[1] // HEAD_DIM
    kv_cols = [slice(jt * LANES, (jt + 1) * LANES) for jt in range(kv_heads // 2)]
    kn = [_head_rmsnorm(kcat[:, c], k_gain, bd) for c in kv_cols]
    kn_r = [pltpu.roll(x, HEAD_DIM, 1) for x in kn]
    vt = [vcat[:, c] for c in kv_cols]
    vt_r = [pltpu.roll(x, HEAD_DIM, 1) for x in vt]
    own = lambda hk: lo if hk % 2 == 0 else jnp.logical_not(lo)
    k2 = [jnp.where(own(hk), kn[hk // 2], kn_r[hk // 2]).astype(BF16) for hk in range(kv_heads)]
    v2 = [jnp.where(own(hk), vt[hk // 2], vt_r[hk // 2]).astype(BF16) for hk in range(kv_heads)]
    return k2, v2


def _attn_tiles(tiles, q_of, k2, v2, q_gain, bd, valid, lo, top_rows, sink_ref, tiles_per_kv):
    each = lambda f, *ls: [f(*a) for a in zip(*ls)]
    blk = valid.shape[0] // 2
    qn = [_head_rmsnorm(q_of(t).astype(F32), q_gain, bd) for t in tiles]
    qst = [jnp.concatenate([jnp.where(lo, x, 0.0), jnp.where(lo, 0.0, x)], axis=0).astype(BF16)
           for x in qn]
    s = [jnp.where(valid, lax.dot_general(x, k2[t // tiles_per_kv], _NT,
                                          preferred_element_type=F32), -jnp.inf)
         for t, x in zip(tiles, qst)]
    sink = [jnp.where(top_rows, sink_ref[2 * t], sink_ref[2 * t + 1]) for t in tiles]
    mx = each(lambda x, sk: jnp.maximum(jnp.max(x, axis=-1, keepdims=True), sk), s, sink)
    pr = each(lambda x, m: jnp.exp(x - m), s, mx)
    inv = each(lambda p, sk, m: 1.0 / (jnp.sum(p, axis=-1, keepdims=True) + jnp.exp(sk - m)),
               pr, sink, mx)
    o = [_dot((p * r).astype(BF16), v2[t // tiles_per_kv]) for t, p, r in zip(tiles, pr, inv)]
    return [jnp.where(lo, x[:blk], x[blk:]) for x in o]


def _mixers_body(sink_ref, rt_ref, at_ref, bt_ref, kt_ref, v_ref, pc_ref, g_ref, bonus_ref,
                 gnw_ref, gnb_ref, e_ref, et_ref, *refs, group):
    (kp_ref, vp_ref, kc_ref, vc_ref, qg_ref, kg_ref, bd_ref,
     oa_ref, ob_ref, s_ref, y_ref) = refs[-11:]
    q_pieces = refs[:-11]
    tr, width = rt_ref.shape
    npairs = width // LANES
    blk = WINDOW
    seq_first = pl.program_id(1) == 0

    @pl.when(seq_first)
    def _():
        s_ref[...] = jnp.zeros_like(s_ref)

    lo_c = lax.broadcasted_iota(I32, (CHUNK, LANES), 1) < HEAD_DIM
    ri = lax.broadcasted_iota(I32, (2 * CHUNK, 2 * CHUNK), 0)
    ci = lax.broadcasted_iota(I32, (2 * CHUNK, 2 * CHUNK), 1)
    same = (ri // CHUNK) == (ci // CHUNK)
    ri2 = lax.broadcasted_iota(I32, (2 * CHUNK, 4 * CHUNK), 0)
    ci2 = lax.broadcasted_iota(I32, (2 * CHUNK, 4 * CHUNK), 1) % (2 * CHUNK)
    incl = ((ri2 // CHUNK) == (ci2 // CHUNK)) & (ri2 >= ci2)
    masks = (lo_c, same & (ri > ci), incl)
    cols = [slice(pr * LANES, (pr + 1) * LANES) for pr in range(npairs)]

    def chunk_step(c):
        rows = pl.ds(pl.multiple_of(c * CHUNK, CHUNK), CHUNK)
        this_chunk = lax.broadcasted_iota(I32, (pc_ref.shape[0], 1), 0) == c
        pcs = [jnp.sum(jnp.where(this_chunk, pc_ref[:, cl], 0.0), axis=0, keepdims=True)
               for cl in cols]
        load = lambda ref: [ref[rows, cl] for cl in cols]
        ys, s_new = _chunk_pairs(load(rt_ref), load(at_ref), load(bt_ref), load(kt_ref),
                                 load(v_ref), pcs, [s_ref[pr] for pr in range(npairs)], masks)
        for pr in range(npairs):
            s_ref[pr] = s_new[pr]
            y_ref[rows, cols[pr]] = ys[pr]

    bd = bd_ref[...]
    q_gain = qg_ref[...] * (HEAD_DIM ** -0.5)
    lo = lax.broadcasted_iota(I32, (1, LANES), 1) < HEAD_DIM
    qi = lax.broadcasted_iota(I32, (2 * blk, 2 * blk), 0)
    kj = lax.broadcasted_iota(I32, (2 * blk, 2 * blk), 1)
    rel = blk + (qi % blk) - kj
    in_window = (rel >= 0) & (rel < WINDOW)
    top_rows = lax.broadcasted_iota(I32, (2 * blk, 1), 0) < blk
    q_tiles = ob_ref.shape[1] // LANES
    piece_tiles = q_pieces[0].shape[1] // LANES
    tiles_per_kv = group // 2
    chunks_per_blk = blk // CHUNK

    def block_step(jb, carry):
        cur = pl.ds(pl.multiple_of(jb * blk, blk), blk)
        prv = pl.ds(pl.multiple_of(jnp.maximum(jb - 1, 0) * blk, blk), blk)
        first_blk = jb == 0
        kprev = jnp.where(first_blk, kp_ref[...], kc_ref[prv, :])
        vprev = jnp.where(first_blk, vp_ref[...], vc_ref[prv, :])
        kcat = jnp.concatenate([kprev, kc_ref[cur, :]], axis=0).astype(F32)
        vcat = jnp.concatenate([vprev, vc_ref[cur, :]], axis=0).astype(F32)
        k2, v2 = _attn_kv(kcat, vcat, kg_ref[...], bd, lo)
        first_key = jnp.where(seq_first & first_blk, blk, 0)
        valid = in_window & (kj >= first_key)
        q_of = lambda t: q_pieces[t // piece_tiles][cur, cols[t % piece_tiles]]
        for part in range(chunks_per_blk):
            chunk_step(jb * chunks_per_blk + part)
            tiles = list(range(part * q_tiles // chunks_per_blk,
                               (part + 1) * q_tiles // chunks_per_blk))
            outs = _attn_tiles(tiles, q_of, k2, v2, q_gain, bd, valid, lo, top_rows, sink_ref,
                               tiles_per_kv)
            for t, o in zip(tiles, outs):
                ob_ref[cur, cols[t]] = o.astype(ob_ref.dtype)
        return carry

    lax.fori_loop(0, tr // blk, block_step, 0)

    e = e_ref[...]
    et = et_ref[...]
    y = y_ref[...]
    mean = _head_sum(y, e, et) * (1.0 / HEAD_DIM)
    d = y - mean
    var = _head_sum(d * d, e, et) * (1.0 / HEAD_DIM)
    out = d * lax.rsqrt(var + GN_EPS) * gnw_ref[...] + gnb_ref[...]
    out = (out + bonus_ref[...]) * g_ref[...].astype(F32)
    oa_ref[...] = out.astype(oa_ref.dtype)


def mixers(rt, at, bt, kt, vv, pc, g, bonus, gn_w, gn_b, proj, q_gain, k_gain, sinks, *, batch,
           q_col0, q_width, kv_width, tr=512):
    n, width = rt.shape
    nst = n // batch // tr
    assert q_width == width and WINDOW % CHUNK == 0 and tr % WINDOW == 0
    group = (q_width // HEAD_DIM) // (kv_width // HEAD_DIM)
    q_piece = math.gcd(q_col0, q_width)
    assert q_piece % LANES == 0 and (q_col0 + q_width) % kv_width == 0
    n_q = q_width // q_piece
    kcol = (q_col0 + q_width) // kv_width
    e, et = _head_indicator(width)
    t = jnp.arange(LANES)
    bd = ((t[:, None] // HEAD_DIM) == (t[None, :] // HEAD_DIM)).astype(BF16)
    qg = jnp.tile(q_gain.reshape(1, HEAD_DIM), (1, 2))
    kg = jnp.tile(k_gain.reshape(1, HEAD_DIM), (1, 2))
    tile = lambda rows, w, col=0: pl.BlockSpec((rows, w), lambda b, s, *_: (b * nst + s, col))
    prev = lambda col: pl.BlockSpec(
        (WINDOW, kv_width),
        lambda b, s, *_: (jnp.maximum((b * nst + s) * (tr // WINDOW) - 1, 0), col))
    const = lambda shape: pl.BlockSpec(shape, lambda b, s, *_: (0, 0))
    grid_spec = pltpu.PrefetchScalarGridSpec(
        num_scalar_prefetch=1,
        grid=(batch, nst),
        in_specs=[tile(tr, width)] * 5 + [tile(tr // CHUNK, width), tile(tr, width),
                  tile(tr, width), const((1, width)), const((1, width)), const(e.shape),
                  const(et.shape)]
                 + [tile(tr, q_piece, q_col0 // q_piece + i) for i in range(n_q)]
                 + [prev(kcol), prev(kcol + 1), tile(tr, kv_width, kcol),
                    tile(tr, kv_width, kcol + 1), const((1, LANES)), const((1, LANES)),
                    const((LANES, LANES))],
        out_specs=[tile(tr, width), tile(tr, q_width)],
        scratch_shapes=[pltpu.VMEM((width // LANES, LANES, LANES), F32),
                        pltpu.VMEM((tr, width), F32)],
    )
    return pl.pallas_call(
        functools.partial(_mixers_body, group=group),
        grid_spec=grid_spec,
        out_shape=[jax.ShapeDtypeStruct((n, width), BF16),
                   jax.ShapeDtypeStruct((n, q_width), BF16)],
        compiler_params=_cparams(("parallel", "arbitrary")),
        name="mixers",
    )(sinks, rt, at, bt, kt, vv, pc, g, bonus, gn_w, gn_b, e, et, *([proj] * (n_q + 4)), qg, kg, bd)


def _head_rmsnorm(x, gain, bd):
    hi, mid = _split2(x * x)
    ms = (_dot(hi, bd) + _dot(mid, bd)) * (1.0 / HEAD_DIM)
    return x * lax.rsqrt(ms + NORM_EPS) * gain


def _merge_route_body(a_ref, b_ref, *refs, gate_pieces):
    gates = refs[:2 * gate_pieces]
    (x_ref, pr_ref, pa_ref, wo_ref, n2_ref, rhm_ref, rh_ref, rb_ref,
     x1_ref, h2_ref, route_ref, logits_ref) = refs[2 * gate_pieces:]
    gate = lambda pieces: jnp.concatenate([g[...] for g in pieces], axis=1).astype(F32)
    @pl.when(pl.program_id(0) == 0)
    def _():
        logits_ref[...] = jnp.zeros_like(logits_ref)

    _route(logits_ref[...], route_ref)

    merged = (gate(gates[:gate_pieces]) * _dot(a_ref[...], pr_ref[...])
              + gate(gates[gate_pieces:]) * _dot(b_ref[...], pa_ref[...]))
    x1 = x_ref[...] + _dot(merged.astype(BF16), wo_ref[...])
    x1_ref[...] = x1
    ms = jnp.mean(x1 * x1, axis=-1, keepdims=True)
    h2 = x1 * lax.rsqrt(ms + NORM_EPS) * n2_ref[...]
    h2_ref[...] = _pack_bf16_halves(h2)

    hi, mid = _split2(h2)
    hw = _dot(hi, rhm_ref[...])
    logits_ref[...] = hw[:, :LANES] + (hw[:, LANES:] + _dot(mid, rh_ref[...])) + rb_ref[...]


def _route(logits, route_ref):
    lane = lax.broadcasted_iota(I32, logits.shape, 1)
    big = jnp.int32(1 << 20)
    neg = -jnp.inf
    is_coarse = lane < N_GROUPS
    cl = jnp.where(is_coarse, logits, neg)
    ce = jnp.exp(cl - jnp.max(cl, axis=-1, keepdims=True))
    cp = ce / jnp.sum(ce, axis=-1, keepdims=True)
    g_prob = jnp.max(cp, axis=-1, keepdims=True)
    g_idx = jnp.min(jnp.where(is_coarse & (cp == g_prob), lane, big), axis=-1, keepdims=True)

    fine_lane = lane - N_GROUPS
    in_group = ((lane >= N_GROUPS) & (lane < N_GROUPS + N_GROUPS * GROUP_SIZE)
                & ((fine_lane // GROUP_SIZE) == g_idx))
    fl = jnp.where(in_group, logits, neg)
    fe = jnp.exp(fl - jnp.max(fl, axis=-1, keepdims=True))
    fp = fe / jnp.sum(fe, axis=-1, keepdims=True)
    p1 = jnp.max(jnp.where(in_group, fp, -1.0), axis=-1, keepdims=True)
    i1 = jnp.min(jnp.where(in_group & (fp == p1), lane, big), axis=-1, keepdims=True)
    rest = in_group & (lane != i1)
    p2 = jnp.max(jnp.where(rest, fp, -1.0), axis=-1, keepdims=True)
    i2 = jnp.min(jnp.where(rest & (fp == p2), lane, big), axis=-1, keepdims=True)
    den = p1 + p2
    w1 = g_prob * p1 / den
    w2 = g_prob * p2 / den
    e1 = (i1 - N_GROUPS).astype(F32)
    e2 = (i2 - N_GROUPS).astype(F32)
    route_ref[...] = jnp.where(lane == 0, w1, jnp.where(lane == 1, w2,
                     jnp.where(lane == 2, e1, jnp.where(lane == 3, e2, 0.0))))


def merge_route(a_out, b_out, proj, gate_col0, x2d, proj_r, proj_a, w_out, norm2_w, router_w,
                router_b, *, tm=256):
    n, d = x2d.shape
    g_piece = math.gcd(gate_col0, d)
    assert g_piece % LANES == 0
    gate_pieces = d // g_piece
    wa = a_out.shape[1]
    rh, rm = _split2(router_w)
    rhm = jnp.concatenate([rh, rm], axis=1)
    nt = n // tm
    const = lambda arr: pl.BlockSpec(arr.shape, lambda i: (0, 0), pipeline_mode=pl.Buffered(1))
    row = lambda w, col=0: pl.BlockSpec((tm, w), lambda i: (jnp.minimum(i, nt - 1), col))
    return pl.pallas_call(
        functools.partial(_merge_route_body, gate_pieces=gate_pieces),
        grid=(nt + 1,),
        in_specs=[row(wa), row(wa)]
                 + [row(g_piece, gate_col0 // g_piece + t) for t in range(2 * gate_pieces)]
                 + [row(d),
                  const(proj_r), const(proj_a), const(w_out), const(norm2_w),
                  const(rhm), const(rh), const(router_b)],
        out_specs=[row(d), row(d // 2),
                   pl.BlockSpec((tm, LANES), lambda i: (jnp.maximum(i - 1, 0), 0))],
        scratch_shapes=[pltpu.VMEM((tm, LANES), F32)],
        out_shape=[jax.ShapeDtypeStruct((n, d), F32), jax.ShapeDtypeStruct((n, d // 2), jnp.uint32),
                   jax.ShapeDtypeStruct((n, LANES), F32)],
        compiler_params=_cparams(("arbitrary",)),
        name="merge_route",
    )(a_out, b_out, *([proj] * (2 * gate_pieces)), x2d, proj_r, proj_a, w_out, norm2_w, rhm, rh,
      router_b)


def _moe_rank_body(ids_ref, ut_ref, rank_ref, counts_ref, carry_ref):
    @pl.when(pl.program_id(0) == 0)
    def _():
        carry_ref[...] = jnp.zeros_like(carry_ref)

    ids = ids_ref[...]
    n_exp = carry_ref.shape[0]
    sub = lax.broadcasted_iota(I32, (n_exp, ids.shape[1]), 0)
    onehot = (sub == ids).astype(F32)
    before = _dot(onehot.astype(BF16), ut_ref[...])
    carry = carry_ref[...]
    rank_ref[...] = jnp.sum(onehot * (before + carry), axis=0, keepdims=True).astype(I32)
    carry = carry + jnp.sum(onehot, axis=1, keepdims=True)
    carry_ref[...] = carry
    counts_ref[...] = jnp.broadcast_to(carry, counts_ref.shape)


def moe_rank(ids_row, n_experts, *, tb=1024):
    m = ids_row.shape[1]
    t = jnp.arange(tb)
    ut = (t[:, None] < t[None, :]).astype(BF16)
    return pl.pallas_call(
        _moe_rank_body,
        grid=(m // tb,),
        in_specs=[pl.BlockSpec((1, tb), lambda i: (0, i)),
                  pl.BlockSpec((tb, tb), lambda i: (0, 0))],
        out_specs=[pl.BlockSpec((1, tb), lambda i: (0, i)),
                   pl.BlockSpec((n_experts, LANES), lambda i: (0, 0))],
        out_shape=[jax.ShapeDtypeStruct((1, m), I32),
                   jax.ShapeDtypeStruct((n_experts, LANES), F32)],
        scratch_shapes=[pltpu.VMEM((n_experts, 1), F32)],
        compiler_params=_cparams(("arbitrary",)),
        name="moe_rank",
    )(ids_row, ut)


def _moe_dest_body(ids_ref, rank_ref, start_ref, dest_ref):
    ids = ids_ref[...]
    sub = lax.broadcasted_iota(I32, (start_ref.shape[0], ids.shape[1]), 0)
    start = jnp.sum(jnp.where(sub == ids, start_ref[...], 0.0), axis=0, keepdims=True)
    dest_ref[...] = start.astype(I32) + rank_ref[...]


def moe_dest(ids_row, rank_row, pad_start_col, *, tb=2048):
    m = ids_row.shape[1]
    row = pl.BlockSpec((1, tb), lambda i: (0, i))
    return pl.pallas_call(
        _moe_dest_body,
        grid=(m // tb,),
        in_specs=[row, row, pl.BlockSpec(pad_start_col.shape, lambda i: (0, 0))],
        out_specs=row,
        out_shape=jax.ShapeDtypeStruct((1, m), I32),
        compiler_params=_cparams(("parallel",)),
        name="moe_dest",
    )(ids_row, rank_row, pad_start_col)


def _moe_dispatch_body(dest_ref, start_ref, end_ref, h_ref, buf_ref, zeros_ref, sem, zsem):
    tm = h_ref.shape[0] * h_ref.shape[1]
    base = pl.program_id(0) * tm * TOP_K

    @pl.when(pl.program_id(0) == 0)
    def _():
        zeros_ref[...] = jnp.zeros_like(zeros_ref)

        def zero_last_block(e):
            last = pl.multiple_of(end_ref[e] - MOE_BLOCK, MOE_BLOCK)
            return pltpu.make_async_copy(zeros_ref, buf_ref.at[pl.ds(last, MOE_BLOCK)], zsem)

        def zstart(e, c):
            @pl.when(end_ref[e] > start_ref[e])
            def _():
                zero_last_block(e).start()
            return c

        def zwait(e, c):
            @pl.when(end_ref[e] > start_ref[e])
            def _():
                zero_last_block(e).wait()
            return c

        lax.fori_loop(0, start_ref.shape[0], zstart, 0)
        lax.fori_loop(0, start_ref.shape[0], zwait, 0)

        tail0 = end_ref[end_ref.shape[0] - 1]

        def zero_tail_block(r):
            first = pl.multiple_of(tail0 + r * MOE_BLOCK, MOE_BLOCK)
            return pltpu.make_async_copy(zeros_ref, buf_ref.at[pl.ds(first, MOE_BLOCK)], zsem)

        def tstart(r, c):
            zero_tail_block(r).start()
            return c

        def twait(r, c):
            zero_tail_block(r).wait()
            return c

        n_tail = (buf_ref.shape[0] - tail0) // MOE_BLOCK
        lax.fori_loop(0, n_tail, tstart, 0)
        lax.fori_loop(0, n_tail, twait, 0)

    def for_rows(fn):
        def body(q, c):
            for u in range(SUBLANES):
                for k in range(TOP_K):
                    slot = dest_ref[base + (q * SUBLANES + u) * TOP_K + k]
                    fn(pltpu.make_async_copy(h_ref.at[q, pl.ds(u, 1)], buf_ref.at[pl.ds(slot, 1)],
                                             sem), k)
            return c
        lax.fori_loop(0, h_ref.shape[0], body, 0)

    for_rows(lambda cp, k: cp.start(priority=k % 2))
    for_rows(lambda cp, k: cp.wait())


def moe_dispatch(dest, pad_start, pad_end, h2, n_slots, *, tm=512):
    n, d = h2.shape
    h2 = h2.reshape(n // SUBLANES, SUBLANES, d)
    grid_spec = pltpu.PrefetchScalarGridSpec(
        num_scalar_prefetch=3,
        grid=(n // tm,),
        in_specs=[pl.BlockSpec((tm // SUBLANES, SUBLANES, d), lambda i, *_: (i, 0, 0))],
        out_specs=pl.BlockSpec(memory_space=pl.ANY),
        scratch_shapes=[pltpu.VMEM((MOE_BLOCK, d), h2.dtype),
                        pltpu.SemaphoreType.DMA(()), pltpu.SemaphoreType.DMA(())],
    )
    return pl.pallas_call(
        _moe_dispatch_body,
        grid_spec=grid_spec,
        out_shape=jax.ShapeDtypeStruct((n_slots, d), h2.dtype),
        compiler_params=_cparams(("arbitrary",)),
        name="moe_dispatch",
    )(dest, pad_start, pad_end, h2)


def _moe_expert_body(sbe_ref, row0_ref, nb_ref, tot_ref, wg_ref, wu_ref, wd_ref, buf_ref, out_ref,
                     wgb, wub, wdb, xp, acc, pending, sem_in, sem_out):
    del sbe_ref
    s = pl.program_id(0)
    f = pl.program_id(1)
    n_sb = pl.num_programs(0)
    last_f = pl.num_programs(1) - 1
    nb = nb_ref[s]
    half = xp.shape[2]

    def blk(first_row, r):
        return pl.ds(pl.multiple_of(first_row + r * MOE_BLOCK, MOE_BLOCK), MOE_BLOCK)

    def rows_in(sb, r):
        return pltpu.make_async_copy(buf_ref.at[blk(row0_ref[sb], r)], xp.at[sb % 2, blk(0, r)],
                                     sem_in.at[sb % 2])

    def rows_out(r):
        return pltpu.make_async_copy(acc.at[blk(0, r)], out_ref.at[blk(row0_ref[s], r)], sem_out)

    def for_blocks(n, fn):
        def body(r, carry):
            fn(r)
            return carry
        lax.fori_loop(0, n, body, 0)

    def drain_out():
        for_blocks(pending[0], lambda r: rows_out(0).wait())
        pending[0] = 0

    def mlp_rows(first_row, n_rows, first):
        rows = pl.ds(pl.multiple_of(first_row, MOE_BLOCK), n_rows)
        xlo, xhi = _unpack_bf16_halves(xp[s % 2, rows, :])
        hg = _dot(xlo, wgb[:half, :]) + _dot(xhi, wgb[half:, :])
        hu = _dot(xlo, wub[:half, :]) + _dot(xhi, wub[half:, :])
        y = _dot((jax.nn.silu(hg) * hu).astype(BF16), wdb[...])
        if first:
            acc[rows, :] = y
        else:
            acc[rows, :] += y

    def all_rows(first):
        for_blocks(nb // 2, lambda q: mlp_rows(q * (2 * MOE_BLOCK), 2 * MOE_BLOCK, first))

        @pl.when(nb % 2 == 1)
        def _():
            mlp_rows((nb - 1) * MOE_BLOCK, MOE_BLOCK, first)

    @pl.when((s == 0) & (f == 0))
    def _():
        pending[0] = 0
        for_blocks(nb, lambda r: rows_in(0, r).start())

    @pl.when(nb > 0)
    def _():
        @pl.when(f == 0)
        def _():
            nxt = jnp.minimum(s + 1, n_sb - 1)
            nb_next = jnp.where(s + 1 < n_sb, nb_ref[nxt], 0)
            for_blocks(nb_next, lambda r: rows_in(nxt, r).start())

        wgb[...] = wg_ref[0].astype(BF16)
        wub[...] = wu_ref[0].astype(BF16)
        wdb[...] = wd_ref[0].astype(BF16)

        @pl.when(f == 0)
        def _():
            for_blocks(nb, lambda r: rows_in(s, r).wait())
            drain_out()
            all_rows(True)

        @pl.when(f > 0)
        def _():
            all_rows(False)

        @pl.when(f == last_f)
        def _():
            for_blocks(nb, lambda r: rows_out(r).start())
            pending[0] = nb

    @pl.when((s == n_sb - 1) & (f == last_f))
    def _():
        drain_out()
        acc[pl.ds(0, MOE_BLOCK), :] = jnp.zeros((MOE_BLOCK, acc.shape[1]), acc.dtype)
        tail0 = tot_ref[0]

        def zero_out(r):
            return pltpu.make_async_copy(acc.at[pl.ds(0, MOE_BLOCK)], out_ref.at[blk(tail0, r)],
                                         sem_out)

        ntail = (out_ref.shape[0] - tail0) // MOE_BLOCK
        for_blocks(ntail, lambda r: zero_out(r).start())
        for_blocks(ntail, lambda r: zero_out(r).wait())


def moe_experts(sb_expert, sb_row0, sb_blocks, total_rows, buf, wg, wu, wd):
    p, half = buf.shape
    d = 2 * half
    ff = wg.shape[2]
    nf = ff // FF_TILE
    f_eff = lambda s, f, nb: jnp.where(nb[s] > 0, f, nf - 1)
    grid_spec = pltpu.PrefetchScalarGridSpec(
        num_scalar_prefetch=4,
        grid=(sb_expert.shape[0], nf),
        in_specs=[pl.BlockSpec((1, d, FF_TILE), lambda s, f, e, r0, nb, t: (e[s], 0, f_eff(s, f, nb))),
                  pl.BlockSpec((1, d, FF_TILE), lambda s, f, e, r0, nb, t: (e[s], 0, f_eff(s, f, nb))),
                  pl.BlockSpec((1, FF_TILE, d), lambda s, f, e, r0, nb, t: (e[s], f_eff(s, f, nb), 0)),
                  pl.BlockSpec(memory_space=pl.ANY)],
        out_specs=pl.BlockSpec(memory_space=pl.ANY),
        scratch_shapes=[pltpu.VMEM((d, FF_TILE), BF16), pltpu.VMEM((d, FF_TILE), BF16),
                        pltpu.VMEM((FF_TILE, d), BF16), pltpu.VMEM((2, SUPER_ROWS, half), jnp.uint32),
                        pltpu.VMEM((SUPER_ROWS, d), F32), pltpu.SMEM((1,), I32),
                        pltpu.SemaphoreType.DMA((2,)), pltpu.SemaphoreType.DMA(())],
    )
    return pl.pallas_call(
        _moe_expert_body,
        grid_spec=grid_spec,
        out_shape=jax.ShapeDtypeStruct((p, d), F32),
        compiler_params=_cparams(("arbitrary", "arbitrary")),
        name="moe_experts",
    )(sb_expert, sb_row0, sb_blocks, total_rows, wg, wu, wd, buf)


def _moe_combine_body(dest_ref, x1_ref, route_ref, out_ref, y_ref, rows_ref, sem):
    tm, d = x1_ref.shape
    i = pl.program_id(0)
    sublanes = rows_ref.shape[3]

    def row_copy(tile, q, u, k):
        slot = dest_ref[(tile * tm + q * sublanes + u) * TOP_K + k]
        return pltpu.make_async_copy(out_ref.at[pl.ds(slot, 1)],
                                     rows_ref.at[tile % 2, k, q, pl.ds(u, 1)], sem.at[tile % 2])

    def for_rows(tile, fn):
        def body(q, c):
            for u in range(sublanes):
                for k in range(TOP_K):
                    fn(row_copy(tile, q, u, k), k)
            return c
        lax.fori_loop(0, tm // sublanes, body, 0)

    @pl.when(i == 0)
    def _():
        for_rows(0, lambda cp, k: cp.start(priority=k % 2))

    @pl.when(i + 1 < pl.num_programs(0))
    def _():
        for_rows(i + 1, lambda cp, k: cp.start(priority=k % 2))

    for_rows(i, lambda cp, k: cp.wait())
    route = route_ref[...]
    rows = rows_ref[i % 2]
    y_ref[...] = x1_ref[...] + (rows[0].reshape(tm, d) * route[:, 0:1]
                                + rows[1].reshape(tm, d) * route[:, 1:2])


def moe_combine(dest, x1, route, expert_out, *, tm=512):
    n, d = x1.shape
    grid_spec = pltpu.PrefetchScalarGridSpec(
        num_scalar_prefetch=1,
        grid=(n // tm,),
        in_specs=[pl.BlockSpec((tm, d), lambda i, *_: (i, 0)),
                  pl.BlockSpec((tm, LANES), lambda i, *_: (i, 0)),
                  pl.BlockSpec(memory_space=pl.ANY)],
        out_specs=pl.BlockSpec((tm, d), lambda i, *_: (i, 0)),
        scratch_shapes=[pltpu.VMEM((2, TOP_K, tm // SUBLANES, SUBLANES, d), F32),
                        pltpu.SemaphoreType.DMA((2,))],
    )
    return pl.pallas_call(
        _moe_combine_body,
        grid_spec=grid_spec,
        out_shape=jax.ShapeDtypeStruct((n, d), F32),
        compiler_params=_cparams(("arbitrary",)),
        name="moe_combine",
    )(dest, x1, route, expert_out)


def hierarchical_moe(x1, h2, route, wg, wu, wd):
    n = h2.shape[0]
    n_experts = wg.shape[0]
    m = n * TOP_K
    ids_row = route[:, 2:4].astype(I32).reshape(1, m)
    rank_row, counts = moe_rank(ids_row, n_experts)
    counts = counts[:, 0].astype(I32)
    padded = (counts + MOE_BLOCK - 1) // MOE_BLOCK * MOE_BLOCK
    pad_end = jnp.cumsum(padded)
    pad_start = pad_end - padded
    dest = moe_dest(ids_row, rank_row, pad_start.astype(F32).reshape(n_experts, 1)).reshape(m)
    n_slots = m + n_experts * MOE_BLOCK
    n_sb = (padded + SUPER_ROWS - 1) // SUPER_ROWS
    sb_end = jnp.cumsum(n_sb)
    sb_start = sb_end - n_sb
    s_idx = jnp.arange(n_experts + m // SUPER_ROWS, dtype=I32)
    used = s_idx < sb_end[-1]
    e_of = jnp.minimum(jnp.sum(sb_end[None, :] <= s_idx[:, None], axis=1), n_experts - 1)
    piece = s_idx - sb_start[e_of]
    sb_row0 = jnp.where(used, pad_start[e_of] + piece * SUPER_ROWS, 0).astype(I32)
    sb_rows = jnp.clip(padded[e_of] - piece * SUPER_ROWS, 0, SUPER_ROWS)
    sb_blocks = jnp.where(used, sb_rows // MOE_BLOCK, 0).astype(I32)
    sb_expert = jnp.where(used, e_of, e_of[jnp.maximum(sb_end[-1] - 1, 0)]).astype(I32)
    buf = moe_dispatch(dest, pad_start.astype(I32), pad_end.astype(I32), h2, n_slots)
    expert_out = moe_experts(sb_expert, sb_row0, sb_blocks, pad_end[-1:].astype(I32), buf,
                             wg, wu, wd)
    return moe_combine(dest, x1, route, expert_out)


def _w_regroup_body(w_ref, o_ref, *, c0, dl, il):
    w = w_ref[...]
    lane = lax.broadcasted_iota(I32, (1, LANES), 1)
    o_ref[:, :c0] = w[:, :c0].astype(BF16)
    o_ref[:, c0:c0 + LANES] = jnp.where(lane < dl, w[:, c0:c0 + LANES], 0.0).astype(BF16)
    o_ref[:, c0 + LANES:c0 + 2 * LANES] = jnp.where(
        lane < il, w[:, c0 + dl:c0 + dl + LANES], 0.0).astype(BF16)
    o_ref[:, c0 + 2 * LANES:] = w[:, c0 + dl + il:].astype(BF16)


def regroup_w_in(w_in, layer, c0, dl, il, *, tr=256):
    _, d, cols = w_in.shape
    out_cols = cols + 2 * LANES - dl - il
    return pl.pallas_call(
        functools.partial(_w_regroup_body, c0=c0, dl=dl, il=il),
        grid=(d // tr,),
        in_specs=[pl.BlockSpec((None, tr, cols), lambda i: (layer, i, 0))],
        out_specs=pl.BlockSpec((tr, out_cols), lambda i: (i, 0)),
        out_shape=jax.ShapeDtypeStruct((d, out_cols), BF16),
        compiler_params=_cparams(("parallel",)),
        name="regroup_w_in",
    )(w_in)


def _pad_cols(w, to):
    return jnp.pad(w, ((0, 0), (0, to - w.shape[1])))


def _pad_rows(w, to):
    return jnp.pad(w, ((0, to - w.shape[0]), (0, 0)))


def _layer(x2d, batch, norm1_w, w_in, mu, w0, w2, a0, a2, g2, k_k, k_a, r_k, gn_w, gn_b,
           q_norm_w, k_norm_w, sinks, proj_rwkv, proj_attn, w_out, norm2_w, wc, bc, wf, bf,
           wg, wu, wd):
    n, d = x2d.shape
    seq = n // batch
    width = w0.shape[0]
    dl, il, gl = w2.shape[0], a2.shape[0], g2.shape[0]
    q_width = proj_attn.shape[0]
    rwkv_cols = 3 * width + dl + il + gl
    w_in_all, layer = w_in
    kv_width = (w_in_all.shape[2] - rwkv_cols - q_width - 2 * d) // 2
    row = lambda v: v.reshape(1, -1).astype(F32)

    c0 = 3 * width
    w_all = regroup_w_in(w_in_all, layer, c0, dl, il)
    mu_p = jnp.concatenate([mu[:c0], jnp.pad(mu[c0:c0 + dl], (0, LANES - dl)),
                            jnp.pad(mu[c0 + dl:c0 + dl + il], (0, LANES - il)),
                            mu[c0 + dl + il:]]).reshape(1, -1)
    rwkv_w = c0 + 2 * LANES + gl
    proj = norm_proj(x2d, row(norm1_w), w_all, 2 * d)

    rt, at, bt, kt, vv, g, bonus, pc = rwkv_prep(
        proj, mu_p, row(w0), _pad_rows(w2, LANES).astype(BF16), row(a0),
        _pad_rows(a2, LANES).astype(BF16), g2.astype(BF16), row(k_k), row(k_a), row(r_k),
        seq_len=seq)
    a_out, b_out = mixers(rt, at, bt, kt, vv, pc, g, bonus, row(gn_w), row(gn_b), proj, q_norm_w,
                          k_norm_w, sinks.astype(F32), batch=batch, q_col0=rwkv_w,
                          q_width=q_width, kv_width=kv_width)

    n_groups, n_experts = wc.shape[1], wf.shape[1]
    router_w = _pad_cols(jnp.concatenate([wc, wf], axis=1), LANES)
    router_b = _pad_cols(jnp.concatenate([bc, bf]).reshape(1, -1), LANES)
    assert n_groups == N_GROUPS and n_experts == N_GROUPS * GROUP_SIZE
    x1, h2, route = merge_route(a_out, b_out, proj, rwkv_w + q_width + 2 * kv_width, x2d,
                                proj_rwkv.astype(BF16),
                                proj_attn.astype(BF16), w_out.astype(BF16), row(norm2_w),
                                router_w, router_b)
    return hierarchical_moe(x1, h2, route, wg, wu, wd)


def kernel(x, norm1_w, w_in, rwkv_mu, rwkv_w0, rwkv_w2, rwkv_a0, rwkv_a2, rwkv_g2, rwkv_k_k,
           rwkv_k_a, rwkv_r_k, rwkv_gn_w, rwkv_gn_b, q_norm_w, k_norm_w, attn_sinks, proj_rwkv,
           proj_attn, w_out, norm2_w, router_coarse_w, router_coarse_b, router_fine_w,
           router_fine_b, expert_w_gate, expert_w_up, expert_w_down):
    batch, seq, d = x.shape
    x2d = x.reshape(batch * seq, d)
    for layer in range(norm1_w.shape[0]):
        x2d = _layer(x2d, batch, norm1_w[layer], (w_in, layer), rwkv_mu[layer], rwkv_w0[layer],
                     rwkv_w2[layer], rwkv_a0[layer], rwkv_a2[layer], rwkv_g2[layer],
                     rwkv_k_k[layer], rwkv_k_a[layer], rwkv_r_k[layer].reshape(-1),
                     rwkv_gn_w[layer], rwkv_gn_b[layer], q_norm_w[layer], k_norm_w[layer],
                     attn_sinks[layer], proj_rwkv[layer], proj_attn[layer], w_out[layer],
                     norm2_w[layer], router_coarse_w[layer], router_coarse_b[layer],
                     router_fine_w[layer], router_fine_b[layer], expert_w_gate[layer],
                     expert_w_up[layer], expert_w_down[layer])
    return x2d.reshape(batch, seq, d)
```
